```python
import math
import jax, jax.numpy as jnp
from jax import lax
import numpy as np

D_MODEL = 1024
BATCH = 4
SEQ = 4096
DEPTH = 2
DEC_BATCH = 32
DEC_SEQ = 1
PAST_LEN = 16384
PAGE_SIZE = 128

HEAD_DIM = 64
MIX_WIDTH = 3 * D_MODEL // 2
GROUP_WIDTH = MIX_WIDTH // 4
N_HEADS = GROUP_WIDTH // HEAD_DIM
DIL_PAIRS = ((128, 1), (512, 4), (2048, 16))
WIN_MAX = 2048
ATT_BLOCK = 128
ROPE_THETA = 10000.0
DELTA_CONV = 4
SHORT_CONV = 3
CHUNK = 64
D_FF = 4 * D_MODEL
EPS = 1e-6
NEG = -1e30

IN_SPLITS = (
    ('a_q', GROUP_WIDTH), ('a_k', GROUP_WIDTH), ('a_v', GROUP_WIDTH),
    ('b_q', GROUP_WIDTH), ('b_k', GROUP_WIDTH), ('b_v', GROUP_WIDTH), ('b_z', GROUP_WIDTH),
    ('b_a', N_HEADS), ('b_b', N_HEADS),
    ('c_q', GROUP_WIDTH), ('c_k', GROUP_WIDTH), ('c_v', GROUP_WIDTH), ('c_g', GROUP_WIDTH),
    ('d_x', GROUP_WIDTH), ('d_c', GROUP_WIDTH), ('d_b', GROUP_WIDTH),
)
IN_WIDTH = 4 * 3 * GROUP_WIDTH + GROUP_WIDTH + 2 * N_HEADS + GROUP_WIDTH

kernel_name = 'hybrid_dilated_delta_retention_shortconv_step'


def _split_cols(z):
    out, off = {}, 0
    for name, w in IN_SPLITS:
        out[name] = z[..., off:off + w]
        off += w
    return out


def _heads(a):
    return a.reshape(a.shape[:-1] + (-1, HEAD_DIM))


def rmsnorm(x, g):
    xf = x.astype(jnp.float32)
    y = xf * lax.rsqrt(jnp.mean(xf * xf, axis=-1, keepdims=True) + EPS)
    return (y * g.astype(jnp.float32)).astype(x.dtype)


def l2norm(x):
    xf = x.astype(jnp.float32)
    return xf * lax.rsqrt(jnp.sum(xf * xf, axis=-1, keepdims=True) + EPS)


def _rope_freq():
    return ROPE_THETA ** (-jnp.arange(0, HEAD_DIM, 2, dtype=jnp.float32) / HEAD_DIM)


def _retnet_freq():
    return 1.0 / (10000.0 ** jnp.linspace(0.0, 1.0, HEAD_DIM // 2, dtype=jnp.float32))


def _retention_log_decay():
    return jnp.log(1.0 - 2.0 ** (-5.0 - jnp.arange(N_HEADS, dtype=jnp.float32)))


def rotate(x, pos, inv_freq):
    ang = pos.astype(jnp.float32)[:, None] * inv_freq[None, :]
    cos, sin = jnp.cos(ang)[None, :, None, :], jnp.sin(ang)[None, :, None, :]
    xf = x.astype(jnp.float32)
    x1, x2 = xf[..., :HEAD_DIM // 2], xf[..., HEAD_DIM // 2:]
    return jnp.concatenate([x1 * cos - x2 * sin, x2 * cos + x1 * sin], axis=-1).astype(x.dtype)


def causal_dwconv(x, prev, w):
    width, T = w.shape[0], x.shape[1]
    xp = jnp.concatenate([prev.astype(x.dtype), x], axis=1)
    y = xp[:, 0:T] * w[0]
    for j in range(1, width):
        y = y + xp[:, j:j + T] * w[j]
    return y, xp[:, T:]


def dilated_branch_prompt(q, k, v, window, dil):
    N, S, H, E = q.shape
    L = S // dil
    nk = window // dil
    nb = -(-L // ATT_BLOCK)
    Lp = nb * ATT_BLOCK

    def by_residue(a):
        return a.astype(jnp.float32).reshape(N, L, dil, H, E).transpose(0, 2, 3, 1, 4)

    qs = jnp.pad(by_residue(q), [(0, 0)] * 3 + [(0, Lp - L), (0, 0)]).reshape(N, dil, H, nb, ATT_BLOCK, E)

    def key_blocks(a):
        a = jnp.pad(by_residue(a), [(0, 0)] * 3 + [(ATT_BLOCK, Lp - L), (0, 0)])
        a = a.reshape(N, dil, H, nb + 1, ATT_BLOCK, E)
        return jnp.concatenate([a[:, :, :, :-1], a[:, :, :, 1:]], axis=4)

    kk, vv = key_blocks(k), key_blocks(v)
    s = jnp.einsum('ndhbqe,ndhbke->ndhbqk', qs, kk) * (E ** -0.5)
    dist = ATT_BLOCK + jnp.arange(ATT_BLOCK)[:, None] - jnp.arange(2 * ATT_BLOCK)[None, :]
    kidx = (jnp.arange(nb)[:, None] - 1) * ATT_BLOCK + jnp.arange(2 * ATT_BLOCK)[None, :]
    valid = ((dist >= 0) & (dist <= nk))[None] & (kidx >= 0)[:, None, :]
    s = jnp.where(valid, s, NEG)
    lse = jax.nn.logsumexp(s, axis=-1)
    o = jnp.einsum('ndhbqk,ndhbke->ndhbqe', jnp.exp(s - lse[..., None]), vv)
    o = o.reshape(N, dil, H, Lp, E)[:, :, :, :L].transpose(0, 3, 1, 2, 4).reshape(N, S, H, E)
    lse = lse.reshape(N, dil, H, Lp)[..., :L].transpose(0, 3, 1, 2).reshape(N, S, H)
    return o, lse


def dilated_branch_gather(q, k_all, v_all, q_pos, buf_start, window, dil):
    nk = window // dil
    kpos = q_pos[:, None] - dil * jnp.arange(nk + 1)[None, :]
    idx = kpos - buf_start
    valid = idx >= 0
    idx = jnp.clip(idx, 0, k_all.shape[1] - 1)
    kg = jnp.take(k_all, idx, axis=1).astype(jnp.float32)
    vg = jnp.take(v_all, idx, axis=1).astype(jnp.float32)
    s = jnp.einsum('nthe,ntjhe->nthj', q.astype(jnp.float32), kg) * (q.shape[-1] ** -0.5)
    s = jnp.where(valid[None, :, None, :], s, NEG)
    lse = jax.nn.logsumexp(s, axis=-1)
    o = jnp.einsum('nthj,ntjhe->nthe', jnp.exp(s - lse[..., None]), vg)
    return o, lse


def mix_by_denominators(outs, lses):
    w = jax.nn.softmax(jnp.stack(lses), axis=0)
    return jnp.einsum('gnth,gnthe->nthe', w, jnp.stack(outs))


def _to_chunks(a, Tp):
    pad = Tp - a.shape[1]
    a = jnp.pad(a.astype(jnp.float32), [(0, 0), (0, pad)] + [(0, 0)] * (a.ndim - 2))
    a = jnp.moveaxis(a, 1, 2)
    return a.reshape(a.shape[:2] + (Tp // CHUNK, CHUNK) + a.shape[3:])


def _from_chunks(o, T):
    nc, N, H, C, E = o.shape
    return o.transpose(1, 0, 3, 2, 4).reshape(N, nc * C, H, E)[:, :T]


def _chunk_decay(g):
    G = jnp.cumsum(g, axis=-1)
    incl = jnp.tril(jnp.ones((CHUNK, CHUNK), bool))
    D = jnp.exp(jnp.where(incl, G[..., :, None] - G[..., None, :], -jnp.inf))
    return G, D


def gated_delta_rule(q, k, v, g, beta, s0):
    T = q.shape[1]
    Tp = -(-T // CHUNK) * CHUNK
    qc, kc, vc = _to_chunks(q, Tp), _to_chunks(k, Tp), _to_chunks(v, Tp)
    gc, bc = _to_chunks(g, Tp), _to_chunks(beta, Tp)
    G, D = _chunk_decay(gc)
    strict = jnp.tril(jnp.ones((CHUNK, CHUNK), bool), -1)
    kb, vb = kc * bc[..., None], vc * bc[..., None]
    A = jnp.where(strict, jnp.einsum('nhbid,nhbjd->nhbij', kb, kc) * D, 0.0)
    M = A + jnp.eye(CHUNK, dtype=jnp.float32)
    rhs = jnp.concatenate([vb, kb * jnp.exp(G)[..., None]], axis=-1)
    sol = lax.linalg.triangular_solve(M, rhs, left_side=True, lower=True)
    dv = v.shape[-1]
    u, w = sol[..., :dv], sol[..., dv:]
    qk = jnp.einsum('nhbid,nhbjd->nhbij', qc, kc) * D
    Glast = G[..., -1:]
    xs = tuple(jnp.moveaxis(a, 2, 0) for a in (
        qc * jnp.exp(G)[..., None], kc * jnp.exp(Glast - G)[..., None], u, w, qk, jnp.exp(Glast[..., 0])))

    def step(S, inp):
        qg_c, kd_c, u_c, w_c, qk_c, dl_c = inp
        v_new = u_c - jnp.einsum('nhid,nhde->nhie', w_c, S)
        o = jnp.einsum('nhid,nhde->nhie', qg_c, S) + jnp.einsum('nhij,nhje->nhie', qk_c, v_new)
        S = S * dl_c[..., None, None] + jnp.einsum('nhid,nhie->nhde', kd_c, v_new)
        return S, o

    S, o = lax.scan(step, s0.astype(jnp.float32), xs)
    return _from_chunks(o, T), S


def retention(q, k, v, g, s0):
    T = q.shape[1]
    Tp = -(-T // CHUNK) * CHUNK
    qc, kc, vc, gc = (_to_chunks(a, Tp) for a in (q, k, v, g))
    G, D = _chunk_decay(gc)
    qk = jnp.einsum('nhbid,nhbjd->nhbij', qc, kc) * D
    Glast = G[..., -1:]
    xs = tuple(jnp.moveaxis(a, 2, 0) for a in (
        qc * jnp.exp(G)[..., None], kc * jnp.exp(Glast - G)[..., None], vc, qk, jnp.exp(Glast[..., 0])))

    def step(S, inp):
        qg_c, kd_c, v_c, qk_c, dl_c = inp
        o = jnp.einsum('nhid,nhde->nhie', qg_c, S) + jnp.einsum('nhij,nhje->nhie', qk_c, v_c)
        S = S * dl_c[..., None, None] + jnp.einsum('nhid,nhie->nhde', kd_c, v_c)
        return S, o

    S, o = lax.scan(step, s0.astype(jnp.float32), xs)
    return _from_chunks(o, T), S


def trunk_layer(x, pos0, prev, lw, is_prompt):
    a_k_prev, a_v_prev, b_conv_prev, b_rec_prev, c_rec_prev, d_conv_prev = prev
    (norm_mix, w_in, q_norm_a, k_norm_a, conv_b, a_log_b, dt_bias_b, onorm_b, onorm_c,
     conv_d, w_out, norm_ffn, w_up, w_down) = lw
    N, T, _ = x.shape
    dt = x.dtype
    f32 = jnp.float32
    pos = pos0 + jnp.arange(T, dtype=jnp.int32)
    z = _split_cols(rmsnorm(x, norm_mix) @ w_in)

    qa = rotate(rmsnorm(_heads(z['a_q']), q_norm_a), pos, _rope_freq())
    ka = rotate(rmsnorm(_heads(z['a_k']), k_norm_a), pos, _rope_freq())
    va = _heads(z['a_v'])
    if is_prompt:
        k_all, v_all = ka, va
        res = [dilated_branch_prompt(qa, ka, va, w, d) for w, d in DIL_PAIRS]
    else:
        k_all = jnp.concatenate([a_k_prev.astype(dt), ka], axis=1)
        v_all = jnp.concatenate([a_v_prev.astype(dt), va], axis=1)
        buf_start = pos0 - a_k_prev.shape[1]
        res = [dilated_branch_gather(qa, k_all, v_all, pos, buf_start, w, d) for w, d in DIL_PAIRS]
    o_a = mix_by_denominators([r[0] for r in res], [r[1] for r in res])
    keep = min(WIN_MAX, k_all.shape[1])

    qkv_b = jnp.concatenate([z['b_q'], z['b_k'], z['b_v']], axis=-1)
    conv_out, b_conv_new = causal_dwconv(qkv_b, b_conv_prev, conv_b)
    qb, kb, vb = jnp.split(jax.nn.silu(conv_out), 3, axis=-1)
    qb = l2norm(_heads(qb)) * (HEAD_DIM ** -0.5)
    kb = l2norm(_heads(kb))
    vb = _heads(vb)
    beta = jax.nn.sigmoid(z['b_b'].astype(f32))
    g_b = -jnp.exp(a_log_b.astype(f32)) * jax.nn.softplus(z['b_a'].astype(f32) + dt_bias_b.astype(f32))
    o_b, b_rec_new = gated_delta_rule(qb, kb, vb, g_b, beta, b_rec_prev)
    o_b = rmsnorm(o_b, onorm_b) * jax.nn.silu(_heads(z['b_z']).astype(f32))

    qc = rotate(_heads(z['c_q']), pos, _retnet_freq())
    kc = rotate(_heads(z['c_k']), pos, _retnet_freq()) * (HEAD_DIM ** -0.5)
    vc = _heads(z['c_v'])
    g_c = jnp.broadcast_to(_retention_log_decay(), (N, T, N_HEADS))
    o_c, c_rec_new = retention(qc, kc, vc, g_c, c_rec_prev)
    o_c = rmsnorm(o_c, onorm_c) * jax.nn.silu(_heads(z['c_g']).astype(f32))

    conv_d_out, d_conv_new = causal_dwconv(z['d_c'] * z['d_x'], d_conv_prev, conv_d)
    o_d = z['d_b'] * conv_d_out

    mix = jnp.concatenate([o_a.reshape(N, T, GROUP_WIDTH).astype(dt), o_b.reshape(N, T, GROUP_WIDTH).astype(dt),
                           o_c.reshape(N, T, GROUP_WIDTH).astype(dt), o_d.astype(dt)], axis=-1)
    x = x + mix @ w_out
    x = x + jnp.square(jax.nn.relu(rmsnorm(x, norm_ffn) @ w_up)) @ w_down
    new = (k_all[:, -keep:], v_all[:, -keep:], b_conv_new.astype(dt), b_rec_new.astype(dt),
           c_rec_new.astype(dt), d_conv_new.astype(dt))
    return x, new


def setup_inputs(seed: int = 0) -> dict:
    key = jax.random.key(seed)
    ks = jax.random.split(key, 24)
    f32 = jnp.float32

    def nrm(k, shape, s):
        return jax.random.normal(k, shape, f32) * s

    def gain(k, shape):
        return 1.0 + nrm(k, shape, 0.02)

    a_buf = min(WIN_MAX, PAST_LEN)
    dt = jnp.exp(jax.random.uniform(ks[13], (DEPTH, N_HEADS), f32, math.log(1e-3), math.log(1e-1)))
    return {
        'x_prompt': nrm(ks[0], (BATCH, SEQ, D_MODEL), 1.0),
        'x_sample': nrm(ks[1], (DEC_BATCH, DEC_SEQ, D_MODEL), 1.0),
        'cache_a_k': nrm(ks[2], (DEPTH, DEC_BATCH, a_buf, N_HEADS, HEAD_DIM), 1.0),
        'cache_a_v': nrm(ks[3], (DEPTH, DEC_BATCH, a_buf, N_HEADS, HEAD_DIM), 1.0),
        'state_b_conv': nrm(ks[4], (DEPTH, DEC_BATCH, DELTA_CONV - 1, 3 * GROUP_WIDTH), 1.0),
        'state_b_rec': nrm(ks[5], (DEPTH, DEC_BATCH, N_HEADS, HEAD_DIM, HEAD_DIM), 0.1),
        'state_c_rec': nrm(ks[6], (DEPTH, DEC_BATCH, N_HEADS, HEAD_DIM, HEAD_DIM), 1.0),
        'state_d_conv': nrm(ks[7], (DEPTH, DEC_BATCH, SHORT_CONV - 1, GROUP_WIDTH), 1.0),
        'norm_mix': gain(ks[8], (DEPTH, D_MODEL)),
        'w_in': nrm(ks[9], (DEPTH, D_MODEL, IN_WIDTH), D_MODEL ** -0.5),
        'q_norm_a': gain(ks[10], (DEPTH, HEAD_DIM)),
        'k_norm_a': gain(ks[11], (DEPTH, HEAD_DIM)),
        'conv_b': nrm(ks[12], (DEPTH, DELTA_CONV, 3 * GROUP_WIDTH), DELTA_CONV ** -0.5),
        'a_log_b': jnp.log(jax.random.uniform(ks[14], (DEPTH, N_HEADS), f32, 1.0, 16.0)),
        'dt_bias_b': dt + jnp.log(-jnp.expm1(-dt)),
        'onorm_b': gain(ks[15], (DEPTH, HEAD_DIM)),
        'onorm_c': gain(ks[16], (DEPTH, HEAD_DIM)),
        'conv_d': nrm(ks[17], (DEPTH, SHORT_CONV, GROUP_WIDTH), SHORT_CONV ** -0.5),
        'w_out': nrm(ks[18], (DEPTH, MIX_WIDTH, D_MODEL), MIX_WIDTH ** -0.5),
        'norm_ffn': gain(ks[19], (DEPTH, D_MODEL)),
        'w_up': nrm(ks[20], (DEPTH, D_MODEL, D_FF), D_MODEL ** -0.5),
        'w_down': nrm(ks[21], (DEPTH, D_FF, D_MODEL), D_FF ** -0.5),
    }


def reference(x_prompt, x_sample, cache_a_k, cache_a_v, state_b_conv, state_b_rec, state_c_rec, state_d_conv,
              norm_mix, w_in, q_norm_a, k_norm_a, conv_b, a_log_b, dt_bias_b, onorm_b, onorm_c, conv_d,
              w_out, norm_ffn, w_up, w_down):
    weights = (norm_mix, w_in, q_norm_a, k_norm_a, conv_b, a_log_b, dt_bias_b, onorm_b, onorm_c, conv_d,
               w_out, norm_ffn, w_up, w_down)
    bp, dtp = x_prompt.shape[0], x_prompt.dtype
    yp, ys = x_prompt, x_sample
    new_p, new_s = [], []
    for l in range(DEPTH):
        lw = tuple(w[l] for w in weights)
        prev_p = (None, None,
                  jnp.zeros((bp, DELTA_CONV - 1, 3 * GROUP_WIDTH), dtp),
                  jnp.zeros((bp, N_HEADS, HEAD_DIM, HEAD_DIM), dtp),
                  jnp.zeros((bp, N_HEADS, HEAD_DIM, HEAD_DIM), dtp),
                  jnp.zeros((bp, SHORT_CONV - 1, GROUP_WIDTH), dtp))
        yp, sp = trunk_layer(yp, 0, prev_p, lw, True)
        prev_s = (cache_a_k[l], cache_a_v[l], state_b_conv[l], state_b_rec[l], state_c_rec[l], state_d_conv[l])
        ys, ss = trunk_layer(ys, PAST_LEN, prev_s, lw, False)
        new_p.append(sp)
        new_s.append(ss)
    p_a_k = jnp.stack([s[0] for s in new_p])
    p_a_v = jnp.stack([s[1] for s in new_p])
    p_b_conv = jnp.stack([s[2] for s in new_p])
    p_b_rec = jnp.stack([s[3] for s in new_p])
    p_c_rec = jnp.stack([s[4] for s in new_p])
    p_d_conv = jnp.stack([s[5] for s in new_p])
    s_a_k = jnp.stack([s[0] for s in new_s])
    s_a_v = jnp.stack([s[1] for s in new_s])
    s_b_conv = jnp.stack([s[2] for s in new_s])
    s_b_rec = jnp.stack([s[3] for s in new_s])
    s_c_rec = jnp.stack([s[4] for s in new_s])
    s_d_conv = jnp.stack([s[5] for s in new_s])
    return (yp, ys, p_a_k, p_a_v, p_b_conv, p_b_rec, p_c_rec, p_d_conv,
            s_a_k, s_a_v, s_b_conv, s_b_rec, s_c_rec, s_d_conv)
```

```python
import functools

import jax
import jax.numpy as jnp
from jax import lax
from jax.experimental import pallas as pl
from jax.experimental.pallas import tpu as pltpu

f32, bf16, i32 = jnp.float32, jnp.bfloat16, jnp.int32
HI = lax.Precision.HIGHEST

D_MODEL = 1024
HEAD_DIM = 64
N_HEADS = 6
N_PAIRS = N_HEADS // 2
GROUP_WIDTH = N_HEADS * HEAD_DIM
D_FF = 4 * D_MODEL
DIL_PAIRS = ((128, 1), (512, 4), (2048, 16))
WIN_MAX = 2048
ATT_BLOCK = 128
ROPE_THETA = 10000.0
DELTA_CONV = 4
SHORT_CONV = 3
CHUNK = 64
EPS = 1e-6
NEG = -1e30
PAST_LEN = 16384
DEPTH = 2
LANES = 128
SUBLANES = 8
VMEM_LIMIT = 56 * 1024 * 1024

IN_WIDTH = 14 * GROUP_WIDTH + 2 * N_HEADS
AB_OFFSET_SRC = 7 * GROUP_WIDTH
Z_WIDTH = 14 * GROUP_WIDTH + LANES
ZB_QKV_BLK = 1
ZB_Z_BLK = 6
ZC_Q_BLK, ZC_K_BLK, ZC_V_BLK, ZC_G_BLK = 7, 8, 9, 10
ZD_X_BLK, ZD_C_BLK, ZD_B_BLK = 11, 12, 13
ZAB_BLK = 14 * GROUP_WIDTH // LANES


def _iota(shape, dim):
    return lax.broadcasted_iota(i32, shape, dim)


def _params(n_axes):
    return pltpu.CompilerParams(dimension_semantics=("arbitrary",) * n_axes, vmem_limit_bytes=VMEM_LIMIT)


def _same_head_matrix(width, value):
    r = _iota((width, width), 0) // HEAD_DIM
    c = _iota((width, width), 1) // HEAD_DIM
    return jnp.where(r == c, value, 0.0).astype(f32)


def _head_sum(x, mat):
    return jnp.dot(x, mat, precision=HI, preferred_element_type=f32)


def _rope_slab(x, cos, sin_signed):
    p = _iota((1, LANES), 1) % HEAD_DIM
    partner = jnp.where(p < HEAD_DIM // 2, pltpu.roll(x, LANES - HEAD_DIM // 2, 1), pltpu.roll(x, HEAD_DIM // 2, 1))
    return x * cos + partner * sin_signed


def _rope_wide(x, cos, sin_signed):
    return jnp.concatenate(
        [_rope_slab(x[:, LANES * i:LANES * (i + 1)], cos, sin_signed) for i in range(x.shape[1] // LANES)], axis=1)


def _stack_heads(x, first_head):
    return jnp.concatenate([jnp.where(first_head, x, 0.0), jnp.where(first_head, 0.0, x)], axis=0)


def _dot_nt(a, b):
    return lax.dot_general(a, b, (((1,), (1,)), ((), ())), preferred_element_type=f32)


def _dot_tn(a, b):
    return lax.dot_general(a, b, (((0,), (0,)), ((), ())), preferred_element_type=f32)


def _silu(x):
    return x * jax.nn.sigmoid(x)


def _inproj_kernel(x_ref, g_ref, w_ref, o_ref):
    x = x_ref[...]
    ms = jnp.mean(x * x, axis=-1, keepdims=True)
    xn = (x * lax.rsqrt(ms + EPS) * g_ref[...]).astype(bf16)
    step = 4 * LANES
    for c0 in range(0, Z_WIDTH, step):
        cw = min(step, Z_WIDTH - c0)
        o_ref[:, c0:c0 + cw] = jnp.dot(xn, w_ref[:, c0:c0 + cw], preferred_element_type=f32)


def _inproj(x2d, gain, w_pack, tm):
    m = x2d.shape[0]
    return pl.pallas_call(
        _inproj_kernel,
        grid=(m // tm,),
        in_specs=[pl.BlockSpec((tm, D_MODEL), lambda i: (i, 0)),
                  pl.BlockSpec((1, D_MODEL), lambda i: (0, 0)),
                  pl.BlockSpec((D_MODEL, Z_WIDTH), lambda i: (0, 0), pipeline_mode=pl.Buffered(1))],
        out_specs=pl.BlockSpec((tm, Z_WIDTH), lambda i: (i, 0)),
        out_shape=jax.ShapeDtypeStruct((m, Z_WIDTH), f32),
        compiler_params=_params(1), name="inproj")(x2d, gain, w_pack)


def _ffn_kernel(x_ref, oa_ref, ob_ref, oc_ref, od_ref, wo_ref, g_ref, wu_ref, wd_ref, y_ref):
    mix = jnp.concatenate([r[...].astype(bf16) for r in (oa_ref, ob_ref, oc_ref, od_ref)], axis=1)
    h = x_ref[...] + jnp.dot(mix, wo_ref[...], preferred_element_type=f32)
    ms = jnp.mean(h * h, axis=-1, keepdims=True)
    hn = (h * lax.rsqrt(ms + EPS) * g_ref[...]).astype(bf16)
    y_ref[...] = h
    step = D_FF // 4
    for c in range(0, D_FF, step):
        u = jnp.dot(hn, wu_ref[:, c:c + step], preferred_element_type=f32)
        u = jnp.square(jnp.maximum(u, 0.0)).astype(bf16)
        y_ref[...] += jnp.dot(u, wd_ref[c:c + step, :], preferred_element_type=f32)


def _ffn(x2d, oa, ob, oc, od, w_out, gain, w_up, w_down, tm):
    m = x2d.shape[0]
    row = lambda width: pl.BlockSpec((tm, width), lambda i: (i, 0))
    whole = lambda a: pl.BlockSpec(a.shape, lambda i: (0, 0), pipeline_mode=pl.Buffered(1))
    return pl.pallas_call(
        _ffn_kernel,
        grid=(m // tm,),
        in_specs=[row(D_MODEL), row(GROUP_WIDTH), row(GROUP_WIDTH), row(GROUP_WIDTH), row(GROUP_WIDTH),
                  whole(w_out), pl.BlockSpec((1, D_MODEL), lambda i: (0, 0)), whole(w_up), whole(w_down)],
        out_specs=row(D_MODEL),
        out_shape=jax.ShapeDtypeStruct((m, D_MODEL), f32),
        compiler_params=_params(1), name="ffn")(x2d, oa, ob, oc, od, w_out, gain, w_up, w_down)


def _attn_prompt_kernel(q_ref, k_ref, v_ref, cos_ref, sin_ref, qg_ref, kg_ref,
                        o_ref, kt_ref, vt_ref, qs, ks, acc, m_s, l_s, *, seq, keep):
    mean_mat = _same_head_matrix(LANES, 1.0 / HEAD_DIM)
    rc = 512

    def prep(i, carry):
        rows = pl.ds(pl.multiple_of(i * rc, rc), rc)
        cos, sin = cos_ref[rows, :], sin_ref[rows, :]
        q, k = q_ref[rows, :], k_ref[rows, :]
        q = q * lax.rsqrt(_head_sum(q * q, mean_mat) + EPS) * qg_ref[...]
        k = k * lax.rsqrt(_head_sum(k * k, mean_mat) + EPS) * kg_ref[...]
        qs[rows, :] = _rope_slab(q, cos, sin) * (HEAD_DIM ** -0.5)
        ks[rows, :] = _rope_slab(k, cos, sin)
        return carry

    lax.fori_loop(0, seq // rc, prep, 0)

    for c in range(keep // rc):
        r0 = seq - keep + c * rc
        kt_ref[:, c * rc:(c + 1) * rc] = ks[r0:r0 + rc, :].T
        vt_ref[:, c * rc:(c + 1) * rc] = v_ref[r0:r0 + rc, :].T

    first_head = _iota((1, LANES), 1) < HEAD_DIM
    blk = ATT_BLOCK

    def tile(start, dil, first, merge):
        def rows_of(st, n):
            return pl.ds(st, n, stride=dil) if dil > 1 else pl.ds(st, n)

        nk = blk if first else 2 * blk
        kstart = start if first else start - blk * dil
        q = qs[rows_of(start, blk), :]
        k = ks[rows_of(kstart, nk), :].astype(bf16)
        v = v_ref[rows_of(kstart, nk), :].astype(bf16)
        qi, kj = _iota((blk, nk), 0), _iota((blk, nk), 1)
        valid = (kj <= qi) if first else ((kj >= qi) & (kj <= qi + blk))
        res = []
        for hh in range(2):
            qh = jnp.where(first_head if hh == 0 else ~first_head, q, 0.0).astype(bf16)
            s = jnp.where(valid, _dot_nt(qh, k), NEG)
            mt = jnp.max(s, axis=1, keepdims=True)
            p = jnp.exp(s - mt)
            lt = jnp.sum(p, axis=1, keepdims=True)
            res.append((mt, lt, jnp.dot(p.astype(bf16), v, preferred_element_type=f32)))
        mt, lt, pv = (jnp.where(first_head, a, b) for a, b in zip(res[0], res[1]))
        rows = rows_of(start, blk)
        if not merge:
            acc[rows, :], m_s[rows, :], l_s[rows, :] = pv, mt, lt
        else:
            mo = m_s[rows, :]
            mn = jnp.maximum(mo, mt)
            a, b = jnp.exp(mo - mn), jnp.exp(mt - mn)
            acc[rows, :] = acc[rows, :] * a + pv * b
            l_s[rows, :] = l_s[rows, :] * a + lt * b
            m_s[rows, :] = mn

    for bi, (window, dil) in enumerate(DIL_PAIRS):
        nb = seq // dil // blk
        merge = bi > 0

        def per_residue(r, carry, dil=dil, nb=nb, merge=merge):
            tile(r, dil, True, merge)

            def per_block(b, c):
                tile(r + b * (blk * dil), dil, False, merge)
                return c

            lax.fori_loop(1, nb, per_block, 0)
            return carry

        lax.fori_loop(0, dil, per_residue, 0)

    def finish(i, carry):
        rows = pl.ds(pl.multiple_of(i * rc, rc), rc)
        o_ref[rows, :] = acc[rows, :] / l_s[rows, :]
        return carry

    lax.fori_loop(0, seq // rc, finish, 0)


def _attn_prompt(z, n, seq, cos_t, sin_t, qgain, kgain):
    keep = min(WIN_MAX, seq)
    for window, dil in DIL_PAIRS:
        assert window // dil == ATT_BLOCK and seq % (dil * ATT_BLOCK) == 0
    per_group = GROUP_WIDTH // LANES
    zspec = lambda base: pl.BlockSpec((seq, LANES), lambda b, p, base=base: (b, base * per_group + p))
    whole = lambda rows: pl.BlockSpec((rows, LANES), lambda b, p: (0, 0))
    slab = lambda: pltpu.VMEM((seq, LANES), f32)
    return pl.pallas_call(
        functools.partial(_attn_prompt_kernel, seq=seq, keep=keep),
        grid=(n, N_PAIRS),
        in_specs=[zspec(0), zspec(1), zspec(2), whole(seq), whole(seq), whole(1), whole(1)],
        out_specs=[pl.BlockSpec((seq, LANES), lambda b, p: (b, p)),
                   pl.BlockSpec((None, LANES, keep), lambda b, p: (b, p, 0)),
                   pl.BlockSpec((None, LANES, keep), lambda b, p: (b, p, 0))],
        out_shape=[jax.ShapeDtypeStruct((n * seq, GROUP_WIDTH), f32),
                   jax.ShapeDtypeStruct((n, GROUP_WIDTH, keep), f32),
                   jax.ShapeDtypeStruct((n, GROUP_WIDTH, keep), f32)],
        scratch_shapes=[slab(), slab(), slab(), slab(), slab()],
        compiler_params=_params(2), name="attn_prompt")(z, z, z, cos_t, sin_t, qgain, kgain)


def _attn_decode_prep_kernel(z_ref, cos_ref, sin_ref, qg_ref, kg_ref, q_ref, k_ref, v_ref):
    mean_mat = _same_head_matrix(GROUP_WIDTH, 1.0 / HEAD_DIM)
    z = z_ref[...]
    q, k = z[:, :GROUP_WIDTH], z[:, GROUP_WIDTH:2 * GROUP_WIDTH]
    q = q * lax.rsqrt(_head_sum(q * q, mean_mat) + EPS) * qg_ref[...]
    k = k * lax.rsqrt(_head_sum(k * k, mean_mat) + EPS) * kg_ref[...]
    q_ref[...] = _rope_wide(q, cos_ref[...], sin_ref[...]) * (HEAD_DIM ** -0.5)
    k_ref[...] = _rope_wide(k, cos_ref[...], sin_ref[...])
    v_ref[...] = z[:, 2 * GROUP_WIDTH:]


def _attn_decode_prep(z, cos_row, sin_row, qgain, kgain):
    m = z.shape[0]
    small = lambda a: pl.BlockSpec(a.shape, lambda i: (0, 0))
    out = jax.ShapeDtypeStruct((m, GROUP_WIDTH), f32)
    ospec = pl.BlockSpec((m, GROUP_WIDTH), lambda i: (0, 0))
    return pl.pallas_call(
        _attn_decode_prep_kernel, grid=(1,),
        in_specs=[pl.BlockSpec((m, 3 * GROUP_WIDTH), lambda i: (0, 0)), small(cos_row), small(sin_row),
                  small(qgain), small(kgain)],
        out_specs=[ospec, ospec, ospec], out_shape=[out, out, out],
        compiler_params=_params(1), name="attn_decode_prep")(z, cos_row, sin_row, qgain, kgain)


def _attn_decode_kernel(*refs, buf, aliased):
    if aliased:
        kc_ref, vc_ref, q_ref, kn_ref, vn_ref, _, _, ko_ref, vo_ref, o_ref = refs
    else:
        kc_ref, vc_ref, q_ref, kn_ref, vn_ref, ko_ref, vo_ref, o_ref = refs
    kc = kc_ref[...]
    vc = vc_ref[...]
    qc = q_ref[...][:, :, :1]
    knew = kn_ref[...][:, :, :1]
    vnew = vn_ref[...][:, :, :1]
    dist = buf - _iota((1, 1, buf), 2)
    mult = jnp.zeros((1, 1, buf), f32)
    for window, dil in DIL_PAIRS:
        mult = mult + jnp.where((dist <= window) & (dist % dil == 0), 1.0, 0.0)
    s = jnp.sum(kc * qc, axis=1, keepdims=True)
    s_new = jnp.sum(knew * qc, axis=1, keepdims=True)
    s = jnp.where(mult > 0.0, s, NEG)
    mx = jnp.maximum(jnp.max(s, axis=2, keepdims=True), s_new)
    p = mult * jnp.exp(s - mx)
    p_new = float(len(DIL_PAIRS)) * jnp.exp(s_new - mx)
    den = jnp.sum(p, axis=2, keepdims=True) + p_new
    o = (jnp.sum(vc * p, axis=2, keepdims=True) + vnew * p_new) / den
    o_ref[...] = jnp.broadcast_to(o, o_ref.shape)
    last = _iota(kc.shape, 2) == buf - 1
    ko_ref[...] = jnp.where(last, knew, pltpu.roll(kc, buf - 1, 2))
    vo_ref[...] = jnp.where(last, vnew, pltpu.roll(vc, buf - 1, 2))


def _attn_decode(layer, cache_kt, cache_vt, q_col, k_col, v_col, prev_out):
    depth, nb, nh, hd, buf = cache_kt.shape
    cache_spec = pl.BlockSpec((None, None, nh, hd, buf), lambda b: (layer, b, 0, 0, 0))
    col_spec = pl.BlockSpec((None, nh, hd, LANES), lambda b: (b, 0, 0, 0))
    in_specs = [cache_spec, cache_spec, col_spec, col_spec, col_spec]
    args = [cache_kt, cache_vt, q_col, k_col, v_col]
    aliases = {}
    if prev_out is not None:
        in_specs += [pl.BlockSpec(memory_space=pl.ANY), pl.BlockSpec(memory_space=pl.ANY)]
        args += list(prev_out)
        aliases = {5: 0, 6: 1}
    cache_shape = jax.ShapeDtypeStruct(cache_kt.shape, f32)
    return pl.pallas_call(
        functools.partial(_attn_decode_kernel, buf=buf, aliased=prev_out is not None),
        grid=(nb,), in_specs=in_specs,
        out_specs=[cache_spec, cache_spec, col_spec],
        out_shape=[cache_shape, cache_shape, jax.ShapeDtypeStruct((nb, nh, hd, LANES), f32)],
        input_output_aliases=aliases,
        compiler_params=_params(1), name=f"attn_decode_l{layer}")(*args)


def _solve_unit_lower_pair(a, u, w, first_head8):
    ng = CHUNK // SUBLANES
    ag = [a[SUBLANES * g:SUBLANES * (g + 1), :] for g in range(ng)]
    ug = [u[SUBLANES * g:SUBLANES * (g + 1), :] for g in range(ng)]
    wg = [w[SUBLANES * g:SUBLANES * (g + 1), :] for g in range(ng)]
    for j in range(CHUNK - 1):
        g0, r = divmod(j, SUBLANES)
        urow = jnp.broadcast_to(ug[g0][r:r + 1, :], (SUBLANES, LANES))
        wrow = jnp.broadcast_to(wg[g0][r:r + 1, :], (SUBLANES, LANES))
        for g in range(g0, ng):
            c0 = jnp.broadcast_to(ag[g][:, j:j + 1], (SUBLANES, LANES))
            c1 = jnp.broadcast_to(ag[g][:, HEAD_DIM + j:HEAD_DIM + j + 1], (SUBLANES, LANES))
            col = jnp.where(first_head8, c0, c1)
            ug[g] = ug[g] - col * urow
            wg[g] = wg[g] - col * wrow
    return jnp.concatenate(ug, axis=0), jnp.concatenate(wg, axis=0)


def _load_state_pairs(sbd, s0_ref):
    sbd[...] = jnp.zeros_like(sbd)
    for h in range(N_HEADS):
        p, hh = divmod(h, 2)
        sbd[p, hh * HEAD_DIM:(hh + 1) * HEAD_DIM, hh * HEAD_DIM:(hh + 1) * HEAD_DIM] = s0_ref[h]


def _store_state_pairs(sn_ref, sbd):
    for h in range(N_HEADS):
        p, hh = divmod(h, 2)
        sn_ref[h] = sbd[p, hh * HEAD_DIM:(hh + 1) * HEAD_DIM, hh * HEAD_DIM:(hh + 1) * HEAD_DIM]


def _delta_kernel(zqkv_ref, zz_ref, zab_ref, conv0_ref, s0_ref, cw_ref, alog_ref, dtb_ref, on_ref,
                  o_ref, convn_ref, sn_ref, xp, sbd, *, ts, n_valid, n_tiles):
    t = pl.program_id(1)
    lead = SUBLANES
    nv = min(ts, n_valid)

    @pl.when(t == 0)
    def _():
        xp[0:lead, :] = conv0_ref[...]
        _load_state_pairs(sbd, s0_ref)

    xp[lead:lead + ts, :] = zqkv_ref[...]
    cw = cw_ref[...]
    y = cw[0:1, :] * xp[lead - 3:lead - 3 + ts, :]
    for j in range(1, DELTA_CONV):
        y = y + cw[j:j + 1, :] * xp[lead - 3 + j:lead - 3 + j + ts, :]
    convn_ref[...] = xp[lead + nv - (DELTA_CONV - 1):lead + nv, :]
    xp[0:lead, :] = xp[ts:ts + lead, :]
    y = _silu(y)

    sum_mat = _same_head_matrix(GROUP_WIDTH, 1.0)
    yq, yk, v = y[:, :GROUP_WIDTH], y[:, GROUP_WIDTH:2 * GROUP_WIDTH], y[:, 2 * GROUP_WIDTH:]
    q = yq * lax.rsqrt(_head_sum(yq * yq, sum_mat) + EPS) * (HEAD_DIM ** -0.5)
    k = yk * lax.rsqrt(_head_sum(yk * yk, sum_mat) + EPS)

    zab = zab_ref[...]
    src = _iota((LANES, GROUP_WIDTH), 0)
    head = _iota((LANES, GROUP_WIDTH), 1) // HEAD_DIM
    a_bc = jnp.dot(zab, (src == head).astype(f32), precision=HI, preferred_element_type=f32)
    b_bc = jnp.dot(zab, (src == head + N_HEADS).astype(f32), precision=HI, preferred_element_type=f32)
    xa = a_bc + dtb_ref[...]
    g = -jnp.exp(alog_ref[...]) * (jnp.maximum(xa, 0.0) + jnp.log1p(jnp.exp(-jnp.abs(xa))))
    beta = jax.nn.sigmoid(b_bc)
    if nv < ts:
        live = _iota((ts, 1), 0) < nv
        g, beta = jnp.where(live, g, 0.0), jnp.where(live, beta, 0.0)
        q, k, v = jnp.where(live, q, 0.0), jnp.where(live, k, 0.0), jnp.where(live, v, 0.0)

    ri, cj = _iota((ts, ts), 0), _iota((ts, ts), 1)
    same = (ri // CHUNK) == (cj // CHUNK)
    gcum = jnp.dot((same & (cj <= ri)).astype(f32), g, precision=HI, preferred_element_type=f32)
    glast = jnp.dot(same.astype(f32), g, precision=HI, preferred_element_type=f32)
    eg = jnp.exp(gcum)
    qg, kb, vb = q * eg, k * beta, v * beta
    kbe = kb * eg
    kd = k * jnp.exp(glast - gcum)
    dl = jnp.exp(glast)

    ii = _iota((CHUNK, LANES), 0)
    jj = _iota((CHUNK, LANES), 1) % HEAD_DIM
    eye2 = (ii == jj).astype(f32)
    ones = jnp.ones((CHUNK, CHUNK), f32)
    first_head = _iota((1, LANES), 1) < HEAD_DIM
    first_head8 = _iota((SUBLANES, LANES), 1) < HEAD_DIM
    same_head = (_iota((LANES, LANES), 0) // HEAD_DIM) == (_iota((LANES, LANES), 1) // HEAD_DIM)

    out_rows = []
    for c in range(ts // CHUNK):
        r = slice(c * CHUNK, (c + 1) * CHUNK)
        out_pairs = []
        for p in range(N_PAIRS):
            l = slice(p * LANES, (p + 1) * LANES)
            gc = gcum[r, l]
            grow = jnp.dot(ones, eye2 * gc, precision=HI, preferred_element_type=f32)
            dm = jnp.where(ii >= jj, jnp.exp(jnp.minimum(gc - grow, 0.0)), 0.0)
            kst = _stack_heads(k[r, l], first_head).astype(bf16)
            kk = _dot_nt(kb[r, l].astype(bf16), kst)
            qk = _dot_nt(q[r, l].astype(bf16), kst) * dm
            a = jnp.where(ii > jj, kk * dm, 0.0)
            u, w = _solve_unit_lower_pair(a, vb[r, l], kbe[r, l], first_head8)
            s_prev = sbd[p]
            s_bf = s_prev.astype(bf16)
            vnew = u - jnp.dot(w.astype(bf16), s_bf, preferred_element_type=f32)
            vst = _stack_heads(vnew, first_head).astype(bf16)
            o = (jnp.dot(qg[r, l].astype(bf16), s_bf, preferred_element_type=f32)
                 + jnp.dot(qk.astype(bf16), vst, preferred_element_type=f32))
            upd = _dot_tn(kd[r, l].astype(bf16), vnew.astype(bf16))
            sbd[p] = s_prev * dl[c * CHUNK:c * CHUNK + 1, l] + jnp.where(same_head, upd, 0.0)
            out_pairs.append(o)
        out_rows.append(jnp.concatenate(out_pairs, axis=1))
    o = jnp.concatenate(out_rows, axis=0)
    mean_mat = _same_head_matrix(GROUP_WIDTH, 1.0 / HEAD_DIM)
    o = o * lax.rsqrt(_head_sum(o * o, mean_mat) + EPS) * on_ref[...]
    o_ref[...] = o * _silu(zz_ref[...])

    @pl.when(t == n_tiles - 1)
    def _():
        _store_state_pairs(sn_ref, sbd)


def _delta(z, n, seq, n_valid, conv0, s0, conv_w, alog_bc, dtb_bc, onorm_bc, ts):
    assert n_valid == seq or seq == ts
    n_tiles = seq // ts
    w3 = 3 * GROUP_WIDTH
    row = lambda width, blk: pl.BlockSpec((ts, width), lambda b, t, blk=blk: (b * n_tiles + t, blk))
    small = lambda a: pl.BlockSpec(a.shape, lambda b, t: (0, 0))
    state = pl.BlockSpec((None, N_HEADS, HEAD_DIM, HEAD_DIM), lambda b, t: (b, 0, 0, 0))
    return pl.pallas_call(
        functools.partial(_delta_kernel, ts=ts, n_valid=n_valid, n_tiles=n_tiles),
        grid=(n, n_tiles),
        in_specs=[row(w3, ZB_QKV_BLK), row(GROUP_WIDTH, ZB_Z_BLK), row(LANES, ZAB_BLK),
                  pl.BlockSpec((None, SUBLANES, w3), lambda b, t: (b, 0, 0)), state,
                  small(conv_w), small(alog_bc), small(dtb_bc), small(onorm_bc)],
        out_specs=[pl.BlockSpec((ts, GROUP_WIDTH), lambda b, t: (b * n_tiles + t, 0)),
                   pl.BlockSpec((None, DELTA_CONV - 1, w3), lambda b, t: (b, 0, 0)), state],
        out_shape=[jax.ShapeDtypeStruct((n * seq, GROUP_WIDTH), f32),
                   jax.ShapeDtypeStruct((n, DELTA_CONV - 1, w3), f32),
                   jax.ShapeDtypeStruct((n, N_HEADS, HEAD_DIM, HEAD_DIM), f32)],
        scratch_shapes=[pltpu.VMEM((ts + SUBLANES, w3), f32), pltpu.VMEM((N_PAIRS, LANES, LANES), f32)],
        compiler_params=_params(2), name="delta")(z, z, z, conv0, s0, conv_w, alog_bc, dtb_bc, onorm_bc)


def _retention_kernel(zq_ref, zk_ref, zv_ref, zg_ref, dx_ref, dc_ref, db_ref, cos_ref, sin_ref, lg_ref, on_ref,
                      cwd_ref, s0_ref, dconv0_ref, o_ref, od_ref, sn_ref, dconvn_ref, sbd, dmask, xpd,
                      *, tc, n_valid, n_tiles):
    b, t = pl.program_id(0), pl.program_id(1)
    lead = SUBLANES
    nv = min(tc, n_valid)
    lg = lg_ref[...]

    @pl.when((b == 0) & (t == 0))
    def _():
        col = _iota((tc, 2 * tc), 1)
        diff = (_iota((tc, 2 * tc), 0) - col % tc).astype(f32)
        for p in range(N_PAIRS):
            lgp = lg[:, p * LANES:(p + 1) * LANES]
            rate = jnp.where(col < tc, lgp[:, 0:1], lgp[:, HEAD_DIM:HEAD_DIM + 1])
            dmask[p] = jnp.where(diff >= 0.0, jnp.exp(jnp.maximum(diff, 0.0) * rate), 0.0)

    @pl.when(t == 0)
    def _():
        _load_state_pairs(sbd, s0_ref)
        xpd[0:lead, :] = dconv0_ref[...]

    cos, sin = cos_ref[...], sin_ref[...]
    q = _rope_wide(zq_ref[...], cos, sin)
    k = _rope_wide(zk_ref[...], cos, sin) * (HEAD_DIM ** -0.5)
    v = zv_ref[...]
    if nv < tc:
        live = _iota((tc, 1), 0) < nv
        k, v = jnp.where(live, k, 0.0), jnp.where(live, v, 0.0)
    pos = _iota((tc, 1), 0).astype(f32)
    qd = q * jnp.exp((pos + 1.0) * lg)
    kdk = k * jnp.exp(jnp.maximum(float(nv - 1) - pos, 0.0) * lg)
    tile_decay = jnp.exp(float(nv) * lg)

    first_head = _iota((1, LANES), 1) < HEAD_DIM
    same_head = (_iota((LANES, LANES), 0) // HEAD_DIM) == (_iota((LANES, LANES), 1) // HEAD_DIM)
    outs = []
    for p in range(N_PAIRS):
        l = slice(p * LANES, (p + 1) * LANES)
        kst = _stack_heads(k[:, l], first_head).astype(bf16)
        vst = _stack_heads(v[:, l], first_head).astype(bf16)
        sc = _dot_nt(q[:, l].astype(bf16), kst) * dmask[p]
        s_prev = sbd[p]
        o = (jnp.dot(sc.astype(bf16), vst, preferred_element_type=f32)
             + jnp.dot(qd[:, l].astype(bf16), s_prev.astype(bf16), preferred_element_type=f32))
        upd = _dot_tn(kdk[:, l].astype(bf16), v[:, l].astype(bf16))
        sbd[p] = s_prev * tile_decay[:, l] + jnp.where(same_head, upd, 0.0)
        outs.append(o)
    o = jnp.concatenate(outs, axis=1)
    mean_mat = _same_head_matrix(GROUP_WIDTH, 1.0 / HEAD_DIM)
    o = o * lax.rsqrt(_head_sum(o * o, mean_mat) + EPS) * on_ref[...]
    o_ref[...] = o * _silu(zg_ref[...])

    xpd[lead:lead + tc, :] = dc_ref[...] * dx_ref[...]
    cwd = cwd_ref[...]
    yd = cwd[0:1, :] * xpd[lead - 2:lead - 2 + tc, :]
    for j in range(1, SHORT_CONV):
        yd = yd + cwd[j:j + 1, :] * xpd[lead - 2 + j:lead - 2 + j + tc, :]
    od_ref[...] = db_ref[...] * yd
    dconvn_ref[...] = xpd[lead + nv - (SHORT_CONV - 1):lead + nv, :]
    xpd[0:lead, :] = xpd[tc:tc + lead, :]

    @pl.when(t == n_tiles - 1)
    def _():
        _store_state_pairs(sn_ref, sbd)


def _retention(z, n, seq, n_valid, s0, dconv0, cos_t, sin_t, lg_bc, onorm_bc, conv_d, tc):
    assert n_valid == seq or seq == tc
    n_tiles = seq // tc
    row = lambda blk: pl.BlockSpec((tc, GROUP_WIDTH), lambda b, t, blk=blk: (b * n_tiles + t, blk))
    small = lambda a: pl.BlockSpec(a.shape, lambda b, t: (0, 0))
    tab = pl.BlockSpec((tc, LANES), lambda b, t: (t, 0))
    state = pl.BlockSpec((None, N_HEADS, HEAD_DIM, HEAD_DIM), lambda b, t: (b, 0, 0, 0))
    out_row = pl.BlockSpec((tc, GROUP_WIDTH), lambda b, t: (b * n_tiles + t, 0))
    out_rows = jax.ShapeDtypeStruct((n * seq, GROUP_WIDTH), f32)
    return pl.pallas_call(
        functools.partial(_retention_kernel, tc=tc, n_valid=n_valid, n_tiles=n_tiles),
        grid=(n, n_tiles),
        in_specs=[row(ZC_Q_BLK), row(ZC_K_BLK), row(ZC_V_BLK), row(ZC_G_BLK),
                  row(ZD_X_BLK), row(ZD_C_BLK), row(ZD_B_BLK), tab, tab,
                  small(lg_bc), small(onorm_bc), small(conv_d), state,
                  pl.BlockSpec((None, SUBLANES, GROUP_WIDTH), lambda b, t: (b, 0, 0))],
        out_specs=[out_row, out_row, state,
                   pl.BlockSpec((None, SHORT_CONV - 1, GROUP_WIDTH), lambda b, t: (b, 0, 0))],
        out_shape=[out_rows, out_rows, jax.ShapeDtypeStruct((n, N_HEADS, HEAD_DIM, HEAD_DIM), f32),
                   jax.ShapeDtypeStruct((n, SHORT_CONV - 1, GROUP_WIDTH), f32)],
        scratch_shapes=[pltpu.VMEM((N_PAIRS, LANES, LANES), f32), pltpu.VMEM((N_PAIRS, tc, 2 * tc), f32),
                        pltpu.VMEM((tc + SUBLANES, GROUP_WIDTH), f32)],
        compiler_params=_params(2), name="retention")(
            z, z, z, z, z, z, z, cos_t, sin_t, lg_bc, onorm_bc, conv_d, s0, dconv0)


def _rope_tables(pos, inv_freq):
    ang = pos.astype(f32)[:, None] * inv_freq[None, :]
    cos, sin = jnp.cos(ang), jnp.sin(ang)
    reps = LANES // HEAD_DIM
    return (jnp.tile(jnp.concatenate([cos, cos], axis=1), (1, reps)),
            jnp.tile(jnp.concatenate([-sin, sin], axis=1), (1, reps)))


def _per_head_lanes(v):
    return jnp.repeat(v.astype(f32), HEAD_DIM)[None, :]


def _tiled_lanes(v, width):
    return jnp.tile(v.astype(f32), width // HEAD_DIM)[None, :]


def _pack_w_in(w):
    pad = jnp.zeros((D_MODEL, LANES - 2 * N_HEADS), w.dtype)
    ab = w[:, AB_OFFSET_SRC:AB_OFFSET_SRC + 2 * N_HEADS]
    return jnp.concatenate([w[:, :AB_OFFSET_SRC], w[:, AB_OFFSET_SRC + 2 * N_HEADS:], ab, pad], axis=1).astype(bf16)


def _lead_pad(state):
    return jnp.pad(state, ((0, 0), (SUBLANES - state.shape[1], 0), (0, 0)))


def kernel(x_prompt, x_sample, cache_a_k, cache_a_v, state_b_conv, state_b_rec, state_c_rec, state_d_conv,
           norm_mix, w_in, q_norm_a, k_norm_a, conv_b, a_log_b, dt_bias_b, onorm_b, onorm_c, conv_d,
           w_out, norm_ffn, w_up, w_down):
    nb, seq, _ = x_prompt.shape
    nd, dseq, _ = x_sample.shape
    assert dseq == 1
    dpad = CHUNK
    rope_freq = ROPE_THETA ** (-jnp.arange(0, HEAD_DIM, 2, dtype=f32) / HEAD_DIM)
    ret_freq = 1.0 / (10000.0 ** jnp.linspace(0.0, 1.0, HEAD_DIM // 2, dtype=f32))
    ret_lg = _per_head_lanes(jnp.log(1.0 - 2.0 ** (-5.0 - jnp.arange(N_HEADS, dtype=f32))))
    pos_p = jnp.arange(seq, dtype=jnp.int32)
    pos_d = PAST_LEN + jnp.arange(dpad, dtype=jnp.int32)
    cos_ap, sin_ap = _rope_tables(pos_p, rope_freq)
    cos_cp, sin_cp = _rope_tables(pos_p, ret_freq)
    cos_ad, sin_ad = _rope_tables(pos_d[:1], rope_freq)
    cos_cd, sin_cd = _rope_tables(pos_d, ret_freq)

    cache_kt = jnp.transpose(cache_a_k, (0, 1, 3, 4, 2))
    cache_vt = jnp.transpose(cache_a_v, (0, 1, 3, 4, 2))

    yp = x_prompt.reshape(nb * seq, D_MODEL)
    ys = x_sample.reshape(nd, D_MODEL)
    zeros_conv_b = jnp.zeros((nb, SUBLANES, 3 * GROUP_WIDTH), f32)
    zeros_conv_d = jnp.zeros((nb, SUBLANES, GROUP_WIDTH), f32)
    zeros_rec = jnp.zeros((nb, N_HEADS, HEAD_DIM, HEAD_DIM), f32)
    new_p, new_s = [], []
    dec_cache = None
    for l in range(DEPTH):
        w_pack = _pack_w_in(w_in[l])
        wo, wu, wd = w_out[l].astype(bf16), w_up[l].astype(bf16), w_down[l].astype(bf16)
        g_mix, g_ffn = norm_mix[l][None, :], norm_ffn[l][None, :]
        qg, kg = _tiled_lanes(q_norm_a[l], LANES), _tiled_lanes(k_norm_a[l], LANES)
        qg3, kg3 = _tiled_lanes(q_norm_a[l], GROUP_WIDTH), _tiled_lanes(k_norm_a[l], GROUP_WIDTH)
        alog, dtb = _per_head_lanes(a_log_b[l]), _per_head_lanes(dt_bias_b[l])
        on_b, on_c = _tiled_lanes(onorm_b[l], GROUP_WIDTH), _tiled_lanes(onorm_c[l], GROUP_WIDTH)

        z = _inproj(yp, g_mix, w_pack, 512)
        oa, kt, vt = _attn_prompt(z, nb, seq, cos_ap, sin_ap, qg, kg)
        ob, p_bconv, p_brec = _delta(z, nb, seq, seq, zeros_conv_b, zeros_rec, conv_b[l], alog, dtb, on_b, 256)
        oc, od, p_crec, p_dconv = _retention(z, nb, seq, seq, zeros_rec, zeros_conv_d, cos_cp, sin_cp, ret_lg,
                                             on_c, conv_d[l], 256)
        yp = _ffn(yp, oa, ob, oc, od, wo, g_ffn, wu, wd, 512)
        new_p.append((kt, vt, p_bconv, p_brec, p_crec, p_dconv))

        zs = _inproj(ys, g_mix, w_pack, nd)
        qa, ka, va = _attn_decode_prep(zs, cos_ad, sin_ad, qg3, kg3)
        col = lambda a: jnp.broadcast_to(a.reshape(nd, N_HEADS, HEAD_DIM, 1), (nd, N_HEADS, HEAD_DIM, LANES))
        s_kt, s_vt, oa_col = _attn_decode(l, cache_kt, cache_vt, col(qa), col(ka), col(va), dec_cache)
        dec_cache = (s_kt, s_vt)
        oa_s = oa_col[..., 0].reshape(nd, GROUP_WIDTH)
        zpad = jnp.zeros((nd, dpad, Z_WIDTH), f32).at[:, 0, :].set(zs).reshape(nd * dpad, Z_WIDTH)
        ob_s, s_bconv, s_brec = _delta(zpad, nd, dpad, 1, _lead_pad(state_b_conv[l]), state_b_rec[l], conv_b[l],
                                       alog, dtb, on_b, dpad)
        oc_s, od_s, s_crec, s_dconv = _retention(zpad, nd, dpad, 1, state_c_rec[l], _lead_pad(state_d_conv[l]),
                                                 cos_cd, sin_cd, ret_lg, on_c, conv_d[l], dpad)
        first = lambda a: a.reshape(nd, dpad, GROUP_WIDTH)[:, 0, :]
        ys = _ffn(ys, oa_s, first(ob_s), first(oc_s), first(od_s), wo, g_ffn, wu, wd, nd)
        new_s.append((s_bconv, s_brec, s_crec, s_dconv))

    keep = min(WIN_MAX, seq)
    cache_out = lambda i: jnp.transpose(
        jnp.stack([s[i] for s in new_p]).reshape(DEPTH, nb, N_HEADS, HEAD_DIM, keep), (0, 1, 4, 2, 3))
    stack_p = lambda i: jnp.stack([s[i] for s in new_p])
    stack_s = lambda i: jnp.stack([s[i] for s in new_s])
    return (yp.reshape(nb, seq, D_MODEL), ys.reshape(nd, 1, D_MODEL),
            cache_out(0), cache_out(1), stack_p(2), stack_p(3), stack_p(4), stack_p(5),
            jnp.transpose(dec_cache[0], (0, 1, 4, 2, 3)), jnp.transpose(dec_cache[1], (0, 1, 4, 2, 3)),
            stack_s(0), stack_s(1), stack_s(2), stack_s(3))
```

```python
import functools

import jax
import jax.numpy as jnp
from jax import lax
from jax.experimental import pallas as pl
from jax.experimental.pallas import tpu as pltpu

f32, bf16, i32 = jnp.float32, jnp.bfloat16, jnp.int32

D_MODEL = 1024
HEAD_DIM = 64
N_HEADS = 6
N_PAIRS = N_HEADS // 2
GROUP_WIDTH = N_HEADS * HEAD_DIM
D_FF = 4 * D_MODEL
DIL_PAIRS = ((128, 1), (512, 4), (2048, 16))
WIN_MAX = 2048
ATT_BLOCK = 128
ROPE_THETA = 10000.0
DELTA_CONV = 4
SHORT_CONV = 3
CHUNK = 64
EPS = 1e-6
NEG = -1e30
PAST_LEN = 16384
DEPTH = 2
LANES = 128
SUBLANES = 8
VMEM_LIMIT = 56 * 1024 * 1024

IN_WIDTH = 14 * GROUP_WIDTH + 2 * N_HEADS
AB_OFFSET_SRC = 7 * GROUP_WIDTH
Z_WIDTH = 14 * GROUP_WIDTH + LANES
ZB_QKV_BLK = 1
ZB_Z_BLK = 6
ZC_Q_BLK, ZC_K_BLK, ZC_V_BLK, ZC_G_BLK = 7, 8, 9, 10
ZD_X_BLK, ZD_C_BLK, ZD_B_BLK = 11, 12, 13
ZAB_BLK = 14 * GROUP_WIDTH // LANES


def _iota(shape, dim):
    return lax.broadcasted_iota(i32, shape, dim)


def _params(n_axes):
    return pltpu.CompilerParams(dimension_semantics=("arbitrary",) * n_axes, vmem_limit_bytes=VMEM_LIMIT)


def _same_head_matrix(width, value):
    r = _iota((width, width), 0) // HEAD_DIM
    c = _iota((width, width), 1) // HEAD_DIM
    return jnp.where(r == c, value, 0.0).astype(f32)


def _head_sum(x, mat):
    return _dot_split_lhs(x, mat.astype(bf16), 2)


def _rope_slab(x, cos, sin_signed):
    p = _iota((1, LANES), 1) % HEAD_DIM
    partner = jnp.where(p < HEAD_DIM // 2, pltpu.roll(x, LANES - HEAD_DIM // 2, 1), pltpu.roll(x, HEAD_DIM // 2, 1))
    return x * cos + partner * sin_signed


def _rope_wide(x, cos, sin_signed):
    return jnp.concatenate(
        [_rope_slab(x[:, LANES * i:LANES * (i + 1)], cos, sin_signed) for i in range(x.shape[1] // LANES)], axis=1)


def _stack_heads(x, first_head):
    return jnp.concatenate([jnp.where(first_head, x, 0.0), jnp.where(first_head, 0.0, x)], axis=0)


def _dot_nt(a, b):
    return lax.dot_general(a, b, (((1,), (1,)), ((), ())), preferred_element_type=f32)


def _dot_tn(a, b):
    return lax.dot_general(a, b, (((0,), (0,)), ((), ())), preferred_element_type=f32)


def _silu(x):
    return x * jax.nn.sigmoid(x)


def _inproj_kernel(x_ref, g_ref, w_ref, o_ref):
    x = x_ref[...]
    ms = jnp.mean(x * x, axis=-1, keepdims=True)
    xn = (x * lax.rsqrt(ms + EPS) * g_ref[...]).astype(bf16)
    step = 4 * LANES
    for c0 in range(0, Z_WIDTH, step):
        cw = min(step, Z_WIDTH - c0)
        o_ref[:, c0:c0 + cw] = jnp.dot(xn, w_ref[:, c0:c0 + cw], preferred_element_type=f32)


def _pack_w_in_kernel(w_ref, o_ref):
    for l in range(DEPTH):
        o_ref[l] = w_ref[:, l, :].T.astype(bf16)


def _pack_w_in(w_in):
    wt = jnp.transpose(w_in, (2, 0, 1))
    n_ab = 2 * N_HEADS
    wcat = jnp.concatenate([wt[:AB_OFFSET_SRC], wt[AB_OFFSET_SRC + n_ab:], wt[AB_OFFSET_SRC:AB_OFFSET_SRC + n_ab],
                            jnp.zeros((LANES - n_ab, DEPTH, D_MODEL), f32)], axis=0)
    return pl.pallas_call(
        _pack_w_in_kernel, grid=(Z_WIDTH // LANES,),
        in_specs=[pl.BlockSpec((LANES, DEPTH, D_MODEL), lambda i: (i, 0, 0))],
        out_specs=pl.BlockSpec((DEPTH, D_MODEL, LANES), lambda i: (0, 0, i)),
        out_shape=jax.ShapeDtypeStruct((DEPTH, D_MODEL, Z_WIDTH), bf16),
        compiler_params=_params(1), name="pack_w_in")(wcat)


def _layer_block(a, layer):
    return pl.BlockSpec((None,) + a.shape[1:], lambda *_: (layer, 0, 0), pipeline_mode=pl.Buffered(1))


def _inproj(x2d, gain, w_pack, layer, tm):
    m = x2d.shape[0]
    return pl.pallas_call(
        _inproj_kernel,
        grid=(m // tm,),
        in_specs=[pl.BlockSpec((tm, D_MODEL), lambda i: (i, 0)),
                  pl.BlockSpec((1, D_MODEL), lambda i: (0, 0)),
                  _layer_block(w_pack, layer)],
        out_specs=pl.BlockSpec((tm, Z_WIDTH), lambda i: (i, 0)),
        out_shape=jax.ShapeDtypeStruct((m, Z_WIDTH), f32),
        compiler_params=_params(1), name="inproj")(x2d, gain, w_pack)


def _ffn_kernel(x_ref, oa_ref, ob_ref, oc_ref, od_ref, wo_ref, g_ref, wu_ref, wd_ref, y_ref):
    mix = jnp.concatenate([r[...].astype(bf16) for r in (oa_ref, ob_ref, oc_ref, od_ref)], axis=1)
    h = x_ref[...] + jnp.dot(mix, wo_ref[...], preferred_element_type=f32)
    ms = jnp.mean(h * h, axis=-1, keepdims=True)
    hn = (h * lax.rsqrt(ms + EPS) * g_ref[...]).astype(bf16)
    y_ref[...] = h
    step = D_FF // 4
    for c in range(0, D_FF, step):
        u = jnp.dot(hn, wu_ref[:, c:c + step], preferred_element_type=f32)
        u = jnp.square(jnp.maximum(u, 0.0)).astype(bf16)
        y_ref[...] += jnp.dot(u, wd_ref[c:c + step, :], preferred_element_type=f32)


def _ffn(x2d, oa, ob, oc, od, w_out, gain, w_up, w_down, layer, tm):
    m = x2d.shape[0]
    row = lambda width: pl.BlockSpec((tm, width), lambda i: (i, 0))
    whole = lambda a: _layer_block(a, layer)
    return pl.pallas_call(
        _ffn_kernel,
        grid=(m // tm,),
        in_specs=[row(D_MODEL), row(GROUP_WIDTH), row(GROUP_WIDTH), row(GROUP_WIDTH), row(GROUP_WIDTH),
                  whole(w_out), pl.BlockSpec((1, D_MODEL), lambda i: (0, 0)), whole(w_up), whole(w_down)],
        out_specs=row(D_MODEL),
        out_shape=jax.ShapeDtypeStruct((m, D_MODEL), f32),
        compiler_params=_params(1), name="ffn")(x2d, oa, ob, oc, od, w_out, gain, w_up, w_down)


def _attn_prompt_kernel(q_ref, k_ref, v_ref, cos_ref, sin_ref, qg_ref, kg_ref,
                        o_ref, kt_ref, vt_ref, qh_s, ks, vh_s, acc_s, m_s, bias_first, bias_rest, *, seq, keep):
    mean_mat = _same_head_matrix(LANES, 1.0 / HEAD_DIM)
    first_head = _iota((1, LANES), 1) < HEAD_DIM
    rc = 512
    blk = ATT_BLOCK

    def prep(i, carry):
        rows = pl.ds(pl.multiple_of(i * rc, rc), rc)
        cos, sin = cos_ref[rows, :], sin_ref[rows, :]
        q, k, v = q_ref[rows, :], k_ref[rows, :], v_ref[rows, :]
        q = q * lax.rsqrt(_head_sum(q * q, mean_mat) + EPS) * qg_ref[...]
        k = k * lax.rsqrt(_head_sum(k * k, mean_mat) + EPS) * kg_ref[...]
        q = _rope_slab(q, cos, sin) * (HEAD_DIM ** -0.5)
        qh_s[0, rows, :] = jnp.where(first_head, q, 0.0)
        qh_s[1, rows, :] = jnp.where(first_head, 0.0, q)
        ks[rows, :] = _rope_slab(k, cos, sin)
        vh_s[0, rows, :] = jnp.where(first_head, v, 1.0)
        vh_s[1, rows, :] = jnp.where(first_head, 1.0, v)
        return carry

    lax.fori_loop(0, seq // rc, prep, 0)

    for c in range(keep // rc):
        r0 = seq - keep + c * rc
        kt_ref[:, c * rc:(c + 1) * rc] = ks[r0:r0 + rc, :].T
        vt_ref[:, c * rc:(c + 1) * rc] = v_ref[r0:r0 + rc, :].T

    qi, kj = _iota((blk, blk), 0), _iota((blk, blk), 1)
    bias_first[...] = jnp.where(kj <= qi, 0.0, NEG)
    qi, kj = _iota((blk, 2 * blk), 0), _iota((blk, 2 * blk), 1)
    bias_rest[...] = jnp.where((kj >= qi) & (kj <= qi + blk), 0.0, NEG)

    def tile(start, dil, first, merge):
        def rows_of(st, n):
            return pl.ds(st, n, stride=dil) if dil > 1 else pl.ds(st, n)

        nk = blk if first else 2 * blk
        kstart = start if first else start - blk * dil
        rows, krows = rows_of(start, blk), rows_of(kstart, nk)
        k = ks[krows, :].astype(bf16)
        bias = bias_first[...] if first else bias_rest[...]
        for hh in range(2):
            s = _dot_nt(qh_s[hh, rows, :].astype(bf16), k) + bias
            mt = jnp.max(s, axis=1, keepdims=True)
            if not merge:
                p = jnp.exp(s - mt)
                acc_s[hh, rows, :] = jnp.dot(p.astype(bf16), vh_s[hh, krows, :].astype(bf16),
                                             preferred_element_type=f32)
                m_s[hh, rows, :] = jnp.broadcast_to(mt, (blk, LANES))
            else:
                mo = m_s[hh, rows, :]
                mn = jnp.maximum(mo, mt)
                p = jnp.exp(s - mn[:, :1])
                pv = jnp.dot(p.astype(bf16), vh_s[hh, krows, :].astype(bf16), preferred_element_type=f32)
                acc_s[hh, rows, :] = acc_s[hh, rows, :] * jnp.exp(mo - mn) + pv
                m_s[hh, rows, :] = mn

    for bi, (window, dil) in enumerate(DIL_PAIRS):
        nb = seq // dil // blk
        merge = bi > 0

        def per_residue(r, carry, dil=dil, nb=nb, merge=merge):
            tile(r, dil, True, merge)

            def per_block(b, c):
                tile(r + b * (blk * dil), dil, False, merge)
                return c

            lax.fori_loop(1, nb, per_block, 0, unroll=min(4, nb - 1) if nb > 1 else 1)
            return carry

        lax.fori_loop(0, dil, per_residue, 0, unroll=2 if nb <= 2 else 1)

    def finish(i, carry):
        rows = pl.ds(pl.multiple_of(i * rc, rc), rc)
        a0, a1 = acc_s[0, rows, :], acc_s[1, rows, :]
        half = LANES // 2
        o_ref[rows, :] = jnp.where(first_head, a0 / pltpu.roll(a0, half, 1), a1 / pltpu.roll(a1, half, 1))
        return carry

    lax.fori_loop(0, seq // rc, finish, 0)


def _attn_prompt(z, n, seq, cos_t, sin_t, qgain, kgain):
    keep = min(WIN_MAX, seq)
    for window, dil in DIL_PAIRS:
        assert window // dil == ATT_BLOCK and seq % (dil * ATT_BLOCK) == 0
    per_group = GROUP_WIDTH // LANES
    zspec = lambda base: pl.BlockSpec((seq, LANES), lambda b, p, base=base: (b, base * per_group + p))
    whole = lambda rows: pl.BlockSpec((rows, LANES), lambda b, p: (0, 0), pipeline_mode=pl.Buffered(1))
    slab = lambda: pltpu.VMEM((seq, LANES), f32)
    slab2 = lambda: pltpu.VMEM((2, seq, LANES), f32)
    return pl.pallas_call(
        functools.partial(_attn_prompt_kernel, seq=seq, keep=keep),
        grid=(n, N_PAIRS),
        in_specs=[zspec(0), zspec(1), zspec(2), whole(seq), whole(seq), whole(1), whole(1)],
        out_specs=[pl.BlockSpec((seq, LANES), lambda b, p: (b, p)),
                   pl.BlockSpec((None, LANES, keep), lambda b, p: (b, p, 0)),
                   pl.BlockSpec((None, LANES, keep), lambda b, p: (b, p, 0))],
        out_shape=[jax.ShapeDtypeStruct((n * seq, GROUP_WIDTH), f32),
                   jax.ShapeDtypeStruct((n, GROUP_WIDTH, keep), f32),
                   jax.ShapeDtypeStruct((n, GROUP_WIDTH, keep), f32)],
        scratch_shapes=[slab2(), slab(), slab2(), slab2(), slab2(),
                        pltpu.VMEM((ATT_BLOCK, ATT_BLOCK), f32), pltpu.VMEM((ATT_BLOCK, 2 * ATT_BLOCK), f32)],
        compiler_params=_params(2), name="attn_prompt")(z, z, z, cos_t, sin_t, qgain, kgain)


def _attn_decode_prep_kernel(z_ref, cos_ref, sin_ref, qg_ref, kg_ref, q_ref, k_ref, v_ref):
    mean_mat = _same_head_matrix(GROUP_WIDTH, 1.0 / HEAD_DIM)
    z = z_ref[...]
    q, k = z[:, :GROUP_WIDTH], z[:, GROUP_WIDTH:2 * GROUP_WIDTH]
    q = q * lax.rsqrt(_head_sum(q * q, mean_mat) + EPS) * qg_ref[...]
    k = k * lax.rsqrt(_head_sum(k * k, mean_mat) + EPS) * kg_ref[...]
    q_ref[...] = _rope_wide(q, cos_ref[...], sin_ref[...]) * (HEAD_DIM ** -0.5)
    k_ref[...] = _rope_wide(k, cos_ref[...], sin_ref[...])
    v_ref[...] = z[:, 2 * GROUP_WIDTH:]


def _attn_decode_prep(z, cos_row, sin_row, qgain, kgain):
    m = z.shape[0]
    small = lambda a: pl.BlockSpec(a.shape, lambda i: (0, 0))
    out = jax.ShapeDtypeStruct((m, GROUP_WIDTH), f32)
    ospec = pl.BlockSpec((m, GROUP_WIDTH), lambda i: (0, 0))
    return pl.pallas_call(
        _attn_decode_prep_kernel, grid=(1,),
        in_specs=[pl.BlockSpec((m, 3 * GROUP_WIDTH), lambda i: (0, 0)), small(cos_row), small(sin_row),
                  small(qgain), small(kgain)],
        out_specs=[ospec, ospec, ospec], out_shape=[out, out, out],
        compiler_params=_params(1), name="attn_decode_prep")(z, cos_row, sin_row, qgain, kgain)


def _attn_decode_kernel(*refs, buf, aliased):
    if aliased:
        kc_ref, vc_ref, q_ref, kn_ref, vn_ref, _, _, ko_ref, vo_ref, o_ref = refs
    else:
        kc_ref, vc_ref, q_ref, kn_ref, vn_ref, ko_ref, vo_ref, o_ref = refs
    kc = kc_ref[...]
    vc = vc_ref[...]
    qc = q_ref[...][:, :, :1]
    knew = kn_ref[...][:, :, :1]
    vnew = vn_ref[...][:, :, :1]
    dist = buf - _iota((1, 1, buf), 2)
    mult = jnp.zeros((1, 1, buf), f32)
    for window, dil in DIL_PAIRS:
        mult = mult + jnp.where((dist <= window) & (dist % dil == 0), 1.0, 0.0)
    s = jnp.sum(kc * qc, axis=1, keepdims=True)
    s_new = jnp.sum(knew * qc, axis=1, keepdims=True)
    s = jnp.where(mult > 0.0, s, NEG)
    mx = jnp.maximum(jnp.max(s, axis=2, keepdims=True), s_new)
    p = mult * jnp.exp(s - mx)
    p_new = float(len(DIL_PAIRS)) * jnp.exp(s_new - mx)
    den = jnp.sum(p, axis=2, keepdims=True) + p_new
    o = (jnp.sum(vc * p, axis=2, keepdims=True) + vnew * p_new) / den
    o_ref[...] = jnp.broadcast_to(o, o_ref.shape)
    last = _iota(kc.shape, 2) == buf - 1
    ko_ref[...] = jnp.where(last, knew, pltpu.roll(kc, buf - 1, 2))
    vo_ref[...] = jnp.where(last, vnew, pltpu.roll(vc, buf - 1, 2))


def _attn_decode(layer, cache_kt, cache_vt, q_col, k_col, v_col, prev_out):
    depth, nb, nh, hd, buf = cache_kt.shape
    cache_spec = pl.BlockSpec((None, None, nh, hd, buf), lambda b: (layer, b, 0, 0, 0))
    col_spec = pl.BlockSpec((None, nh, hd, LANES), lambda b: (b, 0, 0, 0))
    in_specs = [cache_spec, cache_spec, col_spec, col_spec, col_spec]
    args = [cache_kt, cache_vt, q_col, k_col, v_col]
    aliases = {}
    if prev_out is not None:
        in_specs += [pl.BlockSpec(memory_space=pl.ANY), pl.BlockSpec(memory_space=pl.ANY)]
        args += list(prev_out)
        aliases = {5: 0, 6: 1}
    cache_shape = jax.ShapeDtypeStruct(cache_kt.shape, f32)
    return pl.pallas_call(
        functools.partial(_attn_decode_kernel, buf=buf, aliased=prev_out is not None),
        grid=(nb,), in_specs=in_specs,
        out_specs=[cache_spec, cache_spec, col_spec],
        out_shape=[cache_shape, cache_shape, jax.ShapeDtypeStruct((nb, nh, hd, LANES), f32)],
        input_output_aliases=aliases,
        compiler_params=_params(1), name=f"attn_decode_l{layer}")(*args)


SOLVE_BLOCK = 16


def _solve_unit_lower_pairs(systems, first_head):
    ng = CHUNK // SUBLANES
    gpb = SOLVE_BLOCK // SUBLANES
    col_in_head = _iota((SOLVE_BLOCK, LANES), 1) % HEAD_DIM
    split = lambda x: [x[SUBLANES * g:SUBLANES * (g + 1), :] for g in range(ng)]
    ags = [split(a) for a, _, _ in systems]
    ugs = [split(u) for _, u, _ in systems]
    wgs = [split(w) for _, _, w in systems]
    for blk in range(CHUNK // SOLVE_BLOCK):
        r0 = blk * SOLVE_BLOCK
        if blk > 0:
            for (a, _, _), ug, wg in zip(systems, ugs, wgs):
                left = jnp.where(col_in_head < r0, a[r0:r0 + SOLVE_BLOCK, :], 0.0).astype(bf16)
                solved = jnp.concatenate([_stack_heads(jnp.concatenate(ug, axis=0), first_head),
                                          _stack_heads(jnp.concatenate(wg, axis=0), first_head)], axis=1).astype(bf16)
                upd = jnp.dot(left, solved, preferred_element_type=f32)
                for g in range(gpb):
                    rows = slice(SUBLANES * g, SUBLANES * (g + 1))
                    ug[blk * gpb + g] = ug[blk * gpb + g] - upd[rows, :LANES]
                    wg[blk * gpb + g] = wg[blk * gpb + g] - upd[rows, LANES:]
        for j in range(r0, r0 + SOLVE_BLOCK - 1):
            g0, r = divmod(j, SUBLANES)
            for ag, ug, wg in zip(ags, ugs, wgs):
                urow = jnp.broadcast_to(ug[g0][r:r + 1, :], (SUBLANES, LANES))
                wrow = jnp.broadcast_to(wg[g0][r:r + 1, :], (SUBLANES, LANES))
                for g in range(g0, (blk + 1) * gpb):
                    c0 = jnp.broadcast_to(ag[g][:, j:j + 1], (SUBLANES, LANES))
                    c1 = jnp.broadcast_to(ag[g][:, HEAD_DIM + j:HEAD_DIM + j + 1], (SUBLANES, LANES))
                    col = jnp.where(first_head, c0, c1)
                    ug[g] = ug[g] - col * urow
                    wg[g] = wg[g] - col * wrow
    return [(jnp.concatenate(ug, axis=0), jnp.concatenate(wg, axis=0)) for ug, wg in zip(ugs, wgs)]


def _dot_split_lhs(x, mat, terms):
    out, rest = None, x
    for t in range(terms):
        piece = rest.astype(bf16)
        d = jnp.dot(piece, mat, preferred_element_type=f32)
        out = d if out is None else out + d
        if t + 1 < terms:
            rest = rest - piece.astype(f32)
    return out


def _dot_split_rhs(mat, x, terms):
    out, rest = None, x
    for t in range(terms):
        piece = rest.astype(bf16)
        d = jnp.dot(mat, piece, preferred_element_type=f32)
        out = d if out is None else out + d
        if t + 1 < terms:
            rest = rest - piece.astype(f32)
    return out


def _load_state_pairs(sbd, s0_ref):
    sbd[...] = jnp.zeros_like(sbd)
    for h in range(N_HEADS):
        p, hh = divmod(h, 2)
        sbd[p, hh * HEAD_DIM:(hh + 1) * HEAD_DIM, hh * HEAD_DIM:(hh + 1) * HEAD_DIM] = s0_ref[h]


def _store_state_pairs(sn_ref, sbd):
    for h in range(N_HEADS):
        p, hh = divmod(h, 2)
        sn_ref[h] = sbd[p, hh * HEAD_DIM:(hh + 1) * HEAD_DIM, hh * HEAD_DIM:(hh + 1) * HEAD_DIM]


def _delta_chunks(q, k, v, g, beta, sbd, first_head, same_head, ts):
    ri, cj = _iota((ts, ts), 0), _iota((ts, ts), 1)
    same = (ri // CHUNK) == (cj // CHUNK)
    gcum = _dot_split_rhs((same & (cj <= ri)).astype(bf16), g, 3)
    glast = _dot_split_rhs(same.astype(bf16), g, 3)
    eg = jnp.exp(gcum)
    qg, kb, vb = q * eg, k * beta, v * beta
    kbe = kb * eg
    kd = k * jnp.exp(glast - gcum)
    dl = jnp.exp(glast)

    ii = _iota((CHUNK, LANES), 0)
    jj = _iota((CHUNK, LANES), 1) % HEAD_DIM
    eye2 = (ii == jj).astype(f32)
    ones = jnp.ones((CHUNK, CHUNK), bf16)
    n_chunks = ts // CHUNK
    where = [(slice(c * CHUNK, (c + 1) * CHUNK), slice(p * LANES, (p + 1) * LANES))
             for c in range(n_chunks) for p in range(N_PAIRS)]

    systems, qks = [], []
    for r, l in where:
        gc = gcum[r, l]
        grow = _dot_split_rhs(ones, eye2 * gc, 3)
        dm = jnp.where(ii >= jj, jnp.exp(jnp.minimum(gc - grow, 0.0)), 0.0)
        kst = _stack_heads(k[r, l], first_head).astype(bf16)
        kk = _dot_nt(kb[r, l].astype(bf16), kst)
        qks.append((_dot_nt(q[r, l].astype(bf16), kst) * dm).astype(bf16))
        systems.append((jnp.where(ii > jj, kk * dm, 0.0), vb[r, l], kbe[r, l]))

    group = 2 * N_PAIRS
    solved = []
    for i in range(0, len(systems), group):
        solved += _solve_unit_lower_pairs(systems[i:i + group], first_head)

    out_rows = []
    for c in range(n_chunks):
        out_pairs = []
        for p in range(N_PAIRS):
            idx = c * N_PAIRS + p
            r, l = where[idx]
            u, w = solved[idx]
            s_prev = sbd[p]
            s_bf = s_prev.astype(bf16)
            vnew = u - jnp.dot(w.astype(bf16), s_bf, preferred_element_type=f32)
            vst = _stack_heads(vnew, first_head).astype(bf16)
            o = (jnp.dot(qg[r, l].astype(bf16), s_bf, preferred_element_type=f32)
                 + jnp.dot(qks[idx], vst, preferred_element_type=f32))
            upd = _dot_tn(kd[r, l].astype(bf16), vnew.astype(bf16))
            sbd[p] = s_prev * dl[c * CHUNK:c * CHUNK + 1, l] + jnp.where(same_head, upd, 0.0)
            out_pairs.append(o)
        out_rows.append(jnp.concatenate(out_pairs, axis=1))
    return jnp.concatenate(out_rows, axis=0)


def _delta_kernel(zqkv_ref, zz_ref, zab_ref, conv0_ref, s0_ref, cw_ref, alog_ref, dtb_ref, on_ref,
                  o_ref, convn_ref, sn_ref, xp, sbd, *, ts, n_valid, n_tiles):
    t = pl.program_id(1)
    lead = SUBLANES
    nv = min(ts, n_valid)

    @pl.when(t == 0)
    def _():
        xp[0:lead, :] = conv0_ref[...]
        _load_state_pairs(sbd, s0_ref)

    xp[lead:lead + ts, :] = zqkv_ref[...]
    cw = cw_ref[...]
    y = cw[0:1, :] * xp[lead - 3:lead - 3 + ts, :]
    for j in range(1, DELTA_CONV):
        y = y + cw[j:j + 1, :] * xp[lead - 3 + j:lead - 3 + j + ts, :]
    convn_ref[...] = xp[lead + nv - (DELTA_CONV - 1):lead + nv, :]
    xp[0:lead, :] = xp[ts:ts + lead, :]
    y = _silu(y)

    sum_mat = _same_head_matrix(GROUP_WIDTH, 1.0)
    yq, yk, v = y[:, :GROUP_WIDTH], y[:, GROUP_WIDTH:2 * GROUP_WIDTH], y[:, 2 * GROUP_WIDTH:]
    q = yq * lax.rsqrt(_head_sum(yq * yq, sum_mat) + EPS) * (HEAD_DIM ** -0.5)
    k = yk * lax.rsqrt(_head_sum(yk * yk, sum_mat) + EPS)

    zab = zab_ref[...]
    src = _iota((LANES, GROUP_WIDTH), 0)
    head = _iota((LANES, GROUP_WIDTH), 1) // HEAD_DIM
    a_bc = _dot_split_lhs(zab, (src == head).astype(bf16), 3)
    b_bc = _dot_split_lhs(zab, (src == head + N_HEADS).astype(bf16), 3)
    xa = a_bc + dtb_ref[...]
    g = -jnp.exp(alog_ref[...]) * (jnp.maximum(xa, 0.0) + jnp.log1p(jnp.exp(-jnp.abs(xa))))
    beta = jax.nn.sigmoid(b_bc)
    if nv < ts:
        live = _iota((ts, 1), 0) < nv
        g, beta = jnp.where(live, g, 0.0), jnp.where(live, beta, 0.0)
        q, k, v = jnp.where(live, q, 0.0), jnp.where(live, k, 0.0), jnp.where(live, v, 0.0)

    first_head = _iota((1, LANES), 1) < HEAD_DIM
    same_head = (_iota((LANES, LANES), 0) // HEAD_DIM) == (_iota((LANES, LANES), 1) // HEAD_DIM)
    if n_valid == 1:
        eg = jnp.exp(g)
        qg, kbe, vb = q * eg, k * beta * eg, v * beta
        qk = _head_sum(q * k, sum_mat)
        out_pairs = []
        for p in range(N_PAIRS):
            l = slice(p * LANES, (p + 1) * LANES)
            s_prev = sbd[p]
            s_bf = s_prev.astype(bf16)
            vnew = vb[:, l] - jnp.dot(kbe[:, l].astype(bf16), s_bf, preferred_element_type=f32)
            out_pairs.append(jnp.dot(qg[:, l].astype(bf16), s_bf, preferred_element_type=f32) + qk[:, l] * vnew)
            upd = _dot_tn(k[:, l].astype(bf16), vnew.astype(bf16))
            sbd[p] = s_prev * eg[0:1, l] + jnp.where(same_head, upd, 0.0)
        o = jnp.concatenate(out_pairs, axis=1)
    else:
        o = _delta_chunks(q, k, v, g, beta, sbd, first_head, same_head, ts)
    mean_mat = _same_head_matrix(GROUP_WIDTH, 1.0 / HEAD_DIM)
    o = o * lax.rsqrt(_head_sum(o * o, mean_mat) + EPS) * on_ref[...]
    o_ref[...] = o * _silu(zz_ref[...])

    @pl.when(t == n_tiles - 1)
    def _():
        _store_state_pairs(sn_ref, sbd)


def _delta(z, n, seq, n_valid, conv0, s0, conv_w, alog_bc, dtb_bc, onorm_bc, ts):
    assert n_valid == seq or seq == ts
    n_tiles = seq // ts
    w3 = 3 * GROUP_WIDTH
    row = lambda width, blk: pl.BlockSpec((ts, width), lambda b, t, blk=blk: (b * n_tiles + t, blk))
    small = lambda a: pl.BlockSpec(a.shape, lambda b, t: (0, 0))
    state = pl.BlockSpec((None, N_HEADS, HEAD_DIM, HEAD_DIM), lambda b, t: (b, 0, 0, 0))
    return pl.pallas_call(
        functools.partial(_delta_kernel, ts=ts, n_valid=n_valid, n_tiles=n_tiles),
        grid=(n, n_tiles),
        in_specs=[row(w3, ZB_QKV_BLK), row(GROUP_WIDTH, ZB_Z_BLK), row(LANES, ZAB_BLK),
                  pl.BlockSpec((None, SUBLANES, w3), lambda b, t: (b, 0, 0)), state,
                  small(conv_w), small(alog_bc), small(dtb_bc), small(onorm_bc)],
        out_specs=[pl.BlockSpec((ts, GROUP_WIDTH), lambda b, t: (b * n_tiles + t, 0)),
                   pl.BlockSpec((None, DELTA_CONV - 1, w3), lambda b, t: (b, 0, 0)), state],
        out_shape=[jax.ShapeDtypeStruct((n * seq, GROUP_WIDTH), f32),
                   jax.ShapeDtypeStruct((n, DELTA_CONV - 1, w3), f32),
                   jax.ShapeDtypeStruct((n, N_HEADS, HEAD_DIM, HEAD_DIM), f32)],
        scratch_shapes=[pltpu.VMEM((ts + SUBLANES, w3), f32), pltpu.VMEM((N_PAIRS, LANES, LANES), f32)],
        compiler_params=_params(2), name="delta")(z, z, z, conv0, s0, conv_w, alog_bc, dtb_bc, onorm_bc)


def _retention_kernel(zq_ref, zk_ref, zv_ref, zg_ref, dx_ref, dc_ref, db_ref, cos_ref, sin_ref, lg_ref, on_ref,
                      cwd_ref, s0_ref, dconv0_ref, o_ref, od_ref, sn_ref, dconvn_ref, sbd, dmask, xpd,
                      *, tc, n_valid, n_tiles):
    b, t = pl.program_id(0), pl.program_id(1)
    lead = SUBLANES
    nv = min(tc, n_valid)
    lg = lg_ref[...]

    @pl.when((b == 0) & (t == 0))
    def _():
        col = _iota((tc, 2 * tc), 1)
        diff = (_iota((tc, 2 * tc), 0) - col % tc).astype(f32)
        for p in range(N_PAIRS):
            lgp = lg[:, p * LANES:(p + 1) * LANES]
            rate = jnp.where(col < tc, lgp[:, 0:1], lgp[:, HEAD_DIM:HEAD_DIM + 1])
            dmask[p] = jnp.where(diff >= 0.0, jnp.exp(jnp.maximum(diff, 0.0) * rate), 0.0)

    @pl.when(t == 0)
    def _():
        _load_state_pairs(sbd, s0_ref)
        xpd[0:lead, :] = dconv0_ref[...]

    cos, sin = cos_ref[...], sin_ref[...]
    q = _rope_wide(zq_ref[...], cos, sin)
    k = _rope_wide(zk_ref[...], cos, sin) * (HEAD_DIM ** -0.5)
    v = zv_ref[...]
    if nv < tc:
        live = _iota((tc, 1), 0) < nv
        k, v = jnp.where(live, k, 0.0), jnp.where(live, v, 0.0)
    pos = _iota((tc, 1), 0).astype(f32)
    qd = q * jnp.exp((pos + 1.0) * lg)
    kdk = k * jnp.exp(jnp.maximum(float(nv - 1) - pos, 0.0) * lg)
    tile_decay = jnp.exp(float(nv) * lg)

    first_head = _iota((1, LANES), 1) < HEAD_DIM
    same_head = (_iota((LANES, LANES), 0) // HEAD_DIM) == (_iota((LANES, LANES), 1) // HEAD_DIM)
    outs = []
    for p in range(N_PAIRS):
        l = slice(p * LANES, (p + 1) * LANES)
        kst = _stack_heads(k[:, l], first_head).astype(bf16)
        vst = _stack_heads(v[:, l], first_head).astype(bf16)
        sc = _dot_nt(q[:, l].astype(bf16), kst) * dmask[p]
        s_prev = sbd[p]
        o = (jnp.dot(sc.astype(bf16), vst, preferred_element_type=f32)
             + jnp.dot(qd[:, l].astype(bf16), s_prev.astype(bf16), preferred_element_type=f32))
        upd = _dot_tn(kdk[:, l].astype(bf16), v[:, l].astype(bf16))
        sbd[p] = s_prev * tile_decay[:, l] + jnp.where(same_head, upd, 0.0)
        outs.append(o)
    o = jnp.concatenate(outs, axis=1)
    mean_mat = _same_head_matrix(GROUP_WIDTH, 1.0 / HEAD_DIM)
    o = o * lax.rsqrt(_head_sum(o * o, mean_mat) + EPS) * on_ref[...]
    o_ref[...] = o * _silu(zg_ref[...])

    xpd[lead:lead + tc, :] = dc_ref[...] * dx_ref[...]
    cwd = cwd_ref[...]
    yd = cwd[0:1, :] * xpd[lead - 2:lead - 2 + tc, :]
    for j in range(1, SHORT_CONV):
        yd = yd + cwd[j:j + 1, :] * xpd[lead - 2 + j:lead - 2 + j + tc, :]
    od_ref[...] = db_ref[...] * yd
    dconvn_ref[...] = xpd[lead + nv - (SHORT_CONV - 1):lead + nv, :]
    xpd[0:lead, :] = xpd[tc:tc + lead, :]

    @pl.when(t == n_tiles - 1)
    def _():
        _store_state_pairs(sn_ref, sbd)


def _retention(z, n, seq, n_valid, s0, dconv0, cos_t, sin_t, lg_bc, onorm_bc, conv_d, tc):
    assert n_valid == seq or seq == tc
    n_tiles = seq // tc
    row = lambda blk: pl.BlockSpec((tc, GROUP_WIDTH), lambda b, t, blk=blk: (b * n_tiles + t, blk))
    small = lambda a: pl.BlockSpec(a.shape, lambda b, t: (0, 0))
    tab = pl.BlockSpec((tc, LANES), lambda b, t: (t, 0))
    state = pl.BlockSpec((None, N_HEADS, HEAD_DIM, HEAD_DIM), lambda b, t: (b, 0, 0, 0))
    out_row = pl.BlockSpec((tc, GROUP_WIDTH), lambda b, t: (b * n_tiles + t, 0))
    out_rows = jax.ShapeDtypeStruct((n * seq, GROUP_WIDTH), f32)
    return pl.pallas_call(
        functools.partial(_retention_kernel, tc=tc, n_valid=n_valid, n_tiles=n_tiles),
        grid=(n, n_tiles),
        in_specs=[row(ZC_Q_BLK), row(ZC_K_BLK), row(ZC_V_BLK), row(ZC_G_BLK),
                  row(ZD_X_BLK), row(ZD_C_BLK), row(ZD_B_BLK), tab, tab,
                  small(lg_bc), small(onorm_bc), small(conv_d), state,
                  pl.BlockSpec((None, SUBLANES, GROUP_WIDTH), lambda b, t: (b, 0, 0))],
        out_specs=[out_row, out_row, state,
                   pl.BlockSpec((None, SHORT_CONV - 1, GROUP_WIDTH), lambda b, t: (b, 0, 0))],
        out_shape=[out_rows, out_rows, jax.ShapeDtypeStruct((n, N_HEADS, HEAD_DIM, HEAD_DIM), f32),
                   jax.ShapeDtypeStruct((n, SHORT_CONV - 1, GROUP_WIDTH), f32)],
        scratch_shapes=[pltpu.VMEM((N_PAIRS, LANES, LANES), f32), pltpu.VMEM((N_PAIRS, tc, 2 * tc), f32),
                        pltpu.VMEM((tc + SUBLANES, GROUP_WIDTH), f32)],
        compiler_params=_params(2), name="retention")(
            z, z, z, z, z, z, z, cos_t, sin_t, lg_bc, onorm_bc, conv_d, s0, dconv0)


def _rope_tables(pos, inv_freq):
    ang = pos.astype(f32)[:, None] * inv_freq[None, :]
    cos, sin = jnp.cos(ang), jnp.sin(ang)
    reps = LANES // HEAD_DIM
    return (jnp.tile(jnp.concatenate([cos, cos], axis=1), (1, reps)),
            jnp.tile(jnp.concatenate([-sin, sin], axis=1), (1, reps)))


def _per_head_lanes(v):
    return jnp.repeat(v.astype(f32), HEAD_DIM)[None, :]


def _tiled_lanes(v, width):
    return jnp.tile(v.astype(f32), width // HEAD_DIM)[None, :]


def _lead_pad(state):
    return jnp.pad(state, ((0, 0), (SUBLANES - state.shape[1], 0), (0, 0)))


def kernel(x_prompt, x_sample, cache_a_k, cache_a_v, state_b_conv, state_b_rec, state_c_rec, state_d_conv,
           norm_mix, w_in, q_norm_a, k_norm_a, conv_b, a_log_b, dt_bias_b, onorm_b, onorm_c, conv_d,
           w_out, norm_ffn, w_up, w_down):
    nb, seq, _ = x_prompt.shape
    nd, dseq, _ = x_sample.shape
    assert dseq == 1
    dpad = SUBLANES
    rope_freq = ROPE_THETA ** (-jnp.arange(0, HEAD_DIM, 2, dtype=f32) / HEAD_DIM)
    ret_freq = 1.0 / (10000.0 ** jnp.linspace(0.0, 1.0, HEAD_DIM // 2, dtype=f32))
    ret_lg = _per_head_lanes(jnp.log(1.0 - 2.0 ** (-5.0 - jnp.arange(N_HEADS, dtype=f32))))
    pos_p = jnp.arange(seq, dtype=jnp.int32)
    pos_d = PAST_LEN + jnp.arange(dpad, dtype=jnp.int32)
    cos_ap, sin_ap = _rope_tables(pos_p, rope_freq)
    cos_cp, sin_cp = _rope_tables(pos_p, ret_freq)
    cos_ad, sin_ad = _rope_tables(pos_d[:1], rope_freq)
    cos_cd, sin_cd = _rope_tables(pos_d, ret_freq)

    cache_kt = jnp.transpose(cache_a_k, (0, 1, 3, 4, 2))
    cache_vt = jnp.transpose(cache_a_v, (0, 1, 3, 4, 2))

    yp = x_prompt.reshape(nb * seq, D_MODEL)
    ys = x_sample.reshape(nd, D_MODEL)
    zeros_conv_b = jnp.zeros((nb, SUBLANES, 3 * GROUP_WIDTH), f32)
    zeros_conv_d = jnp.zeros((nb, SUBLANES, GROUP_WIDTH), f32)
    zeros_rec = jnp.zeros((nb, N_HEADS, HEAD_DIM, HEAD_DIM), f32)
    new_p, new_s = [], []
    dec_cache = None
    w_pack = _pack_w_in(w_in)
    wo, wu, wd = w_out.astype(bf16), w_up.astype(bf16), w_down.astype(bf16)
    for l in range(DEPTH):
        g_mix, g_ffn = norm_mix[l][None, :], norm_ffn[l][None, :]
        qg, kg = _tiled_lanes(q_norm_a[l], LANES), _tiled_lanes(k_norm_a[l], LANES)
        qg3, kg3 = _tiled_lanes(q_norm_a[l], GROUP_WIDTH), _tiled_lanes(k_norm_a[l], GROUP_WIDTH)
        alog, dtb = _per_head_lanes(a_log_b[l]), _per_head_lanes(dt_bias_b[l])
        on_b, on_c = _tiled_lanes(onorm_b[l], GROUP_WIDTH), _tiled_lanes(onorm_c[l], GROUP_WIDTH)

        z = _inproj(yp, g_mix, w_pack, l, 512)
        oa, kt, vt = _attn_prompt(z, nb, seq, cos_ap, sin_ap, qg, kg)
        ob, p_bconv, p_brec = _delta(z, nb, seq, seq, zeros_conv_b, zeros_rec, conv_b[l], alog, dtb, on_b, 256)
        oc, od, p_crec, p_dconv = _retention(z, nb, seq, seq, zeros_rec, zeros_conv_d, cos_cp, sin_cp, ret_lg,
                                             on_c, conv_d[l], 256)
        yp = _ffn(yp, oa, ob, oc, od, wo, g_ffn, wu, wd, l, 512)
        new_p.append((kt, vt, p_bconv, p_brec, p_crec, p_dconv))

        zs = _inproj(ys, g_mix, w_pack, l, nd)
        qa, ka, va = _attn_decode_prep(zs, cos_ad, sin_ad, qg3, kg3)
        col = lambda a: jnp.broadcast_to(a.reshape(nd, N_HEADS, HEAD_DIM, 1), (nd, N_HEADS, HEAD_DIM, LANES))
        s_kt, s_vt, oa_col = _attn_decode(l, cache_kt, cache_vt, col(qa), col(ka), col(va), dec_cache)
        dec_cache = (s_kt, s_vt)
        oa_s = oa_col[..., 0].reshape(nd, GROUP_WIDTH)
        zpad = jnp.zeros((nd, dpad, Z_WIDTH), f32).at[:, 0, :].set(zs).reshape(nd * dpad, Z_WIDTH)
        ob_s, s_bconv, s_brec = _delta(zpad, nd, dpad, 1, _lead_pad(state_b_conv[l]), state_b_rec[l], conv_b[l],
                                       alog, dtb, on_b, dpad)
        oc_s, od_s, s_crec, s_dconv = _retention(zpad, nd, dpad, 1, state_c_rec[l], _lead_pad(state_d_conv[l]),
                                                 cos_cd, sin_cd, ret_lg, on_c, conv_d[l], dpad)
        first = lambda a: a.reshape(nd, dpad, GROUP_WIDTH)[:, 0, :]
        ys = _ffn(ys, oa_s, first(ob_s), first(oc_s), first(od_s), wo, g_ffn, wu, wd, l, nd)
        new_s.append((s_bconv, s_brec, s_crec, s_dconv))

    keep = min(WIN_MAX, seq)
    cache_out = lambda i: jnp.transpose(
        jnp.stack([s[i] for s in new_p]).reshape(DEPTH, nb, N_HEADS, HEAD_DIM, keep), (0, 1, 4, 2, 3))
    stack_p = lambda i: jnp.stack([s[i] for s in new_p])
    stack_s = lambda i: jnp.stack([s[i] for s in new_s])
    return (yp.reshape(nb, seq, D_MODEL), ys.reshape(nd, 1, D_MODEL),
            cache_out(0), cache_out(1), stack_p(2), stack_p(3), stack_p(4), stack_p(5),
            jnp.transpose(dec_cache[0], (0, 1, 4, 2, 3)), jnp.transpose(dec_cache[1], (0, 1, 4, 2, 3)),
            stack_s(0), stack_s(1), stack_s(2), stack_s(3))
```

```python
import functools

import jax
import jax.numpy as jnp
from jax import lax
from jax.experimental import pallas as pl
from jax.experimental.pallas import tpu as pltpu

f32, bf16, i32 = jnp.float32, jnp.bfloat16, jnp.int32

D_MODEL = 1024
HEAD_DIM = 64
N_HEADS = 6
N_PAIRS = N_HEADS // 2
GROUP_WIDTH = N_HEADS * HEAD_DIM
D_FF = 4 * D_MODEL
DIL_PAIRS = ((128, 1), (512, 4), (2048, 16))
WIN_MAX = 2048
ATT_BLOCK = 128
ROPE_THETA = 10000.0
DELTA_CONV = 4
SHORT_CONV = 3
CHUNK = 64
EPS = 1e-6
NEG = -1e30
PAST_LEN = 16384
DEPTH = 2
LANES = 128
SUBLANES = 8
VMEM_LIMIT = 56 * 1024 * 1024

IN_WIDTH = 14 * GROUP_WIDTH + 2 * N_HEADS
AB_OFFSET_SRC = 7 * GROUP_WIDTH
Z_WIDTH = 14 * GROUP_WIDTH + LANES
ZB_QKV_BLK = 1
ZB_Z_BLK = 6
ZC_Q_BLK, ZC_K_BLK, ZC_V_BLK, ZC_G_BLK = 7, 8, 9, 10
ZD_X_BLK, ZD_C_BLK, ZD_B_BLK = 11, 12, 13
ZAB_BLK = 14 * GROUP_WIDTH // LANES


def _iota(shape, dim):
    return lax.broadcasted_iota(i32, shape, dim)


def _params(n_axes):
    return pltpu.CompilerParams(dimension_semantics=("arbitrary",) * n_axes, vmem_limit_bytes=VMEM_LIMIT)


def _same_head_matrix(width, value):
    r = _iota((width, width), 0) // HEAD_DIM
    c = _iota((width, width), 1) // HEAD_DIM
    return jnp.where(r == c, value, 0.0).astype(f32)


def _head_sum(x, mat):
    return _dot_split_lhs(x, mat.astype(bf16), 2)


def _rope_slab(x, cos, sin_signed):
    p = _iota((1, LANES), 1) % HEAD_DIM
    partner = jnp.where(p < HEAD_DIM // 2, pltpu.roll(x, LANES - HEAD_DIM // 2, 1), pltpu.roll(x, HEAD_DIM // 2, 1))
    return x * cos + partner * sin_signed


def _rope_wide(x, cos, sin_signed):
    return jnp.concatenate(
        [_rope_slab(x[:, LANES * i:LANES * (i + 1)], cos, sin_signed) for i in range(x.shape[1] // LANES)], axis=1)


def _stack_heads(x, first_head):
    return jnp.concatenate([jnp.where(first_head, x, 0.0), jnp.where(first_head, 0.0, x)], axis=0)


def _dot_nt(a, b):
    return lax.dot_general(a, b, (((1,), (1,)), ((), ())), preferred_element_type=f32)


def _dot_tn(a, b):
    return lax.dot_general(a, b, (((0,), (0,)), ((), ())), preferred_element_type=f32)


def _silu(x):
    return x * jax.nn.sigmoid(x)


def _inproj_kernel(x_ref, g_ref, w_ref, o_ref):
    x = x_ref[...]
    ms = jnp.mean(x * x, axis=-1, keepdims=True)
    xn = (x * lax.rsqrt(ms + EPS) * g_ref[...]).astype(bf16)
    step = 4 * LANES
    for c0 in range(0, Z_WIDTH, step):
        cw = min(step, Z_WIDTH - c0)
        o_ref[:, c0:c0 + cw] = jnp.dot(xn, w_ref[:, c0:c0 + cw], preferred_element_type=f32)


def _pack_w_in_kernel(w_ref, o_ref):
    for l in range(DEPTH):
        o_ref[l] = w_ref[:, l, :].T.astype(bf16)


def _pack_w_in(w_in):
    wt = jnp.transpose(w_in, (2, 0, 1))
    n_ab = 2 * N_HEADS
    wcat = jnp.concatenate([wt[:AB_OFFSET_SRC], wt[AB_OFFSET_SRC + n_ab:], wt[AB_OFFSET_SRC:AB_OFFSET_SRC + n_ab],
                            jnp.zeros((LANES - n_ab, DEPTH, D_MODEL), f32)], axis=0)
    return pl.pallas_call(
        _pack_w_in_kernel, grid=(Z_WIDTH // LANES,),
        in_specs=[pl.BlockSpec((LANES, DEPTH, D_MODEL), lambda i: (i, 0, 0))],
        out_specs=pl.BlockSpec((DEPTH, D_MODEL, LANES), lambda i: (0, 0, i)),
        out_shape=jax.ShapeDtypeStruct((DEPTH, D_MODEL, Z_WIDTH), bf16),
        compiler_params=_params(1), name="pack_w_in")(wcat)


def _layer_block(a, layer):
    return pl.BlockSpec((None,) + a.shape[1:], lambda *_: (layer, 0, 0), pipeline_mode=pl.Buffered(1))


def _inproj(x2d, gain, w_pack, layer, tm):
    m = x2d.shape[0]
    return pl.pallas_call(
        _inproj_kernel,
        grid=(m // tm,),
        in_specs=[pl.BlockSpec((tm, D_MODEL), lambda i: (i, 0)),
                  pl.BlockSpec((1, D_MODEL), lambda i: (0, 0)),
                  _layer_block(w_pack, layer)],
        out_specs=pl.BlockSpec((tm, Z_WIDTH), lambda i: (i, 0)),
        out_shape=jax.ShapeDtypeStruct((m, Z_WIDTH), f32),
        compiler_params=_params(1), name="inproj")(x2d, gain, w_pack)


def _ffn_kernel(x_ref, oa_ref, ob_ref, oc_ref, od_ref, wo_ref, g_ref, wu_ref, wd_ref, y_ref):
    mix = jnp.concatenate([r[...].astype(bf16) for r in (oa_ref, ob_ref, oc_ref, od_ref)], axis=1)
    h = x_ref[...] + jnp.dot(mix, wo_ref[...], preferred_element_type=f32)
    ms = jnp.mean(h * h, axis=-1, keepdims=True)
    hn = (h * lax.rsqrt(ms + EPS) * g_ref[...]).astype(bf16)
    y_ref[...] = h
    step = D_FF // 4
    for c in range(0, D_FF, step):
        u = jnp.dot(hn, wu_ref[:, c:c + step], preferred_element_type=f32)
        u = jnp.square(jnp.maximum(u, 0.0)).astype(bf16)
        y_ref[...] += jnp.dot(u, wd_ref[c:c + step, :], preferred_element_type=f32)


def _ffn(x2d, oa, ob, oc, od, w_out, gain, w_up, w_down, layer, tm):
    m = x2d.shape[0]
    row = lambda width: pl.BlockSpec((tm, width), lambda i: (i, 0))
    whole = lambda a: _layer_block(a, layer)
    return pl.pallas_call(
        _ffn_kernel,
        grid=(m // tm,),
        in_specs=[row(D_MODEL), row(GROUP_WIDTH), row(GROUP_WIDTH), row(GROUP_WIDTH), row(GROUP_WIDTH),
                  whole(w_out), pl.BlockSpec((1, D_MODEL), lambda i: (0, 0)), whole(w_up), whole(w_down)],
        out_specs=row(D_MODEL),
        out_shape=jax.ShapeDtypeStruct((m, D_MODEL), f32),
        compiler_params=_params(1), name="ffn")(x2d, oa, ob, oc, od, w_out, gain, w_up, w_down)


def _attn_prompt_kernel(q_ref, k_ref, v_ref, cos_ref, sin_ref, qg_ref, kg_ref, *rest,
                        seq, keep, wide_dil, wide_pitch, aliased):
    if aliased:
        rest = rest[2:]
    (o_ref, kt_ref, vt_ref, qh_s, ks, vh_s, acc_s, m_s, t_acc, t_m, qw, kw, vw, bias_first, bias_rest) = rest
    mean_mat = _same_head_matrix(LANES, 1.0 / HEAD_DIM)
    first_head = _iota((1, LANES), 1) < HEAD_DIM
    rc = 512
    blk = ATT_BLOCK

    def prep(i, carry):
        rows = pl.ds(pl.multiple_of(i * rc, rc), rc)
        cos, sin = cos_ref[rows, :], sin_ref[rows, :]
        q, k, v = q_ref[rows, :], k_ref[rows, :], v_ref[rows, :]
        q = q * lax.rsqrt(_head_sum(q * q, mean_mat) + EPS) * qg_ref[...]
        k = k * lax.rsqrt(_head_sum(k * k, mean_mat) + EPS) * kg_ref[...]
        q = _rope_slab(q, cos, sin) * (HEAD_DIM ** -0.5)
        qh_s[0, rows, :] = jnp.where(first_head, q, 0.0)
        qh_s[1, rows, :] = jnp.where(first_head, 0.0, q)
        ks[rows, :] = _rope_slab(k, cos, sin)
        vh_s[0, rows, :] = jnp.where(first_head, v, 1.0)
        vh_s[1, rows, :] = jnp.where(first_head, 1.0, v)
        k = ks[rows, :]
        for g in range(rc // wide_dil):
            src = slice(g * wide_dil, (g + 1) * wide_dil)
            dst = pl.ds(pl.multiple_of(i * (rc // wide_dil * wide_pitch), SUBLANES) + g * wide_pitch, wide_dil)
            qw[dst, :], kw[dst, :], vw[dst, :] = q[src, :], k[src, :], v[src, :]
        return carry

    lax.fori_loop(0, seq // rc, prep, 0)

    for c in range(keep // rc):
        r0 = seq - keep + c * rc
        kt_ref[:, c * rc:(c + 1) * rc] = ks[r0:r0 + rc, :].T
        vt_ref[:, c * rc:(c + 1) * rc] = v_ref[r0:r0 + rc, :].T

    qi, kj = _iota((blk, blk), 0), _iota((blk, blk), 1)
    bias_first[...] = jnp.where(kj <= qi, 0.0, NEG)
    qi, kj = _iota((blk, 2 * blk), 0), _iota((blk, 2 * blk), 1)
    bias_rest[...] = jnp.where((kj >= qi) & (kj <= qi + blk), 0.0, NEG)

    def rows_of(st, n, dil):
        return pl.ds(st, n, stride=dil) if dil > 1 else pl.ds(st, n)

    def tile(r, b, dil, first, dst_acc, dst_m, dst_row):
        nk = blk if first else 2 * blk
        if dil == wide_dil:
            qrow = wide_pitch * (b * blk) + r
            krow = qrow if first else qrow - wide_pitch * blk
            rows, krows = pl.ds(qrow, blk, stride=wide_pitch), pl.ds(krow, nk, stride=wide_pitch)
            q, k, v = qw[rows, :], kw[krows, :].astype(bf16), vw[krows, :]
            qs = [jnp.where(first_head, q, 0.0).astype(bf16), jnp.where(first_head, 0.0, q).astype(bf16)]
            vs = [jnp.where(first_head, v, 1.0).astype(bf16), jnp.where(first_head, 1.0, v).astype(bf16)]
        else:
            start = r + b * (blk * dil)
            kstart = start if first else start - blk * dil
            rows, krows = rows_of(start, blk, dil), rows_of(kstart, nk, dil)
            k = ks[krows, :].astype(bf16)
            qs = [qh_s[hh, rows, :].astype(bf16) for hh in range(2)]
            vs = [vh_s[hh, krows, :].astype(bf16) for hh in range(2)]
        bias = bias_first[...] if first else bias_rest[...]
        dst = pl.ds(dst_row, blk)
        for hh in range(2):
            s = _dot_nt(qs[hh], k) + bias
            mt = jnp.max(s, axis=1, keepdims=True)
            p = jnp.exp(s - mt)
            dst_acc[hh, dst, :] = jnp.dot(p.astype(bf16), vs[hh], preferred_element_type=f32)
            dst_m[hh, dst, :] = jnp.broadcast_to(mt, (blk, LANES))

    merge_group = 4
    tiles_per_body = 8

    for bi, (window, dil) in enumerate(DIL_PAIRS):
        nb = seq // dil // blk
        dst_acc, dst_m = (acc_s, m_s) if bi == 0 else (t_acc, t_m)
        pitch = nb * blk + (SUBLANES if dil == wide_dil else 0)

        def per_residue(r, carry, dil=dil, nb=nb, dst_acc=dst_acc, dst_m=dst_m, pitch=pitch):
            tile(r, 0, dil, True, dst_acc, dst_m, pl.multiple_of(r * pitch, SUBLANES))

            def per_block(b, c):
                tile(r, b, dil, False, dst_acc, dst_m, pl.multiple_of(r * pitch + b * blk, SUBLANES))
                return c

            if nb <= tiles_per_body:
                for b in range(1, nb):
                    per_block(b, 0)
            else:
                lax.fori_loop(1, nb, per_block, 0, unroll=tiles_per_body // 2)
            return carry

        lax.fori_loop(0, dil, per_residue, 0, unroll=min(dil, max(1, tiles_per_body // nb)))

        if bi > 0 and dil != wide_dil:
            def merge(g, carry, dil=dil, nb=nb):
                pending = []
                for u in range(merge_group):
                    idx = g * merge_group + u
                    r, b = idx // nb, idx % nb
                    rows = rows_of(r + b * (blk * dil), blk, dil)
                    src = pl.ds(pl.multiple_of(idx * blk, blk), blk)
                    for hh in range(2):
                        mo, mt = m_s[hh, rows, :], t_m[hh, src, :]
                        mn = jnp.maximum(mo, mt)
                        an = acc_s[hh, rows, :] * jnp.exp(mo - mn) + t_acc[hh, src, :] * jnp.exp(mt - mn)
                        pending.append((hh, rows, an, mn))
                for hh, rows, an, mn in pending:
                    acc_s[hh, rows, :] = an
                    m_s[hh, rows, :] = mn
                return carry

            lax.fori_loop(0, seq // blk // merge_group, merge, 0)
        elif bi > 0:
            def merge_wide(g, carry, dil=dil, pitch=pitch):
                pending = []
                for u in range(merge_group):
                    t0 = pl.multiple_of((g * merge_group + u) * blk, blk)
                    rows = pl.ds(t0, blk)
                    for hh in range(2):
                        ta, tm = [], []
                        for v8 in range(blk // SUBLANES):
                            src = pl.ds(((v8 * SUBLANES) % dil) * pitch + t0 // dil + (v8 * SUBLANES) // dil,
                                        SUBLANES, stride=pitch)
                            ta.append(t_acc[hh, src, :])
                            tm.append(t_m[hh, src, :])
                        ta, mt = jnp.concatenate(ta, axis=0), jnp.concatenate(tm, axis=0)
                        mo = m_s[hh, rows, :]
                        mn = jnp.maximum(mo, mt)
                        an = acc_s[hh, rows, :] * jnp.exp(mo - mn) + ta * jnp.exp(mt - mn)
                        pending.append((hh, rows, an, mn))
                for hh, rows, an, mn in pending:
                    acc_s[hh, rows, :] = an
                    m_s[hh, rows, :] = mn
                return carry

            lax.fori_loop(0, seq // blk // merge_group, merge_wide, 0)

    def finish(i, carry):
        rows = pl.ds(pl.multiple_of(i * rc, rc), rc)
        a0, a1 = acc_s[0, rows, :], acc_s[1, rows, :]
        half = LANES // 2
        o_ref[rows, :] = jnp.where(first_head, a0 / pltpu.roll(a0, half, 1), a1 / pltpu.roll(a1, half, 1))
        return carry

    lax.fori_loop(0, seq // rc, finish, 0)


def _attn_prompt(layer, z, n, seq, cos_t, sin_t, qgain, kgain, prev_out):
    keep = min(WIN_MAX, seq)
    for window, dil in DIL_PAIRS:
        assert window // dil == ATT_BLOCK and seq % (dil * ATT_BLOCK) == 0
    wide = [dil for _, dil in DIL_PAIRS if dil % SUBLANES == 0]
    assert len(wide) == 1 and ATT_BLOCK % wide[0] == 0
    wide_dil, wide_pitch = wide[0], wide[0] + SUBLANES
    per_group = GROUP_WIDTH // LANES
    zspec = lambda base: pl.BlockSpec((seq, LANES), lambda b, p, base=base: (b, base * per_group + p),
                                      pipeline_mode=pl.Buffered(1))
    whole = lambda rows: pl.BlockSpec((rows, LANES), lambda b, p: (0, 0), pipeline_mode=pl.Buffered(1))
    slab = lambda: pltpu.VMEM((seq, LANES), f32)
    slab2 = lambda: pltpu.VMEM((2, seq, LANES), f32)
    branch = lambda: pltpu.VMEM((2, seq + wide_dil * SUBLANES, LANES), f32)
    padded = lambda: pltpu.VMEM((seq // wide_dil * wide_pitch, LANES), f32)
    in_specs = [zspec(0), zspec(1), zspec(2), whole(seq), whole(seq), whole(1), whole(1)]
    args = [z, z, z, cos_t, sin_t, qgain, kgain]
    aliases = {}
    if prev_out is not None:
        in_specs += [pl.BlockSpec(memory_space=pl.ANY), pl.BlockSpec(memory_space=pl.ANY)]
        args += list(prev_out)
        aliases = {len(args) - 2: 1, len(args) - 1: 2}
    cache_spec = pl.BlockSpec((None, None, LANES, keep), lambda b, p: (layer, b, p, 0))
    cache_shape = jax.ShapeDtypeStruct((DEPTH, n, GROUP_WIDTH, keep), f32)
    return pl.pallas_call(
        functools.partial(_attn_prompt_kernel, seq=seq, keep=keep, wide_dil=wide_dil, wide_pitch=wide_pitch,
                          aliased=prev_out is not None),
        grid=(n, N_PAIRS),
        in_specs=in_specs,
        out_specs=[pl.BlockSpec((seq, LANES), lambda b, p: (b, p)), cache_spec, cache_spec],
        out_shape=[jax.ShapeDtypeStruct((n * seq, GROUP_WIDTH), f32), cache_shape, cache_shape],
        input_output_aliases=aliases,
        scratch_shapes=[slab2(), slab(), slab2(), slab2(), slab2(), branch(), branch(), padded(), padded(), padded(),
                        pltpu.VMEM((ATT_BLOCK, ATT_BLOCK), f32), pltpu.VMEM((ATT_BLOCK, 2 * ATT_BLOCK), f32)],
        compiler_params=_params(2), name=f"attn_prompt_l{layer}")(*args)


def _attn_decode_prep_kernel(z_ref, cos_ref, sin_ref, qg_ref, kg_ref, q_ref, k_ref, v_ref):
    mean_mat = _same_head_matrix(GROUP_WIDTH, 1.0 / HEAD_DIM)
    z = z_ref[...]
    q, k = z[:, :GROUP_WIDTH], z[:, GROUP_WIDTH:2 * GROUP_WIDTH]
    q = q * lax.rsqrt(_head_sum(q * q, mean_mat) + EPS) * qg_ref[...]
    k = k * lax.rsqrt(_head_sum(k * k, mean_mat) + EPS) * kg_ref[...]
    q_ref[...] = _rope_wide(q, cos_ref[...], sin_ref[...]) * (HEAD_DIM ** -0.5)
    k_ref[...] = _rope_wide(k, cos_ref[...], sin_ref[...])
    v_ref[...] = z[:, 2 * GROUP_WIDTH:]


def _attn_decode_prep(z, cos_row, sin_row, qgain, kgain):
    m = z.shape[0]
    small = lambda a: pl.BlockSpec(a.shape, lambda i: (0, 0))
    out = jax.ShapeDtypeStruct((m, GROUP_WIDTH), f32)
    ospec = pl.BlockSpec((m, GROUP_WIDTH), lambda i: (0, 0))
    return pl.pallas_call(
        _attn_decode_prep_kernel, grid=(1,),
        in_specs=[pl.BlockSpec((m, 3 * GROUP_WIDTH), lambda i: (0, 0)), small(cos_row), small(sin_row),
                  small(qgain), small(kgain)],
        out_specs=[ospec, ospec, ospec], out_shape=[out, out, out],
        compiler_params=_params(1), name="attn_decode_prep")(z, cos_row, sin_row, qgain, kgain)


def _attn_decode_kernel(*refs, buf, aliased):
    if aliased:
        kc_ref, vc_ref, q_ref, kn_ref, vn_ref, _, _, ko_ref, vo_ref, o_ref = refs
    else:
        kc_ref, vc_ref, q_ref, kn_ref, vn_ref, ko_ref, vo_ref, o_ref = refs
    kc = kc_ref[...]
    vc = vc_ref[...]
    qc = q_ref[...][:, :, :1]
    knew = kn_ref[...][:, :, :1]
    vnew = vn_ref[...][:, :, :1]
    dist = buf - _iota((1, 1, buf), 2)
    mult = jnp.zeros((1, 1, buf), f32)
    for window, dil in DIL_PAIRS:
        mult = mult + jnp.where((dist <= window) & (dist % dil == 0), 1.0, 0.0)
    s = jnp.sum(kc * qc, axis=1, keepdims=True)
    s_new = jnp.sum(knew * qc, axis=1, keepdims=True)
    s = jnp.where(mult > 0.0, s, NEG)
    mx = jnp.maximum(jnp.max(s, axis=2, keepdims=True), s_new)
    p = mult * jnp.exp(s - mx)
    p_new = float(len(DIL_PAIRS)) * jnp.exp(s_new - mx)
    den = jnp.sum(p, axis=2, keepdims=True) + p_new
    o = (jnp.sum(vc * p, axis=2, keepdims=True) + vnew * p_new) / den
    o_ref[...] = jnp.broadcast_to(o, o_ref.shape)
    last = _iota(kc.shape, 2) == buf - 1
    ko_ref[...] = jnp.where(last, knew, pltpu.roll(kc, buf - 1, 2))
    vo_ref[...] = jnp.where(last, vnew, pltpu.roll(vc, buf - 1, 2))


def _attn_decode(layer, cache_kt, cache_vt, q_col, k_col, v_col, prev_out):
    depth, nb, nh, hd, buf = cache_kt.shape
    cache_spec = pl.BlockSpec((None, None, nh, hd, buf), lambda b: (layer, b, 0, 0, 0))
    col_spec = pl.BlockSpec((None, nh, hd, LANES), lambda b: (b, 0, 0, 0))
    in_specs = [cache_spec, cache_spec, col_spec, col_spec, col_spec]
    args = [cache_kt, cache_vt, q_col, k_col, v_col]
    aliases = {}
    if prev_out is not None:
        in_specs += [pl.BlockSpec(memory_space=pl.ANY), pl.BlockSpec(memory_space=pl.ANY)]
        args += list(prev_out)
        aliases = {5: 0, 6: 1}
    cache_shape = jax.ShapeDtypeStruct(cache_kt.shape, f32)
    return pl.pallas_call(
        functools.partial(_attn_decode_kernel, buf=buf, aliased=prev_out is not None),
        grid=(nb,), in_specs=in_specs,
        out_specs=[cache_spec, cache_spec, col_spec],
        out_shape=[cache_shape, cache_shape, jax.ShapeDtypeStruct((nb, nh, hd, LANES), f32)],
        input_output_aliases=aliases,
        compiler_params=_params(1), name=f"attn_decode_l{layer}")(*args)


SOLVE_BLOCK = 16


def _solve_unit_lower_pairs(systems, first_head):
    ng = CHUNK // SUBLANES
    gpb = SOLVE_BLOCK // SUBLANES
    col_in_head = _iota((SOLVE_BLOCK, LANES), 1) % HEAD_DIM
    split = lambda x: [x[SUBLANES * g:SUBLANES * (g + 1), :] for g in range(ng)]
    ags = [split(a) for a, _, _ in systems]
    ugs = [split(u) for _, u, _ in systems]
    wgs = [split(w) for _, _, w in systems]
    for blk in range(CHUNK // SOLVE_BLOCK):
        r0 = blk * SOLVE_BLOCK
        if blk > 0:
            for (a, _, _), ug, wg in zip(systems, ugs, wgs):
                left = jnp.where(col_in_head < r0, a[r0:r0 + SOLVE_BLOCK, :], 0.0).astype(bf16)
                solved = jnp.concatenate([_stack_heads(jnp.concatenate(ug, axis=0), first_head),
                                          _stack_heads(jnp.concatenate(wg, axis=0), first_head)], axis=1).astype(bf16)
                upd = jnp.dot(left, solved, preferred_element_type=f32)
                for g in range(gpb):
                    rows = slice(SUBLANES * g, SUBLANES * (g + 1))
                    ug[blk * gpb + g] = ug[blk * gpb + g] - upd[rows, :LANES]
                    wg[blk * gpb + g] = wg[blk * gpb + g] - upd[rows, LANES:]
        for j in range(r0, r0 + SOLVE_BLOCK - 1):
            g0, r = divmod(j, SUBLANES)
            for ag, ug, wg in zip(ags, ugs, wgs):
                urow = jnp.broadcast_to(ug[g0][r:r + 1, :], (SUBLANES, LANES))
                wrow = jnp.broadcast_to(wg[g0][r:r + 1, :], (SUBLANES, LANES))
                for g in range(g0, (blk + 1) * gpb):
                    c0 = jnp.broadcast_to(ag[g][:, j:j + 1], (SUBLANES, LANES))
                    c1 = jnp.broadcast_to(ag[g][:, HEAD_DIM + j:HEAD_DIM + j + 1], (SUBLANES, LANES))
                    col = jnp.where(first_head, c0, c1)
                    ug[g] = ug[g] - col * urow
                    wg[g] = wg[g] - col * wrow
    return [(jnp.concatenate(ug, axis=0), jnp.concatenate(wg, axis=0)) for ug, wg in zip(ugs, wgs)]


def _dot_split_lhs(x, mat, terms):
    out, rest = None, x
    for t in range(terms):
        piece = rest.astype(bf16)
        d = jnp.dot(piece, mat, preferred_element_type=f32)
        out = d if out is None else out + d
        if t + 1 < terms:
            rest = rest - piece.astype(f32)
    return out


def _dot_split_rhs(mat, x, terms):
    out, rest = None, x
    for t in range(terms):
        piece = rest.astype(bf16)
        d = jnp.dot(mat, piece, preferred_element_type=f32)
        out = d if out is None else out + d
        if t + 1 < terms:
            rest = rest - piece.astype(f32)
    return out


def _load_state_pairs(sbd, s0_ref):
    sbd[...] = jnp.zeros_like(sbd)
    for h in range(N_HEADS):
        p, hh = divmod(h, 2)
        sbd[p, hh * HEAD_DIM:(hh + 1) * HEAD_DIM, hh * HEAD_DIM:(hh + 1) * HEAD_DIM] = s0_ref[h]


def _store_state_pairs(sn_ref, sbd):
    for h in range(N_HEADS):
        p, hh = divmod(h, 2)
        sn_ref[h] = sbd[p, hh * HEAD_DIM:(hh + 1) * HEAD_DIM, hh * HEAD_DIM:(hh + 1) * HEAD_DIM]


def _delta_chunks(q, k, v, g, beta, sbd, first_head, same_head, ts):
    ri, cj = _iota((ts, ts), 0), _iota((ts, ts), 1)
    same = (ri // CHUNK) == (cj // CHUNK)
    gcum = _dot_split_rhs((same & (cj <= ri)).astype(bf16), g, 3)
    glast = _dot_split_rhs(same.astype(bf16), g, 3)
    eg = jnp.exp(gcum)
    qg, kb, vb = q * eg, k * beta, v * beta
    kbe = kb * eg
    kd = k * jnp.exp(glast - gcum)
    dl = jnp.exp(glast)

    ii = _iota((CHUNK, LANES), 0)
    jj = _iota((CHUNK, LANES), 1) % HEAD_DIM
    eye2 = (ii == jj).astype(f32)
    ones = jnp.ones((CHUNK, CHUNK), bf16)
    n_chunks = ts // CHUNK
    where = [(slice(c * CHUNK, (c + 1) * CHUNK), slice(p * LANES, (p + 1) * LANES))
             for c in range(n_chunks) for p in range(N_PAIRS)]

    systems, qks = [], []
    for r, l in where:
        gc = gcum[r, l]
        grow = _dot_split_rhs(ones, eye2 * gc, 3)
        dm = jnp.where(ii >= jj, jnp.exp(jnp.minimum(gc - grow, 0.0)), 0.0)
        kst = _stack_heads(k[r, l], first_head).astype(bf16)
        kk = _dot_nt(kb[r, l].astype(bf16), kst)
        qks.append((_dot_nt(q[r, l].astype(bf16), kst) * dm).astype(bf16))
        systems.append((jnp.where(ii > jj, kk * dm, 0.0), vb[r, l], kbe[r, l]))

    group = 2 * N_PAIRS
    solved = []
    for i in range(0, len(systems), group):
        solved += _solve_unit_lower_pairs(systems[i:i + group], first_head)

    out_rows = []
    for c in range(n_chunks):
        out_pairs = []
        for p in range(N_PAIRS):
            idx = c * N_PAIRS + p
            r, l = where[idx]
            u, w = solved[idx]
            s_prev = sbd[p]
            s_bf = s_prev.astype(bf16)
            vnew = u - jnp.dot(w.astype(bf16), s_bf, preferred_element_type=f32)
            vst = _stack_heads(vnew, first_head).astype(bf16)
            o = (jnp.dot(qg[r, l].astype(bf16), s_bf, preferred_element_type=f32)
                 + jnp.dot(qks[idx], vst, preferred_element_type=f32))
            upd = _dot_tn(kd[r, l].astype(bf16), vnew.astype(bf16))
            sbd[p] = s_prev * dl[c * CHUNK:c * CHUNK + 1, l] + jnp.where(same_head, upd, 0.0)
            out_pairs.append(o)
        out_rows.append(jnp.concatenate(out_pairs, axis=1))
    return jnp.concatenate(out_rows, axis=0)


def _delta_kernel(zqkv_ref, zz_ref, zab_ref, conv0_ref, s0_ref, cw_ref, alog_ref, dtb_ref, on_ref,
                  o_ref, convn_ref, sn_ref, xp, sbd, *, ts, n_valid, n_tiles, group):
    t = pl.program_id(1)
    for s in range(group):
        rows = pl.ds(s * ts, ts)
        _delta_sequence(t, zqkv_ref.at[rows], zz_ref.at[rows], zab_ref.at[rows], conv0_ref.at[s], s0_ref.at[s],
                        cw_ref, alog_ref, dtb_ref, on_ref, o_ref.at[rows], convn_ref.at[s], sn_ref.at[s],
                        xp.at[s], sbd.at[s], ts=ts, n_valid=n_valid, n_tiles=n_tiles)


def _delta_sequence(t, zqkv_ref, zz_ref, zab_ref, conv0_ref, s0_ref, cw_ref, alog_ref, dtb_ref, on_ref,
                    o_ref, convn_ref, sn_ref, xp, sbd, *, ts, n_valid, n_tiles):
    lead = SUBLANES
    nv = min(ts, n_valid)

    @pl.when(t == 0)
    def _():
        xp[0:lead, :] = conv0_ref[...]
        _load_state_pairs(sbd, s0_ref)

    xp[lead:lead + ts, :] = zqkv_ref[...]
    cw = cw_ref[...]
    y = cw[0:1, :] * xp[lead - 3:lead - 3 + ts, :]
    for j in range(1, DELTA_CONV):
        y = y + cw[j:j + 1, :] * xp[lead - 3 + j:lead - 3 + j + ts, :]
    convn_ref[...] = xp[lead + nv - (DELTA_CONV - 1):lead + nv, :]
    xp[0:lead, :] = xp[ts:ts + lead, :]
    y = _silu(y)

    sum_mat = _same_head_matrix(GROUP_WIDTH, 1.0)
    yq, yk, v = y[:, :GROUP_WIDTH], y[:, GROUP_WIDTH:2 * GROUP_WIDTH], y[:, 2 * GROUP_WIDTH:]
    q = yq * lax.rsqrt(_head_sum(yq * yq, sum_mat) + EPS) * (HEAD_DIM ** -0.5)
    k = yk * lax.rsqrt(_head_sum(yk * yk, sum_mat) + EPS)

    zab = zab_ref[...]
    src = _iota((LANES, GROUP_WIDTH), 0)
    head = _iota((LANES, GROUP_WIDTH), 1) // HEAD_DIM
    a_bc = _dot_split_lhs(zab, (src == head).astype(bf16), 3)
    b_bc = _dot_split_lhs(zab, (src == head + N_HEADS).astype(bf16), 3)
    xa = a_bc + dtb_ref[...]
    g = -jnp.exp(alog_ref[...]) * (jnp.maximum(xa, 0.0) + jnp.log1p(jnp.exp(-jnp.abs(xa))))
    beta = jax.nn.sigmoid(b_bc)
    if nv < ts:
        live = _iota((ts, 1), 0) < nv
        g, beta = jnp.where(live, g, 0.0), jnp.where(live, beta, 0.0)
        q, k, v = jnp.where(live, q, 0.0), jnp.where(live, k, 0.0), jnp.where(live, v, 0.0)

    first_head = _iota((1, LANES), 1) < HEAD_DIM
    same_head = (_iota((LANES, LANES), 0) // HEAD_DIM) == (_iota((LANES, LANES), 1) // HEAD_DIM)
    if n_valid == 1:
        eg = jnp.exp(g)
        qg, kbe, vb = q * eg, k * beta * eg, v * beta
        qk = _head_sum(q * k, sum_mat)
        out_pairs = []
        for p in range(N_PAIRS):
            l = slice(p * LANES, (p + 1) * LANES)
            s_prev = sbd[p]
            s_bf = s_prev.astype(bf16)
            vnew = vb[:, l] - jnp.dot(kbe[:, l].astype(bf16), s_bf, preferred_element_type=f32)
            out_pairs.append(jnp.dot(qg[:, l].astype(bf16), s_bf, preferred_element_type=f32) + qk[:, l] * vnew)
            upd = _dot_tn(k[:, l].astype(bf16), vnew.astype(bf16))
            sbd[p] = s_prev * eg[0:1, l] + jnp.where(same_head, upd, 0.0)
        o = jnp.concatenate(out_pairs, axis=1)
    else:
        o = _delta_chunks(q, k, v, g, beta, sbd, first_head, same_head, ts)
    mean_mat = _same_head_matrix(GROUP_WIDTH, 1.0 / HEAD_DIM)
    o = o * lax.rsqrt(_head_sum(o * o, mean_mat) + EPS) * on_ref[...]
    o_ref[...] = o * _silu(zz_ref[...])

    @pl.when(t == n_tiles - 1)
    def _():
        _store_state_pairs(sn_ref, sbd)


def _delta(z, n, seq, n_valid, conv0, s0, conv_w, alog_bc, dtb_bc, onorm_bc, ts, group=1):
    assert n_valid == seq or seq == ts
    n_tiles = seq // ts
    assert n % group == 0 and (group == 1 or n_tiles == 1)
    w3 = 3 * GROUP_WIDTH
    row = lambda width, blk: pl.BlockSpec((group * ts, width), lambda b, t, blk=blk: (b * n_tiles + t, blk))
    small = lambda a: pl.BlockSpec(a.shape, lambda b, t: (0, 0))
    state = pl.BlockSpec((group, N_HEADS, HEAD_DIM, HEAD_DIM), lambda b, t: (b, 0, 0, 0))
    return pl.pallas_call(
        functools.partial(_delta_kernel, ts=ts, n_valid=n_valid, n_tiles=n_tiles, group=group),
        grid=(n // group, n_tiles),
        in_specs=[row(w3, ZB_QKV_BLK), row(GROUP_WIDTH, ZB_Z_BLK), row(LANES, ZAB_BLK),
                  pl.BlockSpec((group, SUBLANES, w3), lambda b, t: (b, 0, 0)), state,
                  small(conv_w), small(alog_bc), small(dtb_bc), small(onorm_bc)],
        out_specs=[pl.BlockSpec((group * ts, GROUP_WIDTH), lambda b, t: (b * n_tiles + t, 0)),
                   pl.BlockSpec((group, DELTA_CONV - 1, w3), lambda b, t: (b, 0, 0)), state],
        out_shape=[jax.ShapeDtypeStruct((n * seq, GROUP_WIDTH), f32),
                   jax.ShapeDtypeStruct((n, DELTA_CONV - 1, w3), f32),
                   jax.ShapeDtypeStruct((n, N_HEADS, HEAD_DIM, HEAD_DIM), f32)],
        scratch_shapes=[pltpu.VMEM((group, ts + SUBLANES, w3), f32),
                        pltpu.VMEM((group, N_PAIRS, LANES, LANES), f32)],
        compiler_params=_params(2), name="delta")(z, z, z, conv0, s0, conv_w, alog_bc, dtb_bc, onorm_bc)


def _retention_kernel(zq_ref, zk_ref, zv_ref, zg_ref, dx_ref, dc_ref, db_ref, cos_ref, sin_ref, lg_ref, on_ref,
                      cwd_ref, s0_ref, dconv0_ref, o_ref, od_ref, sn_ref, dconvn_ref, sbd, dmask, xpd,
                      *, tc, n_valid, n_tiles, group):
    b, t = pl.program_id(0), pl.program_id(1)
    lg = lg_ref[...]

    @pl.when((b == 0) & (t == 0))
    def _():
        col = _iota((tc, 2 * tc), 1)
        diff = (_iota((tc, 2 * tc), 0) - col % tc).astype(f32)
        for p in range(N_PAIRS):
            lgp = lg[:, p * LANES:(p + 1) * LANES]
            rate = jnp.where(col < tc, lgp[:, 0:1], lgp[:, HEAD_DIM:HEAD_DIM + 1])
            dmask[p] = jnp.where(diff >= 0.0, jnp.exp(jnp.maximum(diff, 0.0) * rate), 0.0)

    for s in range(group):
        rows = pl.ds(s * tc, tc)
        _retention_sequence(t, zq_ref.at[rows], zk_ref.at[rows], zv_ref.at[rows], zg_ref.at[rows], dx_ref.at[rows],
                            dc_ref.at[rows], db_ref.at[rows], cos_ref, sin_ref, lg, on_ref, cwd_ref, s0_ref.at[s],
                            dconv0_ref.at[s], o_ref.at[rows], od_ref.at[rows], sn_ref.at[s], dconvn_ref.at[s],
                            sbd.at[s], dmask, xpd.at[s], tc=tc, n_valid=n_valid, n_tiles=n_tiles)


def _retention_sequence(t, zq_ref, zk_ref, zv_ref, zg_ref, dx_ref, dc_ref, db_ref, cos_ref, sin_ref, lg, on_ref,
                        cwd_ref, s0_ref, dconv0_ref, o_ref, od_ref, sn_ref, dconvn_ref, sbd, dmask, xpd,
                        *, tc, n_valid, n_tiles):
    lead = SUBLANES
    nv = min(tc, n_valid)

    @pl.when(t == 0)
    def _():
        _load_state_pairs(sbd, s0_ref)
        xpd[0:lead, :] = dconv0_ref[...]

    cos, sin = cos_ref[...], sin_ref[...]
    q = _rope_wide(zq_ref[...], cos, sin)
    k = _rope_wide(zk_ref[...], cos, sin) * (HEAD_DIM ** -0.5)
    v = zv_ref[...]
    if nv < tc:
        live = _iota((tc, 1), 0) < nv
        k, v = jnp.where(live, k, 0.0), jnp.where(live, v, 0.0)
    pos = _iota((tc, 1), 0).astype(f32)
    qd = q * jnp.exp((pos + 1.0) * lg)
    kdk = k * jnp.exp(jnp.maximum(float(nv - 1) - pos, 0.0) * lg)
    tile_decay = jnp.exp(float(nv) * lg)

    first_head = _iota((1, LANES), 1) < HEAD_DIM
    same_head = (_iota((LANES, LANES), 0) // HEAD_DIM) == (_iota((LANES, LANES), 1) // HEAD_DIM)
    outs = []
    for p in range(N_PAIRS):
        l = slice(p * LANES, (p + 1) * LANES)
        kst = _stack_heads(k[:, l], first_head).astype(bf16)
        vst = _stack_heads(v[:, l], first_head).astype(bf16)
        sc = _dot_nt(q[:, l].astype(bf16), kst) * dmask[p]
        s_prev = sbd[p]
        o = (jnp.dot(sc.astype(bf16), vst, preferred_element_type=f32)
             + jnp.dot(qd[:, l].astype(bf16), s_prev.astype(bf16), preferred_element_type=f32))
        upd = _dot_tn(kdk[:, l].astype(bf16), v[:, l].astype(bf16))
        sbd[p] = s_prev * tile_decay[:, l] + jnp.where(same_head, upd, 0.0)
        outs.append(o)
    o = jnp.concatenate(outs, axis=1)
    mean_mat = _same_head_matrix(GROUP_WIDTH, 1.0 / HEAD_DIM)
    o = o * lax.rsqrt(_head_sum(o * o, mean_mat) + EPS) * on_ref[...]
    o_ref[...] = o * _silu(zg_ref[...])

    xpd[lead:lead + tc, :] = dc_ref[...] * dx_ref[...]
    cwd = cwd_ref[...]
    yd = cwd[0:1, :] * xpd[lead - 2:lead - 2 + tc, :]
    for j in range(1, SHORT_CONV):
        yd = yd + cwd[j:j + 1, :] * xpd[lead - 2 + j:lead - 2 + j + tc, :]
    od_ref[...] = db_ref[...] * yd
    dconvn_ref[...] = xpd[lead + nv - (SHORT_CONV - 1):lead + nv, :]
    xpd[0:lead, :] = xpd[tc:tc + lead, :]

    @pl.when(t == n_tiles - 1)
    def _():
        _store_state_pairs(sn_ref, sbd)


def _retention(z, n, seq, n_valid, s0, dconv0, cos_t, sin_t, lg_bc, onorm_bc, conv_d, tc, group=1):
    assert n_valid == seq or seq == tc
    n_tiles = seq // tc
    assert n % group == 0 and (group == 1 or n_tiles == 1)
    row = lambda blk: pl.BlockSpec((group * tc, GROUP_WIDTH), lambda b, t, blk=blk: (b * n_tiles + t, blk))
    small = lambda a: pl.BlockSpec(a.shape, lambda b, t: (0, 0))
    tab = pl.BlockSpec((tc, LANES), lambda b, t: (t, 0))
    state = pl.BlockSpec((group, N_HEADS, HEAD_DIM, HEAD_DIM), lambda b, t: (b, 0, 0, 0))
    out_row = pl.BlockSpec((group * tc, GROUP_WIDTH), lambda b, t: (b * n_tiles + t, 0))
    out_rows = jax.ShapeDtypeStruct((n * seq, GROUP_WIDTH), f32)
    return pl.pallas_call(
        functools.partial(_retention_kernel, tc=tc, n_valid=n_valid, n_tiles=n_tiles, group=group),
        grid=(n // group, n_tiles),
        in_specs=[row(ZC_Q_BLK), row(ZC_K_BLK), row(ZC_V_BLK), row(ZC_G_BLK),
                  row(ZD_X_BLK), row(ZD_C_BLK), row(ZD_B_BLK), tab, tab,
                  small(lg_bc), small(onorm_bc), small(conv_d), state,
                  pl.BlockSpec((group, SUBLANES, GROUP_WIDTH), lambda b, t: (b, 0, 0))],
        out_specs=[out_row, out_row, state,
                   pl.BlockSpec((group, SHORT_CONV - 1, GROUP_WIDTH), lambda b, t: (b, 0, 0))],
        out_shape=[out_rows, out_rows, jax.ShapeDtypeStruct((n, N_HEADS, HEAD_DIM, HEAD_DIM), f32),
                   jax.ShapeDtypeStruct((n, SHORT_CONV - 1, GROUP_WIDTH), f32)],
        scratch_shapes=[pltpu.VMEM((group, N_PAIRS, LANES, LANES), f32), pltpu.VMEM((N_PAIRS, tc, 2 * tc), f32),
                        pltpu.VMEM((group, tc + SUBLANES, GROUP_WIDTH), f32)],
        compiler_params=_params(2), name="retention")(
            z, z, z, z, z, z, z, cos_t, sin_t, lg_bc, onorm_bc, conv_d, s0, dconv0)


def _rope_tables(pos, inv_freq):
    ang = pos.astype(f32)[:, None] * inv_freq[None, :]
    cos, sin = jnp.cos(ang), jnp.sin(ang)
    reps = LANES // HEAD_DIM
    return (jnp.tile(jnp.concatenate([cos, cos], axis=1), (1, reps)),
            jnp.tile(jnp.concatenate([-sin, sin], axis=1), (1, reps)))


def _per_head_lanes(v):
    return jnp.repeat(v.astype(f32), HEAD_DIM)[None, :]


def _tiled_lanes(v, width):
    return jnp.tile(v.astype(f32), width // HEAD_DIM)[None, :]


def _lead_pad(state):
    return jnp.pad(state, ((0, 0), (SUBLANES - state.shape[1], 0), (0, 0)))


def kernel(x_prompt, x_sample, cache_a_k, cache_a_v, state_b_conv, state_b_rec, state_c_rec, state_d_conv,
           norm_mix, w_in, q_norm_a, k_norm_a, conv_b, a_log_b, dt_bias_b, onorm_b, onorm_c, conv_d,
           w_out, norm_ffn, w_up, w_down):
    nb, seq, _ = x_prompt.shape
    nd, dseq, _ = x_sample.shape
    assert dseq == 1
    dpad = SUBLANES
    dgroup = max(g for g in (8, 4, 2, 1) if nd % g == 0)
    rope_freq = ROPE_THETA ** (-jnp.arange(0, HEAD_DIM, 2, dtype=f32) / HEAD_DIM)
    ret_freq = 1.0 / (10000.0 ** jnp.linspace(0.0, 1.0, HEAD_DIM // 2, dtype=f32))
    ret_lg = _per_head_lanes(jnp.log(1.0 - 2.0 ** (-5.0 - jnp.arange(N_HEADS, dtype=f32))))
    pos_p = jnp.arange(seq, dtype=jnp.int32)
    pos_d = PAST_LEN + jnp.arange(dpad, dtype=jnp.int32)
    cos_ap, sin_ap = _rope_tables(pos_p, rope_freq)
    cos_cp, sin_cp = _rope_tables(pos_p, ret_freq)
    cos_ad, sin_ad = _rope_tables(pos_d[:1], rope_freq)
    cos_cd, sin_cd = _rope_tables(pos_d, ret_freq)

    cache_kt = jnp.transpose(cache_a_k, (0, 1, 3, 4, 2))
    cache_vt = jnp.transpose(cache_a_v, (0, 1, 3, 4, 2))

    yp = x_prompt.reshape(nb * seq, D_MODEL)
    ys = x_sample.reshape(nd, D_MODEL)
    zeros_conv_b = jnp.zeros((nb, SUBLANES, 3 * GROUP_WIDTH), f32)
    zeros_conv_d = jnp.zeros((nb, SUBLANES, GROUP_WIDTH), f32)
    zeros_rec = jnp.zeros((nb, N_HEADS, HEAD_DIM, HEAD_DIM), f32)
    new_p, new_s = [], []
    dec_cache = prompt_cache = None
    w_pack = _pack_w_in(w_in)
    wo, wu, wd = w_out.astype(bf16), w_up.astype(bf16), w_down.astype(bf16)
    for l in range(DEPTH):
        g_mix, g_ffn = norm_mix[l][None, :], norm_ffn[l][None, :]
        qg, kg = _tiled_lanes(q_norm_a[l], LANES), _tiled_lanes(k_norm_a[l], LANES)
        qg3, kg3 = _tiled_lanes(q_norm_a[l], GROUP_WIDTH), _tiled_lanes(k_norm_a[l], GROUP_WIDTH)
        alog, dtb = _per_head_lanes(a_log_b[l]), _per_head_lanes(dt_bias_b[l])
        on_b, on_c = _tiled_lanes(onorm_b[l], GROUP_WIDTH), _tiled_lanes(onorm_c[l], GROUP_WIDTH)

        z = _inproj(yp, g_mix, w_pack, l, 512)
        oa, kt, vt = _attn_prompt(l, z, nb, seq, cos_ap, sin_ap, qg, kg, prompt_cache)
        prompt_cache = (kt, vt)
        ob, p_bconv, p_brec = _delta(z, nb, seq, seq, zeros_conv_b, zeros_rec, conv_b[l], alog, dtb, on_b, 256)
        oc, od, p_crec, p_dconv = _retention(z, nb, seq, seq, zeros_rec, zeros_conv_d, cos_cp, sin_cp, ret_lg,
                                             on_c, conv_d[l], 256)
        yp = _ffn(yp, oa, ob, oc, od, wo, g_ffn, wu, wd, l, 512)
        new_p.append((kt, vt, p_bconv, p_brec, p_crec, p_dconv))

        zs = _inproj(ys, g_mix, w_pack, l, nd)
        qa, ka, va = _attn_decode_prep(zs, cos_ad, sin_ad, qg3, kg3)
        col = lambda a: jnp.broadcast_to(a.reshape(nd, N_HEADS, HEAD_DIM, 1), (nd, N_HEADS, HEAD_DIM, LANES))
        s_kt, s_vt, oa_col = _attn_decode(l, cache_kt, cache_vt, col(qa), col(ka), col(va), dec_cache)
        dec_cache = (s_kt, s_vt)
        oa_s = oa_col[..., 0].reshape(nd, GROUP_WIDTH)
        zpad = jnp.zeros((nd, dpad, Z_WIDTH), f32).at[:, 0, :].set(zs).reshape(nd * dpad, Z_WIDTH)
        ob_s, s_bconv, s_brec = _delta(zpad, nd, dpad, 1, _lead_pad(state_b_conv[l]), state_b_rec[l], conv_b[l],
                                       alog, dtb, on_b, dpad, group=dgroup)
        oc_s, od_s, s_crec, s_dconv = _retention(zpad, nd, dpad, 1, state_c_rec[l], _lead_pad(state_d_conv[l]),
                                                 cos_cd, sin_cd, ret_lg, on_c, conv_d[l], dpad, group=dgroup)
        first = lambda a: a.reshape(nd, dpad, GROUP_WIDTH)[:, 0, :]
        ys = _ffn(ys, oa_s, first(ob_s), first(oc_s), first(od_s), wo, g_ffn, wu, wd, l, nd)
        new_s.append((s_bconv, s_brec, s_crec, s_dconv))

    keep = min(WIN_MAX, seq)
    cache_out = lambda i: jnp.transpose(
        prompt_cache[i].reshape(DEPTH, nb, N_HEADS, HEAD_DIM, keep), (0, 1, 4, 2, 3))
    stack_p = lambda i: jnp.stack([s[i] for s in new_p])
    stack_s = lambda i: jnp.stack([s[i] for s in new_s])
    return (yp.reshape(nb, seq, D_MODEL), ys.reshape(nd, 1, D_MODEL),
            cache_out(0), cache_out(1), stack_p(2), stack_p(3), stack_p(4), stack_p(5),
            jnp.transpose(dec_cache[0], (0, 1, 4, 2, 3)), jnp.transpose(dec_cache[1], (0, 1, 4, 2, 3)),
            stack_s(0), stack_s(1), stack_s(2), stack_s(3))
```

```python
import functools

import jax
import jax.numpy as jnp
from jax import lax
from jax.experimental import pallas as pl
from jax.experimental.pallas import tpu as pltpu

f32, bf16, i32 = jnp.float32, jnp.bfloat16, jnp.int32

D_MODEL = 1024
HEAD_DIM = 64
N_HEADS = 6
N_PAIRS = N_HEADS // 2
GROUP_WIDTH = N_HEADS * HEAD_DIM
D_FF = 4 * D_MODEL
DIL_PAIRS = ((128, 1), (512, 4), (2048, 16))
WIN_MAX = 2048
ATT_BLOCK = 128
ROPE_THETA = 10000.0
DELTA_CONV = 4
SHORT_CONV = 3
CHUNK = 64
EPS = 1e-6
NEG = -1e30
PAST_LEN = 16384
DEPTH = 2
LANES = 128
SUBLANES = 8
VMEM_LIMIT = 56 * 1024 * 1024

IN_WIDTH = 14 * GROUP_WIDTH + 2 * N_HEADS
AB_OFFSET_SRC = 7 * GROUP_WIDTH
Z_WIDTH = 14 * GROUP_WIDTH + LANES
ZB_QKV_BLK = 1
ZB_Z_BLK = 6
ZC_Q_BLK, ZC_K_BLK, ZC_V_BLK, ZC_G_BLK = 7, 8, 9, 10
ZD_X_BLK, ZD_C_BLK, ZD_B_BLK = 11, 12, 13
ZAB_BLK = 14 * GROUP_WIDTH // LANES


def _iota(shape, dim):
    return lax.broadcasted_iota(i32, shape, dim)


def _params(n_axes):
    return pltpu.CompilerParams(dimension_semantics=("arbitrary",) * n_axes, vmem_limit_bytes=VMEM_LIMIT)


def _same_head_matrix(width, value):
    r = _iota((width, width), 0) // HEAD_DIM
    c = _iota((width, width), 1) // HEAD_DIM
    return jnp.where(r == c, value, 0.0).astype(f32)


def _head_sum(x, mat):
    return _dot_split_lhs(x, mat.astype(bf16), 2)


def _rope_slab(x, cos, sin_signed):
    p = _iota((1, LANES), 1) % HEAD_DIM
    partner = jnp.where(p < HEAD_DIM // 2, pltpu.roll(x, LANES - HEAD_DIM // 2, 1), pltpu.roll(x, HEAD_DIM // 2, 1))
    return x * cos + partner * sin_signed


def _rope_wide(x, cos, sin_signed):
    return jnp.concatenate(
        [_rope_slab(x[:, LANES * i:LANES * (i + 1)], cos, sin_signed) for i in range(x.shape[1] // LANES)], axis=1)


def _stack_heads(x, first_head):
    return jnp.concatenate([jnp.where(first_head, x, 0.0), jnp.where(first_head, 0.0, x)], axis=0)


def _dot_nt(a, b):
    return lax.dot_general(a, b, (((1,), (1,)), ((), ())), preferred_element_type=f32)


def _dot_tn(a, b):
    return lax.dot_general(a, b, (((0,), (0,)), ((), ())), preferred_element_type=f32)


def _silu(x):
    return x * jax.nn.sigmoid(x)


def _inproj_kernel(x_ref, g_ref, w_ref, o_ref):
    x = x_ref[...]
    ms = jnp.mean(x * x, axis=-1, keepdims=True)
    xn = (x * lax.rsqrt(ms + EPS) * g_ref[...]).astype(bf16)
    step = 4 * LANES
    for c0 in range(0, Z_WIDTH, step):
        cw = min(step, Z_WIDTH - c0)
        o_ref[:, c0:c0 + cw] = jnp.dot(xn, w_ref[:, c0:c0 + cw], preferred_element_type=f32)


def _pack_w_in_kernel(w_ref, o_ref):
    for l in range(DEPTH):
        o_ref[l] = w_ref[:, l, :].T.astype(bf16)


def _pack_w_in(w_in):
    wt = jnp.transpose(w_in, (2, 0, 1))
    n_ab = 2 * N_HEADS
    wcat = jnp.concatenate([wt[:AB_OFFSET_SRC], wt[AB_OFFSET_SRC + n_ab:], wt[AB_OFFSET_SRC:AB_OFFSET_SRC + n_ab],
                            jnp.zeros((LANES - n_ab, DEPTH, D_MODEL), f32)], axis=0)
    return pl.pallas_call(
        _pack_w_in_kernel, grid=(Z_WIDTH // LANES,),
        in_specs=[pl.BlockSpec((LANES, DEPTH, D_MODEL), lambda i: (i, 0, 0))],
        out_specs=pl.BlockSpec((DEPTH, D_MODEL, LANES), lambda i: (0, 0, i)),
        out_shape=jax.ShapeDtypeStruct((DEPTH, D_MODEL, Z_WIDTH), bf16),
        compiler_params=_params(1), name="pack_w_in")(wcat)


def _layer_block(a, layer):
    return pl.BlockSpec((None,) + a.shape[1:], lambda *_: (layer, 0, 0), pipeline_mode=pl.Buffered(1))


def _inproj(x2d, gain, w_pack, layer, tm):
    m = x2d.shape[0]
    return pl.pallas_call(
        _inproj_kernel,
        grid=(m // tm,),
        in_specs=[pl.BlockSpec((tm, D_MODEL), lambda i: (i, 0)),
                  pl.BlockSpec((1, D_MODEL), lambda i: (0, 0)),
                  _layer_block(w_pack, layer)],
        out_specs=pl.BlockSpec((tm, Z_WIDTH), lambda i: (i, 0)),
        out_shape=jax.ShapeDtypeStruct((m, Z_WIDTH), f32),
        compiler_params=_params(1), name="inproj")(x2d, gain, w_pack)


def _ffn_kernel(x_ref, oa_ref, ob_ref, oc_ref, od_ref, wo_ref, g_ref, wu_ref, wd_ref, y_ref):
    mix = jnp.concatenate([r[...].astype(bf16) for r in (oa_ref, ob_ref, oc_ref, od_ref)], axis=1)
    h = x_ref[...] + jnp.dot(mix, wo_ref[...], preferred_element_type=f32)
    ms = jnp.mean(h * h, axis=-1, keepdims=True)
    hn = (h * lax.rsqrt(ms + EPS) * g_ref[...]).astype(bf16)
    y_ref[...] = h
    step = D_FF // 4
    for c in range(0, D_FF, step):
        u = jnp.dot(hn, wu_ref[:, c:c + step], preferred_element_type=f32)
        u = jnp.square(jnp.maximum(u, 0.0)).astype(bf16)
        y_ref[...] += jnp.dot(u, wd_ref[c:c + step, :], preferred_element_type=f32)


def _ffn(x2d, oa, ob, oc, od, w_out, gain, w_up, w_down, layer, tm):
    m = x2d.shape[0]
    row = lambda width: pl.BlockSpec((tm, width), lambda i: (i, 0))
    whole = lambda a: _layer_block(a, layer)
    return pl.pallas_call(
        _ffn_kernel,
        grid=(m // tm,),
        in_specs=[row(D_MODEL), row(GROUP_WIDTH), row(GROUP_WIDTH), row(GROUP_WIDTH), row(GROUP_WIDTH),
                  whole(w_out), pl.BlockSpec((1, D_MODEL), lambda i: (0, 0)), whole(w_up), whole(w_down)],
        out_specs=row(D_MODEL),
        out_shape=jax.ShapeDtypeStruct((m, D_MODEL), f32),
        compiler_params=_params(1), name="ffn")(x2d, oa, ob, oc, od, w_out, gain, w_up, w_down)


def _attn_prompt_kernel(q_ref, k_ref, v_ref, cos_ref, sin_ref, qg_ref, kg_ref, *rest,
                        seq, keep, wide_dil, wide_pitch, aliased):
    if aliased:
        rest = rest[2:]
    (o_ref, kt_ref, vt_ref, q_s, ks, acc_s, m_s, t_acc, t_m, qw, kw, vw, bias_first, bias_rest) = rest
    mean_mat = _same_head_matrix(LANES, 1.0 / HEAD_DIM)
    first_head = _iota((1, LANES), 1) < HEAD_DIM
    rc = 512
    blk = ATT_BLOCK

    def prep(i, carry):
        rows = pl.ds(pl.multiple_of(i * rc, rc), rc)
        cos, sin = cos_ref[rows, :], sin_ref[rows, :]
        q, k, v = q_ref[rows, :], k_ref[rows, :], v_ref[rows, :]
        q = q * lax.rsqrt(_head_sum(q * q, mean_mat) + EPS) * qg_ref[...]
        k = k * lax.rsqrt(_head_sum(k * k, mean_mat) + EPS) * kg_ref[...]
        q = _rope_slab(q, cos, sin) * (HEAD_DIM ** -0.5)
        q_s[rows, :] = q
        ks[rows, :] = _rope_slab(k, cos, sin)
        k = ks[rows, :]
        for g in range(rc // wide_dil):
            src = slice(g * wide_dil, (g + 1) * wide_dil)
            dst = pl.ds(pl.multiple_of(i * (rc // wide_dil * wide_pitch), SUBLANES) + g * wide_pitch, wide_dil)
            qw[dst, :], kw[dst, :], vw[dst, :] = q[src, :], k[src, :], v[src, :]
        return carry

    lax.fori_loop(0, seq // rc, prep, 0)

    for c in range(keep // rc):
        r0 = seq - keep + c * rc
        kt_ref[:, c * rc:(c + 1) * rc] = ks[r0:r0 + rc, :].T
        vt_ref[:, c * rc:(c + 1) * rc] = v_ref[r0:r0 + rc, :].T

    qi, kj = _iota((blk, blk), 0), _iota((blk, blk), 1)
    bias_first[...] = jnp.where(kj <= qi, 0.0, NEG)
    qi, kj = _iota((blk, 2 * blk), 0), _iota((blk, 2 * blk), 1)
    bias_rest[...] = jnp.where((kj >= qi) & (kj <= qi + blk), 0.0, NEG)

    def rows_of(st, n, dil):
        return pl.ds(st, n, stride=dil) if dil > 1 else pl.ds(st, n)

    def tile_group(tiles, dil, dst_acc, dst_m, pitch):
        staged = []
        for r, b, first in tiles:
            nk = blk if first else 2 * blk
            if dil == wide_dil:
                qrow = wide_pitch * (b * blk) + r
                krow = qrow if first else qrow - wide_pitch * blk
                rows, krows = pl.ds(qrow, blk, stride=wide_pitch), pl.ds(krow, nk, stride=wide_pitch)
                q, k, v = qw[rows, :], kw[krows, :].astype(bf16), vw[krows, :]
            else:
                start = r + b * (blk * dil)
                kstart = start if first else start - blk * dil
                rows, krows = rows_of(start, blk, dil), rows_of(kstart, nk, dil)
                q, k, v = q_s[rows, :], ks[krows, :].astype(bf16), v_ref[krows, :]
            qs = [jnp.where(first_head, q, 0.0).astype(bf16), jnp.where(first_head, 0.0, q).astype(bf16)]
            vs = [jnp.where(first_head, v, 1.0).astype(bf16), jnp.where(first_head, 1.0, v).astype(bf16)]
            bias = bias_first[...] if first else bias_rest[...]
            dst = pl.ds(pl.multiple_of(r * pitch + b * blk, SUBLANES), blk)
            staged.append((dst, vs, [_dot_nt(qs[hh], k) + bias for hh in range(2)]))
        probs = []
        for dst, vs, scores in staged:
            for hh in range(2):
                mt = jnp.max(scores[hh], axis=1, keepdims=True)
                probs.append((dst, hh, vs[hh], jnp.exp(scores[hh] - mt).astype(bf16), mt))
        for dst, hh, v, p, mt in probs:
            dst_acc[hh, dst, :] = jnp.dot(p, v, preferred_element_type=f32)
            dst_m[hh, dst, :] = jnp.broadcast_to(mt, (blk, LANES))

    merge_group = 4
    tiles_per_body = 8

    for bi, (window, dil) in enumerate(DIL_PAIRS):
        nb = seq // dil // blk
        dst_acc, dst_m = (acc_s, m_s) if bi == 0 else (t_acc, t_m)
        pitch = nb * blk + (SUBLANES if dil == wide_dil else 0)
        if nb >= tiles_per_body:
            assert nb % tiles_per_body == 0

            def per_residue(r, carry, dil=dil, nb=nb, dst_acc=dst_acc, dst_m=dst_m, pitch=pitch):
                tile_group([(r, b, b == 0) for b in range(tiles_per_body)], dil, dst_acc, dst_m, pitch)

                def per_group(g, c):
                    tile_group([(r, g * tiles_per_body + u, False) for u in range(tiles_per_body)],
                               dil, dst_acc, dst_m, pitch)
                    return c

                lax.fori_loop(1, nb // tiles_per_body, per_group, 0)
                return carry

            lax.fori_loop(0, dil, per_residue, 0)
        else:
            res_per_body = tiles_per_body // nb
            assert dil % res_per_body == 0

            def per_residues(g, carry, dil=dil, nb=nb, dst_acc=dst_acc, dst_m=dst_m, pitch=pitch, rpb=res_per_body):
                tile_group([(g * rpb + u, b, b == 0) for u in range(rpb) for b in range(nb)],
                           dil, dst_acc, dst_m, pitch)
                return carry

            lax.fori_loop(0, dil // res_per_body, per_residues, 0)

        if bi > 0 and dil != wide_dil:
            def merge(g, carry, dil=dil, nb=nb):
                pending = []
                for u in range(merge_group):
                    idx = g * merge_group + u
                    r, b = idx // nb, idx % nb
                    rows = rows_of(r + b * (blk * dil), blk, dil)
                    src = pl.ds(pl.multiple_of(idx * blk, blk), blk)
                    for hh in range(2):
                        mo, mt = m_s[hh, rows, :], t_m[hh, src, :]
                        mn = jnp.maximum(mo, mt)
                        an = acc_s[hh, rows, :] * jnp.exp(mo - mn) + t_acc[hh, src, :] * jnp.exp(mt - mn)
                        pending.append((hh, rows, an, mn))
                for hh, rows, an, mn in pending:
                    acc_s[hh, rows, :] = an
                    m_s[hh, rows, :] = mn
                return carry

            lax.fori_loop(0, seq // blk // merge_group, merge, 0)
        elif bi > 0:
            def merge_wide(g, carry, dil=dil, pitch=pitch):
                pending = []
                for u in range(merge_group):
                    t0 = pl.multiple_of((g * merge_group + u) * blk, blk)
                    rows = pl.ds(t0, blk)
                    for hh in range(2):
                        ta, tm = [], []
                        for v8 in range(blk // SUBLANES):
                            src = pl.ds(((v8 * SUBLANES) % dil) * pitch + t0 // dil + (v8 * SUBLANES) // dil,
                                        SUBLANES, stride=pitch)
                            ta.append(t_acc[hh, src, :])
                            tm.append(t_m[hh, src, :])
                        ta, mt = jnp.concatenate(ta, axis=0), jnp.concatenate(tm, axis=0)
                        mo = m_s[hh, rows, :]
                        mn = jnp.maximum(mo, mt)
                        an = acc_s[hh, rows, :] * jnp.exp(mo - mn) + ta * jnp.exp(mt - mn)
                        pending.append((hh, rows, an, mn))
                for hh, rows, an, mn in pending:
                    acc_s[hh, rows, :] = an
                    m_s[hh, rows, :] = mn
                return carry

            lax.fori_loop(0, seq // blk // merge_group, merge_wide, 0)

    def finish(i, carry):
        rows = pl.ds(pl.multiple_of(i * rc, rc), rc)
        a0, a1 = acc_s[0, rows, :], acc_s[1, rows, :]
        half = LANES // 2
        o_ref[rows, :] = jnp.where(first_head, a0 / pltpu.roll(a0, half, 1), a1 / pltpu.roll(a1, half, 1))
        return carry

    lax.fori_loop(0, seq // rc, finish, 0)


def _attn_prompt(layer, z, n, seq, cos_t, sin_t, qgain, kgain, prev_out):
    keep = min(WIN_MAX, seq)
    for window, dil in DIL_PAIRS:
        assert window // dil == ATT_BLOCK and seq % (dil * ATT_BLOCK) == 0
    wide = [dil for _, dil in DIL_PAIRS if dil % SUBLANES == 0]
    assert len(wide) == 1 and ATT_BLOCK % wide[0] == 0
    wide_dil, wide_pitch = wide[0], wide[0] + SUBLANES
    per_group = GROUP_WIDTH // LANES
    zspec = lambda base: pl.BlockSpec((seq, LANES), lambda b, p, base=base: (b, base * per_group + p),
                                      pipeline_mode=pl.Buffered(1))
    whole = lambda rows: pl.BlockSpec((rows, LANES), lambda b, p: (0, 0), pipeline_mode=pl.Buffered(1))
    slab = lambda: pltpu.VMEM((seq, LANES), f32)
    slab2 = lambda: pltpu.VMEM((2, seq, LANES), f32)
    branch = lambda: pltpu.VMEM((2, seq + wide_dil * SUBLANES, LANES), f32)
    padded = lambda: pltpu.VMEM((seq // wide_dil * wide_pitch, LANES), f32)
    in_specs = [zspec(0), zspec(1), zspec(2), whole(seq), whole(seq), whole(1), whole(1)]
    args = [z, z, z, cos_t, sin_t, qgain, kgain]
    aliases = {}
    if prev_out is not None:
        in_specs += [pl.BlockSpec(memory_space=pl.ANY), pl.BlockSpec(memory_space=pl.ANY)]
        args += list(prev_out)
        aliases = {len(args) - 2: 1, len(args) - 1: 2}
    cache_spec = pl.BlockSpec((None, None, LANES, keep), lambda b, p: (layer, b, p, 0))
    cache_shape = jax.ShapeDtypeStruct((DEPTH, n, GROUP_WIDTH, keep), f32)
    return pl.pallas_call(
        functools.partial(_attn_prompt_kernel, seq=seq, keep=keep, wide_dil=wide_dil, wide_pitch=wide_pitch,
                          aliased=prev_out is not None),
        grid=(n, N_PAIRS),
        in_specs=in_specs,
        out_specs=[pl.BlockSpec((seq, LANES), lambda b, p: (b, p)), cache_spec, cache_spec],
        out_shape=[jax.ShapeDtypeStruct((n * seq, GROUP_WIDTH), f32), cache_shape, cache_shape],
        input_output_aliases=aliases,
        scratch_shapes=[slab(), slab(), slab2(), slab2(), branch(), branch(), padded(), padded(), padded(),
                        pltpu.VMEM((ATT_BLOCK, ATT_BLOCK), f32), pltpu.VMEM((ATT_BLOCK, 2 * ATT_BLOCK), f32)],
        compiler_params=_params(2), name=f"attn_prompt_l{layer}")(*args)


def _attn_decode_prep_kernel(z_ref, cos_ref, sin_ref, qg_ref, kg_ref, q_ref, k_ref, v_ref):
    mean_mat = _same_head_matrix(GROUP_WIDTH, 1.0 / HEAD_DIM)
    z = z_ref[...]
    q, k = z[:, :GROUP_WIDTH], z[:, GROUP_WIDTH:2 * GROUP_WIDTH]
    q = q * lax.rsqrt(_head_sum(q * q, mean_mat) + EPS) * qg_ref[...]
    k = k * lax.rsqrt(_head_sum(k * k, mean_mat) + EPS) * kg_ref[...]
    q_ref[...] = _rope_wide(q, cos_ref[...], sin_ref[...]) * (HEAD_DIM ** -0.5)
    k_ref[...] = _rope_wide(k, cos_ref[...], sin_ref[...])
    v_ref[...] = z[:, 2 * GROUP_WIDTH:]


def _attn_decode_prep(z, cos_row, sin_row, qgain, kgain):
    m = z.shape[0]
    small = lambda a: pl.BlockSpec(a.shape, lambda i: (0, 0))
    out = jax.ShapeDtypeStruct((m, GROUP_WIDTH), f32)
    ospec = pl.BlockSpec((m, GROUP_WIDTH), lambda i: (0, 0))
    return pl.pallas_call(
        _attn_decode_prep_kernel, grid=(1,),
        in_specs=[pl.BlockSpec((m, 3 * GROUP_WIDTH), lambda i: (0, 0)), small(cos_row), small(sin_row),
                  small(qgain), small(kgain)],
        out_specs=[ospec, ospec, ospec], out_shape=[out, out, out],
        compiler_params=_params(1), name="attn_decode_prep")(z, cos_row, sin_row, qgain, kgain)


def _attn_decode_kernel(*refs, buf, aliased):
    if aliased:
        kc_ref, vc_ref, q_ref, kn_ref, vn_ref, _, _, ko_ref, vo_ref, o_ref = refs
    else:
        kc_ref, vc_ref, q_ref, kn_ref, vn_ref, ko_ref, vo_ref, o_ref = refs
    kc = kc_ref[...]
    vc = vc_ref[...]
    qc = q_ref[...][:, :, :1]
    knew = kn_ref[...][:, :, :1]
    vnew = vn_ref[...][:, :, :1]
    dist = buf - _iota((1, 1, buf), 2)
    mult = jnp.zeros((1, 1, buf), f32)
    for window, dil in DIL_PAIRS:
        mult = mult + jnp.where((dist <= window) & (dist % dil == 0), 1.0, 0.0)
    s = jnp.sum(kc * qc, axis=1, keepdims=True)
    s_new = jnp.sum(knew * qc, axis=1, keepdims=True)
    s = jnp.where(mult > 0.0, s, NEG)
    mx = jnp.maximum(jnp.max(s, axis=2, keepdims=True), s_new)
    p = mult * jnp.exp(s - mx)
    p_new = float(len(DIL_PAIRS)) * jnp.exp(s_new - mx)
    den = jnp.sum(p, axis=2, keepdims=True) + p_new
    o = (jnp.sum(vc * p, axis=2, keepdims=True) + vnew * p_new) / den
    o_ref[...] = jnp.broadcast_to(o, o_ref.shape)
    last = _iota(kc.shape, 2) == buf - 1
    ko_ref[...] = jnp.where(last, knew, pltpu.roll(kc, buf - 1, 2))
    vo_ref[...] = jnp.where(last, vnew, pltpu.roll(vc, buf - 1, 2))


def _attn_decode(layer, cache_kt, cache_vt, q_col, k_col, v_col, prev_out):
    depth, nb, nh, hd, buf = cache_kt.shape
    cache_spec = pl.BlockSpec((None, None, nh, hd, buf), lambda b: (layer, b, 0, 0, 0))
    col_spec = pl.BlockSpec((None, nh, hd, LANES), lambda b: (b, 0, 0, 0))
    in_specs = [cache_spec, cache_spec, col_spec, col_spec, col_spec]
    args = [cache_kt, cache_vt, q_col, k_col, v_col]
    aliases = {}
    if prev_out is not None:
        in_specs += [pl.BlockSpec(memory_space=pl.ANY), pl.BlockSpec(memory_space=pl.ANY)]
        args += list(prev_out)
        aliases = {5: 0, 6: 1}
    cache_shape = jax.ShapeDtypeStruct(cache_kt.shape, f32)
    return pl.pallas_call(
        functools.partial(_attn_decode_kernel, buf=buf, aliased=prev_out is not None),
        grid=(nb,), in_specs=in_specs,
        out_specs=[cache_spec, cache_spec, col_spec],
        out_shape=[cache_shape, cache_shape, jax.ShapeDtypeStruct((nb, nh, hd, LANES), f32)],
        input_output_aliases=aliases,
        compiler_params=_params(1), name=f"attn_decode_l{layer}")(*args)


SOLVE_BLOCK = 16


def _solve_unit_lower_pairs(systems, first_head):
    ng = CHUNK // SUBLANES
    gpb = SOLVE_BLOCK // SUBLANES
    col_in_head = _iota((SOLVE_BLOCK, LANES), 1) % HEAD_DIM
    split = lambda x: [x[SUBLANES * g:SUBLANES * (g + 1), :] for g in range(ng)]
    ags = [split(a) for a, _, _ in systems]
    ugs = [split(u) for _, u, _ in systems]
    wgs = [split(w) for _, _, w in systems]
    for blk in range(CHUNK // SOLVE_BLOCK):
        r0 = blk * SOLVE_BLOCK
        if blk > 0:
            for (a, _, _), ug, wg in zip(systems, ugs, wgs):
                left = jnp.where(col_in_head < r0, a[r0:r0 + SOLVE_BLOCK, :], 0.0).astype(bf16)
                solved = jnp.concatenate([_stack_heads(jnp.concatenate(ug, axis=0), first_head),
                                          _stack_heads(jnp.concatenate(wg, axis=0), first_head)], axis=1).astype(bf16)
                upd = jnp.dot(left, solved, preferred_element_type=f32)
                for g in range(gpb):
                    rows = slice(SUBLANES * g, SUBLANES * (g + 1))
                    ug[blk * gpb + g] = ug[blk * gpb + g] - upd[rows, :LANES]
                    wg[blk * gpb + g] = wg[blk * gpb + g] - upd[rows, LANES:]
        for j in range(r0, r0 + SOLVE_BLOCK - 1):
            g0, r = divmod(j, SUBLANES)
            for ag, ug, wg in zip(ags, ugs, wgs):
                urow = jnp.broadcast_to(ug[g0][r:r + 1, :], (SUBLANES, LANES))
                wrow = jnp.broadcast_to(wg[g0][r:r + 1, :], (SUBLANES, LANES))
                for g in range(g0, (blk + 1) * gpb):
                    c0 = jnp.broadcast_to(ag[g][:, j:j + 1], (SUBLANES, LANES))
                    c1 = jnp.broadcast_to(ag[g][:, HEAD_DIM + j:HEAD_DIM + j + 1], (SUBLANES, LANES))
                    col = jnp.where(first_head, c0, c1)
                    ug[g] = ug[g] - col * urow
                    wg[g] = wg[g] - col * wrow
    return [(jnp.concatenate(ug, axis=0), jnp.concatenate(wg, axis=0)) for ug, wg in zip(ugs, wgs)]


def _dot_split_lhs(x, mat, terms):
    out, rest = None, x
    for t in range(terms):
        piece = rest.astype(bf16)
        d = jnp.dot(piece, mat, preferred_element_type=f32)
        out = d if out is None else out + d
        if t + 1 < terms:
            rest = rest - piece.astype(f32)
    return out


def _dot_split_rhs(mat, x, terms):
    out, rest = None, x
    for t in range(terms):
        piece = rest.astype(bf16)
        d = jnp.dot(mat, piece, preferred_element_type=f32)
        out = d if out is None else out + d
        if t + 1 < terms:
            rest = rest - piece.astype(f32)
    return out


def _load_state_pairs(sbd, s0_ref):
    sbd[...] = jnp.zeros_like(sbd)
    for h in range(N_HEADS):
        p, hh = divmod(h, 2)
        sbd[p, hh * HEAD_DIM:(hh + 1) * HEAD_DIM, hh * HEAD_DIM:(hh + 1) * HEAD_DIM] = s0_ref[h]


def _store_state_pairs(sn_ref, sbd):
    for h in range(N_HEADS):
        p, hh = divmod(h, 2)
        sn_ref[h] = sbd[p, hh * HEAD_DIM:(hh + 1) * HEAD_DIM, hh * HEAD_DIM:(hh + 1) * HEAD_DIM]


def _delta_chunks(q, k, v, g, beta, sbd, first_head, same_head, ts):
    ri, cj = _iota((ts, ts), 0), _iota((ts, ts), 1)
    same = (ri // CHUNK) == (cj // CHUNK)
    gcum = _dot_split_rhs((same & (cj <= ri)).astype(bf16), g, 3)
    glast = _dot_split_rhs(same.astype(bf16), g, 3)
    eg = jnp.exp(gcum)
    qg, kb, vb = q * eg, k * beta, v * beta
    kbe = kb * eg
    kd = k * jnp.exp(glast - gcum)
    dl = jnp.exp(glast)

    ii = _iota((CHUNK, LANES), 0)
    jj = _iota((CHUNK, LANES), 1) % HEAD_DIM
    eye2 = (ii == jj).astype(f32)
    ones = jnp.ones((CHUNK, CHUNK), bf16)
    n_chunks = ts // CHUNK
    where = [(slice(c * CHUNK, (c + 1) * CHUNK), slice(p * LANES, (p + 1) * LANES))
             for c in range(n_chunks) for p in range(N_PAIRS)]

    systems, qks = [], []
    for r, l in where:
        gc = gcum[r, l]
        grow = _dot_split_rhs(ones, eye2 * gc, 3)
        dm = jnp.where(ii >= jj, jnp.exp(jnp.minimum(gc - grow, 0.0)), 0.0)
        kst = _stack_heads(k[r, l], first_head).astype(bf16)
        kk = _dot_nt(kb[r, l].astype(bf16), kst)
        qks.append((_dot_nt(q[r, l].astype(bf16), kst) * dm).astype(bf16))
        systems.append((jnp.where(ii > jj, kk * dm, 0.0), vb[r, l], kbe[r, l]))

    group = 2 * N_PAIRS
    solved = []
    for i in range(0, len(systems), group):
        solved += _solve_unit_lower_pairs(systems[i:i + group], first_head)

    out_rows = []
    for c in range(n_chunks):
        out_pairs = []
        for p in range(N_PAIRS):
            idx = c * N_PAIRS + p
            r, l = where[idx]
            u, w = solved[idx]
            s_prev = sbd[p]
            s_bf = s_prev.astype(bf16)
            vnew = u - jnp.dot(w.astype(bf16), s_bf, preferred_element_type=f32)
            vst = _stack_heads(vnew, first_head).astype(bf16)
            o = (jnp.dot(qg[r, l].astype(bf16), s_bf, preferred_element_type=f32)
                 + jnp.dot(qks[idx], vst, preferred_element_type=f32))
            upd = _dot_tn(kd[r, l].astype(bf16), vnew.astype(bf16))
            sbd[p] = s_prev * dl[c * CHUNK:c * CHUNK + 1, l] + jnp.where(same_head, upd, 0.0)
            out_pairs.append(o)
        out_rows.append(jnp.concatenate(out_pairs, axis=1))
    return jnp.concatenate(out_rows, axis=0)


def _delta_kernel(zqkv_ref, zz_ref, zab_ref, conv0_ref, s0_ref, cw_ref, alog_ref, dtb_ref, on_ref,
                  o_ref, convn_ref, sn_ref, xp, sbd, *, ts, n_valid, n_tiles, group):
    t = pl.program_id(1)
    for s in range(group):
        rows = pl.ds(s * ts, ts)
        _delta_sequence(t, zqkv_ref.at[rows], zz_ref.at[rows], zab_ref.at[rows], conv0_ref.at[s], s0_ref.at[s],
                        cw_ref, alog_ref, dtb_ref, on_ref, o_ref.at[rows], convn_ref.at[s], sn_ref.at[s],
                        xp.at[s], sbd.at[s], ts=ts, n_valid=n_valid, n_tiles=n_tiles)


def _delta_sequence(t, zqkv_ref, zz_ref, zab_ref, conv0_ref, s0_ref, cw_ref, alog_ref, dtb_ref, on_ref,
                    o_ref, convn_ref, sn_ref, xp, sbd, *, ts, n_valid, n_tiles):
    lead = SUBLANES
    nv = min(ts, n_valid)

    @pl.when(t == 0)
    def _():
        xp[0:lead, :] = conv0_ref[...]
        _load_state_pairs(sbd, s0_ref)

    xp[lead:lead + ts, :] = zqkv_ref[...]
    cw = cw_ref[...]
    y = cw[0:1, :] * xp[lead - 3:lead - 3 + ts, :]
    for j in range(1, DELTA_CONV):
        y = y + cw[j:j + 1, :] * xp[lead - 3 + j:lead - 3 + j + ts, :]
    convn_ref[...] = xp[lead + nv - (DELTA_CONV - 1):lead + nv, :]
    xp[0:lead, :] = xp[ts:ts + lead, :]
    y = _silu(y)

    sum_mat = _same_head_matrix(GROUP_WIDTH, 1.0)
    yq, yk, v = y[:, :GROUP_WIDTH], y[:, GROUP_WIDTH:2 * GROUP_WIDTH], y[:, 2 * GROUP_WIDTH:]
    q = yq * lax.rsqrt(_head_sum(yq * yq, sum_mat) + EPS) * (HEAD_DIM ** -0.5)
    k = yk * lax.rsqrt(_head_sum(yk * yk, sum_mat) + EPS)

    zab = zab_ref[...]
    src = _iota((LANES, GROUP_WIDTH), 0)
    head = _iota((LANES, GROUP_WIDTH), 1) // HEAD_DIM
    a_bc = _dot_split_lhs(zab, (src == head).astype(bf16), 3)
    b_bc = _dot_split_lhs(zab, (src == head + N_HEADS).astype(bf16), 3)
    xa = a_bc + dtb_ref[...]
    g = -jnp.exp(alog_ref[...]) * (jnp.maximum(xa, 0.0) + jnp.log1p(jnp.exp(-jnp.abs(xa))))
    beta = jax.nn.sigmoid(b_bc)
    if nv < ts:
        live = _iota((ts, 1), 0) < nv
        g, beta = jnp.where(live, g, 0.0), jnp.where(live, beta, 0.0)
        q, k, v = jnp.where(live, q, 0.0), jnp.where(live, k, 0.0), jnp.where(live, v, 0.0)

    first_head = _iota((1, LANES), 1) < HEAD_DIM
    same_head = (_iota((LANES, LANES), 0) // HEAD_DIM) == (_iota((LANES, LANES), 1) // HEAD_DIM)
    if n_valid == 1:
        eg = jnp.exp(g)
        qg, kbe, vb = q * eg, k * beta * eg, v * beta
        qk = _head_sum(q * k, sum_mat)
        out_pairs = []
        for p in range(N_PAIRS):
            l = slice(p * LANES, (p + 1) * LANES)
            s_prev = sbd[p]
            s_bf = s_prev.astype(bf16)
            vnew = vb[:, l] - jnp.dot(kbe[:, l].astype(bf16), s_bf, preferred_element_type=f32)
            out_pairs.append(jnp.dot(qg[:, l].astype(bf16), s_bf, preferred_element_type=f32) + qk[:, l] * vnew)
            upd = _dot_tn(k[:, l].astype(bf16), vnew.astype(bf16))
            sbd[p] = s_prev * eg[0:1, l] + jnp.where(same_head, upd, 0.0)
        o = jnp.concatenate(out_pairs, axis=1)
    else:
        o = _delta_chunks(q, k, v, g, beta, sbd, first_head, same_head, ts)
    mean_mat = _same_head_matrix(GROUP_WIDTH, 1.0 / HEAD_DIM)
    o = o * lax.rsqrt(_head_sum(o * o, mean_mat) + EPS) * on_ref[...]
    o_ref[...] = o * _silu(zz_ref[...])

    @pl.when(t == n_tiles - 1)
    def _():
        _store_state_pairs(sn_ref, sbd)


def _delta(z, n, seq, n_valid, conv0, s0, conv_w, alog_bc, dtb_bc, onorm_bc, ts, group=1):
    assert n_valid == seq or seq == ts
    n_tiles = seq // ts
    assert n % group == 0 and (group == 1 or n_tiles == 1)
    w3 = 3 * GROUP_WIDTH
    row = lambda width, blk: pl.BlockSpec((group * ts, width), lambda b, t, blk=blk: (b * n_tiles + t, blk))
    small = lambda a: pl.BlockSpec(a.shape, lambda b, t: (0, 0))
    state = pl.BlockSpec((group, N_HEADS, HEAD_DIM, HEAD_DIM), lambda b, t: (b, 0, 0, 0))
    return pl.pallas_call(
        functools.partial(_delta_kernel, ts=ts, n_valid=n_valid, n_tiles=n_tiles, group=group),
        grid=(n // group, n_tiles),
        in_specs=[row(w3, ZB_QKV_BLK), row(GROUP_WIDTH, ZB_Z_BLK), row(LANES, ZAB_BLK),
                  pl.BlockSpec((group, SUBLANES, w3), lambda b, t: (b, 0, 0)), state,
                  small(conv_w), small(alog_bc), small(dtb_bc), small(onorm_bc)],
        out_specs=[pl.BlockSpec((group * ts, GROUP_WIDTH), lambda b, t: (b * n_tiles + t, 0)),
                   pl.BlockSpec((group, DELTA_CONV - 1, w3), lambda b, t: (b, 0, 0)), state],
        out_shape=[jax.ShapeDtypeStruct((n * seq, GROUP_WIDTH), f32),
                   jax.ShapeDtypeStruct((n, DELTA_CONV - 1, w3), f32),
                   jax.ShapeDtypeStruct((n, N_HEADS, HEAD_DIM, HEAD_DIM), f32)],
        scratch_shapes=[pltpu.VMEM((group, ts + SUBLANES, w3), f32),
                        pltpu.VMEM((group, N_PAIRS, LANES, LANES), f32)],
        compiler_params=_params(2), name="delta")(z, z, z, conv0, s0, conv_w, alog_bc, dtb_bc, onorm_bc)


def _retention_kernel(zq_ref, zk_ref, zv_ref, zg_ref, dx_ref, dc_ref, db_ref, cos_ref, sin_ref, lg_ref, on_ref,
                      cwd_ref, s0_ref, dconv0_ref, o_ref, od_ref, sn_ref, dconvn_ref, sbd, dmask, xpd,
                      *, tc, n_valid, n_tiles, group):
    b, t = pl.program_id(0), pl.program_id(1)
    lg = lg_ref[...]

    @pl.when((b == 0) & (t == 0))
    def _():
        col = _iota((tc, 2 * tc), 1)
        diff = (_iota((tc, 2 * tc), 0) - col % tc).astype(f32)
        for p in range(N_PAIRS):
            lgp = lg[:, p * LANES:(p + 1) * LANES]
            rate = jnp.where(col < tc, lgp[:, 0:1], lgp[:, HEAD_DIM:HEAD_DIM + 1])
            dmask[p] = jnp.where(diff >= 0.0, jnp.exp(jnp.maximum(diff, 0.0) * rate), 0.0)

    for s in range(group):
        rows = pl.ds(s * tc, tc)
        _retention_sequence(t, zq_ref.at[rows], zk_ref.at[rows], zv_ref.at[rows], zg_ref.at[rows], dx_ref.at[rows],
                            dc_ref.at[rows], db_ref.at[rows], cos_ref, sin_ref, lg, on_ref, cwd_ref, s0_ref.at[s],
                            dconv0_ref.at[s], o_ref.at[rows], od_ref.at[rows], sn_ref.at[s], dconvn_ref.at[s],
                            sbd.at[s], dmask, xpd.at[s], tc=tc, n_valid=n_valid, n_tiles=n_tiles)


def _retention_sequence(t, zq_ref, zk_ref, zv_ref, zg_ref, dx_ref, dc_ref, db_ref, cos_ref, sin_ref, lg, on_ref,
                        cwd_ref, s0_ref, dconv0_ref, o_ref, od_ref, sn_ref, dconvn_ref, sbd, dmask, xpd,
                        *, tc, n_valid, n_tiles):
    lead = SUBLANES
    nv = min(tc, n_valid)

    @pl.when(t == 0)
    def _():
        _load_state_pairs(sbd, s0_ref)
        xpd[0:lead, :] = dconv0_ref[...]

    cos, sin = cos_ref[...], sin_ref[...]
    q = _rope_wide(zq_ref[...], cos, sin)
    k = _rope_wide(zk_ref[...], cos, sin) * (HEAD_DIM ** -0.5)
    v = zv_ref[...]
    if nv < tc:
        live = _iota((tc, 1), 0) < nv
        k, v = jnp.where(live, k, 0.0), jnp.where(live, v, 0.0)
    pos = _iota((tc, 1), 0).astype(f32)
    qd = q * jnp.exp((pos + 1.0) * lg)
    kdk = k * jnp.exp(jnp.maximum(float(nv - 1) - pos, 0.0) * lg)
    tile_decay = jnp.exp(float(nv) * lg)

    first_head = _iota((1, LANES), 1) < HEAD_DIM
    same_head = (_iota((LANES, LANES), 0) // HEAD_DIM) == (_iota((LANES, LANES), 1) // HEAD_DIM)
    outs = []
    for p in range(N_PAIRS):
        l = slice(p * LANES, (p + 1) * LANES)
        kst = _stack_heads(k[:, l], first_head).astype(bf16)
        vst = _stack_heads(v[:, l], first_head).astype(bf16)
        sc = _dot_nt(q[:, l].astype(bf16), kst) * dmask[p]
        s_prev = sbd[p]
        o = (jnp.dot(sc.astype(bf16), vst, preferred_element_type=f32)
             + jnp.dot(qd[:, l].astype(bf16), s_prev.astype(bf16), preferred_element_type=f32))
        upd = _dot_tn(kdk[:, l].astype(bf16), v[:, l].astype(bf16))
        sbd[p] = s_prev * tile_decay[:, l] + jnp.where(same_head, upd, 0.0)
        outs.append(o)
    o = jnp.concatenate(outs, axis=1)
    mean_mat = _same_head_matrix(GROUP_WIDTH, 1.0 / HEAD_DIM)
    o = o * lax.rsqrt(_head_sum(o * o, mean_mat) + EPS) * on_ref[...]
    o_ref[...] = o * _silu(zg_ref[...])

    xpd[lead:lead + tc, :] = dc_ref[...] * dx_ref[...]
    cwd = cwd_ref[...]
    yd = cwd[0:1, :] * xpd[lead - 2:lead - 2 + tc, :]
    for j in range(1, SHORT_CONV):
        yd = yd + cwd[j:j + 1, :] * xpd[lead - 2 + j:lead - 2 + j + tc, :]
    od_ref[...] = db_ref[...] * yd
    dconvn_ref[...] = xpd[lead + nv - (SHORT_CONV - 1):lead + nv, :]
    xpd[0:lead, :] = xpd[tc:tc + lead, :]

    @pl.when(t == n_tiles - 1)
    def _():
        _store_state_pairs(sn_ref, sbd)


def _retention(z, n, seq, n_valid, s0, dconv0, cos_t, sin_t, lg_bc, onorm_bc, conv_d, tc, group=1):
    assert n_valid == seq or seq == tc
    n_tiles = seq // tc
    assert n % group == 0 and (group == 1 or n_tiles == 1)
    row = lambda blk: pl.BlockSpec((group * tc, GROUP_WIDTH), lambda b, t, blk=blk: (b * n_tiles + t, blk))
    small = lambda a: pl.BlockSpec(a.shape, lambda b, t: (0, 0))
    tab = pl.BlockSpec((tc, LANES), lambda b, t: (t, 0))
    state = pl.BlockSpec((group, N_HEADS, HEAD_DIM, HEAD_DIM), lambda b, t: (b, 0, 0, 0))
    out_row = pl.BlockSpec((group * tc, GROUP_WIDTH), lambda b, t: (b * n_tiles + t, 0))
    out_rows = jax.ShapeDtypeStruct((n * seq, GROUP_WIDTH), f32)
    return pl.pallas_call(
        functools.partial(_retention_kernel, tc=tc, n_valid=n_valid, n_tiles=n_tiles, group=group),
        grid=(n // group, n_tiles),
        in_specs=[row(ZC_Q_BLK), row(ZC_K_BLK), row(ZC_V_BLK), row(ZC_G_BLK),
                  row(ZD_X_BLK), row(ZD_C_BLK), row(ZD_B_BLK), tab, tab,
                  small(lg_bc), small(onorm_bc), small(conv_d), state,
                  pl.BlockSpec((group, SUBLANES, GROUP_WIDTH), lambda b, t: (b, 0, 0))],
        out_specs=[out_row, out_row, state,
                   pl.BlockSpec((group, SHORT_CONV - 1, GROUP_WIDTH), lambda b, t: (b, 0, 0))],
        out_shape=[out_rows, out_rows, jax.ShapeDtypeStruct((n, N_HEADS, HEAD_DIM, HEAD_DIM), f32),
                   jax.ShapeDtypeStruct((n, SHORT_CONV - 1, GROUP_WIDTH), f32)],
        scratch_shapes=[pltpu.VMEM((group, N_PAIRS, LANES, LANES), f32), pltpu.VMEM((N_PAIRS, tc, 2 * tc), f32),
                        pltpu.VMEM((group, tc + SUBLANES, GROUP_WIDTH), f32)],
        compiler_params=_params(2), name="retention")(
            z, z, z, z, z, z, z, cos_t, sin_t, lg_bc, onorm_bc, conv_d, s0, dconv0)


def _rope_tables(pos, inv_freq):
    ang = pos.astype(f32)[:, None] * inv_freq[None, :]
    cos, sin = jnp.cos(ang), jnp.sin(ang)
    reps = LANES // HEAD_DIM
    return (jnp.tile(jnp.concatenate([cos, cos], axis=1), (1, reps)),
            jnp.tile(jnp.concatenate([-sin, sin], axis=1), (1, reps)))


def _per_head_lanes(v):
    return jnp.repeat(v.astype(f32), HEAD_DIM)[None, :]


def _tiled_lanes(v, width):
    return jnp.tile(v.astype(f32), width // HEAD_DIM)[None, :]


def _lead_pad(state):
    return jnp.pad(state, ((0, 0), (SUBLANES - state.shape[1], 0), (0, 0)))


def kernel(x_prompt, x_sample, cache_a_k, cache_a_v, state_b_conv, state_b_rec, state_c_rec, state_d_conv,
           norm_mix, w_in, q_norm_a, k_norm_a, conv_b, a_log_b, dt_bias_b, onorm_b, onorm_c, conv_d,
           w_out, norm_ffn, w_up, w_down):
    nb, seq, _ = x_prompt.shape
    nd, dseq, _ = x_sample.shape
    assert dseq == 1
    dpad = SUBLANES
    dgroup = max(g for g in (8, 4, 2, 1) if nd % g == 0)
    rope_freq = ROPE_THETA ** (-jnp.arange(0, HEAD_DIM, 2, dtype=f32) / HEAD_DIM)
    ret_freq = 1.0 / (10000.0 ** jnp.linspace(0.0, 1.0, HEAD_DIM // 2, dtype=f32))
    ret_lg = _per_head_lanes(jnp.log(1.0 - 2.0 ** (-5.0 - jnp.arange(N_HEADS, dtype=f32))))
    pos_p = jnp.arange(seq, dtype=jnp.int32)
    pos_d = PAST_LEN + jnp.arange(dpad, dtype=jnp.int32)
    cos_ap, sin_ap = _rope_tables(pos_p, rope_freq)
    cos_cp, sin_cp = _rope_tables(pos_p, ret_freq)
    cos_ad, sin_ad = _rope_tables(pos_d[:1], rope_freq)
    cos_cd, sin_cd = _rope_tables(pos_d, ret_freq)

    cache_kt = jnp.transpose(cache_a_k, (0, 1, 3, 4, 2))
    cache_vt = jnp.transpose(cache_a_v, (0, 1, 3, 4, 2))

    yp = x_prompt.reshape(nb * seq, D_MODEL)
    ys = x_sample.reshape(nd, D_MODEL)
    zeros_conv_b = jnp.zeros((nb, SUBLANES, 3 * GROUP_WIDTH), f32)
    zeros_conv_d = jnp.zeros((nb, SUBLANES, GROUP_WIDTH), f32)
    zeros_rec = jnp.zeros((nb, N_HEADS, HEAD_DIM, HEAD_DIM), f32)
    new_p, new_s = [], []
    dec_cache = prompt_cache = None
    w_pack = _pack_w_in(w_in)
    wo, wu, wd = w_out.astype(bf16), w_up.astype(bf16), w_down.astype(bf16)
    for l in range(DEPTH):
        g_mix, g_ffn = norm_mix[l][None, :], norm_ffn[l][None, :]
        qg, kg = _tiled_lanes(q_norm_a[l], LANES), _tiled_lanes(k_norm_a[l], LANES)
        qg3, kg3 = _tiled_lanes(q_norm_a[l], GROUP_WIDTH), _tiled_lanes(k_norm_a[l], GROUP_WIDTH)
        alog, dtb = _per_head_lanes(a_log_b[l]), _per_head_lanes(dt_bias_b[l])
        on_b, on_c = _tiled_lanes(onorm_b[l], GROUP_WIDTH), _tiled_lanes(onorm_c[l], GROUP_WIDTH)

        z = _inproj(yp, g_mix, w_pack, l, 512)
        oa, kt, vt = _attn_prompt(l, z, nb, seq, cos_ap, sin_ap, qg, kg, prompt_cache)
        prompt_cache = (kt, vt)
        ob, p_bconv, p_brec = _delta(z, nb, seq, seq, zeros_conv_b, zeros_rec, conv_b[l], alog, dtb, on_b, 256)
        oc, od, p_crec, p_dconv = _retention(z, nb, seq, seq, zeros_rec, zeros_conv_d, cos_cp, sin_cp, ret_lg,
                                             on_c, conv_d[l], 256)
        yp = _ffn(yp, oa, ob, oc, od, wo, g_ffn, wu, wd, l, 512)
        new_p.append((kt, vt, p_bconv, p_brec, p_crec, p_dconv))

        zs = _inproj(ys, g_mix, w_pack, l, nd)
        qa, ka, va = _attn_decode_prep(zs, cos_ad, sin_ad, qg3, kg3)
        col = lambda a: jnp.broadcast_to(a.reshape(nd, N_HEADS, HEAD_DIM, 1), (nd, N_HEADS, HEAD_DIM, LANES))
        s_kt, s_vt, oa_col = _attn_decode(l, cache_kt, cache_vt, col(qa), col(ka), col(va), dec_cache)
        dec_cache = (s_kt, s_vt)
        oa_s = oa_col[..., 0].reshape(nd, GROUP_WIDTH)
        zpad = jnp.zeros((nd, dpad, Z_WIDTH), f32).at[:, 0, :].set(zs).reshape(nd * dpad, Z_WIDTH)
        ob_s, s_bconv, s_brec = _delta(zpad, nd, dpad, 1, _lead_pad(state_b_conv[l]), state_b_rec[l], conv_b[l],
                                       alog, dtb, on_b, dpad, group=dgroup)
        oc_s, od_s, s_crec, s_dconv = _retention(zpad, nd, dpad, 1, state_c_rec[l], _lead_pad(state_d_conv[l]),
                                                 cos_cd, sin_cd, ret_lg, on_c, conv_d[l], dpad, group=dgroup)
        first = lambda a: a.reshape(nd, dpad, GROUP_WIDTH)[:, 0, :]
        ys = _ffn(ys, oa_s, first(ob_s), first(oc_s), first(od_s), wo, g_ffn, wu, wd, l, nd)
        new_s.append((s_bconv, s_brec, s_crec, s_dconv))

    keep = min(WIN_MAX, seq)
    cache_out = lambda i: jnp.transpose(
        prompt_cache[i].reshape(DEPTH, nb, N_HEADS, HEAD_DIM, keep), (0, 1, 4, 2, 3))
    stack_p = lambda i: jnp.stack([s[i] for s in new_p])
    stack_s = lambda i: jnp.stack([s[i] for s in new_s])
    return (yp.reshape(nb, seq, D_MODEL), ys.reshape(nd, 1, D_MODEL),
            cache_out(0), cache_out(1), stack_p(2), stack_p(3), stack_p(4), stack_p(5),
            jnp.transpose(dec_cache[0], (0, 1, 4, 2, 3)), jnp.transpose(dec_cache[1], (0, 1, 4, 2, 3)),
            stack_s(0), stack_s(1), stack_s(2), stack_s(3))
```

```python
import functools

import jax
import jax.numpy as jnp
from jax import lax
from jax.experimental import pallas as pl
from jax.experimental.pallas import tpu as pltpu

f32, bf16, i32 = jnp.float32, jnp.bfloat16, jnp.int32

D_MODEL = 1024
HEAD_DIM = 64
N_HEADS = 6
N_PAIRS = N_HEADS // 2
GROUP_WIDTH = N_HEADS * HEAD_DIM
D_FF = 4 * D_MODEL
DIL_PAIRS = ((128, 1), (512, 4), (2048, 16))
WIN_MAX = 2048
ATT_BLOCK = 128
ROPE_THETA = 10000.0
DELTA_CONV = 4
SHORT_CONV = 3
CHUNK = 64
EPS = 1e-6
NEG = -1e30
PAST_LEN = 16384
DEPTH = 2
LANES = 128
SUBLANES = 8
VMEM_LIMIT = 56 * 1024 * 1024

IN_WIDTH = 14 * GROUP_WIDTH + 2 * N_HEADS
AB_OFFSET_SRC = 7 * GROUP_WIDTH
Z_WIDTH = 14 * GROUP_WIDTH + LANES
ZB_QKV_BLK = 1
ZB_Z_BLK = 6
ZC_Q_BLK, ZC_K_BLK, ZC_V_BLK, ZC_G_BLK = 7, 8, 9, 10
ZD_X_BLK, ZD_C_BLK, ZD_B_BLK = 11, 12, 13
ZAB_BLK = 14 * GROUP_WIDTH // LANES


def _iota(shape, dim):
    return lax.broadcasted_iota(i32, shape, dim)


def _params(n_axes):
    return pltpu.CompilerParams(dimension_semantics=("arbitrary",) * n_axes, vmem_limit_bytes=VMEM_LIMIT)


def _same_head_matrix(width, value):
    r = _iota((width, width), 0) // HEAD_DIM
    c = _iota((width, width), 1) // HEAD_DIM
    return jnp.where(r == c, value, 0.0).astype(f32)


def _head_sum(x, mat):
    return _dot_split_lhs(x, mat.astype(bf16), 2)


def _rope_slab(x, cos, sin_signed):
    p = _iota((1, LANES), 1) % HEAD_DIM
    partner = jnp.where(p < HEAD_DIM // 2, pltpu.roll(x, LANES - HEAD_DIM // 2, 1), pltpu.roll(x, HEAD_DIM // 2, 1))
    return x * cos + partner * sin_signed


def _rope_wide(x, cos, sin_signed):
    return jnp.concatenate(
        [_rope_slab(x[:, LANES * i:LANES * (i + 1)], cos, sin_signed) for i in range(x.shape[1] // LANES)], axis=1)


def _stack_heads(x, first_head):
    return jnp.concatenate([jnp.where(first_head, x, 0.0), jnp.where(first_head, 0.0, x)], axis=0)


def _dot_nt(a, b):
    return lax.dot_general(a, b, (((1,), (1,)), ((), ())), preferred_element_type=f32)


def _dot_tn(a, b):
    return lax.dot_general(a, b, (((0,), (0,)), ((), ())), preferred_element_type=f32)


def _silu(x):
    return x * jax.nn.sigmoid(x)


def _inproj_kernel(x_ref, g_ref, w_ref, o_ref):
    x = x_ref[...]
    ms = jnp.mean(x * x, axis=-1, keepdims=True)
    xn = (x * lax.rsqrt(ms + EPS) * g_ref[...]).astype(bf16)
    step = 4 * LANES
    for c0 in range(0, Z_WIDTH, step):
        cw = min(step, Z_WIDTH - c0)
        o_ref[:, c0:c0 + cw] = jnp.dot(xn, w_ref[:, c0:c0 + cw], preferred_element_type=f32)


def _pack_w_in_kernel(lo_ref, hi_ref, ab_ref, o_ref, *, n_lo, n_hi):
    i = pl.program_id(0)

    def emit(src):
        for l in range(DEPTH):
            o_ref[l] = src(l).T.astype(bf16)

    @pl.when(i < n_lo)
    def _():
        emit(lambda l: lo_ref[:, l, :])

    @pl.when((i >= n_lo) & (i < n_lo + n_hi))
    def _():
        emit(lambda l: hi_ref[:, l, :])

    @pl.when(i == n_lo + n_hi)
    def _():
        live = _iota((LANES, D_MODEL), 0) < 2 * N_HEADS
        emit(lambda l: jnp.where(live, ab_ref[:, l, :], 0.0))


def _pack_w_in(w_in):
    wt = jnp.transpose(w_in, (2, 0, 1))
    n_ab = 2 * N_HEADS
    n_lo = AB_OFFSET_SRC // LANES
    n_hi = (IN_WIDTH - AB_OFFSET_SRC - n_ab) // LANES
    assert n_lo * LANES == AB_OFFSET_SRC and (n_lo + n_hi + 1) * LANES == Z_WIDTH
    rows = lambda start: pl.BlockSpec((pl.Element(LANES), pl.Element(DEPTH), pl.Element(D_MODEL)),
                                      lambda i: (start(i), 0, 0))
    return pl.pallas_call(
        functools.partial(_pack_w_in_kernel, n_lo=n_lo, n_hi=n_hi), grid=(Z_WIDTH // LANES,),
        in_specs=[rows(lambda i: LANES * jnp.minimum(i, n_lo - 1)),
                  rows(lambda i: AB_OFFSET_SRC + n_ab + LANES * jnp.clip(i - n_lo, 0, n_hi - 1)),
                  rows(lambda i: AB_OFFSET_SRC)],
        out_specs=pl.BlockSpec((DEPTH, D_MODEL, LANES), lambda i: (0, 0, i)),
        out_shape=jax.ShapeDtypeStruct((DEPTH, D_MODEL, Z_WIDTH), bf16),
        compiler_params=_params(1), name="pack_w_in")(wt, wt, wt)


def _layer_block(a, layer):
    return pl.BlockSpec((None,) + a.shape[1:], lambda *_: (layer, 0, 0), pipeline_mode=pl.Buffered(1))


def _inproj(x2d, gain, w_pack, layer, tm):
    m = x2d.shape[0]
    return pl.pallas_call(
        _inproj_kernel,
        grid=(m // tm,),
        in_specs=[pl.BlockSpec((tm, D_MODEL), lambda i: (i, 0)),
                  pl.BlockSpec((1, D_MODEL), lambda i: (0, 0)),
                  _layer_block(w_pack, layer)],
        out_specs=pl.BlockSpec((tm, Z_WIDTH), lambda i: (i, 0)),
        out_shape=jax.ShapeDtypeStruct((m, Z_WIDTH), f32),
        compiler_params=_params(1), name="inproj")(x2d, gain, w_pack)


def _ffn_kernel(x_ref, oa_ref, ob_ref, oc_ref, od_ref, wo_ref, g_ref, wu_ref, wd_ref, y_ref):
    mix = jnp.concatenate([r[...].astype(bf16) for r in (oa_ref, ob_ref, oc_ref, od_ref)], axis=1)
    h = x_ref[...] + jnp.dot(mix, wo_ref[...], preferred_element_type=f32)
    ms = jnp.mean(h * h, axis=-1, keepdims=True)
    hn = (h * lax.rsqrt(ms + EPS) * g_ref[...]).astype(bf16)
    y_ref[...] = h
    step = D_FF // 4
    for c in range(0, D_FF, step):
        u = jnp.dot(hn, wu_ref[:, c:c + step], preferred_element_type=f32)
        u = jnp.square(jnp.maximum(u, 0.0)).astype(bf16)
        y_ref[...] += jnp.dot(u, wd_ref[c:c + step, :], preferred_element_type=f32)


def _ffn(x2d, oa, ob, oc, od, w_out, gain, w_up, w_down, layer, tm):
    m = x2d.shape[0]
    row = lambda width: pl.BlockSpec((tm, width), lambda i: (i, 0))
    whole = lambda a: _layer_block(a, layer)
    return pl.pallas_call(
        _ffn_kernel,
        grid=(m // tm,),
        in_specs=[row(D_MODEL), row(GROUP_WIDTH), row(GROUP_WIDTH), row(GROUP_WIDTH), row(GROUP_WIDTH),
                  whole(w_out), pl.BlockSpec((1, D_MODEL), lambda i: (0, 0)), whole(w_up), whole(w_down)],
        out_specs=row(D_MODEL),
        out_shape=jax.ShapeDtypeStruct((m, D_MODEL), f32),
        compiler_params=_params(1), name="ffn")(x2d, oa, ob, oc, od, w_out, gain, w_up, w_down)


def _attn_prompt_kernel(q_ref, k_ref, v_ref, cos_ref, sin_ref, qg_ref, kg_ref, *rest,
                        seq, keep, wide_dil, wide_pitch, aliased):
    if aliased:
        rest = rest[2:]
    (o_ref, kt_ref, vt_ref, q_s, ks, acc_s, m_s, t_acc, t_m, qw, kw, vw, bias_first, bias_rest) = rest
    mean_mat = _same_head_matrix(LANES, 1.0 / HEAD_DIM)
    first_head = _iota((1, LANES), 1) < HEAD_DIM
    rc = 512
    blk = ATT_BLOCK

    def prep(i, carry):
        rows = pl.ds(pl.multiple_of(i * rc, rc), rc)
        cos, sin = cos_ref[rows, :], sin_ref[rows, :]
        q, k, v = q_ref[rows, :], k_ref[rows, :], v_ref[rows, :]
        q = q * lax.rsqrt(_head_sum(q * q, mean_mat) + EPS) * qg_ref[...]
        k = k * lax.rsqrt(_head_sum(k * k, mean_mat) + EPS) * kg_ref[...]
        q = _rope_slab(q, cos, sin) * (HEAD_DIM ** -0.5)
        q_s[rows, :] = q
        ks[rows, :] = _rope_slab(k, cos, sin)
        k = ks[rows, :]
        for g in range(rc // wide_dil):
            src = slice(g * wide_dil, (g + 1) * wide_dil)
            dst = pl.ds(pl.multiple_of(i * (rc // wide_dil * wide_pitch), SUBLANES) + g * wide_pitch, wide_dil)
            qw[dst, :], kw[dst, :], vw[dst, :] = q[src, :], k[src, :], v[src, :]
        return carry

    lax.fori_loop(0, seq // rc, prep, 0)

    for c in range(keep // rc):
        r0 = seq - keep + c * rc
        kt_ref[:, c * rc:(c + 1) * rc] = ks[r0:r0 + rc, :].T
        vt_ref[:, c * rc:(c + 1) * rc] = v_ref[r0:r0 + rc, :].T

    qi, kj = _iota((blk, blk), 0), _iota((blk, blk), 1)
    bias_first[...] = jnp.where(kj <= qi, 0.0, NEG)
    qi, kj = _iota((blk, 2 * blk), 0), _iota((blk, 2 * blk), 1)
    bias_rest[...] = jnp.where((kj >= qi) & (kj <= qi + blk), 0.0, NEG)

    def rows_of(st, n, dil):
        return pl.ds(st, n, stride=dil) if dil > 1 else pl.ds(st, n)

    def tile_group(tiles, dil, dst_acc, dst_m, pitch):
        staged = []
        for r, b, first in tiles:
            nk = blk if first else 2 * blk
            if dil == wide_dil:
                qrow = wide_pitch * (b * blk) + r
                krow = qrow if first else qrow - wide_pitch * blk
                rows, krows = pl.ds(qrow, blk, stride=wide_pitch), pl.ds(krow, nk, stride=wide_pitch)
                q, k, v = qw[rows, :], kw[krows, :].astype(bf16), vw[krows, :]
            else:
                start = r + b * (blk * dil)
                kstart = start if first else start - blk * dil
                rows, krows = rows_of(start, blk, dil), rows_of(kstart, nk, dil)
                q, k, v = q_s[rows, :], ks[krows, :].astype(bf16), v_ref[krows, :]
            qs = [jnp.where(first_head, q, 0.0).astype(bf16), jnp.where(first_head, 0.0, q).astype(bf16)]
            vs = [jnp.where(first_head, v, 1.0).astype(bf16), jnp.where(first_head, 1.0, v).astype(bf16)]
            bias = bias_first[...] if first else bias_rest[...]
            dst = pl.ds(pl.multiple_of(r * pitch + b * blk, SUBLANES), blk)
            staged.append((dst, vs, [_dot_nt(qs[hh], k) + bias for hh in range(2)]))
        probs = []
        for dst, vs, scores in staged:
            for hh in range(2):
                mt = jnp.max(scores[hh], axis=1, keepdims=True)
                probs.append((dst, hh, vs[hh], jnp.exp(scores[hh] - mt).astype(bf16), mt))
        for dst, hh, v, p, mt in probs:
            dst_acc[hh, dst, :] = jnp.dot(p, v, preferred_element_type=f32)
            dst_m[hh, dst, :] = jnp.broadcast_to(mt, (blk, LANES))

    merge_group = 4
    tiles_per_body = 8

    for bi, (window, dil) in enumerate(DIL_PAIRS):
        nb = seq // dil // blk
        dst_acc, dst_m = (acc_s, m_s) if bi == 0 else (t_acc, t_m)
        pitch = nb * blk + (SUBLANES if dil == wide_dil else 0)
        if nb >= tiles_per_body:
            assert nb % tiles_per_body == 0

            def per_residue(r, carry, dil=dil, nb=nb, dst_acc=dst_acc, dst_m=dst_m, pitch=pitch):
                tile_group([(r, b, b == 0) for b in range(tiles_per_body)], dil, dst_acc, dst_m, pitch)

                def per_group(g, c):
                    tile_group([(r, g * tiles_per_body + u, False) for u in range(tiles_per_body)],
                               dil, dst_acc, dst_m, pitch)
                    return c

                lax.fori_loop(1, nb // tiles_per_body, per_group, 0)
                return carry

            lax.fori_loop(0, dil, per_residue, 0)
        else:
            res_per_body = tiles_per_body // nb
            assert dil % res_per_body == 0

            def per_residues(g, carry, dil=dil, nb=nb, dst_acc=dst_acc, dst_m=dst_m, pitch=pitch, rpb=res_per_body):
                tile_group([(g * rpb + u, b, b == 0) for u in range(rpb) for b in range(nb)],
                           dil, dst_acc, dst_m, pitch)
                return carry

            lax.fori_loop(0, dil // res_per_body, per_residues, 0)

        if bi > 0 and dil != wide_dil:
            def merge(g, carry, dil=dil, nb=nb):
                pending = []
                for u in range(merge_group):
                    idx = g * merge_group + u
                    r, b = idx // nb, idx % nb
                    rows = rows_of(r + b * (blk * dil), blk, dil)
                    src = pl.ds(pl.multiple_of(idx * blk, blk), blk)
                    for hh in range(2):
                        mo, mt = m_s[hh, rows, :], t_m[hh, src, :]
                        mn = jnp.maximum(mo, mt)
                        an = acc_s[hh, rows, :] * jnp.exp(mo - mn) + t_acc[hh, src, :] * jnp.exp(mt - mn)
                        pending.append((hh, rows, an, mn))
                for hh, rows, an, mn in pending:
                    acc_s[hh, rows, :] = an
                    m_s[hh, rows, :] = mn
                return carry

            lax.fori_loop(0, seq // blk // merge_group, merge, 0)
        elif bi > 0:
            def merge_wide(g, carry, dil=dil, pitch=pitch):
                pending = []
                for u in range(merge_group):
                    t0 = pl.multiple_of((g * merge_group + u) * blk, blk)
                    rows = pl.ds(t0, blk)
                    for hh in range(2):
                        ta, tm = [], []
                        for v8 in range(blk // SUBLANES):
                            src = pl.ds(((v8 * SUBLANES) % dil) * pitch + t0 // dil + (v8 * SUBLANES) // dil,
                                        SUBLANES, stride=pitch)
                            ta.append(t_acc[hh, src, :])
                            tm.append(t_m[hh, src, :])
                        ta, mt = jnp.concatenate(ta, axis=0), jnp.concatenate(tm, axis=0)
                        mo = m_s[hh, rows, :]
                        mn = jnp.maximum(mo, mt)
                        an = acc_s[hh, rows, :] * jnp.exp(mo - mn) + ta * jnp.exp(mt - mn)
                        pending.append((hh, rows, an, mn))
                for hh, rows, an, mn in pending:
                    acc_s[hh, rows, :] = an
                    m_s[hh, rows, :] = mn
                return carry

            lax.fori_loop(0, seq // blk // merge_group, merge_wide, 0)

    def finish(i, carry):
        rows = pl.ds(pl.multiple_of(i * rc, rc), rc)
        a0, a1 = acc_s[0, rows, :], acc_s[1, rows, :]
        half = LANES // 2
        o_ref[rows, :] = jnp.where(first_head, a0 / pltpu.roll(a0, half, 1), a1 / pltpu.roll(a1, half, 1))
        return carry

    lax.fori_loop(0, seq // rc, finish, 0)


def _attn_prompt(layer, z, n, seq, cos_t, sin_t, qgain, kgain, prev_out):
    keep = min(WIN_MAX, seq)
    for window, dil in DIL_PAIRS:
        assert window // dil == ATT_BLOCK and seq % (dil * ATT_BLOCK) == 0
    wide = [dil for _, dil in DIL_PAIRS if dil % SUBLANES == 0]
    assert len(wide) == 1 and ATT_BLOCK % wide[0] == 0
    wide_dil, wide_pitch = wide[0], wide[0] + SUBLANES
    per_group = GROUP_WIDTH // LANES
    zspec = lambda base: pl.BlockSpec((seq, LANES), lambda b, p, base=base: (b, base * per_group + p),
                                      pipeline_mode=pl.Buffered(1))
    whole = lambda rows: pl.BlockSpec((rows, LANES), lambda b, p: (0, 0), pipeline_mode=pl.Buffered(1))
    slab = lambda: pltpu.VMEM((seq, LANES), f32)
    slab2 = lambda: pltpu.VMEM((2, seq, LANES), f32)
    branch = lambda: pltpu.VMEM((2, seq + wide_dil * SUBLANES, LANES), f32)
    padded = lambda: pltpu.VMEM((seq // wide_dil * wide_pitch, LANES), f32)
    in_specs = [zspec(0), zspec(1), zspec(2), whole(seq), whole(seq), whole(1), whole(1)]
    args = [z, z, z, cos_t, sin_t, qgain, kgain]
    aliases = {}
    if prev_out is not None:
        in_specs += [pl.BlockSpec(memory_space=pl.ANY), pl.BlockSpec(memory_space=pl.ANY)]
        args += list(prev_out)
        aliases = {len(args) - 2: 1, len(args) - 1: 2}
    cache_spec = pl.BlockSpec((None, None, LANES, keep), lambda b, p: (layer, b, p, 0))
    cache_shape = jax.ShapeDtypeStruct((DEPTH, n, GROUP_WIDTH, keep), f32)
    return pl.pallas_call(
        functools.partial(_attn_prompt_kernel, seq=seq, keep=keep, wide_dil=wide_dil, wide_pitch=wide_pitch,
                          aliased=prev_out is not None),
        grid=(n, N_PAIRS),
        in_specs=in_specs,
        out_specs=[pl.BlockSpec((seq, LANES), lambda b, p: (b, p)), cache_spec, cache_spec],
        out_shape=[jax.ShapeDtypeStruct((n * seq, GROUP_WIDTH), f32), cache_shape, cache_shape],
        input_output_aliases=aliases,
        scratch_shapes=[slab(), slab(), slab2(), slab2(), branch(), branch(), padded(), padded(), padded(),
                        pltpu.VMEM((ATT_BLOCK, ATT_BLOCK), f32), pltpu.VMEM((ATT_BLOCK, 2 * ATT_BLOCK), f32)],
        compiler_params=_params(2), name=f"attn_prompt_l{layer}")(*args)


def _attn_decode_prep_kernel(z_ref, cos_ref, sin_ref, qg_ref, kg_ref, q_ref, k_ref, v_ref):
    mean_mat = _same_head_matrix(GROUP_WIDTH, 1.0 / HEAD_DIM)
    z = z_ref[...]
    q, k = z[:, :GROUP_WIDTH], z[:, GROUP_WIDTH:2 * GROUP_WIDTH]
    q = q * lax.rsqrt(_head_sum(q * q, mean_mat) + EPS) * qg_ref[...]
    k = k * lax.rsqrt(_head_sum(k * k, mean_mat) + EPS) * kg_ref[...]
    q_ref[...] = _rope_wide(q, cos_ref[...], sin_ref[...]) * (HEAD_DIM ** -0.5)
    k_ref[...] = _rope_wide(k, cos_ref[...], sin_ref[...])
    v_ref[...] = z[:, 2 * GROUP_WIDTH:]


def _attn_decode_prep(z, cos_row, sin_row, qgain, kgain):
    m = z.shape[0]
    small = lambda a: pl.BlockSpec(a.shape, lambda i: (0, 0))
    out = jax.ShapeDtypeStruct((m, GROUP_WIDTH), f32)
    ospec = pl.BlockSpec((m, GROUP_WIDTH), lambda i: (0, 0))
    return pl.pallas_call(
        _attn_decode_prep_kernel, grid=(1,),
        in_specs=[pl.BlockSpec((m, 3 * GROUP_WIDTH), lambda i: (0, 0)), small(cos_row), small(sin_row),
                  small(qgain), small(kgain)],
        out_specs=[ospec, ospec, ospec], out_shape=[out, out, out],
        compiler_params=_params(1), name="attn_decode_prep")(z, cos_row, sin_row, qgain, kgain)


def _attn_decode_kernel(*refs, buf, aliased):
    if aliased:
        kc_ref, vc_ref, q_ref, kn_ref, vn_ref, _, _, ko_ref, vo_ref, o_ref = refs
    else:
        kc_ref, vc_ref, q_ref, kn_ref, vn_ref, ko_ref, vo_ref, o_ref = refs
    kc = kc_ref[...]
    vc = vc_ref[...]
    qc = q_ref[...][:, :, :1]
    knew = kn_ref[...][:, :, :1]
    vnew = vn_ref[...][:, :, :1]
    dist = buf - _iota((1, 1, buf), 2)
    mult = jnp.zeros((1, 1, buf), f32)
    for window, dil in DIL_PAIRS:
        mult = mult + jnp.where((dist <= window) & (dist % dil == 0), 1.0, 0.0)
    s = jnp.sum(kc * qc, axis=1, keepdims=True)
    s_new = jnp.sum(knew * qc, axis=1, keepdims=True)
    s = jnp.where(mult > 0.0, s, NEG)
    mx = jnp.maximum(jnp.max(s, axis=2, keepdims=True), s_new)
    p = mult * jnp.exp(s - mx)
    p_new = float(len(DIL_PAIRS)) * jnp.exp(s_new - mx)
    den = jnp.sum(p, axis=2, keepdims=True) + p_new
    o = (jnp.sum(vc * p, axis=2, keepdims=True) + vnew * p_new) / den
    o_ref[...] = jnp.broadcast_to(o, o_ref.shape)
    last = _iota(kc.shape, 2) == buf - 1
    ko_ref[...] = jnp.where(last, knew, pltpu.roll(kc, buf - 1, 2))
    vo_ref[...] = jnp.where(last, vnew, pltpu.roll(vc, buf - 1, 2))


def _attn_decode(layer, cache_kt, cache_vt, q_col, k_col, v_col, prev_out):
    depth, nb, nh, hd, buf = cache_kt.shape
    cache_spec = pl.BlockSpec((None, None, nh, hd, buf), lambda b: (layer, b, 0, 0, 0))
    col_spec = pl.BlockSpec((None, nh, hd, LANES), lambda b: (b, 0, 0, 0))
    in_specs = [cache_spec, cache_spec, col_spec, col_spec, col_spec]
    args = [cache_kt, cache_vt, q_col, k_col, v_col]
    aliases = {}
    if prev_out is not None:
        in_specs += [pl.BlockSpec(memory_space=pl.ANY), pl.BlockSpec(memory_space=pl.ANY)]
        args += list(prev_out)
        aliases = {5: 0, 6: 1}
    cache_shape = jax.ShapeDtypeStruct(cache_kt.shape, f32)
    return pl.pallas_call(
        functools.partial(_attn_decode_kernel, buf=buf, aliased=prev_out is not None),
        grid=(nb,), in_specs=in_specs,
        out_specs=[cache_spec, cache_spec, col_spec],
        out_shape=[cache_shape, cache_shape, jax.ShapeDtypeStruct((nb, nh, hd, LANES), f32)],
        input_output_aliases=aliases,
        compiler_params=_params(1), name=f"attn_decode_l{layer}")(*args)


SOLVE_BLOCK = 16


def _solve_unit_lower_pairs(systems, first_head):
    ng = CHUNK // SUBLANES
    gpb = SOLVE_BLOCK // SUBLANES
    col_in_head = _iota((SOLVE_BLOCK, LANES), 1) % HEAD_DIM
    split = lambda x: [x[SUBLANES * g:SUBLANES * (g + 1), :] for g in range(ng)]
    ags = [split(a) for a, _ in systems]
    xgs = [split(x) for _, x in systems]
    for blk in range(CHUNK // SOLVE_BLOCK):
        r0 = blk * SOLVE_BLOCK
        if blk > 0:
            for (a, _), xg in zip(systems, xgs):
                left = jnp.where(col_in_head < r0, a[r0:r0 + SOLVE_BLOCK, :], 0.0).astype(bf16)
                solved = _stack_heads(jnp.concatenate(xg, axis=0), first_head).astype(bf16)
                upd = jnp.dot(left, solved, preferred_element_type=f32)
                for g in range(gpb):
                    xg[blk * gpb + g] = xg[blk * gpb + g] - upd[SUBLANES * g:SUBLANES * (g + 1), :]
        for j in range(r0, r0 + SOLVE_BLOCK - 1):
            g0, r = divmod(j, SUBLANES)
            for ag, xg in zip(ags, xgs):
                xrow = jnp.broadcast_to(xg[g0][r:r + 1, :], (SUBLANES, LANES))
                for g in range(g0, (blk + 1) * gpb):
                    c0 = jnp.broadcast_to(ag[g][:, j:j + 1], (SUBLANES, LANES))
                    c1 = jnp.broadcast_to(ag[g][:, HEAD_DIM + j:HEAD_DIM + j + 1], (SUBLANES, LANES))
                    xg[g] = xg[g] - jnp.where(first_head, c0, c1) * xrow
    return [jnp.concatenate(xg, axis=0) for xg in xgs]


def _bf16_terms(x, terms):
    pieces, rest = [], x
    for t in range(terms):
        pieces.append(rest.astype(bf16))
        if t + 1 < terms:
            rest = rest - pieces[-1].astype(f32)
    return pieces


def _dot_split_lhs(x, mat, terms):
    parts = [jnp.dot(p, mat, preferred_element_type=f32) for p in _bf16_terms(x, terms)]
    return functools.reduce(lambda a, b: a + b, parts)


def _dot_split_rhs(mat, x, terms):
    parts = [jnp.dot(mat, p, preferred_element_type=f32) for p in _bf16_terms(x, terms)]
    return functools.reduce(lambda a, b: a + b, parts)


def _load_state_pairs(sbd, s0_ref):
    sbd[...] = jnp.zeros_like(sbd)
    for h in range(N_HEADS):
        p, hh = divmod(h, 2)
        sbd[p, hh * HEAD_DIM:(hh + 1) * HEAD_DIM, hh * HEAD_DIM:(hh + 1) * HEAD_DIM] = s0_ref[h]


def _store_state_pairs(sn_ref, sbd):
    for h in range(N_HEADS):
        p, hh = divmod(h, 2)
        sn_ref[h] = sbd[p, hh * HEAD_DIM:(hh + 1) * HEAD_DIM, hh * HEAD_DIM:(hh + 1) * HEAD_DIM]


def _delta_chunks(q, k, v, g, beta, sbd, first_head, same_head, ts):
    ri, cj = _iota((ts, ts), 0), _iota((ts, ts), 1)
    same = (ri // CHUNK) == (cj // CHUNK)
    g_terms = _bf16_terms(g, 2)
    rsum = lambda mat: functools.reduce(lambda a, b: a + b,
                                        [jnp.dot(mat, t, preferred_element_type=f32) for t in g_terms])
    gcum = rsum((same & (cj <= ri)).astype(bf16))
    glast = rsum(same.astype(bf16))
    eg = jnp.exp(gcum)
    qg, kb, vb = q * eg, k * beta, v * beta
    kbe = kb * eg
    kd = k * jnp.exp(glast - gcum)
    dl = jnp.exp(glast)

    ii = _iota((CHUNK, LANES), 0)
    jj = _iota((CHUNK, LANES), 1) % HEAD_DIM
    eye2 = (ii == jj).astype(f32)
    ones = jnp.ones((CHUNK, CHUNK), bf16)
    n_chunks = ts // CHUNK
    where = [(slice(c * CHUNK, (c + 1) * CHUNK), slice(p * LANES, (p + 1) * LANES))
             for c in range(n_chunks) for p in range(N_PAIRS)]

    amats, qks = [], []
    for r, l in where:
        gc = gcum[r, l]
        grow = _dot_split_rhs(ones, eye2 * gc, 2)
        dm = jnp.where(ii >= jj, jnp.exp(jnp.minimum(gc - grow, 0.0)), 0.0)
        kst = _stack_heads(k[r, l], first_head).astype(bf16)
        kk = _dot_nt(kb[r, l].astype(bf16), kst)
        qks.append((_dot_nt(q[r, l].astype(bf16), kst) * dm).astype(bf16))
        amats.append(jnp.where(ii > jj, kk * dm, 0.0))

    out_rows = []
    for c in range(n_chunks):
        idxs = [c * N_PAIRS + p for p in range(N_PAIRS)]
        s_prev = [sbd[p] for p in range(N_PAIRS)]
        s_bf = [s.astype(bf16) for s in s_prev]
        rhs = [vb[where[i]] - jnp.dot(kbe[where[i]].astype(bf16), s_bf[p], preferred_element_type=f32)
               for p, i in enumerate(idxs)]
        vnews = _solve_unit_lower_pairs([(amats[i], x) for i, x in zip(idxs, rhs)], first_head)
        out_pairs = []
        for p, (i, vnew) in enumerate(zip(idxs, vnews)):
            r, l = where[i]
            vst = _stack_heads(vnew, first_head).astype(bf16)
            o = (jnp.dot(qg[r, l].astype(bf16), s_bf[p], preferred_element_type=f32)
                 + jnp.dot(qks[i], vst, preferred_element_type=f32))
            upd = _dot_tn(kd[r, l].astype(bf16), vnew.astype(bf16))
            sbd[p] = s_prev[p] * dl[c * CHUNK:c * CHUNK + 1, l] + jnp.where(same_head, upd, 0.0)
            out_pairs.append(o)
        out_rows.append(jnp.concatenate(out_pairs, axis=1))
    return jnp.concatenate(out_rows, axis=0)


def _delta_kernel(zqkv_ref, zz_ref, zab_ref, conv0_ref, s0_ref, cw_ref, alog_ref, dtb_ref, on_ref,
                  o_ref, convn_ref, sn_ref, xp, sbd, *, ts, n_valid, n_tiles, group):
    t = pl.program_id(1)
    for s in range(group):
        _delta_sequence(t, zqkv_ref.at[s], zz_ref.at[s], zab_ref.at[s], conv0_ref.at[s], s0_ref.at[s],
                        cw_ref, alog_ref, dtb_ref, on_ref, o_ref.at[s], convn_ref.at[s], sn_ref.at[s],
                        xp.at[s], sbd.at[s], ts=ts, n_valid=n_valid, n_tiles=n_tiles)


def _delta_sequence(t, zqkv_ref, zz_ref, zab_ref, conv0_ref, s0_ref, cw_ref, alog_ref, dtb_ref, on_ref,
                    o_ref, convn_ref, sn_ref, xp, sbd, *, ts, n_valid, n_tiles):
    lead = SUBLANES
    nv = min(ts, n_valid)

    @pl.when(t == 0)
    def _():
        xp[0:lead, :] = conv0_ref[...]
        _load_state_pairs(sbd, s0_ref)

    xp[lead:lead + ts, :] = zqkv_ref[...]
    cw = cw_ref[...]
    y = cw[0:1, :] * xp[lead - 3:lead - 3 + ts, :]
    for j in range(1, DELTA_CONV):
        y = y + cw[j:j + 1, :] * xp[lead - 3 + j:lead - 3 + j + ts, :]
    convn_ref[...] = xp[lead + nv - (DELTA_CONV - 1):lead + nv, :]
    xp[0:lead, :] = xp[ts:ts + lead, :]
    y = _silu(y)

    sum_mat = _same_head_matrix(GROUP_WIDTH, 1.0)
    yq, yk, v = y[:, :GROUP_WIDTH], y[:, GROUP_WIDTH:2 * GROUP_WIDTH], y[:, 2 * GROUP_WIDTH:]
    q = yq * lax.rsqrt(_head_sum(yq * yq, sum_mat) + EPS) * (HEAD_DIM ** -0.5)
    k = yk * lax.rsqrt(_head_sum(yk * yk, sum_mat) + EPS)

    zab = zab_ref[...]
    src = _iota((LANES, GROUP_WIDTH), 0)
    head = _iota((LANES, GROUP_WIDTH), 1) // HEAD_DIM
    zab_terms = _bf16_terms(zab, 2)
    pick = lambda sel: functools.reduce(
        lambda a, b: a + b, [jnp.dot(t, sel.astype(bf16), preferred_element_type=f32) for t in zab_terms])
    a_bc, b_bc = pick(src == head), pick(src == head + N_HEADS)
    xa = a_bc + dtb_ref[...]
    g = -jnp.exp(alog_ref[...]) * (jnp.maximum(xa, 0.0) + jnp.log1p(jnp.exp(-jnp.abs(xa))))
    beta = jax.nn.sigmoid(b_bc)
    if nv < ts:
        live = _iota((ts, 1), 0) < nv
        g, beta = jnp.where(live, g, 0.0), jnp.where(live, beta, 0.0)
        q, k, v = jnp.where(live, q, 0.0), jnp.where(live, k, 0.0), jnp.where(live, v, 0.0)

    first_head = _iota((1, LANES), 1) < HEAD_DIM
    same_head = (_iota((LANES, LANES), 0) // HEAD_DIM) == (_iota((LANES, LANES), 1) // HEAD_DIM)
    if n_valid == 1:
        eg = jnp.exp(g)
        qg, kbe, vb = q * eg, k * beta * eg, v * beta
        qk = _head_sum(q * k, sum_mat)
        out_pairs = []
        for p in range(N_PAIRS):
            l = slice(p * LANES, (p + 1) * LANES)
            s_prev = sbd[p]
            s_bf = s_prev.astype(bf16)
            vnew = vb[:, l] - jnp.dot(kbe[:, l].astype(bf16), s_bf, preferred_element_type=f32)
            out_pairs.append(jnp.dot(qg[:, l].astype(bf16), s_bf, preferred_element_type=f32) + qk[:, l] * vnew)
            upd = _dot_tn(k[:, l].astype(bf16), vnew.astype(bf16))
            sbd[p] = s_prev * eg[0:1, l] + jnp.where(same_head, upd, 0.0)
        o = jnp.concatenate(out_pairs, axis=1)
    else:
        o = _delta_chunks(q, k, v, g, beta, sbd, first_head, same_head, ts)
    mean_mat = _same_head_matrix(GROUP_WIDTH, 1.0 / HEAD_DIM)
    o = o * lax.rsqrt(_head_sum(o * o, mean_mat) + EPS) * on_ref[...]
    o_ref[...] = o * _silu(zz_ref[...])

    @pl.when(t == n_tiles - 1)
    def _():
        _store_state_pairs(sn_ref, sbd)


def _delta(z, n, seq, n_valid, conv0, s0, conv_w, alog_bc, dtb_bc, onorm_bc, ts, group=1):
    assert n_valid == seq or seq == ts
    n_tiles = seq // ts
    assert n % group == 0
    w3 = 3 * GROUP_WIDTH
    z3 = z.reshape(n, seq, Z_WIDTH)
    row = lambda width, blk: pl.BlockSpec((group, ts, width), lambda b, t, blk=blk: (b, t, blk))
    small = lambda a: pl.BlockSpec(a.shape, lambda b, t: (0, 0))
    state = pl.BlockSpec((group, N_HEADS, HEAD_DIM, HEAD_DIM), lambda b, t: (b, 0, 0, 0))
    o, convn, sn = pl.pallas_call(
        functools.partial(_delta_kernel, ts=ts, n_valid=n_valid, n_tiles=n_tiles, group=group),
        grid=(n // group, n_tiles),
        in_specs=[row(w3, ZB_QKV_BLK), row(GROUP_WIDTH, ZB_Z_BLK), row(LANES, ZAB_BLK),
                  pl.BlockSpec((group, SUBLANES, w3), lambda b, t: (b, 0, 0)), state,
                  small(conv_w), small(alog_bc), small(dtb_bc), small(onorm_bc)],
        out_specs=[pl.BlockSpec((group, ts, GROUP_WIDTH), lambda b, t: (b, t, 0)),
                   pl.BlockSpec((group, DELTA_CONV - 1, w3), lambda b, t: (b, 0, 0)), state],
        out_shape=[jax.ShapeDtypeStruct((n, seq, GROUP_WIDTH), f32),
                   jax.ShapeDtypeStruct((n, DELTA_CONV - 1, w3), f32),
                   jax.ShapeDtypeStruct((n, N_HEADS, HEAD_DIM, HEAD_DIM), f32)],
        scratch_shapes=[pltpu.VMEM((group, ts + SUBLANES, w3), f32),
                        pltpu.VMEM((group, N_PAIRS, LANES, LANES), f32)],
        compiler_params=_params(2), name="delta")(z3, z3, z3, conv0, s0, conv_w, alog_bc, dtb_bc, onorm_bc)
    return o.reshape(n * seq, GROUP_WIDTH), convn, sn


def _retention_kernel(zq_ref, zk_ref, zv_ref, zg_ref, dx_ref, dc_ref, db_ref, cos_ref, sin_ref, lg_ref, on_ref,
                      cwd_ref, s0_ref, dconv0_ref, o_ref, od_ref, sn_ref, dconvn_ref, sbd, dmask, xpd,
                      *, tc, n_valid, n_tiles, group):
    b, t = pl.program_id(0), pl.program_id(1)
    lg = lg_ref[...]

    @pl.when((b == 0) & (t == 0))
    def _():
        col = _iota((tc, 2 * tc), 1)
        diff = (_iota((tc, 2 * tc), 0) - col % tc).astype(f32)
        for p in range(N_PAIRS):
            lgp = lg[:, p * LANES:(p + 1) * LANES]
            rate = jnp.where(col < tc, lgp[:, 0:1], lgp[:, HEAD_DIM:HEAD_DIM + 1])
            dmask[p] = jnp.where(diff >= 0.0, jnp.exp(jnp.maximum(diff, 0.0) * rate), 0.0)

    for s in range(group):
        rows = pl.ds(s * tc, tc)
        _retention_sequence(t, zq_ref.at[rows], zk_ref.at[rows], zv_ref.at[rows], zg_ref.at[rows], dx_ref.at[rows],
                            dc_ref.at[rows], db_ref.at[rows], cos_ref, sin_ref, lg, on_ref, cwd_ref, s0_ref.at[s],
                            dconv0_ref.at[s], o_ref.at[rows], od_ref.at[rows], sn_ref.at[s], dconvn_ref.at[s],
                            sbd.at[s], dmask, xpd.at[s], tc=tc, n_valid=n_valid, n_tiles=n_tiles)


def _retention_sequence(t, zq_ref, zk_ref, zv_ref, zg_ref, dx_ref, dc_ref, db_ref, cos_ref, sin_ref, lg, on_ref,
                        cwd_ref, s0_ref, dconv0_ref, o_ref, od_ref, sn_ref, dconvn_ref, sbd, dmask, xpd,
                        *, tc, n_valid, n_tiles):
    lead = SUBLANES
    nv = min(tc, n_valid)

    @pl.when(t == 0)
    def _():
        _load_state_pairs(sbd, s0_ref)
        xpd[0:lead, :] = dconv0_ref[...]

    cos, sin = cos_ref[...], sin_ref[...]
    q = _rope_wide(zq_ref[...], cos, sin)
    k = _rope_wide(zk_ref[...], cos, sin) * (HEAD_DIM ** -0.5)
    v = zv_ref[...]
    if nv < tc:
        live = _iota((tc, 1), 0) < nv
        k, v = jnp.where(live, k, 0.0), jnp.where(live, v, 0.0)
    pos = _iota((tc, 1), 0).astype(f32)
    qd = q * jnp.exp((pos + 1.0) * lg)
    kdk = k * jnp.exp(jnp.maximum(float(nv - 1) - pos, 0.0) * lg)
    tile_decay = jnp.exp(float(nv) * lg)

    first_head = _iota((1, LANES), 1) < HEAD_DIM
    same_head = (_iota((LANES, LANES), 0) // HEAD_DIM) == (_iota((LANES, LANES), 1) // HEAD_DIM)
    outs = []
    for p in range(N_PAIRS):
        l = slice(p * LANES, (p + 1) * LANES)
        kst = _stack_heads(k[:, l], first_head).astype(bf16)
        vst = _stack_heads(v[:, l], first_head).astype(bf16)
        sc = _dot_nt(q[:, l].astype(bf16), kst) * dmask[p]
        s_prev = sbd[p]
        o = (jnp.dot(sc.astype(bf16), vst, preferred_element_type=f32)
             + jnp.dot(qd[:, l].astype(bf16), s_prev.astype(bf16), preferred_element_type=f32))
        upd = _dot_tn(kdk[:, l].astype(bf16), v[:, l].astype(bf16))
        sbd[p] = s_prev * tile_decay[:, l] + jnp.where(same_head, upd, 0.0)
        outs.append(o)
    o = jnp.concatenate(outs, axis=1)
    mean_mat = _same_head_matrix(GROUP_WIDTH, 1.0 / HEAD_DIM)
    o = o * lax.rsqrt(_head_sum(o * o, mean_mat) + EPS) * on_ref[...]
    o_ref[...] = o * _silu(zg_ref[...])

    xpd[lead:lead + tc, :] = dc_ref[...] * dx_ref[...]
    cwd = cwd_ref[...]
    yd = cwd[0:1, :] * xpd[lead - 2:lead - 2 + tc, :]
    for j in range(1, SHORT_CONV):
        yd = yd + cwd[j:j + 1, :] * xpd[lead - 2 + j:lead - 2 + j + tc, :]
    od_ref[...] = db_ref[...] * yd
    dconvn_ref[...] = xpd[lead + nv - (SHORT_CONV - 1):lead + nv, :]
    xpd[0:lead, :] = xpd[tc:tc + lead, :]

    @pl.when(t == n_tiles - 1)
    def _():
        _store_state_pairs(sn_ref, sbd)


def _retention(z, n, seq, n_valid, s0, dconv0, cos_t, sin_t, lg_bc, onorm_bc, conv_d, tc, group=1):
    assert n_valid == seq or seq == tc
    n_tiles = seq // tc
    assert n % group == 0 and (group == 1 or n_tiles == 1)
    row = lambda blk: pl.BlockSpec((group * tc, GROUP_WIDTH), lambda b, t, blk=blk: (b * n_tiles + t, blk))
    small = lambda a: pl.BlockSpec(a.shape, lambda b, t: (0, 0))
    tab = pl.BlockSpec((tc, LANES), lambda b, t: (t, 0))
    state = pl.BlockSpec((group, N_HEADS, HEAD_DIM, HEAD_DIM), lambda b, t: (b, 0, 0, 0))
    out_row = pl.BlockSpec((group * tc, GROUP_WIDTH), lambda b, t: (b * n_tiles + t, 0))
    out_rows = jax.ShapeDtypeStruct((n * seq, GROUP_WIDTH), f32)
    return pl.pallas_call(
        functools.partial(_retention_kernel, tc=tc, n_valid=n_valid, n_tiles=n_tiles, group=group),
        grid=(n // group, n_tiles),
        in_specs=[row(ZC_Q_BLK), row(ZC_K_BLK), row(ZC_V_BLK), row(ZC_G_BLK),
                  row(ZD_X_BLK), row(ZD_C_BLK), row(ZD_B_BLK), tab, tab,
                  small(lg_bc), small(onorm_bc), small(conv_d), state,
                  pl.BlockSpec((group, SUBLANES, GROUP_WIDTH), lambda b, t: (b, 0, 0))],
        out_specs=[out_row, out_row, state,
                   pl.BlockSpec((group, SHORT_CONV - 1, GROUP_WIDTH), lambda b, t: (b, 0, 0))],
        out_shape=[out_rows, out_rows, jax.ShapeDtypeStruct((n, N_HEADS, HEAD_DIM, HEAD_DIM), f32),
                   jax.ShapeDtypeStruct((n, SHORT_CONV - 1, GROUP_WIDTH), f32)],
        scratch_shapes=[pltpu.VMEM((group, N_PAIRS, LANES, LANES), f32), pltpu.VMEM((N_PAIRS, tc, 2 * tc), f32),
                        pltpu.VMEM((group, tc + SUBLANES, GROUP_WIDTH), f32)],
        compiler_params=_params(2), name="retention")(
            z, z, z, z, z, z, z, cos_t, sin_t, lg_bc, onorm_bc, conv_d, s0, dconv0)


def _rope_tables(pos, inv_freq):
    ang = pos.astype(f32)[:, None] * inv_freq[None, :]
    cos, sin = jnp.cos(ang), jnp.sin(ang)
    reps = LANES // HEAD_DIM
    return (jnp.tile(jnp.concatenate([cos, cos], axis=1), (1, reps)),
            jnp.tile(jnp.concatenate([-sin, sin], axis=1), (1, reps)))


def _per_head_lanes(v):
    return jnp.repeat(v.astype(f32), HEAD_DIM)[None, :]


def _tiled_lanes(v, width):
    return jnp.tile(v.astype(f32), width // HEAD_DIM)[None, :]


def _lead_pad(state):
    return jnp.pad(state, ((0, 0), (SUBLANES - state.shape[1], 0), (0, 0)))


def kernel(x_prompt, x_sample, cache_a_k, cache_a_v, state_b_conv, state_b_rec, state_c_rec, state_d_conv,
           norm_mix, w_in, q_norm_a, k_norm_a, conv_b, a_log_b, dt_bias_b, onorm_b, onorm_c, conv_d,
           w_out, norm_ffn, w_up, w_down):
    nb, seq, _ = x_prompt.shape
    nd, dseq, _ = x_sample.shape
    assert dseq == 1
    dpad = SUBLANES
    dgroup = max(g for g in (8, 4, 2, 1) if nd % g == 0)
    rope_freq = ROPE_THETA ** (-jnp.arange(0, HEAD_DIM, 2, dtype=f32) / HEAD_DIM)
    ret_freq = 1.0 / (10000.0 ** jnp.linspace(0.0, 1.0, HEAD_DIM // 2, dtype=f32))
    ret_lg = _per_head_lanes(jnp.log(1.0 - 2.0 ** (-5.0 - jnp.arange(N_HEADS, dtype=f32))))
    pos_p = jnp.arange(seq, dtype=jnp.int32)
    pos_d = PAST_LEN + jnp.arange(dpad, dtype=jnp.int32)
    cos_ap, sin_ap = _rope_tables(pos_p, rope_freq)
    cos_cp, sin_cp = _rope_tables(pos_p, ret_freq)
    cos_ad, sin_ad = _rope_tables(pos_d[:1], rope_freq)
    cos_cd, sin_cd = _rope_tables(pos_d, ret_freq)

    cache_kt = jnp.transpose(cache_a_k, (0, 1, 3, 4, 2))
    cache_vt = jnp.transpose(cache_a_v, (0, 1, 3, 4, 2))

    yp = x_prompt.reshape(nb * seq, D_MODEL)
    ys = x_sample.reshape(nd, D_MODEL)
    zeros_conv_b = jnp.zeros((nb, SUBLANES, 3 * GROUP_WIDTH), f32)
    zeros_conv_d = jnp.zeros((nb, SUBLANES, GROUP_WIDTH), f32)
    zeros_rec = jnp.zeros((nb, N_HEADS, HEAD_DIM, HEAD_DIM), f32)
    new_p, new_s = [], []
    dec_cache = prompt_cache = None
    w_pack = _pack_w_in(w_in)
    wo, wu, wd = w_out.astype(bf16), w_up.astype(bf16), w_down.astype(bf16)
    for l in range(DEPTH):
        g_mix, g_ffn = norm_mix[l][None, :], norm_ffn[l][None, :]
        qg, kg = _tiled_lanes(q_norm_a[l], LANES), _tiled_lanes(k_norm_a[l], LANES)
        qg3, kg3 = _tiled_lanes(q_norm_a[l], GROUP_WIDTH), _tiled_lanes(k_norm_a[l], GROUP_WIDTH)
        alog, dtb = _per_head_lanes(a_log_b[l]), _per_head_lanes(dt_bias_b[l])
        on_b, on_c = _tiled_lanes(onorm_b[l], GROUP_WIDTH), _tiled_lanes(onorm_c[l], GROUP_WIDTH)

        z = _inproj(yp, g_mix, w_pack, l, 512)
        oa, kt, vt = _attn_prompt(l, z, nb, seq, cos_ap, sin_ap, qg, kg, prompt_cache)
        prompt_cache = (kt, vt)
        ob, p_bconv, p_brec = _delta(z, nb, seq, seq, zeros_conv_b, zeros_rec, conv_b[l], alog, dtb, on_b, 256)
        oc, od, p_crec, p_dconv = _retention(z, nb, seq, seq, zeros_rec, zeros_conv_d, cos_cp, sin_cp, ret_lg,
                                             on_c, conv_d[l], 256)
        yp = _ffn(yp, oa, ob, oc, od, wo, g_ffn, wu, wd, l, 512)
        new_p.append((kt, vt, p_bconv, p_brec, p_crec, p_dconv))

        zs = _inproj(ys, g_mix, w_pack, l, nd)
        qa, ka, va = _attn_decode_prep(zs, cos_ad, sin_ad, qg3, kg3)
        col = lambda a: jnp.broadcast_to(a.reshape(nd, N_HEADS, HEAD_DIM, 1), (nd, N_HEADS, HEAD_DIM, LANES))
        s_kt, s_vt, oa_col = _attn_decode(l, cache_kt, cache_vt, col(qa), col(ka), col(va), dec_cache)
        dec_cache = (s_kt, s_vt)
        oa_s = oa_col[..., 0].reshape(nd, GROUP_WIDTH)
        zpad = jnp.zeros((nd, dpad, Z_WIDTH), f32).at[:, 0, :].set(zs).reshape(nd * dpad, Z_WIDTH)
        ob_s, s_bconv, s_brec = _delta(zpad, nd, dpad, 1, _lead_pad(state_b_conv[l]), state_b_rec[l], conv_b[l],
                                       alog, dtb, on_b, dpad, group=dgroup)
        oc_s, od_s, s_crec, s_dconv = _retention(zpad, nd, dpad, 1, state_c_rec[l], _lead_pad(state_d_conv[l]),
                                                 cos_cd, sin_cd, ret_lg, on_c, conv_d[l], dpad, group=dgroup)
        first = lambda a: a.reshape(nd, dpad, GROUP_WIDTH)[:, 0, :]
        ys = _ffn(ys, oa_s, first(ob_s), first(oc_s), first(od_s), wo, g_ffn, wu, wd, l, nd)
        new_s.append((s_bconv, s_brec, s_crec, s_dconv))

    keep = min(WIN_MAX, seq)
    cache_out = lambda i: jnp.transpose(
        prompt_cache[i].reshape(DEPTH, nb, N_HEADS, HEAD_DIM, keep), (0, 1, 4, 2, 3))
    stack_p = lambda i: jnp.stack([s[i] for s in new_p])
    stack_s = lambda i: jnp.stack([s[i] for s in new_s])
    return (yp.reshape(nb, seq, D_MODEL), ys.reshape(nd, 1, D_MODEL),
            cache_out(0), cache_out(1), stack_p(2), stack_p(3), stack_p(4), stack_p(5),
            jnp.transpose(dec_cache[0], (0, 1, 4, 2, 3)), jnp.transpose(dec_cache[1], (0, 1, 4, 2, 3)),
            stack_s(0), stack_s(1), stack_s(2), stack_s(3))
```

```python
import functools

import jax
import jax.numpy as jnp
from jax import lax
from jax.experimental import pallas as pl
from jax.experimental.pallas import tpu as pltpu

f32, bf16, i32 = jnp.float32, jnp.bfloat16, jnp.int32

D_MODEL = 1024
HEAD_DIM = 64
N_HEADS = 6
N_PAIRS = N_HEADS // 2
GROUP_WIDTH = N_HEADS * HEAD_DIM
D_FF = 4 * D_MODEL
DIL_PAIRS = ((128, 1), (512, 4), (2048, 16))
WIN_MAX = 2048
ATT_BLOCK = 128
ROPE_THETA = 10000.0
DELTA_CONV = 4
SHORT_CONV = 3
CHUNK = 64
EPS = 1e-6
NEG = -1e30
PAST_LEN = 16384
DEPTH = 2
LANES = 128
SUBLANES = 8
VMEM_LIMIT = 56 * 1024 * 1024

IN_WIDTH = 14 * GROUP_WIDTH + 2 * N_HEADS
AB_OFFSET_SRC = 7 * GROUP_WIDTH
Z_WIDTH = 14 * GROUP_WIDTH + LANES
ZB_QKV_BLK = 1
ZB_Z_BLK = 6
ZC_Q_BLK, ZC_K_BLK, ZC_V_BLK, ZC_G_BLK = 7, 8, 9, 10
ZD_X_BLK, ZD_C_BLK, ZD_B_BLK = 11, 12, 13
ZAB_BLK = 14 * GROUP_WIDTH // LANES


def _iota(shape, dim):
    return lax.broadcasted_iota(i32, shape, dim)


def _params(n_axes):
    return pltpu.CompilerParams(dimension_semantics=("arbitrary",) * n_axes, vmem_limit_bytes=VMEM_LIMIT)


def _same_head_matrix(width, value):
    r = _iota((width, width), 0) // HEAD_DIM
    c = _iota((width, width), 1) // HEAD_DIM
    return jnp.where(r == c, value, 0.0).astype(bf16)


def _head_sum(x, mat):
    return _dot_split_lhs(x, mat, 2)


def _rope_slab(x, cos, sin_signed):
    p = _iota((1, LANES), 1) % HEAD_DIM
    partner = jnp.where(p < HEAD_DIM // 2, pltpu.roll(x, LANES - HEAD_DIM // 2, 1), pltpu.roll(x, HEAD_DIM // 2, 1))
    return x * cos + partner * sin_signed


def _rope_wide(x, cos, sin_signed):
    return jnp.concatenate(
        [_rope_slab(x[:, LANES * i:LANES * (i + 1)], cos, sin_signed) for i in range(x.shape[1] // LANES)], axis=1)


def _stack_heads(x, first_head):
    return jnp.concatenate([jnp.where(first_head, x, 0.0), jnp.where(first_head, 0.0, x)], axis=0)


def _dot_nt(a, b):
    return lax.dot_general(a, b, (((1,), (1,)), ((), ())), preferred_element_type=f32)


def _dot_tn(a, b):
    return lax.dot_general(a, b, (((0,), (0,)), ((), ())), preferred_element_type=f32)


def _silu(x):
    return x * jax.nn.sigmoid(x)


def _inproj_kernel(x_ref, g_ref, w_ref, o_ref):
    x = x_ref[...]
    ms = jnp.mean(x * x, axis=-1, keepdims=True)
    xn = (x * lax.rsqrt(ms + EPS) * g_ref[...]).astype(bf16)
    step = 4 * LANES
    for c0 in range(0, Z_WIDTH, step):
        cw = min(step, Z_WIDTH - c0)
        o_ref[:, c0:c0 + cw] = jnp.dot(xn, w_ref[:, c0:c0 + cw], preferred_element_type=f32)


def _pack_w_in_kernel(lo_ref, hi_ref, ab_ref, o_ref, *, n_lo, n_hi):
    i = pl.program_id(0)

    def emit(src):
        for l in range(DEPTH):
            o_ref[l] = src(l).T.astype(bf16)

    @pl.when(i < n_lo)
    def _():
        emit(lambda l: lo_ref[:, l, :])

    @pl.when((i >= n_lo) & (i < n_lo + n_hi))
    def _():
        emit(lambda l: hi_ref[:, l, :])

    @pl.when(i == n_lo + n_hi)
    def _():
        live = _iota((LANES, D_MODEL), 0) < 2 * N_HEADS
        emit(lambda l: jnp.where(live, ab_ref[:, l, :], 0.0))


def _pack_w_in(w_in):
    wt = jnp.transpose(w_in, (2, 0, 1))
    n_ab = 2 * N_HEADS
    n_lo = AB_OFFSET_SRC // LANES
    n_hi = (IN_WIDTH - AB_OFFSET_SRC - n_ab) // LANES
    assert n_lo * LANES == AB_OFFSET_SRC and (n_lo + n_hi + 1) * LANES == Z_WIDTH
    rows = lambda start: pl.BlockSpec((pl.Element(LANES), pl.Element(DEPTH), pl.Element(D_MODEL)),
                                      lambda i: (start(i), 0, 0))
    return pl.pallas_call(
        functools.partial(_pack_w_in_kernel, n_lo=n_lo, n_hi=n_hi), grid=(Z_WIDTH // LANES,),
        in_specs=[rows(lambda i: LANES * jnp.minimum(i, n_lo - 1)),
                  rows(lambda i: AB_OFFSET_SRC + n_ab + LANES * jnp.clip(i - n_lo, 0, n_hi - 1)),
                  rows(lambda i: AB_OFFSET_SRC)],
        out_specs=pl.BlockSpec((DEPTH, D_MODEL, LANES), lambda i: (0, 0, i)),
        out_shape=jax.ShapeDtypeStruct((DEPTH, D_MODEL, Z_WIDTH), bf16),
        compiler_params=_params(1), name="pack_w_in")(wt, wt, wt)


def _layer_block(a, layer):
    return pl.BlockSpec((None,) + a.shape[1:], lambda *_: (layer, 0, 0), pipeline_mode=pl.Buffered(1))


def _inproj(x2d, gain, w_pack, layer, tm):
    m = x2d.shape[0]
    return pl.pallas_call(
        _inproj_kernel,
        grid=(m // tm,),
        in_specs=[pl.BlockSpec((tm, D_MODEL), lambda i: (i, 0)),
                  pl.BlockSpec((1, D_MODEL), lambda i: (0, 0)),
                  _layer_block(w_pack, layer)],
        out_specs=pl.BlockSpec((tm, Z_WIDTH), lambda i: (i, 0)),
        out_shape=jax.ShapeDtypeStruct((m, Z_WIDTH), f32),
        compiler_params=_params(1), name="inproj")(x2d, gain, w_pack)


def _ffn_kernel(x_ref, oa_ref, ob_ref, oc_ref, od_ref, wo_ref, g_ref, wu_ref, wd_ref, y_ref):
    mix = jnp.concatenate([r[...].astype(bf16) for r in (oa_ref, ob_ref, oc_ref, od_ref)], axis=1)
    h = x_ref[...] + jnp.dot(mix, wo_ref[...], preferred_element_type=f32)
    ms = jnp.mean(h * h, axis=-1, keepdims=True)
    hn = (h * lax.rsqrt(ms + EPS) * g_ref[...]).astype(bf16)
    y_ref[...] = h
    step = D_FF // 4
    for c in range(0, D_FF, step):
        u = jnp.dot(hn, wu_ref[:, c:c + step], preferred_element_type=f32)
        u = jnp.square(jnp.maximum(u, 0.0)).astype(bf16)
        y_ref[...] += jnp.dot(u, wd_ref[c:c + step, :], preferred_element_type=f32)


def _ffn(x2d, oa, ob, oc, od, w_out, gain, w_up, w_down, layer, tm):
    m = x2d.shape[0]
    row = lambda width: pl.BlockSpec((tm, width), lambda i: (i, 0))
    whole = lambda a: _layer_block(a, layer)
    return pl.pallas_call(
        _ffn_kernel,
        grid=(m // tm,),
        in_specs=[row(D_MODEL), row(GROUP_WIDTH), row(GROUP_WIDTH), row(GROUP_WIDTH), row(GROUP_WIDTH),
                  whole(w_out), pl.BlockSpec((1, D_MODEL), lambda i: (0, 0)), whole(w_up), whole(w_down)],
        out_specs=row(D_MODEL),
        out_shape=jax.ShapeDtypeStruct((m, D_MODEL), f32),
        compiler_params=_params(1), name="ffn")(x2d, oa, ob, oc, od, w_out, gain, w_up, w_down)


def _attn_prompt_kernel(q_ref, k_ref, v_ref, cos_ref, sin_ref, qg_ref, kg_ref, *rest,
                        seq, keep, wide_dil, wide_pitch, aliased):
    if aliased:
        rest = rest[2:]
    (o_ref, kt_ref, vt_ref, q_s, ks, acc_s, m_s, t_acc, t_m, qw, kw, vw, bias_first, bias_rest) = rest
    mean_mat = _same_head_matrix(LANES, 1.0 / HEAD_DIM)
    first_head = _iota((1, LANES), 1) < HEAD_DIM
    rc = 512
    blk = ATT_BLOCK

    def prep(i, carry):
        rows = pl.ds(pl.multiple_of(i * rc, rc), rc)
        cos, sin = cos_ref[rows, :], sin_ref[rows, :]
        q, k, v = q_ref[rows, :], k_ref[rows, :], v_ref[rows, :]
        q = q * lax.rsqrt(_head_sum(q * q, mean_mat) + EPS) * qg_ref[...]
        k = k * lax.rsqrt(_head_sum(k * k, mean_mat) + EPS) * kg_ref[...]
        q = _rope_slab(q, cos, sin) * (HEAD_DIM ** -0.5)
        q_s[rows, :] = q
        ks[rows, :] = _rope_slab(k, cos, sin)
        k = ks[rows, :]
        for g in range(rc // wide_dil):
            src = slice(g * wide_dil, (g + 1) * wide_dil)
            dst = pl.ds(pl.multiple_of(i * (rc // wide_dil * wide_pitch), SUBLANES) + g * wide_pitch, wide_dil)
            qw[dst, :], kw[dst, :], vw[dst, :] = q[src, :], k[src, :], v[src, :]
        return carry

    lax.fori_loop(0, seq // rc, prep, 0)

    for c in range(keep // rc):
        r0 = seq - keep + c * rc
        kt_ref[:, c * rc:(c + 1) * rc] = ks[r0:r0 + rc, :].T
        vt_ref[:, c * rc:(c + 1) * rc] = v_ref[r0:r0 + rc, :].T

    qi, kj = _iota((blk, blk), 0), _iota((blk, blk), 1)
    bias_first[...] = jnp.where(kj <= qi, 0.0, NEG)
    qi, kj = _iota((blk, 2 * blk), 0), _iota((blk, 2 * blk), 1)
    bias_rest[...] = jnp.where((kj >= qi) & (kj <= qi + blk), 0.0, NEG)

    def rows_of(st, n, dil):
        return pl.ds(st, n, stride=dil) if dil > 1 else pl.ds(st, n)

    def tile_group(tiles, dil, dst_acc, dst_m, pitch):
        staged = []
        for r, b, first in tiles:
            nk = blk if first else 2 * blk
            if dil == wide_dil:
                qrow = wide_pitch * (b * blk) + r
                krow = qrow if first else qrow - wide_pitch * blk
                rows, krows = pl.ds(qrow, blk, stride=wide_pitch), pl.ds(krow, nk, stride=wide_pitch)
                q, k, v = qw[rows, :], kw[krows, :].astype(bf16), vw[krows, :]
            else:
                start = r + b * (blk * dil)
                kstart = start if first else start - blk * dil
                rows, krows = rows_of(start, blk, dil), rows_of(kstart, nk, dil)
                q, k, v = q_s[rows, :], ks[krows, :].astype(bf16), v_ref[krows, :]
            qs = [jnp.where(first_head, q, 0.0).astype(bf16), jnp.where(first_head, 0.0, q).astype(bf16)]
            vs = [jnp.where(first_head, v, 1.0).astype(bf16), jnp.where(first_head, 1.0, v).astype(bf16)]
            bias = bias_first[...] if first else bias_rest[...]
            dst = pl.ds(pl.multiple_of(r * pitch + b * blk, SUBLANES), blk)
            staged.append((dst, vs, [_dot_nt(qs[hh], k) + bias for hh in range(2)]))
        probs = []
        for dst, vs, scores in staged:
            for hh in range(2):
                mt = jnp.max(scores[hh], axis=1, keepdims=True)
                probs.append((dst, hh, vs[hh], jnp.exp(scores[hh] - mt).astype(bf16), mt))
        for dst, hh, v, p, mt in probs:
            dst_acc[hh, dst, :] = jnp.dot(p, v, preferred_element_type=f32)
            dst_m[hh, dst, :] = jnp.broadcast_to(mt, (blk, LANES))

    merge_group = 4
    tiles_per_body = 8

    for bi, (window, dil) in enumerate(DIL_PAIRS):
        nb = seq // dil // blk
        dst_acc, dst_m = (acc_s, m_s) if bi == 0 else (t_acc, t_m)
        pitch = nb * blk + (SUBLANES if dil == wide_dil else 0)
        if nb >= tiles_per_body:
            assert nb % tiles_per_body == 0

            def per_residue(r, carry, dil=dil, nb=nb, dst_acc=dst_acc, dst_m=dst_m, pitch=pitch):
                tile_group([(r, b, b == 0) for b in range(tiles_per_body)], dil, dst_acc, dst_m, pitch)

                def per_group(g, c):
                    tile_group([(r, g * tiles_per_body + u, False) for u in range(tiles_per_body)],
                               dil, dst_acc, dst_m, pitch)
                    return c

                lax.fori_loop(1, nb // tiles_per_body, per_group, 0)
                return carry

            lax.fori_loop(0, dil, per_residue, 0)
        else:
            res_per_body = tiles_per_body // nb
            assert dil % res_per_body == 0

            def per_residues(g, carry, dil=dil, nb=nb, dst_acc=dst_acc, dst_m=dst_m, pitch=pitch, rpb=res_per_body):
                tile_group([(g * rpb + u, b, b == 0) for u in range(rpb) for b in range(nb)],
                           dil, dst_acc, dst_m, pitch)
                return carry

            lax.fori_loop(0, dil // res_per_body, per_residues, 0)

        if bi > 0 and dil != wide_dil:
            def merge(g, carry, dil=dil, nb=nb):
                pending = []
                for u in range(merge_group):
                    idx = g * merge_group + u
                    r, b = idx // nb, idx % nb
                    rows = rows_of(r + b * (blk * dil), blk, dil)
                    src = pl.ds(pl.multiple_of(idx * blk, blk), blk)
                    for hh in range(2):
                        mo, mt = m_s[hh, rows, :], t_m[hh, src, :]
                        mn = jnp.maximum(mo, mt)
                        an = acc_s[hh, rows, :] * jnp.exp(mo - mn) + t_acc[hh, src, :] * jnp.exp(mt - mn)
                        pending.append((hh, rows, an, mn))
                for hh, rows, an, mn in pending:
                    acc_s[hh, rows, :] = an
                    m_s[hh, rows, :] = mn
                return carry

            lax.fori_loop(0, seq // blk // merge_group, merge, 0)
        elif bi > 0:
            def merge_wide(g, carry, dil=dil, pitch=pitch):
                pending = []
                for u in range(merge_group):
                    t0 = pl.multiple_of((g * merge_group + u) * blk, blk)
                    rows = pl.ds(t0, blk)
                    for hh in range(2):
                        ta, tm = [], []
                        for v8 in range(blk // SUBLANES):
                            src = pl.ds(((v8 * SUBLANES) % dil) * pitch + t0 // dil + (v8 * SUBLANES) // dil,
                                        SUBLANES, stride=pitch)
                            ta.append(t_acc[hh, src, :])
                            tm.append(t_m[hh, src, :])
                        ta, mt = jnp.concatenate(ta, axis=0), jnp.concatenate(tm, axis=0)
                        mo = m_s[hh, rows, :]
                        mn = jnp.maximum(mo, mt)
                        pending.append(acc_s[hh, rows, :] * jnp.exp(mo - mn) + ta * jnp.exp(mt - mn))
                    a0, a1 = pending[-2:]
                    half = LANES // 2
                    o_ref[rows, :] = jnp.where(first_head, a0 / pltpu.roll(a0, half, 1), a1 / pltpu.roll(a1, half, 1))
                return carry

            lax.fori_loop(0, seq // blk // merge_group, merge_wide, 0)


def _attn_prompt(layer, z, n, seq, cos_t, sin_t, qgain, kgain, prev_out):
    keep = min(WIN_MAX, seq)
    for window, dil in DIL_PAIRS:
        assert window // dil == ATT_BLOCK and seq % (dil * ATT_BLOCK) == 0
    wide = [dil for _, dil in DIL_PAIRS if dil % SUBLANES == 0]
    assert len(wide) == 1 and ATT_BLOCK % wide[0] == 0 and DIL_PAIRS[-1][1] == wide[0]
    wide_dil, wide_pitch = wide[0], wide[0] + SUBLANES
    per_group = GROUP_WIDTH // LANES
    zspec = lambda base: pl.BlockSpec((seq, LANES), lambda b, p, base=base: (b, base * per_group + p),
                                      pipeline_mode=pl.Buffered(1))
    whole = lambda rows: pl.BlockSpec((rows, LANES), lambda b, p: (0, 0), pipeline_mode=pl.Buffered(1))
    slab = lambda: pltpu.VMEM((seq, LANES), f32)
    slab2 = lambda: pltpu.VMEM((2, seq, LANES), f32)
    branch = lambda: pltpu.VMEM((2, seq + wide_dil * SUBLANES, LANES), f32)
    padded = lambda: pltpu.VMEM((seq // wide_dil * wide_pitch, LANES), f32)
    in_specs = [zspec(0), zspec(1), zspec(2), whole(seq), whole(seq), whole(1), whole(1)]
    args = [z, z, z, cos_t, sin_t, qgain, kgain]
    aliases = {}
    if prev_out is not None:
        in_specs += [pl.BlockSpec(memory_space=pl.ANY), pl.BlockSpec(memory_space=pl.ANY)]
        args += list(prev_out)
        aliases = {len(args) - 2: 1, len(args) - 1: 2}
    cache_spec = pl.BlockSpec((None, None, LANES, keep), lambda b, p: (layer, b, p, 0))
    cache_shape = jax.ShapeDtypeStruct((DEPTH, n, GROUP_WIDTH, keep), f32)
    return pl.pallas_call(
        functools.partial(_attn_prompt_kernel, seq=seq, keep=keep, wide_dil=wide_dil, wide_pitch=wide_pitch,
                          aliased=prev_out is not None),
        grid=(n, N_PAIRS),
        in_specs=in_specs,
        out_specs=[pl.BlockSpec((seq, LANES), lambda b, p: (b, p)), cache_spec, cache_spec],
        out_shape=[jax.ShapeDtypeStruct((n * seq, GROUP_WIDTH), f32), cache_shape, cache_shape],
        input_output_aliases=aliases,
        scratch_shapes=[slab(), slab(), slab2(), slab2(), branch(), branch(), padded(), padded(), padded(),
                        pltpu.VMEM((ATT_BLOCK, ATT_BLOCK), f32), pltpu.VMEM((ATT_BLOCK, 2 * ATT_BLOCK), f32)],
        compiler_params=_params(2), name=f"attn_prompt_l{layer}")(*args)


def _attn_decode_prep_kernel(z_ref, cos_ref, sin_ref, qg_ref, kg_ref, q_ref, k_ref, v_ref):
    mean_mat = _same_head_matrix(GROUP_WIDTH, 1.0 / HEAD_DIM)
    z = z_ref[...]
    q, k = z[:, :GROUP_WIDTH], z[:, GROUP_WIDTH:2 * GROUP_WIDTH]
    q = q * lax.rsqrt(_head_sum(q * q, mean_mat) + EPS) * qg_ref[...]
    k = k * lax.rsqrt(_head_sum(k * k, mean_mat) + EPS) * kg_ref[...]
    q_ref[...] = _rope_wide(q, cos_ref[...], sin_ref[...]) * (HEAD_DIM ** -0.5)
    k_ref[...] = _rope_wide(k, cos_ref[...], sin_ref[...])
    v_ref[...] = z[:, 2 * GROUP_WIDTH:]


def _attn_decode_prep(z, cos_row, sin_row, qgain, kgain):
    m = z.shape[0]
    small = lambda a: pl.BlockSpec(a.shape, lambda i: (0, 0))
    out = jax.ShapeDtypeStruct((m, GROUP_WIDTH), f32)
    ospec = pl.BlockSpec((m, GROUP_WIDTH), lambda i: (0, 0))
    return pl.pallas_call(
        _attn_decode_prep_kernel, grid=(1,),
        in_specs=[pl.BlockSpec((m, 3 * GROUP_WIDTH), lambda i: (0, 0)), small(cos_row), small(sin_row),
                  small(qgain), small(kgain)],
        out_specs=[ospec, ospec, ospec], out_shape=[out, out, out],
        compiler_params=_params(1), name="attn_decode_prep")(z, cos_row, sin_row, qgain, kgain)


def _attn_decode_kernel(*refs, buf, aliased):
    if aliased:
        kc_ref, vc_ref, q_ref, kn_ref, vn_ref, _, _, ko_ref, vo_ref, o_ref = refs
    else:
        kc_ref, vc_ref, q_ref, kn_ref, vn_ref, ko_ref, vo_ref, o_ref = refs
    kc = kc_ref[...]
    vc = vc_ref[...]
    qc = q_ref[...][:, :, :1]
    knew = kn_ref[...][:, :, :1]
    vnew = vn_ref[...][:, :, :1]
    dist = buf - _iota((1, 1, buf), 2)
    mult = jnp.zeros((1, 1, buf), f32)
    for window, dil in DIL_PAIRS:
        mult = mult + jnp.where((dist <= window) & (dist % dil == 0), 1.0, 0.0)
    s = jnp.sum(kc * qc, axis=1, keepdims=True)
    s_new = jnp.sum(knew * qc, axis=1, keepdims=True)
    s = jnp.where(mult > 0.0, s, NEG)
    mx = jnp.maximum(jnp.max(s, axis=2, keepdims=True), s_new)
    p = mult * jnp.exp(s - mx)
    p_new = float(len(DIL_PAIRS)) * jnp.exp(s_new - mx)
    den = jnp.sum(p, axis=2, keepdims=True) + p_new
    o = (jnp.sum(vc * p, axis=2, keepdims=True) + vnew * p_new) / den
    o_ref[...] = jnp.broadcast_to(o, o_ref.shape)
    last = _iota(kc.shape, 2) == buf - 1
    ko_ref[...] = jnp.where(last, knew, pltpu.roll(kc, buf - 1, 2))
    vo_ref[...] = jnp.where(last, vnew, pltpu.roll(vc, buf - 1, 2))


def _attn_decode(layer, cache_kt, cache_vt, q_col, k_col, v_col, prev_out):
    depth, nb, nh, hd, buf = cache_kt.shape
    cache_spec = pl.BlockSpec((None, None, nh, hd, buf), lambda b: (layer, b, 0, 0, 0))
    col_spec = pl.BlockSpec((None, nh, hd, LANES), lambda b: (b, 0, 0, 0))
    in_specs = [cache_spec, cache_spec, col_spec, col_spec, col_spec]
    args = [cache_kt, cache_vt, q_col, k_col, v_col]
    aliases = {}
    if prev_out is not None:
        in_specs += [pl.BlockSpec(memory_space=pl.ANY), pl.BlockSpec(memory_space=pl.ANY)]
        args += list(prev_out)
        aliases = {5: 0, 6: 1}
    cache_shape = jax.ShapeDtypeStruct(cache_kt.shape, f32)
    return pl.pallas_call(
        functools.partial(_attn_decode_kernel, buf=buf, aliased=prev_out is not None),
        grid=(nb,), in_specs=in_specs,
        out_specs=[cache_spec, cache_spec, col_spec],
        out_shape=[cache_shape, cache_shape, jax.ShapeDtypeStruct((nb, nh, hd, LANES), f32)],
        input_output_aliases=aliases,
        compiler_params=_params(1), name=f"attn_decode_l{layer}")(*args)


SOLVE_BLOCK = 16


def _solve_unit_lower_pairs(systems, first_head):
    ng = CHUNK // SUBLANES
    gpb = SOLVE_BLOCK // SUBLANES
    col_in_head = _iota((SOLVE_BLOCK, LANES), 1) % HEAD_DIM
    split = lambda x: [x[SUBLANES * g:SUBLANES * (g + 1), :] for g in range(ng)]
    ags = [split(a) for a, _ in systems]
    xgs = [split(x) for _, x in systems]
    for blk in range(CHUNK // SOLVE_BLOCK):
        r0 = blk * SOLVE_BLOCK
        if blk > 0:
            for (a, _), xg in zip(systems, xgs):
                left = jnp.where(col_in_head < r0, a[r0:r0 + SOLVE_BLOCK, :], 0.0).astype(bf16)
                solved = _stack_heads(jnp.concatenate(xg, axis=0), first_head).astype(bf16)
                upd = jnp.dot(left, solved, preferred_element_type=f32)
                for g in range(gpb):
                    xg[blk * gpb + g] = xg[blk * gpb + g] - upd[SUBLANES * g:SUBLANES * (g + 1), :]
        for j in range(r0, r0 + SOLVE_BLOCK - 1):
            g0, r = divmod(j, SUBLANES)
            for ag, xg in zip(ags, xgs):
                xrow = jnp.broadcast_to(xg[g0][r:r + 1, :], (SUBLANES, LANES))
                for g in range(g0, (blk + 1) * gpb):
                    c0 = jnp.broadcast_to(ag[g][:, j:j + 1], (SUBLANES, LANES))
                    c1 = jnp.broadcast_to(ag[g][:, HEAD_DIM + j:HEAD_DIM + j + 1], (SUBLANES, LANES))
                    xg[g] = xg[g] - jnp.where(first_head, c0, c1) * xrow
    return [jnp.concatenate(xg, axis=0) for xg in xgs]


def _bf16_terms(x, terms):
    pieces, rest = [], x
    for t in range(terms):
        pieces.append(rest.astype(bf16))
        if t + 1 < terms:
            rest = rest - pieces[-1].astype(f32)
    return pieces


def _dot_split_lhs(x, mat, terms):
    parts = [jnp.dot(p, mat, preferred_element_type=f32) for p in _bf16_terms(x, terms)]
    return functools.reduce(lambda a, b: a + b, parts)


def _dot_split_rhs(mat, x, terms):
    parts = [jnp.dot(mat, p, preferred_element_type=f32) for p in _bf16_terms(x, terms)]
    return functools.reduce(lambda a, b: a + b, parts)


def _state_in(s0, group):
    layer = s0[1]
    return pl.BlockSpec((None, group, N_HEADS, HEAD_DIM, HEAD_DIM), lambda b, t: (layer, b, 0, 0, 0))


def _load_state_pairs(sbd, s0_ref):
    sbd[...] = jnp.zeros_like(sbd)
    for h in range(N_HEADS):
        p, hh = divmod(h, 2)
        sbd[p, hh * HEAD_DIM:(hh + 1) * HEAD_DIM, hh * HEAD_DIM:(hh + 1) * HEAD_DIM] = s0_ref[h]


def _store_state_pairs(sn_ref, sbd):
    for h in range(N_HEADS):
        p, hh = divmod(h, 2)
        sn_ref[h] = sbd[p, hh * HEAD_DIM:(hh + 1) * HEAD_DIM, hh * HEAD_DIM:(hh + 1) * HEAD_DIM]


def _delta_chunks(q, k, v, g, beta, sbd, first_head, same_head, ts):
    ri, cj = _iota((ts, ts), 0), _iota((ts, ts), 1)
    same = (ri // CHUNK) == (cj // CHUNK)
    g_terms = _bf16_terms(g, 2)
    rsum = lambda mat: functools.reduce(lambda a, b: a + b,
                                        [jnp.dot(mat, t, preferred_element_type=f32) for t in g_terms])
    gcum = rsum((same & (cj <= ri)).astype(bf16))
    glast = rsum(same.astype(bf16))
    eg = jnp.exp(gcum)
    qg, kb, vb = q * eg, k * beta, v * beta
    kbe = kb * eg
    kd = k * jnp.exp(glast - gcum)
    dl = jnp.exp(glast)

    ii = _iota((CHUNK, LANES), 0)
    jj = _iota((CHUNK, LANES), 1) % HEAD_DIM
    eye2 = (ii == jj).astype(f32)
    ones = jnp.ones((CHUNK, CHUNK), bf16)
    n_chunks = ts // CHUNK
    where = [(slice(c * CHUNK, (c + 1) * CHUNK), slice(p * LANES, (p + 1) * LANES))
             for c in range(n_chunks) for p in range(N_PAIRS)]

    amats, qks = [], []
    for r, l in where:
        gc = gcum[r, l]
        grow = _dot_split_rhs(ones, eye2 * gc, 2)
        dm = jnp.where(ii >= jj, jnp.exp(jnp.minimum(gc - grow, 0.0)), 0.0)
        kst = _stack_heads(k[r, l], first_head).astype(bf16)
        kk = _dot_nt(kb[r, l].astype(bf16), kst)
        qks.append((_dot_nt(q[r, l].astype(bf16), kst) * dm).astype(bf16))
        amats.append(jnp.where(ii > jj, kk * dm, 0.0))

    out_rows = []
    for c in range(n_chunks):
        idxs = [c * N_PAIRS + p for p in range(N_PAIRS)]
        s_prev = [sbd[p] for p in range(N_PAIRS)]
        s_bf = [s.astype(bf16) for s in s_prev]
        rhs = [vb[where[i]] - jnp.dot(kbe[where[i]].astype(bf16), s_bf[p], preferred_element_type=f32)
               for p, i in enumerate(idxs)]
        vnews = _solve_unit_lower_pairs([(amats[i], x) for i, x in zip(idxs, rhs)], first_head)
        out_pairs = []
        for p, (i, vnew) in enumerate(zip(idxs, vnews)):
            r, l = where[i]
            vst = _stack_heads(vnew, first_head).astype(bf16)
            o = (jnp.dot(qg[r, l].astype(bf16), s_bf[p], preferred_element_type=f32)
                 + jnp.dot(qks[i], vst, preferred_element_type=f32))
            upd = _dot_tn(kd[r, l].astype(bf16), vnew.astype(bf16))
            sbd[p] = s_prev[p] * dl[c * CHUNK:c * CHUNK + 1, l] + jnp.where(same_head, upd, 0.0)
            out_pairs.append(o)
        out_rows.append(jnp.concatenate(out_pairs, axis=1))
    return jnp.concatenate(out_rows, axis=0)


def _delta_kernel(zqkv_ref, zz_ref, zab_ref, conv0_ref, s0_ref, cw_ref, alog_ref, dtb_ref, on_ref,
                  o_ref, convn_ref, sn_ref, xp, sbd, *, ts, n_valid, n_tiles, group):
    t = pl.program_id(1)
    for s in range(group):
        _delta_sequence(t, zqkv_ref.at[s], zz_ref.at[s], zab_ref.at[s], conv0_ref.at[s], s0_ref.at[s],
                        cw_ref, alog_ref, dtb_ref, on_ref, o_ref.at[s], convn_ref.at[s], sn_ref.at[s],
                        xp.at[s], sbd.at[s], ts=ts, n_valid=n_valid, n_tiles=n_tiles)


def _delta_sequence(t, zqkv_ref, zz_ref, zab_ref, conv0_ref, s0_ref, cw_ref, alog_ref, dtb_ref, on_ref,
                    o_ref, convn_ref, sn_ref, xp, sbd, *, ts, n_valid, n_tiles):
    lead = SUBLANES
    nv = min(ts, n_valid)

    @pl.when(t == 0)
    def _():
        xp[0:lead, :] = conv0_ref[...]
        _load_state_pairs(sbd, s0_ref)

    xp[lead:lead + ts, :] = zqkv_ref[...]
    cw = cw_ref[...]
    y = cw[0:1, :] * xp[lead - 3:lead - 3 + ts, :]
    for j in range(1, DELTA_CONV):
        y = y + cw[j:j + 1, :] * xp[lead - 3 + j:lead - 3 + j + ts, :]
    convn_ref[...] = xp[lead + nv - (DELTA_CONV - 1):lead + nv, :]
    xp[0:lead, :] = xp[ts:ts + lead, :]
    y = _silu(y)

    sum_mat = _same_head_matrix(GROUP_WIDTH, 1.0)
    yq, yk, v = y[:, :GROUP_WIDTH], y[:, GROUP_WIDTH:2 * GROUP_WIDTH], y[:, 2 * GROUP_WIDTH:]
    q = yq * lax.rsqrt(_head_sum(yq * yq, sum_mat) + EPS) * (HEAD_DIM ** -0.5)
    k = yk * lax.rsqrt(_head_sum(yk * yk, sum_mat) + EPS)

    src = _iota((LANES, GROUP_WIDTH), 0)
    head = _iota((LANES, GROUP_WIDTH), 1) // HEAD_DIM
    zab_terms = _bf16_terms(zab_ref[...], 2)
    pick = lambda sel: functools.reduce(
        lambda a, b: a + b, [jnp.dot(t, sel.astype(bf16), preferred_element_type=f32) for t in zab_terms])
    a_bc, b_bc = pick(src == head), pick(src == head + N_HEADS)
    xa = a_bc + dtb_ref[...]
    g = -jnp.exp(alog_ref[...]) * (jnp.maximum(xa, 0.0) + jnp.log1p(jnp.exp(-jnp.abs(xa))))
    beta = jax.nn.sigmoid(b_bc)
    if nv < ts:
        live = _iota((ts, 1), 0) < nv
        g, beta = jnp.where(live, g, 0.0), jnp.where(live, beta, 0.0)
        q, k, v = jnp.where(live, q, 0.0), jnp.where(live, k, 0.0), jnp.where(live, v, 0.0)

    first_head = _iota((1, LANES), 1) < HEAD_DIM
    same_head = (_iota((LANES, LANES), 0) // HEAD_DIM) == (_iota((LANES, LANES), 1) // HEAD_DIM)
    if n_valid == 1:
        eg = jnp.exp(g)
        qg, kbe, vb = q * eg, k * beta * eg, v * beta
        qk = _head_sum(q * k, sum_mat)
        out_pairs = []
        for p in range(N_PAIRS):
            l = slice(p * LANES, (p + 1) * LANES)
            s_prev = sbd[p]
            s_bf = s_prev.astype(bf16)
            vnew = vb[:, l] - jnp.dot(kbe[:, l].astype(bf16), s_bf, preferred_element_type=f32)
            out_pairs.append(jnp.dot(qg[:, l].astype(bf16), s_bf, preferred_element_type=f32) + qk[:, l] * vnew)
            upd = _dot_tn(k[:, l].astype(bf16), vnew.astype(bf16))
            sbd[p] = s_prev * eg[0:1, l] + jnp.where(same_head, upd, 0.0)
        o = jnp.concatenate(out_pairs, axis=1)
    else:
        o = _delta_chunks(q, k, v, g, beta, sbd, first_head, same_head, ts)
    o = o * lax.rsqrt(_head_sum(o * o, sum_mat) * (1.0 / HEAD_DIM) + EPS) * on_ref[...]
    o_ref[...] = o * _silu(zz_ref[...])

    @pl.when(t == n_tiles - 1)
    def _():
        _store_state_pairs(sn_ref, sbd)


def _delta(z, n, seq, n_valid, conv0, s0, conv_w, alog_bc, dtb_bc, onorm_bc, ts, group=1):
    assert n_valid == seq or seq == ts
    n_tiles = seq // ts
    assert n % group == 0
    w3 = 3 * GROUP_WIDTH
    z3 = z.reshape(n, seq, Z_WIDTH)
    row = lambda width, blk: pl.BlockSpec((group, ts, width), lambda b, t, blk=blk: (b, t, blk))
    small = lambda a: pl.BlockSpec(a.shape, lambda b, t: (0, 0))
    state = pl.BlockSpec((group, N_HEADS, HEAD_DIM, HEAD_DIM), lambda b, t: (b, 0, 0, 0))
    o, convn, sn = pl.pallas_call(
        functools.partial(_delta_kernel, ts=ts, n_valid=n_valid, n_tiles=n_tiles, group=group),
        grid=(n // group, n_tiles),
        in_specs=[row(w3, ZB_QKV_BLK), row(GROUP_WIDTH, ZB_Z_BLK), row(LANES, ZAB_BLK),
                  pl.BlockSpec((group, SUBLANES, w3), lambda b, t: (b, 0, 0)), _state_in(s0, group),
                  small(conv_w), small(alog_bc), small(dtb_bc), small(onorm_bc)],
        out_specs=[pl.BlockSpec((group, ts, GROUP_WIDTH), lambda b, t: (b, t, 0)),
                   pl.BlockSpec((group, DELTA_CONV - 1, w3), lambda b, t: (b, 0, 0)), state],
        out_shape=[jax.ShapeDtypeStruct((n, seq, GROUP_WIDTH), f32),
                   jax.ShapeDtypeStruct((n, DELTA_CONV - 1, w3), f32),
                   jax.ShapeDtypeStruct((n, N_HEADS, HEAD_DIM, HEAD_DIM), f32)],
        scratch_shapes=[pltpu.VMEM((group, ts + SUBLANES, w3), f32),
                        pltpu.VMEM((group, N_PAIRS, LANES, LANES), f32)],
        compiler_params=_params(2), name="delta")(z3, z3, z3, conv0, s0[0], conv_w, alog_bc, dtb_bc, onorm_bc)
    return o.reshape(n * seq, GROUP_WIDTH), convn, sn


def _retention_kernel(zq_ref, zk_ref, zv_ref, zg_ref, dx_ref, dc_ref, db_ref, cos_ref, sin_ref, lg_ref, on_ref,
                      cwd_ref, s0_ref, dconv0_ref, o_ref, od_ref, sn_ref, dconvn_ref, sbd, dmask, xpd,
                      *, tc, n_valid, n_tiles, group):
    b, t = pl.program_id(0), pl.program_id(1)
    lg = lg_ref[...]

    @pl.when((b == 0) & (t == 0))
    def _():
        col = _iota((tc, 2 * tc), 1)
        diff = (_iota((tc, 2 * tc), 0) - col % tc).astype(f32)
        for p in range(N_PAIRS):
            lgp = lg[:, p * LANES:(p + 1) * LANES]
            rate = jnp.where(col < tc, lgp[:, 0:1], lgp[:, HEAD_DIM:HEAD_DIM + 1])
            dmask[p] = jnp.where(diff >= 0.0, jnp.exp(jnp.maximum(diff, 0.0) * rate), 0.0)

    for s in range(group):
        rows = pl.ds(s * tc, tc)
        _retention_sequence(t, zq_ref.at[rows], zk_ref.at[rows], zv_ref.at[rows], zg_ref.at[rows], dx_ref.at[rows],
                            dc_ref.at[rows], db_ref.at[rows], cos_ref, sin_ref, lg, on_ref, cwd_ref, s0_ref.at[s],
                            dconv0_ref.at[s], o_ref.at[rows], od_ref.at[rows], sn_ref.at[s], dconvn_ref.at[s],
                            sbd.at[s], dmask, xpd.at[s], tc=tc, n_valid=n_valid, n_tiles=n_tiles)


def _retention_sequence(t, zq_ref, zk_ref, zv_ref, zg_ref, dx_ref, dc_ref, db_ref, cos_ref, sin_ref, lg, on_ref,
                        cwd_ref, s0_ref, dconv0_ref, o_ref, od_ref, sn_ref, dconvn_ref, sbd, dmask, xpd,
                        *, tc, n_valid, n_tiles):
    lead = SUBLANES
    nv = min(tc, n_valid)

    @pl.when(t == 0)
    def _():
        _load_state_pairs(sbd, s0_ref)
        xpd[0:lead, :] = dconv0_ref[...]

    cos, sin = cos_ref[...], sin_ref[...]
    q = _rope_wide(zq_ref[...], cos, sin)
    k = _rope_wide(zk_ref[...], cos, sin) * (HEAD_DIM ** -0.5)
    v = zv_ref[...]
    if nv < tc:
        live = _iota((tc, 1), 0) < nv
        k, v = jnp.where(live, k, 0.0), jnp.where(live, v, 0.0)
    pos = _iota((tc, 1), 0).astype(f32)
    qd = q * jnp.exp((pos + 1.0) * lg)
    kdk = k * jnp.exp(jnp.maximum(float(nv - 1) - pos, 0.0) * lg)
    tile_decay = jnp.exp(float(nv) * lg)

    first_head = _iota((1, LANES), 1) < HEAD_DIM
    same_head = (_iota((LANES, LANES), 0) // HEAD_DIM) == (_iota((LANES, LANES), 1) // HEAD_DIM)
    outs = []
    for p in range(N_PAIRS):
        l = slice(p * LANES, (p + 1) * LANES)
        kst = _stack_heads(k[:, l], first_head).astype(bf16)
        vst = _stack_heads(v[:, l], first_head).astype(bf16)
        sc = _dot_nt(q[:, l].astype(bf16), kst) * dmask[p]
        s_prev = sbd[p]
        o = (jnp.dot(sc.astype(bf16), vst, preferred_element_type=f32)
             + jnp.dot(qd[:, l].astype(bf16), s_prev.astype(bf16), preferred_element_type=f32))
        upd = _dot_tn(kdk[:, l].astype(bf16), v[:, l].astype(bf16))
        sbd[p] = s_prev * tile_decay[:, l] + jnp.where(same_head, upd, 0.0)
        outs.append(o)
    o = jnp.concatenate(outs, axis=1)
    mean_mat = _same_head_matrix(GROUP_WIDTH, 1.0 / HEAD_DIM)
    o = o * lax.rsqrt(_head_sum(o * o, mean_mat) + EPS) * on_ref[...]
    o_ref[...] = o * _silu(zg_ref[...])

    xpd[lead:lead + tc, :] = dc_ref[...] * dx_ref[...]
    cwd = cwd_ref[...]
    yd = cwd[0:1, :] * xpd[lead - 2:lead - 2 + tc, :]
    for j in range(1, SHORT_CONV):
        yd = yd + cwd[j:j + 1, :] * xpd[lead - 2 + j:lead - 2 + j + tc, :]
    od_ref[...] = db_ref[...] * yd
    dconvn_ref[...] = xpd[lead + nv - (SHORT_CONV - 1):lead + nv, :]
    xpd[0:lead, :] = xpd[tc:tc + lead, :]

    @pl.when(t == n_tiles - 1)
    def _():
        _store_state_pairs(sn_ref, sbd)


def _retention(z, n, seq, n_valid, s0, dconv0, cos_t, sin_t, lg_bc, onorm_bc, conv_d, tc, group=1):
    assert n_valid == seq or seq == tc
    n_tiles = seq // tc
    assert n % group == 0 and (group == 1 or n_tiles == 1)
    row = lambda blk: pl.BlockSpec((group * tc, GROUP_WIDTH), lambda b, t, blk=blk: (b * n_tiles + t, blk))
    small = lambda a: pl.BlockSpec(a.shape, lambda b, t: (0, 0))
    tab = pl.BlockSpec((tc, LANES), lambda b, t: (t, 0))
    state = pl.BlockSpec((group, N_HEADS, HEAD_DIM, HEAD_DIM), lambda b, t: (b, 0, 0, 0))
    out_row = pl.BlockSpec((group * tc, GROUP_WIDTH), lambda b, t: (b * n_tiles + t, 0))
    out_rows = jax.ShapeDtypeStruct((n * seq, GROUP_WIDTH), f32)
    return pl.pallas_call(
        functools.partial(_retention_kernel, tc=tc, n_valid=n_valid, n_tiles=n_tiles, group=group),
        grid=(n // group, n_tiles),
        in_specs=[row(ZC_Q_BLK), row(ZC_K_BLK), row(ZC_V_BLK), row(ZC_G_BLK),
                  row(ZD_X_BLK), row(ZD_C_BLK), row(ZD_B_BLK), tab, tab,
                  small(lg_bc), small(onorm_bc), small(conv_d), _state_in(s0, group),
                  pl.BlockSpec((group, SUBLANES, GROUP_WIDTH), lambda b, t: (b, 0, 0))],
        out_specs=[out_row, out_row, state,
                   pl.BlockSpec((group, SHORT_CONV - 1, GROUP_WIDTH), lambda b, t: (b, 0, 0))],
        out_shape=[out_rows, out_rows, jax.ShapeDtypeStruct((n, N_HEADS, HEAD_DIM, HEAD_DIM), f32),
                   jax.ShapeDtypeStruct((n, SHORT_CONV - 1, GROUP_WIDTH), f32)],
        scratch_shapes=[pltpu.VMEM((group, N_PAIRS, LANES, LANES), f32), pltpu.VMEM((N_PAIRS, tc, 2 * tc), f32),
                        pltpu.VMEM((group, tc + SUBLANES, GROUP_WIDTH), f32)],
        compiler_params=_params(2), name="retention")(
            z, z, z, z, z, z, z, cos_t, sin_t, lg_bc, onorm_bc, conv_d, s0[0], dconv0)


def _rope_tables(pos, inv_freq):
    ang = pos.astype(f32)[:, None] * inv_freq[None, :]
    cos, sin = jnp.cos(ang), jnp.sin(ang)
    reps = LANES // HEAD_DIM
    return (jnp.tile(jnp.concatenate([cos, cos], axis=1), (1, reps)),
            jnp.tile(jnp.concatenate([-sin, sin], axis=1), (1, reps)))


def _per_head_lanes(v):
    return jnp.repeat(v.astype(f32), HEAD_DIM)[None, :]


def _tiled_lanes(v, width):
    return jnp.tile(v.astype(f32), width // HEAD_DIM)[None, :]


def _lead_pad(state):
    return jnp.pad(state, ((0, 0), (SUBLANES - state.shape[1], 0), (0, 0)))


def kernel(x_prompt, x_sample, cache_a_k, cache_a_v, state_b_conv, state_b_rec, state_c_rec, state_d_conv,
           norm_mix, w_in, q_norm_a, k_norm_a, conv_b, a_log_b, dt_bias_b, onorm_b, onorm_c, conv_d,
           w_out, norm_ffn, w_up, w_down):
    nb, seq, _ = x_prompt.shape
    nd, dseq, _ = x_sample.shape
    assert dseq == 1
    dpad = SUBLANES
    dgroup = max(g for g in (8, 4, 2, 1) if nd % g == 0)
    rope_freq = ROPE_THETA ** (-jnp.arange(0, HEAD_DIM, 2, dtype=f32) / HEAD_DIM)
    ret_freq = 1.0 / (10000.0 ** jnp.linspace(0.0, 1.0, HEAD_DIM // 2, dtype=f32))
    ret_lg = _per_head_lanes(jnp.log(1.0 - 2.0 ** (-5.0 - jnp.arange(N_HEADS, dtype=f32))))
    pos_p = jnp.arange(seq, dtype=jnp.int32)
    pos_d = PAST_LEN + jnp.arange(dpad, dtype=jnp.int32)
    cos_ap, sin_ap = _rope_tables(pos_p, rope_freq)
    cos_cp, sin_cp = _rope_tables(pos_p, ret_freq)
    cos_ad, sin_ad = _rope_tables(pos_d[:1], rope_freq)
    cos_cd, sin_cd = _rope_tables(pos_d, ret_freq)

    cache_kt = jnp.transpose(cache_a_k, (0, 1, 3, 4, 2))
    cache_vt = jnp.transpose(cache_a_v, (0, 1, 3, 4, 2))

    yp = x_prompt.reshape(nb * seq, D_MODEL)
    ys = x_sample.reshape(nd, D_MODEL)
    zeros_conv_b = jnp.zeros((nb, SUBLANES, 3 * GROUP_WIDTH), f32)
    zeros_conv_d = jnp.zeros((nb, SUBLANES, GROUP_WIDTH), f32)
    zeros_rec = (jnp.zeros((1, nb, N_HEADS, HEAD_DIM, HEAD_DIM), f32), 0)
    new_p, new_s = [], []
    dec_cache = prompt_cache = None
    w_pack = _pack_w_in(w_in)
    wo, wu, wd = w_out.astype(bf16), w_up.astype(bf16), w_down.astype(bf16)
    for l in range(DEPTH):
        g_mix, g_ffn = norm_mix[l][None, :], norm_ffn[l][None, :]
        qg, kg = _tiled_lanes(q_norm_a[l], LANES), _tiled_lanes(k_norm_a[l], LANES)
        qg3, kg3 = _tiled_lanes(q_norm_a[l], GROUP_WIDTH), _tiled_lanes(k_norm_a[l], GROUP_WIDTH)
        alog, dtb = _per_head_lanes(a_log_b[l]), _per_head_lanes(dt_bias_b[l])
        on_b, on_c = _tiled_lanes(onorm_b[l], GROUP_WIDTH), _tiled_lanes(onorm_c[l], GROUP_WIDTH)

        z = _inproj(yp, g_mix, w_pack, l, 512)
        oa, kt, vt = _attn_prompt(l, z, nb, seq, cos_ap, sin_ap, qg, kg, prompt_cache)
        prompt_cache = (kt, vt)
        ob, p_bconv, p_brec = _delta(z, nb, seq, seq, zeros_conv_b, zeros_rec, conv_b[l], alog, dtb, on_b, 256)
        oc, od, p_crec, p_dconv = _retention(z, nb, seq, seq, zeros_rec, zeros_conv_d, cos_cp, sin_cp, ret_lg,
                                             on_c, conv_d[l], 256)
        yp = _ffn(yp, oa, ob, oc, od, wo, g_ffn, wu, wd, l, 512)
        new_p.append((kt, vt, p_bconv, p_brec, p_crec, p_dconv))

        zs = _inproj(ys, g_mix, w_pack, l, nd)
        qa, ka, va = _attn_decode_prep(zs, cos_ad, sin_ad, qg3, kg3)
        col = lambda a: jnp.broadcast_to(a.reshape(nd, N_HEADS, HEAD_DIM, 1), (nd, N_HEADS, HEAD_DIM, LANES))
        s_kt, s_vt, oa_col = _attn_decode(l, cache_kt, cache_vt, col(qa), col(ka), col(va), dec_cache)
        dec_cache = (s_kt, s_vt)
        oa_s = oa_col[..., 0].reshape(nd, GROUP_WIDTH)
        zpad = jnp.zeros((nd, dpad, Z_WIDTH), f32).at[:, 0, :].set(zs).reshape(nd * dpad, Z_WIDTH)
        ob_s, s_bconv, s_brec = _delta(zpad, nd, dpad, 1, _lead_pad(state_b_conv[l]), (state_b_rec, l), conv_b[l],
                                       alog, dtb, on_b, dpad, group=dgroup)
        oc_s, od_s, s_crec, s_dconv = _retention(zpad, nd, dpad, 1, (state_c_rec, l), _lead_pad(state_d_conv[l]),
                                                 cos_cd, sin_cd, ret_lg, on_c, conv_d[l], dpad, group=dgroup)
        first = lambda a: a.reshape(nd, dpad, GROUP_WIDTH)[:, 0, :]
        ys = _ffn(ys, oa_s, first(ob_s), first(oc_s), first(od_s), wo, g_ffn, wu, wd, l, nd)
        new_s.append((s_bconv, s_brec, s_crec, s_dconv))

    keep = min(WIN_MAX, seq)
    cache_out = lambda i: jnp.transpose(
        prompt_cache[i].reshape(DEPTH, nb, N_HEADS, HEAD_DIM, keep), (0, 1, 4, 2, 3))
    stack_p = lambda i: jnp.stack([s[i] for s in new_p])
    stack_s = lambda i: jnp.stack([s[i] for s in new_s])
    return (yp.reshape(nb, seq, D_MODEL), ys.reshape(nd, 1, D_MODEL),
            cache_out(0), cache_out(1), stack_p(2), stack_p(3), stack_p(4), stack_p(5),
            jnp.transpose(dec_cache[0], (0, 1, 4, 2, 3)), jnp.transpose(dec_cache[1], (0, 1, 4, 2, 3)),
            stack_s(0), stack_s(1), stack_s(2), stack_s(3))
```

```python
import functools

import jax
import jax.numpy as jnp
from jax import lax
from jax.experimental import pallas as pl
from jax.experimental.pallas import tpu as pltpu

f32, bf16, i32 = jnp.float32, jnp.bfloat16, jnp.int32

D_MODEL = 1024
HEAD_DIM = 64
N_HEADS = 6
N_PAIRS = N_HEADS // 2
GROUP_WIDTH = N_HEADS * HEAD_DIM
D_FF = 4 * D_MODEL
DIL_PAIRS = ((128, 1), (512, 4), (2048, 16))
WIN_MAX = 2048
ATT_BLOCK = 128
ROPE_THETA = 10000.0
DELTA_CONV = 4
SHORT_CONV = 3
CHUNK = 64
EPS = 1e-6
NEG = -1e30
PAST_LEN = 16384
DEPTH = 2
LANES = 128
SUBLANES = 8
VMEM_LIMIT = 56 * 1024 * 1024

IN_WIDTH = 14 * GROUP_WIDTH + 2 * N_HEADS
AB_OFFSET_SRC = 7 * GROUP_WIDTH
Z_WIDTH = 14 * GROUP_WIDTH + LANES
ZB_QKV_BLK = 1
ZB_Z_BLK = 6
ZC_Q_BLK, ZC_K_BLK, ZC_V_BLK, ZC_G_BLK = 7, 8, 9, 10
ZD_X_BLK, ZD_C_BLK, ZD_B_BLK = 11, 12, 13
ZAB_BLK = 14 * GROUP_WIDTH // LANES


def _iota(shape, dim):
    return lax.broadcasted_iota(i32, shape, dim)


def _params(n_axes):
    return pltpu.CompilerParams(dimension_semantics=("arbitrary",) * n_axes, vmem_limit_bytes=VMEM_LIMIT)


def _same_head_matrix(width, value):
    r = _iota((width, width), 0) // HEAD_DIM
    c = _iota((width, width), 1) // HEAD_DIM
    return jnp.where(r == c, value, 0.0).astype(bf16)


def _head_sum(x, mat, terms=2):
    return _dot_split_lhs(x, mat, terms)


def _head_sum_sq(x, mat):
    return _dot_split_lhs(x * x, mat, 1)


def _rope_slab(x, cos, sin_signed):
    p = _iota((1, LANES), 1) % HEAD_DIM
    partner = jnp.where(p < HEAD_DIM // 2, pltpu.roll(x, LANES - HEAD_DIM // 2, 1), pltpu.roll(x, HEAD_DIM // 2, 1))
    return x * cos + partner * sin_signed


def _rope_wide(x, cos, sin_signed):
    return jnp.concatenate(
        [_rope_slab(x[:, LANES * i:LANES * (i + 1)], cos, sin_signed) for i in range(x.shape[1] // LANES)], axis=1)


def _stack_heads(x, first_head):
    return jnp.concatenate([jnp.where(first_head, x, 0.0), jnp.where(first_head, 0.0, x)], axis=0)


def _dot_nt(a, b):
    return lax.dot_general(a, b, (((1,), (1,)), ((), ())), preferred_element_type=f32)


def _dot_tn(a, b):
    return lax.dot_general(a, b, (((0,), (0,)), ((), ())), preferred_element_type=f32)


def _silu(x):
    return x * jax.nn.sigmoid(x)


def _inproj_kernel(x_ref, g_ref, w_ref, o_ref):
    x = x_ref[...]
    ms = jnp.mean(x * x, axis=-1, keepdims=True)
    xn = (x * lax.rsqrt(ms + EPS) * g_ref[...]).astype(bf16)
    step = 4 * LANES
    for c0 in range(0, Z_WIDTH, step):
        cw = min(step, Z_WIDTH - c0)
        o_ref[:, c0:c0 + cw] = jnp.dot(xn, w_ref[:, c0:c0 + cw], preferred_element_type=f32)


def _pack_w_in_kernel(lo_ref, hi_ref, ab_ref, o_ref, *, n_lo, n_hi):
    i = pl.program_id(0)

    def emit(src):
        for l in range(DEPTH):
            o_ref[l] = src(l).T.astype(bf16)

    @pl.when(i < n_lo)
    def _():
        emit(lambda l: lo_ref[:, l, :])

    @pl.when((i >= n_lo) & (i < n_lo + n_hi))
    def _():
        emit(lambda l: hi_ref[:, l, :])

    @pl.when(i == n_lo + n_hi)
    def _():
        live = _iota((LANES, D_MODEL), 0) < 2 * N_HEADS
        emit(lambda l: jnp.where(live, ab_ref[:, l, :], 0.0))


def _pack_w_in(w_in):
    wt = jnp.transpose(w_in, (2, 0, 1))
    n_ab = 2 * N_HEADS
    n_lo = AB_OFFSET_SRC // LANES
    n_hi = (IN_WIDTH - AB_OFFSET_SRC - n_ab) // LANES
    assert n_lo * LANES == AB_OFFSET_SRC and (n_lo + n_hi + 1) * LANES == Z_WIDTH
    rows = lambda start: pl.BlockSpec((pl.Element(LANES), pl.Element(DEPTH), pl.Element(D_MODEL)),
                                      lambda i: (start(i), 0, 0))
    return pl.pallas_call(
        functools.partial(_pack_w_in_kernel, n_lo=n_lo, n_hi=n_hi), grid=(Z_WIDTH // LANES,),
        in_specs=[rows(lambda i: LANES * jnp.minimum(i, n_lo - 1)),
                  rows(lambda i: AB_OFFSET_SRC + n_ab + LANES * jnp.clip(i - n_lo, 0, n_hi - 1)),
                  rows(lambda i: AB_OFFSET_SRC)],
        out_specs=pl.BlockSpec((DEPTH, D_MODEL, LANES), lambda i: (0, 0, i)),
        out_shape=jax.ShapeDtypeStruct((DEPTH, D_MODEL, Z_WIDTH), bf16),
        compiler_params=_params(1), name="pack_w_in")(wt, wt, wt)


def _layer_block(a, layer):
    return pl.BlockSpec((None,) + a.shape[1:], lambda *_: (layer, 0, 0), pipeline_mode=pl.Buffered(1))


def _inproj(x2d, gain, w_pack, layer, tm):
    m = x2d.shape[0]
    return pl.pallas_call(
        _inproj_kernel,
        grid=(m // tm,),
        in_specs=[pl.BlockSpec((tm, D_MODEL), lambda i: (i, 0)),
                  pl.BlockSpec((1, D_MODEL), lambda i: (0, 0)),
                  _layer_block(w_pack, layer)],
        out_specs=pl.BlockSpec((tm, Z_WIDTH), lambda i: (i, 0)),
        out_shape=jax.ShapeDtypeStruct((m, Z_WIDTH), f32),
        compiler_params=_params(1), name="inproj")(x2d, gain, w_pack)


def _ffn_kernel(x_ref, oa_ref, ob_ref, oc_ref, od_ref, wo_ref, g_ref, wu_ref, wd_ref, y_ref):
    mix = jnp.concatenate([r[...].astype(bf16) for r in (oa_ref, ob_ref, oc_ref, od_ref)], axis=1)
    h = x_ref[...] + jnp.dot(mix, wo_ref[...], preferred_element_type=f32)
    ms = jnp.mean(h * h, axis=-1, keepdims=True)
    hn = (h * lax.rsqrt(ms + EPS) * g_ref[...]).astype(bf16)
    y_ref[...] = h
    step = D_FF // 4
    for c in range(0, D_FF, step):
        u = jnp.dot(hn, wu_ref[:, c:c + step], preferred_element_type=f32)
        u = jnp.square(jnp.maximum(u, 0.0)).astype(bf16)
        y_ref[...] += jnp.dot(u, wd_ref[c:c + step, :], preferred_element_type=f32)


def _ffn(x2d, oa, ob, oc, od, w_out, gain, w_up, w_down, layer, tm):
    m = x2d.shape[0]
    row = lambda width: pl.BlockSpec((tm, width), lambda i: (i, 0))
    whole = lambda a: _layer_block(a, layer)
    return pl.pallas_call(
        _ffn_kernel,
        grid=(m // tm,),
        in_specs=[row(D_MODEL), row(GROUP_WIDTH), row(GROUP_WIDTH), row(GROUP_WIDTH), row(GROUP_WIDTH),
                  whole(w_out), pl.BlockSpec((1, D_MODEL), lambda i: (0, 0)), whole(w_up), whole(w_down)],
        out_specs=row(D_MODEL),
        out_shape=jax.ShapeDtypeStruct((m, D_MODEL), f32),
        compiler_params=_params(1), name="ffn")(x2d, oa, ob, oc, od, w_out, gain, w_up, w_down)


def _attn_prompt_kernel(q_ref, k_ref, v_ref, cos_ref, sin_ref, qg_ref, kg_ref, *rest,
                        seq, keep, wide_dil, wide_pitch, aliased):
    if aliased:
        rest = rest[2:]
    (o_ref, kt_ref, vt_ref, q_s, ks, acc_s, m_s, t_acc, t_m, qw, kw, vw, bias_first, bias_rest) = rest
    mean_mat = _same_head_matrix(LANES, 1.0 / HEAD_DIM)
    first_head = _iota((1, LANES), 1) < HEAD_DIM
    rc = 512
    blk = ATT_BLOCK

    def prep(i, carry):
        rows = pl.ds(pl.multiple_of(i * rc, rc), rc)
        cos, sin = cos_ref[rows, :], sin_ref[rows, :]
        q, k, v = q_ref[rows, :], k_ref[rows, :], v_ref[rows, :]
        q = q * lax.rsqrt(_head_sum_sq(q, mean_mat) + EPS) * qg_ref[...]
        k = k * lax.rsqrt(_head_sum_sq(k, mean_mat) + EPS) * kg_ref[...]
        q = _rope_slab(q, cos, sin) * (HEAD_DIM ** -0.5)
        q_s[rows, :] = q
        ks[rows, :] = _rope_slab(k, cos, sin)
        k = ks[rows, :]
        for g in range(rc // wide_dil):
            src = slice(g * wide_dil, (g + 1) * wide_dil)
            dst = pl.ds(pl.multiple_of(i * (rc // wide_dil * wide_pitch), SUBLANES) + g * wide_pitch, wide_dil)
            qw[dst, :], kw[dst, :], vw[dst, :] = q[src, :], k[src, :], v[src, :]
        return carry

    lax.fori_loop(0, seq // rc, prep, 0)

    for c in range(keep // rc):
        r0 = seq - keep + c * rc
        kt_ref[:, c * rc:(c + 1) * rc] = ks[r0:r0 + rc, :].T
        vt_ref[:, c * rc:(c + 1) * rc] = v_ref[r0:r0 + rc, :].T

    qi, kj = _iota((blk, blk), 0), _iota((blk, blk), 1)
    bias_first[...] = jnp.where(kj <= qi, 0.0, NEG)
    qi, kj = _iota((blk, 2 * blk), 0), _iota((blk, 2 * blk), 1)
    bias_rest[...] = jnp.where((kj >= qi) & (kj <= qi + blk), 0.0, NEG)

    def rows_of(st, n, dil):
        return pl.ds(st, n, stride=dil) if dil > 1 else pl.ds(st, n)

    def tile_group(tiles, dil, dst_acc, dst_m, pitch):
        staged = []
        for r, b, first in tiles:
            nk = blk if first else 2 * blk
            if dil == wide_dil:
                qrow = wide_pitch * (b * blk) + r
                krow = qrow if first else qrow - wide_pitch * blk
                rows, krows = pl.ds(qrow, blk, stride=wide_pitch), pl.ds(krow, nk, stride=wide_pitch)
                q, k, v = qw[rows, :], kw[krows, :].astype(bf16), vw[krows, :]
            else:
                start = r + b * (blk * dil)
                kstart = start if first else start - blk * dil
                rows, krows = rows_of(start, blk, dil), rows_of(kstart, nk, dil)
                q, k, v = q_s[rows, :], ks[krows, :].astype(bf16), v_ref[krows, :]
            qs = [jnp.where(first_head, q, 0.0).astype(bf16), jnp.where(first_head, 0.0, q).astype(bf16)]
            vs = [jnp.where(first_head, v, 1.0).astype(bf16), jnp.where(first_head, 1.0, v).astype(bf16)]
            bias = bias_first[...] if first else bias_rest[...]
            dst = pl.ds(pl.multiple_of(r * pitch + b * blk, SUBLANES), blk)
            staged.append((dst, vs, [_dot_nt(qs[hh], k) + bias for hh in range(2)]))
        probs = []
        for dst, vs, scores in staged:
            for hh in range(2):
                mt = jnp.max(scores[hh], axis=1, keepdims=True)
                probs.append((dst, hh, vs[hh], jnp.exp(scores[hh] - mt).astype(bf16), mt))
        for dst, hh, v, p, mt in probs:
            dst_acc[hh, dst, :] = jnp.dot(p, v, preferred_element_type=f32)
            dst_m[hh, dst, :] = jnp.broadcast_to(mt, (blk, LANES))

    merge_group = 4
    tiles_per_body = 16

    for bi, (window, dil) in enumerate(DIL_PAIRS):
        nb = seq // dil // blk
        dst_acc, dst_m = (acc_s, m_s) if bi == 0 else (t_acc, t_m)
        pitch = nb * blk + (SUBLANES if dil == wide_dil else 0)
        if nb >= tiles_per_body:
            assert nb % tiles_per_body == 0

            def per_residue(r, carry, dil=dil, nb=nb, dst_acc=dst_acc, dst_m=dst_m, pitch=pitch):
                tile_group([(r, b, b == 0) for b in range(tiles_per_body)], dil, dst_acc, dst_m, pitch)

                def per_group(g, c):
                    tile_group([(r, g * tiles_per_body + u, False) for u in range(tiles_per_body)],
                               dil, dst_acc, dst_m, pitch)
                    return c

                lax.fori_loop(1, nb // tiles_per_body, per_group, 0)
                return carry

            lax.fori_loop(0, dil, per_residue, 0)
        else:
            res_per_body = tiles_per_body // nb
            assert dil % res_per_body == 0

            def per_residues(g, carry, dil=dil, nb=nb, dst_acc=dst_acc, dst_m=dst_m, pitch=pitch, rpb=res_per_body):
                tile_group([(g * rpb + u, b, b == 0) for u in range(rpb) for b in range(nb)],
                           dil, dst_acc, dst_m, pitch)
                return carry

            lax.fori_loop(0, dil // res_per_body, per_residues, 0)

        if bi > 0 and dil != wide_dil:
            def merge(g, carry, dil=dil, nb=nb):
                pending = []
                for u in range(merge_group):
                    idx = g * merge_group + u
                    r, b = idx // nb, idx % nb
                    rows = rows_of(r + b * (blk * dil), blk, dil)
                    src = pl.ds(pl.multiple_of(idx * blk, blk), blk)
                    for hh in range(2):
                        mo, mt = m_s[hh, rows, :], t_m[hh, src, :]
                        mn = jnp.maximum(mo, mt)
                        an = acc_s[hh, rows, :] * jnp.exp(mo - mn) + t_acc[hh, src, :] * jnp.exp(mt - mn)
                        pending.append((hh, rows, an, mn))
                for hh, rows, an, mn in pending:
                    acc_s[hh, rows, :] = an
                    m_s[hh, rows, :] = mn
                return carry

            lax.fori_loop(0, seq // blk // merge_group, merge, 0)
        elif bi > 0:
            def merge_wide(g, carry, dil=dil, pitch=pitch):
                pending = []
                for u in range(merge_group):
                    t0 = pl.multiple_of((g * merge_group + u) * blk, blk)
                    rows = pl.ds(t0, blk)
                    for hh in range(2):
                        ta, tm = [], []
                        for v8 in range(blk // SUBLANES):
                            src = pl.ds(((v8 * SUBLANES) % dil) * pitch + t0 // dil + (v8 * SUBLANES) // dil,
                                        SUBLANES, stride=pitch)
                            ta.append(t_acc[hh, src, :])
                            tm.append(t_m[hh, src, :])
                        ta, mt = jnp.concatenate(ta, axis=0), jnp.concatenate(tm, axis=0)
                        mo = m_s[hh, rows, :]
                        mn = jnp.maximum(mo, mt)
                        pending.append(acc_s[hh, rows, :] * jnp.exp(mo - mn) + ta * jnp.exp(mt - mn))
                    a0, a1 = pending[-2:]
                    half = LANES // 2
                    o_ref[rows, :] = jnp.where(first_head, a0 / pltpu.roll(a0, half, 1), a1 / pltpu.roll(a1, half, 1))
                return carry

            lax.fori_loop(0, seq // blk // merge_group, merge_wide, 0)


def _attn_prompt(layer, z, n, seq, cos_t, sin_t, qgain, kgain, prev_out):
    keep = min(WIN_MAX, seq)
    for window, dil in DIL_PAIRS:
        assert window // dil == ATT_BLOCK and seq % (dil * ATT_BLOCK) == 0
    wide = [dil for _, dil in DIL_PAIRS if dil % SUBLANES == 0]
    assert len(wide) == 1 and ATT_BLOCK % wide[0] == 0 and DIL_PAIRS[-1][1] == wide[0]
    wide_dil, wide_pitch = wide[0], wide[0] + SUBLANES
    per_group = GROUP_WIDTH // LANES
    zspec = lambda base: pl.BlockSpec((seq, LANES), lambda b, p, base=base: (b, base * per_group + p),
                                      pipeline_mode=pl.Buffered(1))
    whole = lambda rows: pl.BlockSpec((rows, LANES), lambda b, p: (0, 0), pipeline_mode=pl.Buffered(1))
    slab = lambda: pltpu.VMEM((seq, LANES), f32)
    slab2 = lambda: pltpu.VMEM((2, seq, LANES), f32)
    branch = lambda: pltpu.VMEM((2, seq + wide_dil * SUBLANES, LANES), f32)
    padded = lambda: pltpu.VMEM((seq // wide_dil * wide_pitch, LANES), f32)
    in_specs = [zspec(0), zspec(1), zspec(2), whole(seq), whole(seq), whole(1), whole(1)]
    args = [z, z, z, cos_t, sin_t, qgain, kgain]
    aliases = {}
    if prev_out is not None:
        in_specs += [pl.BlockSpec(memory_space=pl.ANY), pl.BlockSpec(memory_space=pl.ANY)]
        args += list(prev_out)
        aliases = {len(args) - 2: 1, len(args) - 1: 2}
    cache_spec = pl.BlockSpec((None, None, LANES, keep), lambda b, p: (layer, b, p, 0))
    cache_shape = jax.ShapeDtypeStruct((DEPTH, n, GROUP_WIDTH, keep), f32)
    return pl.pallas_call(
        functools.partial(_attn_prompt_kernel, seq=seq, keep=keep, wide_dil=wide_dil, wide_pitch=wide_pitch,
                          aliased=prev_out is not None),
        grid=(n, N_PAIRS),
        in_specs=in_specs,
        out_specs=[pl.BlockSpec((seq, LANES), lambda b, p: (b, p)), cache_spec, cache_spec],
        out_shape=[jax.ShapeDtypeStruct((n * seq, GROUP_WIDTH), f32), cache_shape, cache_shape],
        input_output_aliases=aliases,
        scratch_shapes=[slab(), slab(), slab2(), slab2(), branch(), branch(), padded(), padded(), padded(),
                        pltpu.VMEM((ATT_BLOCK, ATT_BLOCK), f32), pltpu.VMEM((ATT_BLOCK, 2 * ATT_BLOCK), f32)],
        compiler_params=_params(2), name=f"attn_prompt_l{layer}")(*args)


def _attn_decode_prep_kernel(z_ref, cos_ref, sin_ref, qg_ref, kg_ref, q_ref, k_ref, v_ref):
    mean_mat = _same_head_matrix(GROUP_WIDTH, 1.0 / HEAD_DIM)
    z = z_ref[...]
    q, k = z[:, :GROUP_WIDTH], z[:, GROUP_WIDTH:2 * GROUP_WIDTH]
    q = q * lax.rsqrt(_head_sum_sq(q, mean_mat) + EPS) * qg_ref[...]
    k = k * lax.rsqrt(_head_sum_sq(k, mean_mat) + EPS) * kg_ref[...]
    q_ref[...] = _rope_wide(q, cos_ref[...], sin_ref[...]) * (HEAD_DIM ** -0.5)
    k_ref[...] = _rope_wide(k, cos_ref[...], sin_ref[...])
    v_ref[...] = z[:, 2 * GROUP_WIDTH:]


def _attn_decode_prep(z, cos_row, sin_row, qgain, kgain):
    m = z.shape[0]
    small = lambda a: pl.BlockSpec(a.shape, lambda i: (0, 0))
    out = jax.ShapeDtypeStruct((m, GROUP_WIDTH), f32)
    ospec = pl.BlockSpec((m, GROUP_WIDTH), lambda i: (0, 0))
    return pl.pallas_call(
        _attn_decode_prep_kernel, grid=(1,),
        in_specs=[pl.BlockSpec((m, 3 * GROUP_WIDTH), lambda i: (0, 0)), small(cos_row), small(sin_row),
                  small(qgain), small(kgain)],
        out_specs=[ospec, ospec, ospec], out_shape=[out, out, out],
        compiler_params=_params(1), name="attn_decode_prep")(z, cos_row, sin_row, qgain, kgain)


def _attn_decode_kernel(*refs, buf, aliased):
    if aliased:
        kc_ref, vc_ref, q_ref, kn_ref, vn_ref, _, _, ko_ref, vo_ref, o_ref = refs
    else:
        kc_ref, vc_ref, q_ref, kn_ref, vn_ref, ko_ref, vo_ref, o_ref = refs
    kc = kc_ref[...]
    vc = vc_ref[...]
    qc = q_ref[...][:, :, :1]
    knew = kn_ref[...][:, :, :1]
    vnew = vn_ref[...][:, :, :1]
    dist = buf - _iota((1, 1, buf), 2)
    mult = jnp.zeros((1, 1, buf), f32)
    for window, dil in DIL_PAIRS:
        mult = mult + jnp.where((dist <= window) & (dist % dil == 0), 1.0, 0.0)
    s = jnp.sum(kc * qc, axis=1, keepdims=True)
    s_new = jnp.sum(knew * qc, axis=1, keepdims=True)
    s = jnp.where(mult > 0.0, s, NEG)
    mx = jnp.maximum(jnp.max(s, axis=2, keepdims=True), s_new)
    p = mult * jnp.exp(s - mx)
    p_new = float(len(DIL_PAIRS)) * jnp.exp(s_new - mx)
    den = jnp.sum(p, axis=2, keepdims=True) + p_new
    o = (jnp.sum(vc * p, axis=2, keepdims=True) + vnew * p_new) / den
    o_ref[...] = jnp.broadcast_to(o, o_ref.shape)
    last = _iota(kc.shape, 2) == buf - 1
    ko_ref[...] = jnp.where(last, knew, pltpu.roll(kc, buf - 1, 2))
    vo_ref[...] = jnp.where(last, vnew, pltpu.roll(vc, buf - 1, 2))


def _attn_decode(layer, cache_kt, cache_vt, q_col, k_col, v_col, prev_out):
    depth, nb, nh, hd, buf = cache_kt.shape
    cache_spec = pl.BlockSpec((None, None, nh, hd, buf), lambda b: (layer, b, 0, 0, 0))
    col_spec = pl.BlockSpec((None, nh, hd, LANES), lambda b: (b, 0, 0, 0))
    in_specs = [cache_spec, cache_spec, col_spec, col_spec, col_spec]
    args = [cache_kt, cache_vt, q_col, k_col, v_col]
    aliases = {}
    if prev_out is not None:
        in_specs += [pl.BlockSpec(memory_space=pl.ANY), pl.BlockSpec(memory_space=pl.ANY)]
        args += list(prev_out)
        aliases = {5: 0, 6: 1}
    cache_shape = jax.ShapeDtypeStruct(cache_kt.shape, f32)
    return pl.pallas_call(
        functools.partial(_attn_decode_kernel, buf=buf, aliased=prev_out is not None),
        grid=(nb,), in_specs=in_specs,
        out_specs=[cache_spec, cache_spec, col_spec],
        out_shape=[cache_shape, cache_shape, jax.ShapeDtypeStruct((nb, nh, hd, LANES), f32)],
        input_output_aliases=aliases,
        compiler_params=_params(1), name=f"attn_decode_l{layer}")(*args)


SOLVE_BLOCK = 16


def _solve_unit_lower_pairs(systems, first_head):
    ng = CHUNK // SUBLANES
    gpb = SOLVE_BLOCK // SUBLANES
    col_in_head = _iota((SOLVE_BLOCK, LANES), 1) % HEAD_DIM
    split = lambda x: [x[SUBLANES * g:SUBLANES * (g + 1), :] for g in range(ng)]
    ags = [split(a) for a, _ in systems]
    xgs = [split(x) for _, x in systems]
    for blk in range(CHUNK // SOLVE_BLOCK):
        r0 = blk * SOLVE_BLOCK
        if blk > 0:
            for (a, _), xg in zip(systems, xgs):
                left = jnp.where(col_in_head < r0, a[r0:r0 + SOLVE_BLOCK, :], 0.0).astype(bf16)
                solved = _stack_heads(jnp.concatenate(xg, axis=0), first_head).astype(bf16)
                upd = jnp.dot(left, solved, preferred_element_type=f32)
                for g in range(gpb):
                    xg[blk * gpb + g] = xg[blk * gpb + g] - upd[SUBLANES * g:SUBLANES * (g + 1), :]
        for j in range(r0, r0 + SOLVE_BLOCK - 1):
            g0, r = divmod(j, SUBLANES)
            for ag, xg in zip(ags, xgs):
                xrow = jnp.broadcast_to(xg[g0][r:r + 1, :], (SUBLANES, LANES))
                for g in range(g0, (blk + 1) * gpb):
                    c0 = jnp.broadcast_to(ag[g][:, j:j + 1], (SUBLANES, LANES))
                    c1 = jnp.broadcast_to(ag[g][:, HEAD_DIM + j:HEAD_DIM + j + 1], (SUBLANES, LANES))
                    xg[g] = xg[g] - jnp.where(first_head, c0, c1) * xrow
    return [jnp.concatenate(xg, axis=0) for xg in xgs]


def _bf16_terms(x, terms):
    pieces, rest = [], x
    for t in range(terms):
        pieces.append(rest.astype(bf16))
        if t + 1 < terms:
            rest = rest - pieces[-1].astype(f32)
    return pieces


def _dot_split_lhs(x, mat, terms):
    parts = [jnp.dot(p, mat, preferred_element_type=f32) for p in _bf16_terms(x, terms)]
    return functools.reduce(lambda a, b: a + b, parts)


def _dot_split_rhs(mat, x, terms):
    parts = [jnp.dot(mat, p, preferred_element_type=f32) for p in _bf16_terms(x, terms)]
    return functools.reduce(lambda a, b: a + b, parts)


def _state_in(s0, group):
    layer = s0[1]
    return pl.BlockSpec((None, group, N_HEADS, HEAD_DIM, HEAD_DIM), lambda b, t: (layer, b, 0, 0, 0))


def _load_state_pairs(sbd, s0_ref):
    sbd[...] = jnp.zeros_like(sbd)
    for h in range(N_HEADS):
        p, hh = divmod(h, 2)
        sbd[p, hh * HEAD_DIM:(hh + 1) * HEAD_DIM, hh * HEAD_DIM:(hh + 1) * HEAD_DIM] = s0_ref[h]


def _store_state_pairs(sn_ref, sbd):
    for h in range(N_HEADS):
        p, hh = divmod(h, 2)
        sn_ref[h] = sbd[p, hh * HEAD_DIM:(hh + 1) * HEAD_DIM, hh * HEAD_DIM:(hh + 1) * HEAD_DIM]


def _delta_chunks(q, k, v, g, beta, sbd, first_head, same_head, ts):
    ri, cj = _iota((ts, ts), 0), _iota((ts, ts), 1)
    same = (ri // CHUNK) == (cj // CHUNK)
    g_terms = _bf16_terms(g, 2)
    rsum = lambda mat: functools.reduce(lambda a, b: a + b,
                                        [jnp.dot(mat, t, preferred_element_type=f32) for t in g_terms])
    gcum = rsum((same & (cj <= ri)).astype(bf16))
    glast = rsum(same.astype(bf16))
    eg = jnp.exp(gcum)
    qg, kb, vb = q * eg, k * beta, v * beta
    kbe = kb * eg
    kd = k * jnp.exp(glast - gcum)
    dl = jnp.exp(glast)

    ii = _iota((CHUNK, LANES), 0)
    jj = _iota((CHUNK, LANES), 1) % HEAD_DIM
    eye2 = (ii == jj).astype(f32)
    ones = jnp.ones((CHUNK, CHUNK), bf16)
    n_chunks = ts // CHUNK
    where = [(slice(c * CHUNK, (c + 1) * CHUNK), slice(p * LANES, (p + 1) * LANES))
             for c in range(n_chunks) for p in range(N_PAIRS)]

    amats, qks = [], []
    for r, l in where:
        gc = gcum[r, l]
        grow = _dot_split_rhs(ones, eye2 * gc, 2)
        dm = jnp.where(ii >= jj, jnp.exp(jnp.minimum(gc - grow, 0.0)), 0.0)
        kst = _stack_heads(k[r, l], first_head).astype(bf16)
        kk = _dot_nt(kb[r, l].astype(bf16), kst)
        qks.append((_dot_nt(q[r, l].astype(bf16), kst) * dm).astype(bf16))
        amats.append(jnp.where(ii > jj, kk * dm, 0.0))

    out_rows = []
    for c in range(n_chunks):
        idxs = [c * N_PAIRS + p for p in range(N_PAIRS)]
        s_prev = [sbd[p] for p in range(N_PAIRS)]
        s_bf = [s.astype(bf16) for s in s_prev]
        rhs = [vb[where[i]] - jnp.dot(kbe[where[i]].astype(bf16), s_bf[p], preferred_element_type=f32)
               for p, i in enumerate(idxs)]
        vnews = _solve_unit_lower_pairs([(amats[i], x) for i, x in zip(idxs, rhs)], first_head)
        out_pairs = []
        for p, (i, vnew) in enumerate(zip(idxs, vnews)):
            r, l = where[i]
            vst = _stack_heads(vnew, first_head).astype(bf16)
            o = (jnp.dot(qg[r, l].astype(bf16), s_bf[p], preferred_element_type=f32)
                 + jnp.dot(qks[i], vst, preferred_element_type=f32))
            upd = _dot_tn(kd[r, l].astype(bf16), vnew.astype(bf16))
            sbd[p] = s_prev[p] * dl[c * CHUNK:c * CHUNK + 1, l] + jnp.where(same_head, upd, 0.0)
            out_pairs.append(o)
        out_rows.append(jnp.concatenate(out_pairs, axis=1))
    return jnp.concatenate(out_rows, axis=0)


def _delta_kernel(zqkv_ref, zz_ref, zab_ref, conv0_ref, s0_ref, cw_ref, alog_ref, dtb_ref, on_ref,
                  o_ref, convn_ref, sn_ref, xp, sbd, *, ts, n_valid, n_tiles, group):
    t = pl.program_id(1)
    if n_valid == 1:
        _delta_single_tokens(zqkv_ref, zz_ref, zab_ref, conv0_ref, s0_ref, cw_ref, alog_ref, dtb_ref, on_ref,
                             o_ref, convn_ref, sn_ref, xp, sbd, ts=ts, group=group)
        return
    for s in range(group):
        _delta_sequence(t, zqkv_ref.at[s], zz_ref.at[s], zab_ref.at[s], conv0_ref.at[s], s0_ref.at[s],
                        cw_ref, alog_ref, dtb_ref, on_ref, o_ref.at[s], convn_ref.at[s], sn_ref.at[s],
                        xp.at[s], sbd.at[s], ts=ts, n_valid=n_valid, n_tiles=n_tiles)


def _delta_gates(zab, alog_ref, dtb_ref):
    src = _iota((LANES, GROUP_WIDTH), 0)
    head = _iota((LANES, GROUP_WIDTH), 1) // HEAD_DIM
    zab_terms = _bf16_terms(zab, 2)
    pick = lambda sel: functools.reduce(
        lambda a, b: a + b, [jnp.dot(t, sel.astype(bf16), preferred_element_type=f32) for t in zab_terms])
    a_bc, b_bc = pick(src == head), pick(src == head + N_HEADS)
    xa = a_bc + dtb_ref[...]
    g = -jnp.exp(alog_ref[...]) * (jnp.maximum(xa, 0.0) + jnp.log1p(jnp.exp(-jnp.abs(xa))))
    return g, jax.nn.sigmoid(b_bc)


def _delta_single_tokens(zqkv_ref, zz_ref, zab_ref, conv0_ref, s0_ref, cw_ref, alog_ref, dtb_ref, on_ref,
                         o_ref, convn_ref, sn_ref, xp, sbd, *, ts, group):
    lead, rows, w3 = SUBLANES, group * ts, 3 * GROUP_WIDTH
    xp[:, 0:lead, :] = conv0_ref[...]
    xp[:, lead:lead + ts, :] = zqkv_ref[...]
    cw = cw_ref[...]
    y = cw[0:1, :][None] * xp[:, lead - 3:lead - 3 + ts, :]
    for j in range(1, DELTA_CONV):
        y = y + cw[j:j + 1, :][None] * xp[:, lead - 3 + j:lead - 3 + j + ts, :]
    convn_ref[...] = xp[:, lead + 1 - (DELTA_CONV - 1):lead + 1, :]
    y = _silu(y).reshape(rows, w3)

    sum_mat = _same_head_matrix(GROUP_WIDTH, 1.0)
    yq, yk, v = y[:, :GROUP_WIDTH], y[:, GROUP_WIDTH:2 * GROUP_WIDTH], y[:, 2 * GROUP_WIDTH:]
    q = yq * lax.rsqrt(_head_sum_sq(yq, sum_mat) + EPS) * (HEAD_DIM ** -0.5)
    k = yk * lax.rsqrt(_head_sum_sq(yk, sum_mat) + EPS)
    g, beta = _delta_gates(zab_ref[...].reshape(rows, LANES), alog_ref, dtb_ref)
    live = _iota((rows, 1), 0) % ts == 0
    g, beta = jnp.where(live, g, 0.0), jnp.where(live, beta, 0.0)
    q, k, v = jnp.where(live, q, 0.0), jnp.where(live, k, 0.0), jnp.where(live, v, 0.0)
    eg = jnp.exp(g)
    qg, kbe, vb = q * eg, k * beta * eg, v * beta
    qk = _head_sum(q * k, sum_mat)

    same_head = (_iota((LANES, LANES), 0) // HEAD_DIM) == (_iota((LANES, LANES), 1) // HEAD_DIM)
    for s in range(group):
        _load_state_pairs(sbd.at[s], s0_ref.at[s])
    where = [(s, p, slice(s * ts, (s + 1) * ts), slice(p * LANES, (p + 1) * LANES))
             for s in range(group) for p in range(N_PAIRS)]
    s_prev = [sbd[s, p] for s, p, _, _ in where]
    s_bf = [x.astype(bf16) for x in s_prev]
    vnew = [vb[r, l] - jnp.dot(kbe[r, l].astype(bf16), sb, preferred_element_type=f32)
            for (_, _, r, l), sb in zip(where, s_bf)]
    outs = [jnp.dot(qg[r, l].astype(bf16), sb, preferred_element_type=f32) + qk[r, l] * vn
            for (_, _, r, l), sb, vn in zip(where, s_bf, vnew)]
    for (s, p, r, l), sp, vn in zip(where, s_prev, vnew):
        upd = _dot_tn(k[r, l].astype(bf16), vn.astype(bf16))
        sbd[s, p] = sp * eg[r.start:r.start + 1, l] + jnp.where(same_head, upd, 0.0)
    o = jnp.concatenate([jnp.concatenate(outs[s * N_PAIRS:(s + 1) * N_PAIRS], axis=1) for s in range(group)], axis=0)
    o = o * lax.rsqrt(_head_sum_sq(o, sum_mat) * (1.0 / HEAD_DIM) + EPS) * on_ref[...]
    o_ref[...] = (o * _silu(zz_ref[...].reshape(rows, GROUP_WIDTH))).reshape(group, ts, GROUP_WIDTH)
    for s in range(group):
        _store_state_pairs(sn_ref.at[s], sbd.at[s])


def _delta_sequence(t, zqkv_ref, zz_ref, zab_ref, conv0_ref, s0_ref, cw_ref, alog_ref, dtb_ref, on_ref,
                    o_ref, convn_ref, sn_ref, xp, sbd, *, ts, n_valid, n_tiles):
    lead = SUBLANES
    nv = min(ts, n_valid)

    @pl.when(t == 0)
    def _():
        xp[0:lead, :] = conv0_ref[...]
        _load_state_pairs(sbd, s0_ref)

    xp[lead:lead + ts, :] = zqkv_ref[...]
    cw = cw_ref[...]
    y = cw[0:1, :] * xp[lead - 3:lead - 3 + ts, :]
    for j in range(1, DELTA_CONV):
        y = y + cw[j:j + 1, :] * xp[lead - 3 + j:lead - 3 + j + ts, :]
    convn_ref[...] = xp[lead + nv - (DELTA_CONV - 1):lead + nv, :]
    xp[0:lead, :] = xp[ts:ts + lead, :]
    y = _silu(y)

    sum_mat = _same_head_matrix(GROUP_WIDTH, 1.0)
    yq, yk, v = y[:, :GROUP_WIDTH], y[:, GROUP_WIDTH:2 * GROUP_WIDTH], y[:, 2 * GROUP_WIDTH:]
    q = yq * lax.rsqrt(_head_sum_sq(yq, sum_mat) + EPS) * (HEAD_DIM ** -0.5)
    k = yk * lax.rsqrt(_head_sum_sq(yk, sum_mat) + EPS)

    g, beta = _delta_gates(zab_ref[...], alog_ref, dtb_ref)
    if nv < ts:
        live = _iota((ts, 1), 0) < nv
        g, beta = jnp.where(live, g, 0.0), jnp.where(live, beta, 0.0)
        q, k, v = jnp.where(live, q, 0.0), jnp.where(live, k, 0.0), jnp.where(live, v, 0.0)

    first_head = _iota((1, LANES), 1) < HEAD_DIM
    same_head = (_iota((LANES, LANES), 0) // HEAD_DIM) == (_iota((LANES, LANES), 1) // HEAD_DIM)
    o = _delta_chunks(q, k, v, g, beta, sbd, first_head, same_head, ts)
    o = o * lax.rsqrt(_head_sum_sq(o, sum_mat) * (1.0 / HEAD_DIM) + EPS) * on_ref[...]
    o_ref[...] = o * _silu(zz_ref[...])

    @pl.when(t == n_tiles - 1)
    def _():
        _store_state_pairs(sn_ref, sbd)


def _delta(z, n, seq, n_valid, conv0, s0, conv_w, alog_bc, dtb_bc, onorm_bc, ts, group=1):
    assert n_valid == seq or seq == ts
    n_tiles = seq // ts
    assert n % group == 0
    w3 = 3 * GROUP_WIDTH
    z3 = z.reshape(n, seq, Z_WIDTH)
    row = lambda width, blk: pl.BlockSpec((group, ts, width), lambda b, t, blk=blk: (b, t, blk))
    small = lambda a: pl.BlockSpec(a.shape, lambda b, t: (0, 0))
    state = pl.BlockSpec((group, N_HEADS, HEAD_DIM, HEAD_DIM), lambda b, t: (b, 0, 0, 0))
    o, convn, sn = pl.pallas_call(
        functools.partial(_delta_kernel, ts=ts, n_valid=n_valid, n_tiles=n_tiles, group=group),
        grid=(n // group, n_tiles),
        in_specs=[row(w3, ZB_QKV_BLK), row(GROUP_WIDTH, ZB_Z_BLK), row(LANES, ZAB_BLK),
                  pl.BlockSpec((group, SUBLANES, w3), lambda b, t: (b, 0, 0)), _state_in(s0, group),
                  small(conv_w), small(alog_bc), small(dtb_bc), small(onorm_bc)],
        out_specs=[pl.BlockSpec((group, ts, GROUP_WIDTH), lambda b, t: (b, t, 0)),
                   pl.BlockSpec((group, DELTA_CONV - 1, w3), lambda b, t: (b, 0, 0)), state],
        out_shape=[jax.ShapeDtypeStruct((n, seq, GROUP_WIDTH), f32),
                   jax.ShapeDtypeStruct((n, DELTA_CONV - 1, w3), f32),
                   jax.ShapeDtypeStruct((n, N_HEADS, HEAD_DIM, HEAD_DIM), f32)],
        scratch_shapes=[pltpu.VMEM((group, ts + SUBLANES, w3), f32),
                        pltpu.VMEM((group, N_PAIRS, LANES, LANES), f32)],
        compiler_params=_params(2), name="delta")(z3, z3, z3, conv0, s0[0], conv_w, alog_bc, dtb_bc, onorm_bc)
    return o.reshape(n * seq, GROUP_WIDTH), convn, sn


def _retention_kernel(zq_ref, zk_ref, zv_ref, zg_ref, dx_ref, dc_ref, db_ref, cos_ref, sin_ref, lg_ref, on_ref,
                      cwd_ref, s0_ref, dconv0_ref, o_ref, od_ref, sn_ref, dconvn_ref, sbd, dmask, xpd,
                      *, tc, n_valid, n_tiles, group):
    b, t = pl.program_id(0), pl.program_id(1)
    lg = lg_ref[...]

    @pl.when((b == 0) & (t == 0))
    def _():
        col = _iota((tc, 2 * tc), 1)
        diff = (_iota((tc, 2 * tc), 0) - col % tc).astype(f32)
        for p in range(N_PAIRS):
            lgp = lg[:, p * LANES:(p + 1) * LANES]
            rate = jnp.where(col < tc, lgp[:, 0:1], lgp[:, HEAD_DIM:HEAD_DIM + 1])
            dmask[p] = jnp.where(diff >= 0.0, jnp.exp(jnp.maximum(diff, 0.0) * rate), 0.0)

    if n_valid == 1:
        _retention_single_tokens(zq_ref, zk_ref, zv_ref, zg_ref, dx_ref, dc_ref, db_ref, cos_ref, sin_ref, lg, on_ref,
                                 cwd_ref, s0_ref, dconv0_ref, o_ref, od_ref, sn_ref, dconvn_ref, sbd, xpd,
                                 tc=tc, group=group)
        return
    for s in range(group):
        rows = pl.ds(s * tc, tc)
        _retention_sequence(t, zq_ref.at[rows], zk_ref.at[rows], zv_ref.at[rows], zg_ref.at[rows], dx_ref.at[rows],
                            dc_ref.at[rows], db_ref.at[rows], cos_ref, sin_ref, lg, on_ref, cwd_ref, s0_ref.at[s],
                            dconv0_ref.at[s], o_ref.at[rows], od_ref.at[rows], sn_ref.at[s], dconvn_ref.at[s],
                            sbd.at[s], dmask, xpd.at[s], tc=tc, n_valid=n_valid, n_tiles=n_tiles)


def _retention_single_tokens(zq_ref, zk_ref, zv_ref, zg_ref, dx_ref, dc_ref, db_ref, cos_ref, sin_ref, lg, on_ref,
                             cwd_ref, s0_ref, dconv0_ref, o_ref, od_ref, sn_ref, dconvn_ref, sbd, xpd, *, tc, group):
    lead, rows = SUBLANES, group * tc
    cos = jnp.concatenate([cos_ref[...]] * group, axis=0)
    sin = jnp.concatenate([sin_ref[...]] * group, axis=0)
    live = _iota((rows, 1), 0) % tc == 0
    q = jnp.where(live, _rope_wide(zq_ref[...], cos, sin), 0.0)
    k = jnp.where(live, _rope_wide(zk_ref[...], cos, sin) * (HEAD_DIM ** -0.5), 0.0)
    v = jnp.where(live, zv_ref[...], 0.0)
    gamma = jnp.exp(lg)
    qd = q * gamma
    qk = _head_sum(q * k, _same_head_matrix(GROUP_WIDTH, 1.0))

    same_head = (_iota((LANES, LANES), 0) // HEAD_DIM) == (_iota((LANES, LANES), 1) // HEAD_DIM)
    for s in range(group):
        _load_state_pairs(sbd.at[s], s0_ref.at[s])
    where = [(s, p, slice(s * tc, (s + 1) * tc), slice(p * LANES, (p + 1) * LANES))
             for s in range(group) for p in range(N_PAIRS)]
    s_prev = [sbd[s, p] for s, p, _, _ in where]
    outs = [jnp.dot(qd[r, l].astype(bf16), sp.astype(bf16), preferred_element_type=f32) + qk[r, l] * v[r, l]
            for (_, _, r, l), sp in zip(where, s_prev)]
    for (s, p, r, l), sp in zip(where, s_prev):
        upd = _dot_tn(k[r, l].astype(bf16), v[r, l].astype(bf16))
        sbd[s, p] = sp * gamma[:, l] + jnp.where(same_head, upd, 0.0)
    o = jnp.concatenate([jnp.concatenate(outs[s * N_PAIRS:(s + 1) * N_PAIRS], axis=1) for s in range(group)], axis=0)
    mean_mat = _same_head_matrix(GROUP_WIDTH, 1.0 / HEAD_DIM)
    o = o * lax.rsqrt(_head_sum_sq(o, mean_mat) + EPS) * on_ref[...]
    o_ref[...] = o * _silu(zg_ref[...])
    for s in range(group):
        _store_state_pairs(sn_ref.at[s], sbd.at[s])

    xpd[:, 0:lead, :] = dconv0_ref[...]
    xpd[:, lead:lead + tc, :] = (dc_ref[...] * dx_ref[...]).reshape(group, tc, GROUP_WIDTH)
    cwd = cwd_ref[...]
    yd = cwd[0:1, :][None] * xpd[:, lead - 2:lead - 2 + tc, :]
    for j in range(1, SHORT_CONV):
        yd = yd + cwd[j:j + 1, :][None] * xpd[:, lead - 2 + j:lead - 2 + j + tc, :]
    od_ref[...] = db_ref[...] * yd.reshape(rows, GROUP_WIDTH)
    dconvn_ref[...] = xpd[:, lead + 1 - (SHORT_CONV - 1):lead + 1, :]


def _retention_sequence(t, zq_ref, zk_ref, zv_ref, zg_ref, dx_ref, dc_ref, db_ref, cos_ref, sin_ref, lg, on_ref,
                        cwd_ref, s0_ref, dconv0_ref, o_ref, od_ref, sn_ref, dconvn_ref, sbd, dmask, xpd,
                        *, tc, n_valid, n_tiles):
    lead = SUBLANES
    nv = min(tc, n_valid)

    @pl.when(t == 0)
    def _():
        _load_state_pairs(sbd, s0_ref)
        xpd[0:lead, :] = dconv0_ref[...]

    cos, sin = cos_ref[...], sin_ref[...]
    q = _rope_wide(zq_ref[...], cos, sin)
    k = _rope_wide(zk_ref[...], cos, sin) * (HEAD_DIM ** -0.5)
    v = zv_ref[...]
    if nv < tc:
        live = _iota((tc, 1), 0) < nv
        k, v = jnp.where(live, k, 0.0), jnp.where(live, v, 0.0)
    pos = _iota((tc, 1), 0).astype(f32)
    qd = q * jnp.exp((pos + 1.0) * lg)
    kdk = k * jnp.exp(jnp.maximum(float(nv - 1) - pos, 0.0) * lg)
    tile_decay = jnp.exp(float(nv) * lg)

    first_head = _iota((1, LANES), 1) < HEAD_DIM
    same_head = (_iota((LANES, LANES), 0) // HEAD_DIM) == (_iota((LANES, LANES), 1) // HEAD_DIM)
    outs = []
    for p in range(N_PAIRS):
        l = slice(p * LANES, (p + 1) * LANES)
        kst = _stack_heads(k[:, l], first_head).astype(bf16)
        vst = _stack_heads(v[:, l], first_head).astype(bf16)
        sc = _dot_nt(q[:, l].astype(bf16), kst) * dmask[p]
        s_prev = sbd[p]
        o = (jnp.dot(sc.astype(bf16), vst, preferred_element_type=f32)
             + jnp.dot(qd[:, l].astype(bf16), s_prev.astype(bf16), preferred_element_type=f32))
        upd = _dot_tn(kdk[:, l].astype(bf16), v[:, l].astype(bf16))
        sbd[p] = s_prev * tile_decay[:, l] + jnp.where(same_head, upd, 0.0)
        outs.append(o)
    o = jnp.concatenate(outs, axis=1)
    mean_mat = _same_head_matrix(GROUP_WIDTH, 1.0 / HEAD_DIM)
    o = o * lax.rsqrt(_head_sum_sq(o, mean_mat) + EPS) * on_ref[...]
    o_ref[...] = o * _silu(zg_ref[...])

    xpd[lead:lead + tc, :] = dc_ref[...] * dx_ref[...]
    cwd = cwd_ref[...]
    yd = cwd[0:1, :] * xpd[lead - 2:lead - 2 + tc, :]
    for j in range(1, SHORT_CONV):
        yd = yd + cwd[j:j + 1, :] * xpd[lead - 2 + j:lead - 2 + j + tc, :]
    od_ref[...] = db_ref[...] * yd
    dconvn_ref[...] = xpd[lead + nv - (SHORT_CONV - 1):lead + nv, :]
    xpd[0:lead, :] = xpd[tc:tc + lead, :]

    @pl.when(t == n_tiles - 1)
    def _():
        _store_state_pairs(sn_ref, sbd)


def _retention(z, n, seq, n_valid, s0, dconv0, cos_t, sin_t, lg_bc, onorm_bc, conv_d, tc, group=1):
    assert n_valid == seq or seq == tc
    n_tiles = seq // tc
    assert n % group == 0 and (group == 1 or n_tiles == 1)
    row = lambda blk: pl.BlockSpec((group * tc, GROUP_WIDTH), lambda b, t, blk=blk: (b * n_tiles + t, blk))
    small = lambda a: pl.BlockSpec(a.shape, lambda b, t: (0, 0))
    tab = pl.BlockSpec((tc, LANES), lambda b, t: (t, 0))
    state = pl.BlockSpec((group, N_HEADS, HEAD_DIM, HEAD_DIM), lambda b, t: (b, 0, 0, 0))
    out_row = pl.BlockSpec((group * tc, GROUP_WIDTH), lambda b, t: (b * n_tiles + t, 0))
    out_rows = jax.ShapeDtypeStruct((n * seq, GROUP_WIDTH), f32)
    return pl.pallas_call(
        functools.partial(_retention_kernel, tc=tc, n_valid=n_valid, n_tiles=n_tiles, group=group),
        grid=(n // group, n_tiles),
        in_specs=[row(ZC_Q_BLK), row(ZC_K_BLK), row(ZC_V_BLK), row(ZC_G_BLK),
                  row(ZD_X_BLK), row(ZD_C_BLK), row(ZD_B_BLK), tab, tab,
                  small(lg_bc), small(onorm_bc), small(conv_d), _state_in(s0, group),
                  pl.BlockSpec((group, SUBLANES, GROUP_WIDTH), lambda b, t: (b, 0, 0))],
        out_specs=[out_row, out_row, state,
                   pl.BlockSpec((group, SHORT_CONV - 1, GROUP_WIDTH), lambda b, t: (b, 0, 0))],
        out_shape=[out_rows, out_rows, jax.ShapeDtypeStruct((n, N_HEADS, HEAD_DIM, HEAD_DIM), f32),
                   jax.ShapeDtypeStruct((n, SHORT_CONV - 1, GROUP_WIDTH), f32)],
        scratch_shapes=[pltpu.VMEM((group, N_PAIRS, LANES, LANES), f32), pltpu.VMEM((N_PAIRS, tc, 2 * tc), f32),
                        pltpu.VMEM((group, tc + SUBLANES, GROUP_WIDTH), f32)],
        compiler_params=_params(2), name="retention")(
            z, z, z, z, z, z, z, cos_t, sin_t, lg_bc, onorm_bc, conv_d, s0[0], dconv0)


def _rope_tables(pos, inv_freq):
    ang = pos.astype(f32)[:, None] * inv_freq[None, :]
    cos, sin = jnp.cos(ang), jnp.sin(ang)
    reps = LANES // HEAD_DIM
    return (jnp.tile(jnp.concatenate([cos, cos], axis=1), (1, reps)),
            jnp.tile(jnp.concatenate([-sin, sin], axis=1), (1, reps)))


def _per_head_lanes(v):
    return jnp.repeat(v.astype(f32), HEAD_DIM)[None, :]


def _tiled_lanes(v, width):
    return jnp.tile(v.astype(f32), width // HEAD_DIM)[None, :]


def _lead_pad(state):
    return jnp.pad(state, ((0, 0), (SUBLANES - state.shape[1], 0), (0, 0)))


def kernel(x_prompt, x_sample, cache_a_k, cache_a_v, state_b_conv, state_b_rec, state_c_rec, state_d_conv,
           norm_mix, w_in, q_norm_a, k_norm_a, conv_b, a_log_b, dt_bias_b, onorm_b, onorm_c, conv_d,
           w_out, norm_ffn, w_up, w_down):
    nb, seq, _ = x_prompt.shape
    nd, dseq, _ = x_sample.shape
    assert dseq == 1
    dpad = SUBLANES
    dgroup = max(g for g in (8, 4, 2, 1) if nd % g == 0)
    rope_freq = ROPE_THETA ** (-jnp.arange(0, HEAD_DIM, 2, dtype=f32) / HEAD_DIM)
    ret_freq = 1.0 / (10000.0 ** jnp.linspace(0.0, 1.0, HEAD_DIM // 2, dtype=f32))
    ret_lg = _per_head_lanes(jnp.log(1.0 - 2.0 ** (-5.0 - jnp.arange(N_HEADS, dtype=f32))))
    pos_p = jnp.arange(seq, dtype=jnp.int32)
    pos_d = PAST_LEN + jnp.arange(dpad, dtype=jnp.int32)
    cos_ap, sin_ap = _rope_tables(pos_p, rope_freq)
    cos_cp, sin_cp = _rope_tables(pos_p, ret_freq)
    cos_ad, sin_ad = _rope_tables(pos_d[:1], rope_freq)
    cos_cd, sin_cd = _rope_tables(pos_d, ret_freq)

    cache_kt = jnp.transpose(cache_a_k, (0, 1, 3, 4, 2))
    cache_vt = jnp.transpose(cache_a_v, (0, 1, 3, 4, 2))

    yp = x_prompt.reshape(nb * seq, D_MODEL)
    ys = x_sample.reshape(nd, D_MODEL)
    zeros_conv_b = jnp.zeros((nb, SUBLANES, 3 * GROUP_WIDTH), f32)
    zeros_conv_d = jnp.zeros((nb, SUBLANES, GROUP_WIDTH), f32)
    zeros_rec = (jnp.zeros((1, nb, N_HEADS, HEAD_DIM, HEAD_DIM), f32), 0)
    new_p, new_s = [], []
    dec_cache = prompt_cache = None
    w_pack = _pack_w_in(w_in)
    wo, wu, wd = w_out.astype(bf16), w_up.astype(bf16), w_down.astype(bf16)
    for l in range(DEPTH):
        g_mix, g_ffn = norm_mix[l][None, :], norm_ffn[l][None, :]
        qg, kg = _tiled_lanes(q_norm_a[l], LANES), _tiled_lanes(k_norm_a[l], LANES)
        qg3, kg3 = _tiled_lanes(q_norm_a[l], GROUP_WIDTH), _tiled_lanes(k_norm_a[l], GROUP_WIDTH)
        alog, dtb = _per_head_lanes(a_log_b[l]), _per_head_lanes(dt_bias_b[l])
        on_b, on_c = _tiled_lanes(onorm_b[l], GROUP_WIDTH), _tiled_lanes(onorm_c[l], GROUP_WIDTH)

        z = _inproj(yp, g_mix, w_pack, l, 512)
        oa, kt, vt = _attn_prompt(l, z, nb, seq, cos_ap, sin_ap, qg, kg, prompt_cache)
        prompt_cache = (kt, vt)
        ob, p_bconv, p_brec = _delta(z, nb, seq, seq, zeros_conv_b, zeros_rec, conv_b[l], alog, dtb, on_b, 256)
        oc, od, p_crec, p_dconv = _retention(z, nb, seq, seq, zeros_rec, zeros_conv_d, cos_cp, sin_cp, ret_lg,
                                             on_c, conv_d[l], 256)
        yp = _ffn(yp, oa, ob, oc, od, wo, g_ffn, wu, wd, l, 512)
        new_p.append((kt, vt, p_bconv, p_brec, p_crec, p_dconv))

        zs = _inproj(ys, g_mix, w_pack, l, nd)
        qa, ka, va = _attn_decode_prep(zs, cos_ad, sin_ad, qg3, kg3)
        col = lambda a: jnp.broadcast_to(a.reshape(nd, N_HEADS, HEAD_DIM, 1), (nd, N_HEADS, HEAD_DIM, LANES))
        s_kt, s_vt, oa_col = _attn_decode(l, cache_kt, cache_vt, col(qa), col(ka), col(va), dec_cache)
        dec_cache = (s_kt, s_vt)
        oa_s = oa_col[..., 0].reshape(nd, GROUP_WIDTH)
        zpad = jnp.zeros((nd, dpad, Z_WIDTH), f32).at[:, 0, :].set(zs).reshape(nd * dpad, Z_WIDTH)
        ob_s, s_bconv, s_brec = _delta(zpad, nd, dpad, 1, _lead_pad(state_b_conv[l]), (state_b_rec, l), conv_b[l],
                                       alog, dtb, on_b, dpad, group=dgroup)
        oc_s, od_s, s_crec, s_dconv = _retention(zpad, nd, dpad, 1, (state_c_rec, l), _lead_pad(state_d_conv[l]),
                                                 cos_cd, sin_cd, ret_lg, on_c, conv_d[l], dpad, group=dgroup)
        first = lambda a: a.reshape(nd, dpad, GROUP_WIDTH)[:, 0, :]
        ys = _ffn(ys, oa_s, first(ob_s), first(oc_s), first(od_s), wo, g_ffn, wu, wd, l, nd)
        new_s.append((s_bconv, s_brec, s_crec, s_dconv))

    keep = min(WIN_MAX, seq)
    cache_out = lambda i: jnp.transpose(
        prompt_cache[i].reshape(DEPTH, nb, N_HEADS, HEAD_DIM, keep), (0, 1, 4, 2, 3))
    stack_p = lambda i: jnp.stack([s[i] for s in new_p])
    stack_s = lambda i: jnp.stack([s[i] for s in new_s])
    return (yp.reshape(nb, seq, D_MODEL), ys.reshape(nd, 1, D_MODEL),
            cache_out(0), cache_out(1), stack_p(2), stack_p(3), stack_p(4), stack_p(5),
            jnp.transpose(dec_cache[0], (0, 1, 4, 2, 3)), jnp.transpose(dec_cache[1], (0, 1, 4, 2, 3)),
            stack_s(0), stack_s(1), stack_s(2), stack_s(3))
```

```python
import functools

import jax
import jax.numpy as jnp
from jax import lax
from jax.experimental import pallas as pl
from jax.experimental.pallas import tpu as pltpu

f32, bf16, i32 = jnp.float32, jnp.bfloat16, jnp.int32

D_MODEL = 1024
HEAD_DIM = 64
N_HEADS = 6
N_PAIRS = N_HEADS // 2
GROUP_WIDTH = N_HEADS * HEAD_DIM
D_FF = 4 * D_MODEL
DIL_PAIRS = ((128, 1), (512, 4), (2048, 16))
WIN_MAX = 2048
ATT_BLOCK = 128
ROPE_THETA = 10000.0
DELTA_CONV = 4
SHORT_CONV = 3
CHUNK = 64
EPS = 1e-6
NEG = -1e30
PAST_LEN = 16384
DEPTH = 2
LANES = 128
SUBLANES = 8
VMEM_LIMIT = 56 * 1024 * 1024
DENSE_ROW_TILE = 512
RECURRENT_TILE = 256

IN_WIDTH = 14 * GROUP_WIDTH + 2 * N_HEADS
AB_OFFSET_SRC = 7 * GROUP_WIDTH
Z_WIDTH = 14 * GROUP_WIDTH + LANES
ZB_QKV_BLK = 1
ZB_Z_BLK = 6
ZC_Q_BLK, ZC_K_BLK, ZC_V_BLK, ZC_G_BLK = 7, 8, 9, 10
ZD_X_BLK, ZD_C_BLK, ZD_B_BLK = 11, 12, 13
ZAB_BLK = 14 * GROUP_WIDTH // LANES


def _iota(shape, dim):
    return lax.broadcasted_iota(i32, shape, dim)


def _params(n_axes):
    return pltpu.CompilerParams(dimension_semantics=("arbitrary",) * n_axes, vmem_limit_bytes=VMEM_LIMIT)


def _same_head_matrix(width, value):
    r = _iota((width, width), 0) // HEAD_DIM
    c = _iota((width, width), 1) // HEAD_DIM
    return jnp.where(r == c, value, 0.0).astype(bf16)


def _head_sum(x, mat, terms=2):
    return _dot_split_lhs(x, mat, terms)


def _head_sum_sq(x, mat):
    return _dot_split_lhs(x * x, mat, 1)


def _rope_slab(x, cos, sin_signed):
    p = _iota((1, LANES), 1) % HEAD_DIM
    partner = jnp.where(p < HEAD_DIM // 2, pltpu.roll(x, LANES - HEAD_DIM // 2, 1), pltpu.roll(x, HEAD_DIM // 2, 1))
    return x * cos + partner * sin_signed


def _rope_wide(x, cos, sin_signed):
    return jnp.concatenate(
        [_rope_slab(x[:, LANES * i:LANES * (i + 1)], cos, sin_signed) for i in range(x.shape[1] // LANES)], axis=1)


def _stack_heads(x, first_head):
    return jnp.concatenate([jnp.where(first_head, x, 0.0), jnp.where(first_head, 0.0, x)], axis=0)


def _dot_nt(a, b):
    return lax.dot_general(a, b, (((1,), (1,)), ((), ())), preferred_element_type=f32)


def _dot_tn(a, b):
    return lax.dot_general(a, b, (((0,), (0,)), ((), ())), preferred_element_type=f32)


def _silu(x):
    return x * jax.nn.sigmoid(x)


def _inproj_kernel(x_ref, g_ref, w_ref, o_ref):
    x = x_ref[...]
    ms = jnp.mean(x * x, axis=-1, keepdims=True)
    xn = (x * lax.rsqrt(ms + EPS) * g_ref[...]).astype(bf16)
    step = 4 * LANES
    for c0 in range(0, Z_WIDTH, step):
        cw = min(step, Z_WIDTH - c0)
        o_ref[:, c0:c0 + cw] = jnp.dot(xn, w_ref[:, c0:c0 + cw], preferred_element_type=f32)


def _pack_w_in_kernel(lo_ref, hi_ref, ab_ref, o_ref, *, n_lo, n_hi):
    i = pl.program_id(0)

    def emit(src):
        for l in range(DEPTH):
            o_ref[l] = src(l).T.astype(bf16)

    @pl.when(i < n_lo)
    def _():
        emit(lambda l: lo_ref[:, l, :])

    @pl.when((i >= n_lo) & (i < n_lo + n_hi))
    def _():
        emit(lambda l: hi_ref[:, l, :])

    @pl.when(i == n_lo + n_hi)
    def _():
        live = _iota((LANES, D_MODEL), 0) < 2 * N_HEADS
        emit(lambda l: jnp.where(live, ab_ref[:, l, :], 0.0))


def _pack_w_in(w_in):
    wt = jnp.transpose(w_in, (2, 0, 1))
    n_ab = 2 * N_HEADS
    n_lo = AB_OFFSET_SRC // LANES
    n_hi = (IN_WIDTH - AB_OFFSET_SRC - n_ab) // LANES
    assert n_lo * LANES == AB_OFFSET_SRC and (n_lo + n_hi + 1) * LANES == Z_WIDTH
    rows = lambda start: pl.BlockSpec((pl.Element(LANES), pl.Element(DEPTH), pl.Element(D_MODEL)),
                                      lambda i: (start(i), 0, 0))
    return pl.pallas_call(
        functools.partial(_pack_w_in_kernel, n_lo=n_lo, n_hi=n_hi), grid=(Z_WIDTH // LANES,),
        in_specs=[rows(lambda i: LANES * jnp.minimum(i, n_lo - 1)),
                  rows(lambda i: AB_OFFSET_SRC + n_ab + LANES * jnp.clip(i - n_lo, 0, n_hi - 1)),
                  rows(lambda i: AB_OFFSET_SRC)],
        out_specs=pl.BlockSpec((DEPTH, D_MODEL, LANES), lambda i: (0, 0, i)),
        out_shape=jax.ShapeDtypeStruct((DEPTH, D_MODEL, Z_WIDTH), bf16),
        compiler_params=_params(1), name="pack_w_in")(wt, wt, wt)


def _layer_block(a, layer):
    return pl.BlockSpec((None,) + a.shape[1:], lambda *_: (layer, 0, 0), pipeline_mode=pl.Buffered(1))


def _inproj(x2d, gain, w_pack, layer, tm):
    m = x2d.shape[0]
    return pl.pallas_call(
        _inproj_kernel,
        grid=(m // tm,),
        in_specs=[pl.BlockSpec((tm, D_MODEL), lambda i: (i, 0)),
                  pl.BlockSpec((1, D_MODEL), lambda i: (0, 0)),
                  _layer_block(w_pack, layer)],
        out_specs=pl.BlockSpec((tm, Z_WIDTH), lambda i: (i, 0)),
        out_shape=jax.ShapeDtypeStruct((m, Z_WIDTH), f32),
        compiler_params=_params(1), name="inproj")(x2d, gain, w_pack)


def _ffn_kernel(x_ref, oa_ref, ob_ref, oc_ref, od_ref, wo_ref, g_ref, wu_ref, wd_ref, y_ref):
    mix = jnp.concatenate([r[...].astype(bf16) for r in (oa_ref, ob_ref, oc_ref, od_ref)], axis=1)
    h = x_ref[...] + jnp.dot(mix, wo_ref[...], preferred_element_type=f32)
    ms = jnp.mean(h * h, axis=-1, keepdims=True)
    hn = (h * lax.rsqrt(ms + EPS) * g_ref[...]).astype(bf16)
    y_ref[...] = h
    step = D_FF // 4
    for c in range(0, D_FF, step):
        u = jnp.dot(hn, wu_ref[:, c:c + step], preferred_element_type=f32)
        u = jnp.square(jnp.maximum(u, 0.0)).astype(bf16)
        y_ref[...] += jnp.dot(u, wd_ref[c:c + step, :], preferred_element_type=f32)


def _ffn(x2d, oa, ob, oc, od, w_out, gain, w_up, w_down, layer, tm):
    m = x2d.shape[0]
    row = lambda width: pl.BlockSpec((tm, width), lambda i: (i, 0))
    whole = lambda a: _layer_block(a, layer)
    return pl.pallas_call(
        _ffn_kernel,
        grid=(m // tm,),
        in_specs=[row(D_MODEL), row(GROUP_WIDTH), row(GROUP_WIDTH), row(GROUP_WIDTH), row(GROUP_WIDTH),
                  whole(w_out), pl.BlockSpec((1, D_MODEL), lambda i: (0, 0)), whole(w_up), whole(w_down)],
        out_specs=row(D_MODEL),
        out_shape=jax.ShapeDtypeStruct((m, D_MODEL), f32),
        compiler_params=_params(1), name="ffn")(x2d, oa, ob, oc, od, w_out, gain, w_up, w_down)


def _attn_prompt_kernel(q_ref, k_ref, v_ref, cos_ref, sin_ref, qg_ref, kg_ref, *rest,
                        seq, keep, wide_dil, wide_pitch, aliased):
    if aliased:
        rest = rest[2:]
    (o_ref, kt_ref, vt_ref, q_s, ks, acc_s, m_s, t_acc, t_m, qw, kw, vw, bias_first, bias_rest) = rest
    mean_mat = _same_head_matrix(LANES, 1.0 / HEAD_DIM)
    first_head = _iota((1, LANES), 1) < HEAD_DIM
    rc = 512
    blk = ATT_BLOCK

    def prep(i, carry):
        rows = pl.ds(pl.multiple_of(i * rc, rc), rc)
        cos, sin = cos_ref[rows, :], sin_ref[rows, :]
        q, k, v = q_ref[rows, :], k_ref[rows, :], v_ref[rows, :]
        q = q * lax.rsqrt(_head_sum_sq(q, mean_mat) + EPS) * qg_ref[...]
        k = k * lax.rsqrt(_head_sum_sq(k, mean_mat) + EPS) * kg_ref[...]
        q = _rope_slab(q, cos, sin) * (HEAD_DIM ** -0.5)
        q_s[rows, :] = q
        ks[rows, :] = _rope_slab(k, cos, sin)
        k = ks[rows, :]
        for g in range(rc // wide_dil):
            src = slice(g * wide_dil, (g + 1) * wide_dil)
            dst = pl.ds(pl.multiple_of(i * (rc // wide_dil * wide_pitch), SUBLANES) + g * wide_pitch, wide_dil)
            qw[dst, :], kw[dst, :], vw[dst, :] = q[src, :], k[src, :], v[src, :]
        return carry

    lax.fori_loop(0, seq // rc, prep, 0)

    for c in range(keep // rc):
        r0 = seq - keep + c * rc
        kt_ref[:, c * rc:(c + 1) * rc] = ks[r0:r0 + rc, :].T
        vt_ref[:, c * rc:(c + 1) * rc] = v_ref[r0:r0 + rc, :].T

    qi, kj = _iota((blk, blk), 0), _iota((blk, blk), 1)
    bias_first[...] = jnp.where(kj <= qi, 0.0, NEG)
    qi, kj = _iota((blk, 2 * blk), 0), _iota((blk, 2 * blk), 1)
    bias_rest[...] = jnp.where((kj >= qi) & (kj <= qi + blk), 0.0, NEG)

    def rows_of(st, n, dil):
        return pl.ds(st, n, stride=dil) if dil > 1 else pl.ds(st, n)

    def tile_group(tiles, dil, dst_acc, dst_m, pitch):
        staged = []
        for r, b, first in tiles:
            nk = blk if first else 2 * blk
            if dil == wide_dil:
                qrow = wide_pitch * (b * blk) + r
                krow = qrow if first else qrow - wide_pitch * blk
                rows, krows = pl.ds(qrow, blk, stride=wide_pitch), pl.ds(krow, nk, stride=wide_pitch)
                q, k, v = qw[rows, :], kw[krows, :].astype(bf16), vw[krows, :]
            else:
                start = r + b * (blk * dil)
                kstart = start if first else start - blk * dil
                rows, krows = rows_of(start, blk, dil), rows_of(kstart, nk, dil)
                q, k, v = q_s[rows, :], ks[krows, :].astype(bf16), v_ref[krows, :]
            qs = [jnp.where(first_head, q, 0.0).astype(bf16), jnp.where(first_head, 0.0, q).astype(bf16)]
            vs = [jnp.where(first_head, v, 1.0).astype(bf16), jnp.where(first_head, 1.0, v).astype(bf16)]
            bias = bias_first[...] if first else bias_rest[...]
            dst = pl.ds(pl.multiple_of(r * pitch + b * blk, SUBLANES), blk)
            staged.append((dst, vs, [_dot_nt(qs[hh], k) + bias for hh in range(2)]))
        probs = []
        for dst, vs, scores in staged:
            for hh in range(2):
                mt = jnp.max(scores[hh], axis=1, keepdims=True)
                probs.append((dst, hh, vs[hh], jnp.exp(scores[hh] - mt).astype(bf16), mt))
        for dst, hh, v, p, mt in probs:
            dst_acc[hh, dst, :] = jnp.dot(p, v, preferred_element_type=f32)
            dst_m[hh, dst, :] = jnp.broadcast_to(mt, (blk, LANES))

    merge_group = 4
    tiles_per_body = 16

    for bi, (window, dil) in enumerate(DIL_PAIRS):
        nb = seq // dil // blk
        dst_acc, dst_m = (acc_s, m_s) if bi == 0 else (t_acc, t_m)
        pitch = nb * blk + (SUBLANES if dil == wide_dil else 0)
        if nb >= tiles_per_body:
            assert nb % tiles_per_body == 0

            def per_residue(r, carry, dil=dil, nb=nb, dst_acc=dst_acc, dst_m=dst_m, pitch=pitch):
                tile_group([(r, b, b == 0) for b in range(tiles_per_body)], dil, dst_acc, dst_m, pitch)

                def per_group(g, c):
                    tile_group([(r, g * tiles_per_body + u, False) for u in range(tiles_per_body)],
                               dil, dst_acc, dst_m, pitch)
                    return c

                lax.fori_loop(1, nb // tiles_per_body, per_group, 0)
                return carry

            lax.fori_loop(0, dil, per_residue, 0)
        else:
            res_per_body = tiles_per_body // nb
            assert dil % res_per_body == 0

            def per_residues(g, carry, dil=dil, nb=nb, dst_acc=dst_acc, dst_m=dst_m, pitch=pitch, rpb=res_per_body):
                tile_group([(g * rpb + u, b, b == 0) for u in range(rpb) for b in range(nb)],
                           dil, dst_acc, dst_m, pitch)
                return carry

            lax.fori_loop(0, dil // res_per_body, per_residues, 0)

        if bi > 0 and dil != wide_dil:
            def merge(g, carry, dil=dil, nb=nb):
                pending = []
                for u in range(merge_group):
                    idx = g * merge_group + u
                    r, b = idx // nb, idx % nb
                    rows = rows_of(r + b * (blk * dil), blk, dil)
                    src = pl.ds(pl.multiple_of(idx * blk, blk), blk)
                    for hh in range(2):
                        mo, mt = m_s[hh, rows, :], t_m[hh, src, :]
                        mn = jnp.maximum(mo, mt)
                        an = acc_s[hh, rows, :] * jnp.exp(mo - mn) + t_acc[hh, src, :] * jnp.exp(mt - mn)
                        pending.append((hh, rows, an, mn))
                for hh, rows, an, mn in pending:
                    acc_s[hh, rows, :] = an
                    m_s[hh, rows, :] = mn
                return carry

            lax.fori_loop(0, seq // blk // merge_group, merge, 0)
        elif bi > 0:
            def merge_wide(g, carry, dil=dil, pitch=pitch):
                pending = []
                for u in range(merge_group):
                    t0 = pl.multiple_of((g * merge_group + u) * blk, blk)
                    rows = pl.ds(t0, blk)
                    for hh in range(2):
                        ta, tm = [], []
                        for v8 in range(blk // SUBLANES):
                            src = pl.ds(((v8 * SUBLANES) % dil) * pitch + t0 // dil + (v8 * SUBLANES) // dil,
                                        SUBLANES, stride=pitch)
                            ta.append(t_acc[hh, src, :])
                            tm.append(t_m[hh, src, :])
                        ta, mt = jnp.concatenate(ta, axis=0), jnp.concatenate(tm, axis=0)
                        mo = m_s[hh, rows, :]
                        mn = jnp.maximum(mo, mt)
                        pending.append(acc_s[hh, rows, :] * jnp.exp(mo - mn) + ta * jnp.exp(mt - mn))
                    a0, a1 = pending[-2:]
                    half = LANES // 2
                    o_ref[rows, :] = jnp.where(first_head, a0 / pltpu.roll(a0, half, 1), a1 / pltpu.roll(a1, half, 1))
                return carry

            lax.fori_loop(0, seq // blk // merge_group, merge_wide, 0)


def _attn_prompt(layer, z, n, seq, cos_t, sin_t, qgain, kgain, prev_out):
    keep = min(WIN_MAX, seq)
    for window, dil in DIL_PAIRS:
        assert window // dil == ATT_BLOCK and seq % (dil * ATT_BLOCK) == 0
    wide = [dil for _, dil in DIL_PAIRS if dil % SUBLANES == 0]
    assert len(wide) == 1 and ATT_BLOCK % wide[0] == 0 and DIL_PAIRS[-1][1] == wide[0]
    wide_dil, wide_pitch = wide[0], wide[0] + SUBLANES
    per_group = GROUP_WIDTH // LANES
    zspec = lambda base: pl.BlockSpec((seq, LANES), lambda b, p, base=base: (b, base * per_group + p),
                                      pipeline_mode=pl.Buffered(1))
    whole = lambda rows: pl.BlockSpec((rows, LANES), lambda b, p: (0, 0), pipeline_mode=pl.Buffered(1))
    slab = lambda: pltpu.VMEM((seq, LANES), f32)
    slab2 = lambda: pltpu.VMEM((2, seq, LANES), f32)
    branch = lambda: pltpu.VMEM((2, seq + wide_dil * SUBLANES, LANES), f32)
    padded = lambda: pltpu.VMEM((seq // wide_dil * wide_pitch, LANES), f32)
    in_specs = [zspec(0), zspec(1), zspec(2), whole(seq), whole(seq), whole(1), whole(1)]
    args = [z, z, z, cos_t, sin_t, qgain, kgain]
    aliases = {}
    if prev_out is not None:
        in_specs += [pl.BlockSpec(memory_space=pl.ANY), pl.BlockSpec(memory_space=pl.ANY)]
        args += list(prev_out)
        aliases = {len(args) - 2: 1, len(args) - 1: 2}
    cache_spec = pl.BlockSpec((None, None, LANES, keep), lambda b, p: (layer, b, p, 0))
    cache_shape = jax.ShapeDtypeStruct((DEPTH, n, GROUP_WIDTH, keep), f32)
    return pl.pallas_call(
        functools.partial(_attn_prompt_kernel, seq=seq, keep=keep, wide_dil=wide_dil, wide_pitch=wide_pitch,
                          aliased=prev_out is not None),
        grid=(n, N_PAIRS),
        in_specs=in_specs,
        out_specs=[pl.BlockSpec((seq, LANES), lambda b, p: (b, p)), cache_spec, cache_spec],
        out_shape=[jax.ShapeDtypeStruct((n * seq, GROUP_WIDTH), f32), cache_shape, cache_shape],
        input_output_aliases=aliases,
        scratch_shapes=[slab(), slab(), slab2(), slab2(), branch(), branch(), padded(), padded(), padded(),
                        pltpu.VMEM((ATT_BLOCK, ATT_BLOCK), f32), pltpu.VMEM((ATT_BLOCK, 2 * ATT_BLOCK), f32)],
        compiler_params=_params(2), name=f"attn_prompt_l{layer}")(*args)


def _attn_decode_prep_kernel(z_ref, cos_ref, sin_ref, qg_ref, kg_ref, q_ref, k_ref, v_ref):
    mean_mat = _same_head_matrix(GROUP_WIDTH, 1.0 / HEAD_DIM)
    z = z_ref[...]
    q, k = z[:, :GROUP_WIDTH], z[:, GROUP_WIDTH:2 * GROUP_WIDTH]
    q = q * lax.rsqrt(_head_sum_sq(q, mean_mat) + EPS) * qg_ref[...]
    k = k * lax.rsqrt(_head_sum_sq(k, mean_mat) + EPS) * kg_ref[...]
    q_ref[...] = _rope_wide(q, cos_ref[...], sin_ref[...]) * (HEAD_DIM ** -0.5)
    k_ref[...] = _rope_wide(k, cos_ref[...], sin_ref[...])
    v_ref[...] = z[:, 2 * GROUP_WIDTH:]


def _attn_decode_prep(z, cos_row, sin_row, qgain, kgain):
    m = z.shape[0]
    small = lambda a: pl.BlockSpec(a.shape, lambda i: (0, 0))
    out = jax.ShapeDtypeStruct((m, GROUP_WIDTH), f32)
    ospec = pl.BlockSpec((m, GROUP_WIDTH), lambda i: (0, 0))
    return pl.pallas_call(
        _attn_decode_prep_kernel, grid=(1,),
        in_specs=[pl.BlockSpec((m, 3 * GROUP_WIDTH), lambda i: (0, 0)), small(cos_row), small(sin_row),
                  small(qgain), small(kgain)],
        out_specs=[ospec, ospec, ospec], out_shape=[out, out, out],
        compiler_params=_params(1), name="attn_decode_prep")(z, cos_row, sin_row, qgain, kgain)


def _attn_decode_kernel(*refs, buf, aliased):
    if aliased:
        kc_ref, vc_ref, q_ref, kn_ref, vn_ref, _, _, ko_ref, vo_ref, o_ref = refs
    else:
        kc_ref, vc_ref, q_ref, kn_ref, vn_ref, ko_ref, vo_ref, o_ref = refs
    kc = kc_ref[...]
    vc = vc_ref[...]
    qc = q_ref[...][:, :, :1]
    knew = kn_ref[...][:, :, :1]
    vnew = vn_ref[...][:, :, :1]
    dist = buf - _iota((1, 1, buf), 2)
    mult = jnp.zeros((1, 1, buf), f32)
    for window, dil in DIL_PAIRS:
        mult = mult + jnp.where((dist <= window) & (dist % dil == 0), 1.0, 0.0)
    s = jnp.sum(kc * qc, axis=1, keepdims=True)
    s_new = jnp.sum(knew * qc, axis=1, keepdims=True)
    s = jnp.where(mult > 0.0, s, NEG)
    mx = jnp.maximum(jnp.max(s, axis=2, keepdims=True), s_new)
    p = mult * jnp.exp(s - mx)
    p_new = float(len(DIL_PAIRS)) * jnp.exp(s_new - mx)
    den = jnp.sum(p, axis=2, keepdims=True) + p_new
    o = (jnp.sum(vc * p, axis=2, keepdims=True) + vnew * p_new) / den
    o_ref[...] = jnp.broadcast_to(o, o_ref.shape)
    last = _iota(kc.shape, 2) == buf - 1
    ko_ref[...] = jnp.where(last, knew, pltpu.roll(kc, buf - 1, 2))
    vo_ref[...] = jnp.where(last, vnew, pltpu.roll(vc, buf - 1, 2))


def _attn_decode(layer, cache_kt, cache_vt, q_col, k_col, v_col, prev_out):
    depth, nb, nh, hd, buf = cache_kt.shape
    cache_spec = pl.BlockSpec((None, None, nh, hd, buf), lambda b: (layer, b, 0, 0, 0))
    col_spec = pl.BlockSpec((None, nh, hd, LANES), lambda b: (b, 0, 0, 0))
    in_specs = [cache_spec, cache_spec, col_spec, col_spec, col_spec]
    args = [cache_kt, cache_vt, q_col, k_col, v_col]
    aliases = {}
    if prev_out is not None:
        in_specs += [pl.BlockSpec(memory_space=pl.ANY), pl.BlockSpec(memory_space=pl.ANY)]
        args += list(prev_out)
        aliases = {5: 0, 6: 1}
    cache_shape = jax.ShapeDtypeStruct(cache_kt.shape, f32)
    return pl.pallas_call(
        functools.partial(_attn_decode_kernel, buf=buf, aliased=prev_out is not None),
        grid=(nb,), in_specs=in_specs,
        out_specs=[cache_spec, cache_spec, col_spec],
        out_shape=[cache_shape, cache_shape, jax.ShapeDtypeStruct((nb, nh, hd, LANES), f32)],
        input_output_aliases=aliases,
        compiler_params=_params(1), name=f"attn_decode_l{layer}")(*args)


SOLVE_BLOCK = 16


def _solve_unit_lower_pairs(systems, first_head):
    ng = CHUNK // SUBLANES
    gpb = SOLVE_BLOCK // SUBLANES
    col_in_head = _iota((SOLVE_BLOCK, LANES), 1) % HEAD_DIM
    split = lambda x: [x[SUBLANES * g:SUBLANES * (g + 1), :] for g in range(ng)]
    ags = [split(a) for a, _ in systems]
    xgs = [split(x) for _, x in systems]
    for blk in range(CHUNK // SOLVE_BLOCK):
        r0 = blk * SOLVE_BLOCK
        if blk > 0:
            for (a, _), xg in zip(systems, xgs):
                left = jnp.where(col_in_head < r0, a[r0:r0 + SOLVE_BLOCK, :], 0.0).astype(bf16)
                solved = _stack_heads(jnp.concatenate(xg, axis=0), first_head).astype(bf16)
                upd = jnp.dot(left, solved, preferred_element_type=f32)
                for g in range(gpb):
                    xg[blk * gpb + g] = xg[blk * gpb + g] - upd[SUBLANES * g:SUBLANES * (g + 1), :]
        for j in range(r0, r0 + SOLVE_BLOCK - 1):
            g0, r = divmod(j, SUBLANES)
            for ag, xg in zip(ags, xgs):
                xrow = jnp.broadcast_to(xg[g0][r:r + 1, :], (SUBLANES, LANES))
                for g in range(g0, (blk + 1) * gpb):
                    c0 = jnp.broadcast_to(ag[g][:, j:j + 1], (SUBLANES, LANES))
                    c1 = jnp.broadcast_to(ag[g][:, HEAD_DIM + j:HEAD_DIM + j + 1], (SUBLANES, LANES))
                    xg[g] = xg[g] - jnp.where(first_head, c0, c1) * xrow
    return [jnp.concatenate(xg, axis=0) for xg in xgs]


def _bf16_terms(x, terms):
    pieces, rest = [], x
    for t in range(terms):
        pieces.append(rest.astype(bf16))
        if t + 1 < terms:
            rest = rest - pieces[-1].astype(f32)
    return pieces


def _dot_split_lhs(x, mat, terms):
    parts = [jnp.dot(p, mat, preferred_element_type=f32) for p in _bf16_terms(x, terms)]
    return functools.reduce(lambda a, b: a + b, parts)


def _dot_split_rhs(mat, x, terms):
    parts = [jnp.dot(mat, p, preferred_element_type=f32) for p in _bf16_terms(x, terms)]
    return functools.reduce(lambda a, b: a + b, parts)


def _state_in(s0, group):
    layer = s0[1]
    return pl.BlockSpec((None, group, N_HEADS, HEAD_DIM, HEAD_DIM), lambda b, t: (layer, b, 0, 0, 0))


def _load_state_pairs(sbd, s0_ref):
    sbd[...] = jnp.zeros_like(sbd)
    for h in range(N_HEADS):
        p, hh = divmod(h, 2)
        sbd[p, hh * HEAD_DIM:(hh + 1) * HEAD_DIM, hh * HEAD_DIM:(hh + 1) * HEAD_DIM] = s0_ref[h]


def _store_state_pairs(sn_ref, sbd):
    for h in range(N_HEADS):
        p, hh = divmod(h, 2)
        sn_ref[h] = sbd[p, hh * HEAD_DIM:(hh + 1) * HEAD_DIM, hh * HEAD_DIM:(hh + 1) * HEAD_DIM]


def _delta_chunks(q, k, v, g, beta, sbd, first_head, same_head, ts, group):
    ri, cj = _iota((ts, ts), 0), _iota((ts, ts), 1)
    same = (ri // CHUNK) == (cj // CHUNK)
    cum_mat, tot_mat = (same & (cj <= ri)).astype(bf16), same.astype(bf16)
    g_terms = _bf16_terms(g, 2)
    rsum = lambda mat: jnp.concatenate(
        [functools.reduce(lambda a, b: a + b,
                          [jnp.dot(mat, t[s * ts:(s + 1) * ts, :], preferred_element_type=f32) for t in g_terms])
         for s in range(group)], axis=0)
    gcum = rsum(cum_mat)
    glast = rsum(tot_mat)
    eg = jnp.exp(gcum)
    qg, kb, vb = q * eg, k * beta, v * beta
    kbe = kb * eg
    kd = k * jnp.exp(glast - gcum)
    dl = jnp.exp(glast)

    ii = _iota((CHUNK, LANES), 0)
    jj = _iota((CHUNK, LANES), 1) % HEAD_DIM
    eye2 = (ii == jj).astype(f32)
    ones = jnp.ones((CHUNK, CHUNK), bf16)
    n_chunks = ts // CHUNK
    where = {(s, c, p): (slice(s * ts + c * CHUNK, s * ts + (c + 1) * CHUNK), slice(p * LANES, (p + 1) * LANES))
             for s in range(group) for c in range(n_chunks) for p in range(N_PAIRS)}

    amats, qks = {}, {}
    for key, (r, l) in where.items():
        gc = gcum[r, l]
        grow = _dot_split_rhs(ones, eye2 * gc, 2)
        dm = jnp.where(ii >= jj, jnp.exp(jnp.minimum(gc - grow, 0.0)), 0.0)
        kst = _stack_heads(k[r, l], first_head).astype(bf16)
        kk = _dot_nt(kb[r, l].astype(bf16), kst)
        qks[key] = (_dot_nt(q[r, l].astype(bf16), kst) * dm).astype(bf16)
        amats[key] = jnp.where(ii > jj, kk * dm, 0.0)

    outs = {}
    for c in range(n_chunks):
        keys = [(s, c, p) for s in range(group) for p in range(N_PAIRS)]
        s_prev = {key: sbd[key[0], key[2]] for key in keys}
        s_bf = {key: s_prev[key].astype(bf16) for key in keys}
        rhs = [vb[where[key]] - jnp.dot(kbe[where[key]].astype(bf16), s_bf[key], preferred_element_type=f32)
               for key in keys]
        vnews = _solve_unit_lower_pairs([(amats[key], x) for key, x in zip(keys, rhs)], first_head)
        for key, vnew in zip(keys, vnews):
            r, l = where[key]
            vst = _stack_heads(vnew, first_head).astype(bf16)
            outs[key] = (jnp.dot(qg[r, l].astype(bf16), s_bf[key], preferred_element_type=f32)
                         + jnp.dot(qks[key], vst, preferred_element_type=f32))
            upd = _dot_tn(kd[r, l].astype(bf16), vnew.astype(bf16))
            sbd[key[0], key[2]] = s_prev[key] * dl[r.start:r.start + 1, l] + jnp.where(same_head, upd, 0.0)
    return jnp.concatenate(
        [jnp.concatenate([outs[(s, c, p)] for p in range(N_PAIRS)], axis=1)
         for s in range(group) for c in range(n_chunks)], axis=0)


def _delta_kernel(zqkv_ref, zz_ref, zab_ref, conv0_ref, s0_ref, cw_ref, alog_ref, dtb_ref, on_ref,
                  o_ref, convn_ref, sn_ref, xp, sbd, *, ts, n_valid, n_tiles, group):
    t = pl.program_id(1)
    if n_valid == 1:
        _delta_single_tokens(zqkv_ref, zz_ref, zab_ref, conv0_ref, s0_ref, cw_ref, alog_ref, dtb_ref, on_ref,
                             o_ref, convn_ref, sn_ref, xp, sbd, ts=ts, group=group)
        return
    _delta_tiles(t, zqkv_ref, zz_ref, zab_ref, conv0_ref, s0_ref, cw_ref, alog_ref, dtb_ref, on_ref,
                 o_ref, convn_ref, sn_ref, xp, sbd, ts=ts, n_tiles=n_tiles, group=group)


def _delta_gates(zab, alog_ref, dtb_ref):
    src = _iota((LANES, GROUP_WIDTH), 0)
    head = _iota((LANES, GROUP_WIDTH), 1) // HEAD_DIM
    zab_terms = _bf16_terms(zab, 2)
    pick = lambda sel: functools.reduce(
        lambda a, b: a + b, [jnp.dot(t, sel.astype(bf16), preferred_element_type=f32) for t in zab_terms])
    a_bc, b_bc = pick(src == head), pick(src == head + N_HEADS)
    xa = a_bc + dtb_ref[...]
    g = -jnp.exp(alog_ref[...]) * (jnp.maximum(xa, 0.0) + jnp.log1p(jnp.exp(-jnp.abs(xa))))
    return g, jax.nn.sigmoid(b_bc)


def _delta_single_tokens(zqkv_ref, zz_ref, zab_ref, conv0_ref, s0_ref, cw_ref, alog_ref, dtb_ref, on_ref,
                         o_ref, convn_ref, sn_ref, xp, sbd, *, ts, group):
    lead, rows, w3 = SUBLANES, group * ts, 3 * GROUP_WIDTH
    xp[:, 0:lead, :] = conv0_ref[...]
    xp[:, lead:lead + ts, :] = zqkv_ref[...]
    cw = cw_ref[...]
    y = cw[0:1, :][None] * xp[:, lead - 3:lead - 3 + ts, :]
    for j in range(1, DELTA_CONV):
        y = y + cw[j:j + 1, :][None] * xp[:, lead - 3 + j:lead - 3 + j + ts, :]
    convn_ref[...] = xp[:, lead + 1 - (DELTA_CONV - 1):lead + 1, :]
    y = _silu(y).reshape(rows, w3)

    sum_mat = _same_head_matrix(GROUP_WIDTH, 1.0)
    yq, yk, v = y[:, :GROUP_WIDTH], y[:, GROUP_WIDTH:2 * GROUP_WIDTH], y[:, 2 * GROUP_WIDTH:]
    q = yq * lax.rsqrt(_head_sum_sq(yq, sum_mat) + EPS) * (HEAD_DIM ** -0.5)
    k = yk * lax.rsqrt(_head_sum_sq(yk, sum_mat) + EPS)
    g, beta = _delta_gates(zab_ref[...].reshape(rows, LANES), alog_ref, dtb_ref)
    live = _iota((rows, 1), 0) % ts == 0
    g, beta = jnp.where(live, g, 0.0), jnp.where(live, beta, 0.0)
    q, k, v = jnp.where(live, q, 0.0), jnp.where(live, k, 0.0), jnp.where(live, v, 0.0)
    eg = jnp.exp(g)
    qg, kbe, vb = q * eg, k * beta * eg, v * beta
    qk = _head_sum(q * k, sum_mat)

    same_head = (_iota((LANES, LANES), 0) // HEAD_DIM) == (_iota((LANES, LANES), 1) // HEAD_DIM)
    for s in range(group):
        _load_state_pairs(sbd.at[s], s0_ref.at[s])
    where = [(s, p, slice(s * ts, (s + 1) * ts), slice(p * LANES, (p + 1) * LANES))
             for s in range(group) for p in range(N_PAIRS)]
    s_prev = [sbd[s, p] for s, p, _, _ in where]
    s_bf = [x.astype(bf16) for x in s_prev]
    vnew = [vb[r, l] - jnp.dot(kbe[r, l].astype(bf16), sb, preferred_element_type=f32)
            for (_, _, r, l), sb in zip(where, s_bf)]
    outs = [jnp.dot(qg[r, l].astype(bf16), sb, preferred_element_type=f32) + qk[r, l] * vn
            for (_, _, r, l), sb, vn in zip(where, s_bf, vnew)]
    for (s, p, r, l), sp, vn in zip(where, s_prev, vnew):
        upd = _dot_tn(k[r, l].astype(bf16), vn.astype(bf16))
        sbd[s, p] = sp * eg[r.start:r.start + 1, l] + jnp.where(same_head, upd, 0.0)
    o = jnp.concatenate([jnp.concatenate(outs[s * N_PAIRS:(s + 1) * N_PAIRS], axis=1) for s in range(group)], axis=0)
    o = o * lax.rsqrt(_head_sum_sq(o, sum_mat) * (1.0 / HEAD_DIM) + EPS) * on_ref[...]
    o_ref[...] = (o * _silu(zz_ref[...].reshape(rows, GROUP_WIDTH))).reshape(group, ts, GROUP_WIDTH)
    for s in range(group):
        _store_state_pairs(sn_ref.at[s], sbd.at[s])


def _delta_tiles(t, zqkv_ref, zz_ref, zab_ref, conv0_ref, s0_ref, cw_ref, alog_ref, dtb_ref, on_ref,
                 o_ref, convn_ref, sn_ref, xp, sbd, *, ts, n_tiles, group):
    lead, rows, w3 = SUBLANES, group * ts, 3 * GROUP_WIDTH

    @pl.when(t == 0)
    def _():
        xp[:, 0:lead, :] = conv0_ref[...]
        for s in range(group):
            _load_state_pairs(sbd.at[s], s0_ref.at[s])

    xp[:, lead:lead + ts, :] = zqkv_ref[...]
    cw = cw_ref[...]
    y = cw[0:1, :][None] * xp[:, lead - 3:lead - 3 + ts, :]
    for j in range(1, DELTA_CONV):
        y = y + cw[j:j + 1, :][None] * xp[:, lead - 3 + j:lead - 3 + j + ts, :]
    convn_ref[...] = xp[:, lead + ts - (DELTA_CONV - 1):lead + ts, :]
    xp[:, 0:lead, :] = xp[:, ts:ts + lead, :]
    y = _silu(y).reshape(rows, w3)

    sum_mat = _same_head_matrix(GROUP_WIDTH, 1.0)
    yq, yk, v = y[:, :GROUP_WIDTH], y[:, GROUP_WIDTH:2 * GROUP_WIDTH], y[:, 2 * GROUP_WIDTH:]
    q = yq * lax.rsqrt(_head_sum_sq(yq, sum_mat) + EPS) * (HEAD_DIM ** -0.5)
    k = yk * lax.rsqrt(_head_sum_sq(yk, sum_mat) + EPS)
    g, beta = _delta_gates(zab_ref[...].reshape(rows, LANES), alog_ref, dtb_ref)

    first_head = _iota((1, LANES), 1) < HEAD_DIM
    same_head = (_iota((LANES, LANES), 0) // HEAD_DIM) == (_iota((LANES, LANES), 1) // HEAD_DIM)
    o = _delta_chunks(q, k, v, g, beta, sbd, first_head, same_head, ts, group)
    o = o * lax.rsqrt(_head_sum_sq(o, sum_mat) * (1.0 / HEAD_DIM) + EPS) * on_ref[...]
    o_ref[...] = (o * _silu(zz_ref[...].reshape(rows, GROUP_WIDTH))).reshape(group, ts, GROUP_WIDTH)

    @pl.when(t == n_tiles - 1)
    def _():
        for s in range(group):
            _store_state_pairs(sn_ref.at[s], sbd.at[s])


def _delta(z, n, seq, n_valid, conv0, s0, conv_w, alog_bc, dtb_bc, onorm_bc, ts, group=1):
    assert n_valid == seq or (n_valid == 1 and seq == ts)
    n_tiles = seq // ts
    assert n % group == 0
    w3 = 3 * GROUP_WIDTH
    z3 = z.reshape(n, seq, Z_WIDTH)
    row = lambda width, blk: pl.BlockSpec((group, ts, width), lambda b, t, blk=blk: (b, t, blk))
    small = lambda a: pl.BlockSpec(a.shape, lambda b, t: (0, 0))
    state = pl.BlockSpec((group, N_HEADS, HEAD_DIM, HEAD_DIM), lambda b, t: (b, 0, 0, 0))
    o, convn, sn = pl.pallas_call(
        functools.partial(_delta_kernel, ts=ts, n_valid=n_valid, n_tiles=n_tiles, group=group),
        grid=(n // group, n_tiles),
        in_specs=[row(w3, ZB_QKV_BLK), row(GROUP_WIDTH, ZB_Z_BLK), row(LANES, ZAB_BLK),
                  pl.BlockSpec((group, SUBLANES, w3), lambda b, t: (b, 0, 0)), _state_in(s0, group),
                  small(conv_w), small(alog_bc), small(dtb_bc), small(onorm_bc)],
        out_specs=[pl.BlockSpec((group, ts, GROUP_WIDTH), lambda b, t: (b, t, 0)),
                   pl.BlockSpec((group, DELTA_CONV - 1, w3), lambda b, t: (b, 0, 0)), state],
        out_shape=[jax.ShapeDtypeStruct((n, seq, GROUP_WIDTH), f32),
                   jax.ShapeDtypeStruct((n, DELTA_CONV - 1, w3), f32),
                   jax.ShapeDtypeStruct((n, N_HEADS, HEAD_DIM, HEAD_DIM), f32)],
        scratch_shapes=[pltpu.VMEM((group, ts + SUBLANES, w3), f32),
                        pltpu.VMEM((group, N_PAIRS, LANES, LANES), f32)],
        compiler_params=_params(2), name="delta")(z3, z3, z3, conv0, s0[0], conv_w, alog_bc, dtb_bc, onorm_bc)
    return o.reshape(n * seq, GROUP_WIDTH), convn, sn


def _retention_kernel(zq_ref, zk_ref, zv_ref, zg_ref, dx_ref, dc_ref, db_ref, cos_ref, sin_ref, lg_ref, on_ref,
                      cwd_ref, s0_ref, dconv0_ref, o_ref, od_ref, sn_ref, dconvn_ref, sbd, dmask, xpd,
                      *, tc, n_valid, n_tiles, group):
    b, t = pl.program_id(0), pl.program_id(1)
    lg = lg_ref[...]

    @pl.when((b == 0) & (t == 0))
    def _():
        col = _iota((tc, 2 * tc), 1)
        diff = (_iota((tc, 2 * tc), 0) - col % tc).astype(f32)
        for p in range(N_PAIRS):
            lgp = lg[:, p * LANES:(p + 1) * LANES]
            rate = jnp.where(col < tc, lgp[:, 0:1], lgp[:, HEAD_DIM:HEAD_DIM + 1])
            dmask[p] = jnp.where(diff >= 0.0, jnp.exp(jnp.maximum(diff, 0.0) * rate), 0.0)

    if n_valid == 1:
        _retention_single_tokens(zq_ref, zk_ref, zv_ref, zg_ref, dx_ref, dc_ref, db_ref, cos_ref, sin_ref, lg, on_ref,
                                 cwd_ref, s0_ref, dconv0_ref, o_ref, od_ref, sn_ref, dconvn_ref, sbd, xpd,
                                 tc=tc, group=group)
        return
    for s in range(group):
        rows = pl.ds(s * tc, tc)
        _retention_sequence(t, zq_ref.at[rows], zk_ref.at[rows], zv_ref.at[rows], zg_ref.at[rows], dx_ref.at[rows],
                            dc_ref.at[rows], db_ref.at[rows], cos_ref, sin_ref, lg, on_ref, cwd_ref, s0_ref.at[s],
                            dconv0_ref.at[s], o_ref.at[rows], od_ref.at[rows], sn_ref.at[s], dconvn_ref.at[s],
                            sbd.at[s], dmask, xpd.at[s], tc=tc, n_tiles=n_tiles)


def _retention_single_tokens(zq_ref, zk_ref, zv_ref, zg_ref, dx_ref, dc_ref, db_ref, cos_ref, sin_ref, lg, on_ref,
                             cwd_ref, s0_ref, dconv0_ref, o_ref, od_ref, sn_ref, dconvn_ref, sbd, xpd, *, tc, group):
    lead, rows = SUBLANES, group * tc
    cos = jnp.concatenate([cos_ref[...]] * group, axis=0)
    sin = jnp.concatenate([sin_ref[...]] * group, axis=0)
    live = _iota((rows, 1), 0) % tc == 0
    q = jnp.where(live, _rope_wide(zq_ref[...], cos, sin), 0.0)
    k = jnp.where(live, _rope_wide(zk_ref[...], cos, sin) * (HEAD_DIM ** -0.5), 0.0)
    v = jnp.where(live, zv_ref[...], 0.0)
    gamma = jnp.exp(lg)
    qd = q * gamma
    qk = _head_sum(q * k, _same_head_matrix(GROUP_WIDTH, 1.0))

    same_head = (_iota((LANES, LANES), 0) // HEAD_DIM) == (_iota((LANES, LANES), 1) // HEAD_DIM)
    for s in range(group):
        _load_state_pairs(sbd.at[s], s0_ref.at[s])
    where = [(s, p, slice(s * tc, (s + 1) * tc), slice(p * LANES, (p + 1) * LANES))
             for s in range(group) for p in range(N_PAIRS)]
    s_prev = [sbd[s, p] for s, p, _, _ in where]
    outs = [jnp.dot(qd[r, l].astype(bf16), sp.astype(bf16), preferred_element_type=f32) + qk[r, l] * v[r, l]
            for (_, _, r, l), sp in zip(where, s_prev)]
    for (s, p, r, l), sp in zip(where, s_prev):
        upd = _dot_tn(k[r, l].astype(bf16), v[r, l].astype(bf16))
        sbd[s, p] = sp * gamma[:, l] + jnp.where(same_head, upd, 0.0)
    o = jnp.concatenate([jnp.concatenate(outs[s * N_PAIRS:(s + 1) * N_PAIRS], axis=1) for s in range(group)], axis=0)
    mean_mat = _same_head_matrix(GROUP_WIDTH, 1.0 / HEAD_DIM)
    o = o * lax.rsqrt(_head_sum_sq(o, mean_mat) + EPS) * on_ref[...]
    o_ref[...] = o * _silu(zg_ref[...])
    for s in range(group):
        _store_state_pairs(sn_ref.at[s], sbd.at[s])

    xpd[:, 0:lead, :] = dconv0_ref[...]
    xpd[:, lead:lead + tc, :] = (dc_ref[...] * dx_ref[...]).reshape(group, tc, GROUP_WIDTH)
    cwd = cwd_ref[...]
    yd = cwd[0:1, :][None] * xpd[:, lead - 2:lead - 2 + tc, :]
    for j in range(1, SHORT_CONV):
        yd = yd + cwd[j:j + 1, :][None] * xpd[:, lead - 2 + j:lead - 2 + j + tc, :]
    od_ref[...] = db_ref[...] * yd.reshape(rows, GROUP_WIDTH)
    dconvn_ref[...] = xpd[:, lead + 1 - (SHORT_CONV - 1):lead + 1, :]


def _retention_sequence(t, zq_ref, zk_ref, zv_ref, zg_ref, dx_ref, dc_ref, db_ref, cos_ref, sin_ref, lg, on_ref,
                        cwd_ref, s0_ref, dconv0_ref, o_ref, od_ref, sn_ref, dconvn_ref, sbd, dmask, xpd,
                        *, tc, n_tiles):
    lead = SUBLANES

    @pl.when(t == 0)
    def _():
        _load_state_pairs(sbd, s0_ref)
        xpd[0:lead, :] = dconv0_ref[...]

    cos, sin = cos_ref[...], sin_ref[...]
    q = _rope_wide(zq_ref[...], cos, sin)
    k = _rope_wide(zk_ref[...], cos, sin) * (HEAD_DIM ** -0.5)
    v = zv_ref[...]
    pos = _iota((tc, 1), 0).astype(f32)
    qd = q * jnp.exp((pos + 1.0) * lg)
    kdk = k * jnp.exp((float(tc - 1) - pos) * lg)
    tile_decay = jnp.exp(float(tc) * lg)

    first_head = _iota((1, LANES), 1) < HEAD_DIM
    same_head = (_iota((LANES, LANES), 0) // HEAD_DIM) == (_iota((LANES, LANES), 1) // HEAD_DIM)
    outs = []
    for p in range(N_PAIRS):
        l = slice(p * LANES, (p + 1) * LANES)
        kst = _stack_heads(k[:, l], first_head).astype(bf16)
        vst = _stack_heads(v[:, l], first_head).astype(bf16)
        sc = _dot_nt(q[:, l].astype(bf16), kst) * dmask[p]
        s_prev = sbd[p]
        o = (jnp.dot(sc.astype(bf16), vst, preferred_element_type=f32)
             + jnp.dot(qd[:, l].astype(bf16), s_prev.astype(bf16), preferred_element_type=f32))
        upd = _dot_tn(kdk[:, l].astype(bf16), v[:, l].astype(bf16))
        sbd[p] = s_prev * tile_decay[:, l] + jnp.where(same_head, upd, 0.0)
        outs.append(o)
    o = jnp.concatenate(outs, axis=1)
    mean_mat = _same_head_matrix(GROUP_WIDTH, 1.0 / HEAD_DIM)
    o = o * lax.rsqrt(_head_sum_sq(o, mean_mat) + EPS) * on_ref[...]
    o_ref[...] = o * _silu(zg_ref[...])

    xpd[lead:lead + tc, :] = dc_ref[...] * dx_ref[...]
    cwd = cwd_ref[...]
    yd = cwd[0:1, :] * xpd[lead - 2:lead - 2 + tc, :]
    for j in range(1, SHORT_CONV):
        yd = yd + cwd[j:j + 1, :] * xpd[lead - 2 + j:lead - 2 + j + tc, :]
    od_ref[...] = db_ref[...] * yd
    dconvn_ref[...] = xpd[lead + tc - (SHORT_CONV - 1):lead + tc, :]
    xpd[0:lead, :] = xpd[tc:tc + lead, :]

    @pl.when(t == n_tiles - 1)
    def _():
        _store_state_pairs(sn_ref, sbd)


def _retention(z, n, seq, n_valid, s0, dconv0, cos_t, sin_t, lg_bc, onorm_bc, conv_d, tc, group=1):
    assert n_valid == seq or (n_valid == 1 and seq == tc)
    n_tiles = seq // tc
    assert n % group == 0 and (group == 1 or n_tiles == 1)
    row = lambda blk: pl.BlockSpec((group * tc, GROUP_WIDTH), lambda b, t, blk=blk: (b * n_tiles + t, blk))
    small = lambda a: pl.BlockSpec(a.shape, lambda b, t: (0, 0))
    tab = pl.BlockSpec((tc, LANES), lambda b, t: (t, 0))
    state = pl.BlockSpec((group, N_HEADS, HEAD_DIM, HEAD_DIM), lambda b, t: (b, 0, 0, 0))
    out_row = pl.BlockSpec((group * tc, GROUP_WIDTH), lambda b, t: (b * n_tiles + t, 0))
    out_rows = jax.ShapeDtypeStruct((n * seq, GROUP_WIDTH), f32)
    return pl.pallas_call(
        functools.partial(_retention_kernel, tc=tc, n_valid=n_valid, n_tiles=n_tiles, group=group),
        grid=(n // group, n_tiles),
        in_specs=[row(ZC_Q_BLK), row(ZC_K_BLK), row(ZC_V_BLK), row(ZC_G_BLK),
                  row(ZD_X_BLK), row(ZD_C_BLK), row(ZD_B_BLK), tab, tab,
                  small(lg_bc), small(onorm_bc), small(conv_d), _state_in(s0, group),
                  pl.BlockSpec((group, SUBLANES, GROUP_WIDTH), lambda b, t: (b, 0, 0))],
        out_specs=[out_row, out_row, state,
                   pl.BlockSpec((group, SHORT_CONV - 1, GROUP_WIDTH), lambda b, t: (b, 0, 0))],
        out_shape=[out_rows, out_rows, jax.ShapeDtypeStruct((n, N_HEADS, HEAD_DIM, HEAD_DIM), f32),
                   jax.ShapeDtypeStruct((n, SHORT_CONV - 1, GROUP_WIDTH), f32)],
        scratch_shapes=[pltpu.VMEM((group, N_PAIRS, LANES, LANES), f32), pltpu.VMEM((N_PAIRS, tc, 2 * tc), f32),
                        pltpu.VMEM((group, tc + SUBLANES, GROUP_WIDTH), f32)],
        compiler_params=_params(2), name="retention")(
            z, z, z, z, z, z, z, cos_t, sin_t, lg_bc, onorm_bc, conv_d, s0[0], dconv0)


def _rope_tables(pos, inv_freq):
    ang = pos.astype(f32)[:, None] * inv_freq[None, :]
    cos, sin = jnp.cos(ang), jnp.sin(ang)
    reps = LANES // HEAD_DIM
    return (jnp.tile(jnp.concatenate([cos, cos], axis=1), (1, reps)),
            jnp.tile(jnp.concatenate([-sin, sin], axis=1), (1, reps)))


def _per_head_lanes(v):
    return jnp.repeat(v.astype(f32), HEAD_DIM)[None, :]


def _tiled_lanes(v, width):
    return jnp.tile(v.astype(f32), width // HEAD_DIM)[None, :]


def _lead_pad(state):
    return jnp.pad(state, ((0, 0), (SUBLANES - state.shape[1], 0), (0, 0)))


def kernel(x_prompt, x_sample, cache_a_k, cache_a_v, state_b_conv, state_b_rec, state_c_rec, state_d_conv,
           norm_mix, w_in, q_norm_a, k_norm_a, conv_b, a_log_b, dt_bias_b, onorm_b, onorm_c, conv_d,
           w_out, norm_ffn, w_up, w_down):
    nb, seq, _ = x_prompt.shape
    nd, dseq, _ = x_sample.shape
    assert dseq == 1
    dpad = SUBLANES
    dgroup = max(g for g in (8, 4, 2, 1) if nd % g == 0)
    rope_freq = ROPE_THETA ** (-jnp.arange(0, HEAD_DIM, 2, dtype=f32) / HEAD_DIM)
    ret_freq = 1.0 / (10000.0 ** jnp.linspace(0.0, 1.0, HEAD_DIM // 2, dtype=f32))
    ret_lg = _per_head_lanes(jnp.log(1.0 - 2.0 ** (-5.0 - jnp.arange(N_HEADS, dtype=f32))))
    pos_p = jnp.arange(seq, dtype=jnp.int32)
    pos_d = PAST_LEN + jnp.arange(dpad, dtype=jnp.int32)
    cos_ap, sin_ap = _rope_tables(pos_p, rope_freq)
    cos_cp, sin_cp = _rope_tables(pos_p, ret_freq)
    cos_ad, sin_ad = _rope_tables(pos_d[:1], rope_freq)
    cos_cd, sin_cd = _rope_tables(pos_d, ret_freq)

    cache_kt = jnp.transpose(cache_a_k, (0, 1, 3, 4, 2))
    cache_vt = jnp.transpose(cache_a_v, (0, 1, 3, 4, 2))

    yp = x_prompt.reshape(nb * seq, D_MODEL)
    ys = x_sample.reshape(nd, D_MODEL)
    zeros_conv_b = jnp.zeros((nb, SUBLANES, 3 * GROUP_WIDTH), f32)
    zeros_conv_d = jnp.zeros((nb, SUBLANES, GROUP_WIDTH), f32)
    zeros_rec = (jnp.zeros((1, nb, N_HEADS, HEAD_DIM, HEAD_DIM), f32), 0)
    new_p, new_s = [], []
    dec_cache = prompt_cache = None
    w_pack = _pack_w_in(w_in)
    wo, wu, wd = w_out.astype(bf16), w_up.astype(bf16), w_down.astype(bf16)
    for l in range(DEPTH):
        g_mix, g_ffn = norm_mix[l][None, :], norm_ffn[l][None, :]
        qg, kg = _tiled_lanes(q_norm_a[l], LANES), _tiled_lanes(k_norm_a[l], LANES)
        qg3, kg3 = _tiled_lanes(q_norm_a[l], GROUP_WIDTH), _tiled_lanes(k_norm_a[l], GROUP_WIDTH)
        alog, dtb = _per_head_lanes(a_log_b[l]), _per_head_lanes(dt_bias_b[l])
        on_b, on_c = _tiled_lanes(onorm_b[l], GROUP_WIDTH), _tiled_lanes(onorm_c[l], GROUP_WIDTH)

        z = _inproj(yp, g_mix, w_pack, l, DENSE_ROW_TILE)
        oa, kt, vt = _attn_prompt(l, z, nb, seq, cos_ap, sin_ap, qg, kg, prompt_cache)
        prompt_cache = (kt, vt)
        ob, p_bconv, p_brec = _delta(z, nb, seq, seq, zeros_conv_b, zeros_rec, conv_b[l], alog, dtb, on_b, RECURRENT_TILE,
                                     group=2 if nb % 2 == 0 else 1)
        oc, od, p_crec, p_dconv = _retention(z, nb, seq, seq, zeros_rec, zeros_conv_d, cos_cp, sin_cp, ret_lg,
                                             on_c, conv_d[l], RECURRENT_TILE)
        yp = _ffn(yp, oa, ob, oc, od, wo, g_ffn, wu, wd, l, DENSE_ROW_TILE)
        new_p.append((kt, vt, p_bconv, p_brec, p_crec, p_dconv))

        zs = _inproj(ys, g_mix, w_pack, l, nd)
        qa, ka, va = _attn_decode_prep(zs, cos_ad, sin_ad, qg3, kg3)
        col = lambda a: jnp.broadcast_to(a.reshape(nd, N_HEADS, HEAD_DIM, 1), (nd, N_HEADS, HEAD_DIM, LANES))
        s_kt, s_vt, oa_col = _attn_decode(l, cache_kt, cache_vt, col(qa), col(ka), col(va), dec_cache)
        dec_cache = (s_kt, s_vt)
        oa_s = oa_col[..., 0].reshape(nd, GROUP_WIDTH)
        zpad = jnp.zeros((nd, dpad, Z_WIDTH), f32).at[:, 0, :].set(zs).reshape(nd * dpad, Z_WIDTH)
        ob_s, s_bconv, s_brec = _delta(zpad, nd, dpad, 1, _lead_pad(state_b_conv[l]), (state_b_rec, l), conv_b[l],
                                       alog, dtb, on_b, dpad, group=dgroup)
        oc_s, od_s, s_crec, s_dconv = _retention(zpad, nd, dpad, 1, (state_c_rec, l), _lead_pad(state_d_conv[l]),
                                                 cos_cd, sin_cd, ret_lg, on_c, conv_d[l], dpad, group=dgroup)
        first = lambda a: a.reshape(nd, dpad, GROUP_WIDTH)[:, 0, :]
        ys = _ffn(ys, oa_s, first(ob_s), first(oc_s), first(od_s), wo, g_ffn, wu, wd, l, nd)
        new_s.append((s_bconv, s_brec, s_crec, s_dconv))

    keep = min(WIN_MAX, seq)
    cache_out = lambda i: jnp.transpose(
        prompt_cache[i].reshape(DEPTH, nb, N_HEADS, HEAD_DIM, keep), (0, 1, 4, 2, 3))
    stack_p = lambda i: jnp.stack([s[i] for s in new_p])
    stack_s = lambda i: jnp.stack([s[i] for s in new_s])
    return (yp.reshape(nb, seq, D_MODEL), ys.reshape(nd, 1, D_MODEL),
            cache_out(0), cache_out(1), stack_p(2), stack_p(3), stack_p(4), stack_p(5),
            jnp.transpose(dec_cache[0], (0, 1, 4, 2, 3)), jnp.transpose(dec_cache[1], (0, 1, 4, 2, 3)),
            stack_s(0), stack_s(1), stack_s(2), stack_s(3))
```

```python
import functools

import jax
import jax.numpy as jnp
from jax import lax
from jax.experimental import pallas as pl
from jax.experimental.pallas import tpu as pltpu

f32, bf16, i32 = jnp.float32, jnp.bfloat16, jnp.int32

D_MODEL = 1024
HEAD_DIM = 64
N_HEADS = 6
N_PAIRS = N_HEADS // 2
GROUP_WIDTH = N_HEADS * HEAD_DIM
D_FF = 4 * D_MODEL
DIL_PAIRS = ((128, 1), (512, 4), (2048, 16))
WIN_MAX = 2048
ATT_BLOCK = 128
ROPE_THETA = 10000.0
DELTA_CONV = 4
SHORT_CONV = 3
CHUNK = 64
EPS = 1e-6
NEG = -1e30
PAST_LEN = 16384
DEPTH = 2
LANES = 128
SUBLANES = 8
VMEM_LIMIT = 56 * 1024 * 1024
DENSE_ROW_TILE = 512
RECURRENT_TILE = 256

IN_WIDTH = 14 * GROUP_WIDTH + 2 * N_HEADS
AB_OFFSET_SRC = 7 * GROUP_WIDTH
Z_WIDTH = 14 * GROUP_WIDTH + LANES
ZB_QKV_BLK = 1
ZB_Z_BLK = 6
ZC_Q_BLK, ZC_K_BLK, ZC_V_BLK, ZC_G_BLK = 7, 8, 9, 10
ZD_X_BLK, ZD_C_BLK, ZD_B_BLK = 11, 12, 13
ZAB_BLK = 14 * GROUP_WIDTH // LANES


def _iota(shape, dim):
    return lax.broadcasted_iota(i32, shape, dim)


def _params(n_axes):
    return pltpu.CompilerParams(dimension_semantics=("arbitrary",) * n_axes, vmem_limit_bytes=VMEM_LIMIT)


def _same_head_matrix(width, value):
    r = _iota((width, width), 0) // HEAD_DIM
    c = _iota((width, width), 1) // HEAD_DIM
    return jnp.where(r == c, value, 0.0).astype(bf16)


def _head_sum(x, mat, terms=2):
    return _dot_split_lhs(x, mat, terms)


def _head_sum_sq(x, mat):
    return _dot_split_lhs(x * x, mat, 1)


def _rope_slab(x, cos, sin_signed):
    p = _iota((1, LANES), 1) % HEAD_DIM
    partner = jnp.where(p < HEAD_DIM // 2, pltpu.roll(x, LANES - HEAD_DIM // 2, 1), pltpu.roll(x, HEAD_DIM // 2, 1))
    return x * cos + partner * sin_signed


def _rope_wide(x, cos, sin_signed):
    return jnp.concatenate(
        [_rope_slab(x[:, LANES * i:LANES * (i + 1)], cos, sin_signed) for i in range(x.shape[1] // LANES)], axis=1)


def _stack_heads(x, first_head):
    return jnp.concatenate([jnp.where(first_head, x, 0.0), jnp.where(first_head, 0.0, x)], axis=0)


def _dot_nt(a, b):
    return lax.dot_general(a, b, (((1,), (1,)), ((), ())), preferred_element_type=f32)


def _dot_tn(a, b):
    return lax.dot_general(a, b, (((0,), (0,)), ((), ())), preferred_element_type=f32)


def _silu(x):
    return x * jax.nn.sigmoid(x)


def _inproj_kernel(x_ref, g_ref, w_ref, o_ref):
    x = x_ref[...]
    ms = jnp.mean(x * x, axis=-1, keepdims=True)
    xn = (x * lax.rsqrt(ms + EPS) * g_ref[...]).astype(bf16)
    step = 4 * LANES
    for c0 in range(0, Z_WIDTH, step):
        cw = min(step, Z_WIDTH - c0)
        o_ref[:, c0:c0 + cw] = jnp.dot(xn, w_ref[:, c0:c0 + cw], preferred_element_type=f32)


def _pack_w_in_kernel(lo_ref, hi_ref, ab_ref, o_ref, *, n_lo, n_hi):
    i = pl.program_id(0)

    def emit(src):
        for l in range(DEPTH):
            o_ref[l] = src(l).T.astype(bf16)

    @pl.when(i < n_lo)
    def _():
        emit(lambda l: lo_ref[:, l, :])

    @pl.when((i >= n_lo) & (i < n_lo + n_hi))
    def _():
        emit(lambda l: hi_ref[:, l, :])

    @pl.when(i == n_lo + n_hi)
    def _():
        live = _iota((LANES, D_MODEL), 0) < 2 * N_HEADS
        emit(lambda l: jnp.where(live, ab_ref[:, l, :], 0.0))


def _pack_w_in(w_in):
    wt = jnp.transpose(w_in, (2, 0, 1))
    n_ab = 2 * N_HEADS
    n_lo = AB_OFFSET_SRC // LANES
    n_hi = (IN_WIDTH - AB_OFFSET_SRC - n_ab) // LANES
    assert n_lo * LANES == AB_OFFSET_SRC and (n_lo + n_hi + 1) * LANES == Z_WIDTH
    rows = lambda start: pl.BlockSpec((pl.Element(LANES), pl.Element(DEPTH), pl.Element(D_MODEL)),
                                      lambda i: (start(i), 0, 0))
    return pl.pallas_call(
        functools.partial(_pack_w_in_kernel, n_lo=n_lo, n_hi=n_hi), grid=(Z_WIDTH // LANES,),
        in_specs=[rows(lambda i: LANES * jnp.minimum(i, n_lo - 1)),
                  rows(lambda i: AB_OFFSET_SRC + n_ab + LANES * jnp.clip(i - n_lo, 0, n_hi - 1)),
                  rows(lambda i: AB_OFFSET_SRC)],
        out_specs=pl.BlockSpec((DEPTH, D_MODEL, LANES), lambda i: (0, 0, i)),
        out_shape=jax.ShapeDtypeStruct((DEPTH, D_MODEL, Z_WIDTH), bf16),
        compiler_params=_params(1), name="pack_w_in")(wt, wt, wt)


def _layer_block(a, layer):
    return pl.BlockSpec((None,) + a.shape[1:], lambda *_: (layer, 0, 0), pipeline_mode=pl.Buffered(1))


def _inproj(x2d, gain, w_pack, layer, tm):
    m = x2d.shape[0]
    return pl.pallas_call(
        _inproj_kernel,
        grid=(m // tm,),
        in_specs=[pl.BlockSpec((tm, D_MODEL), lambda i: (i, 0)),
                  pl.BlockSpec((1, D_MODEL), lambda i: (0, 0)),
                  _layer_block(w_pack, layer)],
        out_specs=pl.BlockSpec((tm, Z_WIDTH), lambda i: (i, 0)),
        out_shape=jax.ShapeDtypeStruct((m, Z_WIDTH), f32),
        compiler_params=_params(1), name="inproj")(x2d, gain, w_pack)


def _ffn_kernel(x_ref, oa_ref, ob_ref, oc_ref, od_ref, wo_ref, g_ref, wu_ref, wd_ref, y_ref):
    mix = jnp.concatenate([r[...].astype(bf16) for r in (oa_ref, ob_ref, oc_ref, od_ref)], axis=1)
    h = x_ref[...] + jnp.dot(mix, wo_ref[...], preferred_element_type=f32)
    ms = jnp.mean(h * h, axis=-1, keepdims=True)
    hn = (h * lax.rsqrt(ms + EPS) * g_ref[...]).astype(bf16)
    y_ref[...] = h
    step = D_FF // 4
    for c in range(0, D_FF, step):
        u = jnp.dot(hn, wu_ref[:, c:c + step], preferred_element_type=f32)
        u = jnp.square(jnp.maximum(u, 0.0)).astype(bf16)
        y_ref[...] += jnp.dot(u, wd_ref[c:c + step, :], preferred_element_type=f32)


def _ffn(x2d, oa, ob, oc, od, w_out, gain, w_up, w_down, layer, tm):
    m = x2d.shape[0]
    row = lambda width: pl.BlockSpec((tm, width), lambda i: (i, 0))
    whole = lambda a: _layer_block(a, layer)
    return pl.pallas_call(
        _ffn_kernel,
        grid=(m // tm,),
        in_specs=[row(D_MODEL), row(GROUP_WIDTH), row(GROUP_WIDTH), row(GROUP_WIDTH), row(GROUP_WIDTH),
                  whole(w_out), pl.BlockSpec((1, D_MODEL), lambda i: (0, 0)), whole(w_up), whole(w_down)],
        out_specs=row(D_MODEL),
        out_shape=jax.ShapeDtypeStruct((m, D_MODEL), f32),
        compiler_params=_params(1), name="ffn")(x2d, oa, ob, oc, od, w_out, gain, w_up, w_down)


def _attn_prompt_kernel(q_ref, k_ref, v_ref, cos_ref, sin_ref, qg_ref, kg_ref, *rest,
                        seq, keep, wide_dil, wide_pitch, aliased):
    if aliased:
        rest = rest[2:]
    (o_ref, kt_ref, vt_ref, q_s, ks, acc_s, m_s, t_acc, t_m, qw, kw, vw, bias_first, bias_rest) = rest
    mean_mat = _same_head_matrix(LANES, 1.0 / HEAD_DIM)
    first_head = _iota((1, LANES), 1) < HEAD_DIM
    rc = 512
    blk = ATT_BLOCK

    def prep(i, carry):
        rows = pl.ds(pl.multiple_of(i * rc, rc), rc)
        cos, sin = cos_ref[rows, :], sin_ref[rows, :]
        q, k, v = q_ref[rows, :], k_ref[rows, :], v_ref[rows, :]
        q = q * lax.rsqrt(_head_sum_sq(q, mean_mat) + EPS) * qg_ref[...]
        k = k * lax.rsqrt(_head_sum_sq(k, mean_mat) + EPS) * kg_ref[...]
        q = _rope_slab(q, cos, sin) * (HEAD_DIM ** -0.5)
        q_s[rows, :] = q
        ks[rows, :] = _rope_slab(k, cos, sin)
        k = ks[rows, :]
        for g in range(rc // wide_dil):
            src = slice(g * wide_dil, (g + 1) * wide_dil)
            dst = pl.ds(pl.multiple_of(i * (rc // wide_dil * wide_pitch), SUBLANES) + g * wide_pitch, wide_dil)
            qw[dst, :], kw[dst, :], vw[dst, :] = q[src, :], k[src, :], v[src, :]
        return carry

    lax.fori_loop(0, seq // rc, prep, 0)

    for c in range(keep // rc):
        r0 = seq - keep + c * rc
        kt_ref[:, c * rc:(c + 1) * rc] = ks[r0:r0 + rc, :].T
        vt_ref[:, c * rc:(c + 1) * rc] = v_ref[r0:r0 + rc, :].T

    qi, kj = _iota((blk, blk), 0), _iota((blk, blk), 1)
    bias_first[...] = jnp.where(kj <= qi, 0.0, NEG)
    qi, kj = _iota((blk, 2 * blk), 0), _iota((blk, 2 * blk), 1)
    bias_rest[...] = jnp.where((kj >= qi) & (kj <= qi + blk), 0.0, NEG)

    def rows_of(st, n, dil):
        return pl.ds(st, n, stride=dil) if dil > 1 else pl.ds(st, n)

    def tile_group(tiles, dil, dst_acc, dst_m, pitch):
        staged = []
        for r, b, first in tiles:
            nk = blk if first else 2 * blk
            if dil == wide_dil:
                qrow = wide_pitch * (b * blk) + r
                krow = qrow if first else qrow - wide_pitch * blk
                rows, krows = pl.ds(qrow, blk, stride=wide_pitch), pl.ds(krow, nk, stride=wide_pitch)
                q, k, v = qw[rows, :], kw[krows, :].astype(bf16), vw[krows, :]
            else:
                start = r + b * (blk * dil)
                kstart = start if first else start - blk * dil
                rows, krows = rows_of(start, blk, dil), rows_of(kstart, nk, dil)
                q, k, v = q_s[rows, :], ks[krows, :].astype(bf16), v_ref[krows, :]
            qs = [jnp.where(first_head, q, 0.0).astype(bf16), jnp.where(first_head, 0.0, q).astype(bf16)]
            vs = [jnp.where(first_head, v, 1.0).astype(bf16), jnp.where(first_head, 1.0, v).astype(bf16)]
            bias = bias_first[...] if first else bias_rest[...]
            dst = pl.ds(pl.multiple_of(r * pitch + b * blk, SUBLANES), blk)
            staged.append((dst, vs, [_dot_nt(qs[hh], k) + bias for hh in range(2)]))
        probs = []
        for dst, vs, scores in staged:
            for hh in range(2):
                mt = jnp.max(scores[hh], axis=1, keepdims=True)
                probs.append((dst, hh, vs[hh], jnp.exp(scores[hh] - mt).astype(bf16), mt))
        for dst, hh, v, p, mt in probs:
            dst_acc[hh, dst, :] = jnp.dot(p, v, preferred_element_type=f32)
            dst_m[hh, dst, :] = jnp.broadcast_to(mt, (blk, LANES))

    merge_group = 4
    tiles_per_body = 16

    for bi, (window, dil) in enumerate(DIL_PAIRS):
        nb = seq // dil // blk
        dst_acc, dst_m = (acc_s, m_s) if bi == 0 else (t_acc, t_m)
        pitch = nb * blk + (SUBLANES if dil == wide_dil else 0)
        if nb >= tiles_per_body:
            assert nb % tiles_per_body == 0

            def per_residue(r, carry, dil=dil, nb=nb, dst_acc=dst_acc, dst_m=dst_m, pitch=pitch):
                tile_group([(r, b, b == 0) for b in range(tiles_per_body)], dil, dst_acc, dst_m, pitch)

                def per_group(g, c):
                    tile_group([(r, g * tiles_per_body + u, False) for u in range(tiles_per_body)],
                               dil, dst_acc, dst_m, pitch)
                    return c

                lax.fori_loop(1, nb // tiles_per_body, per_group, 0)
                return carry

            lax.fori_loop(0, dil, per_residue, 0)
        else:
            res_per_body = tiles_per_body // nb
            assert dil % res_per_body == 0

            def per_residues(g, carry, dil=dil, nb=nb, dst_acc=dst_acc, dst_m=dst_m, pitch=pitch, rpb=res_per_body):
                tile_group([(g * rpb + u, b, b == 0) for u in range(rpb) for b in range(nb)],
                           dil, dst_acc, dst_m, pitch)
                return carry

            lax.fori_loop(0, dil // res_per_body, per_residues, 0)

        if bi > 0 and dil != wide_dil:
            def merge(g, carry, dil=dil, nb=nb):
                pending = []
                for u in range(merge_group):
                    idx = g * merge_group + u
                    r, b = idx // nb, idx % nb
                    rows = rows_of(r + b * (blk * dil), blk, dil)
                    src = pl.ds(pl.multiple_of(idx * blk, blk), blk)
                    for hh in range(2):
                        mo, mt = m_s[hh, rows, :], t_m[hh, src, :]
                        mn = jnp.maximum(mo, mt)
                        an = acc_s[hh, rows, :] * jnp.exp(mo - mn) + t_acc[hh, src, :] * jnp.exp(mt - mn)
                        pending.append((hh, rows, an, mn))
                for hh, rows, an, mn in pending:
                    acc_s[hh, rows, :] = an
                    m_s[hh, rows, :] = mn
                return carry

            lax.fori_loop(0, seq // blk // merge_group, merge, 0)
        elif bi > 0:
            def merge_wide(g, carry, dil=dil, pitch=pitch):
                pending = []
                for u in range(merge_group):
                    t0 = pl.multiple_of((g * merge_group + u) * blk, blk)
                    rows = pl.ds(t0, blk)
                    for hh in range(2):
                        ta, tm = [], []
                        for v8 in range(blk // SUBLANES):
                            src = pl.ds(((v8 * SUBLANES) % dil) * pitch + t0 // dil + (v8 * SUBLANES) // dil,
                                        SUBLANES, stride=pitch)
                            ta.append(t_acc[hh, src, :])
                            tm.append(t_m[hh, src, :])
                        ta, mt = jnp.concatenate(ta, axis=0), jnp.concatenate(tm, axis=0)
                        mo = m_s[hh, rows, :]
                        mn = jnp.maximum(mo, mt)
                        pending.append(acc_s[hh, rows, :] * jnp.exp(mo - mn) + ta * jnp.exp(mt - mn))
                    a0, a1 = pending[-2:]
                    half = LANES // 2
                    o_ref[rows, :] = jnp.where(first_head, a0 / pltpu.roll(a0, half, 1), a1 / pltpu.roll(a1, half, 1))
                return carry

            lax.fori_loop(0, seq // blk // merge_group, merge_wide, 0)


def _attn_prompt(layer, z, n, seq, cos_t, sin_t, qgain, kgain, prev_out):
    keep = min(WIN_MAX, seq)
    for window, dil in DIL_PAIRS:
        assert window // dil == ATT_BLOCK and seq % (dil * ATT_BLOCK) == 0
    wide = [dil for _, dil in DIL_PAIRS if dil % SUBLANES == 0]
    assert len(wide) == 1 and ATT_BLOCK % wide[0] == 0 and DIL_PAIRS[-1][1] == wide[0]
    wide_dil, wide_pitch = wide[0], wide[0] + SUBLANES
    per_group = GROUP_WIDTH // LANES
    zspec = lambda base: pl.BlockSpec((seq, LANES), lambda b, p, base=base: (b, base * per_group + p),
                                      pipeline_mode=pl.Buffered(1))
    whole = lambda rows: pl.BlockSpec((rows, LANES), lambda b, p: (0, 0), pipeline_mode=pl.Buffered(1))
    slab = lambda: pltpu.VMEM((seq, LANES), f32)
    slab2 = lambda: pltpu.VMEM((2, seq, LANES), f32)
    branch = lambda: pltpu.VMEM((2, seq + wide_dil * SUBLANES, LANES), f32)
    padded = lambda: pltpu.VMEM((seq // wide_dil * wide_pitch, LANES), f32)
    in_specs = [zspec(0), zspec(1), zspec(2), whole(seq), whole(seq), whole(1), whole(1)]
    args = [z, z, z, cos_t, sin_t, qgain, kgain]
    aliases = {}
    if prev_out is not None:
        in_specs += [pl.BlockSpec(memory_space=pl.ANY), pl.BlockSpec(memory_space=pl.ANY)]
        args += list(prev_out)
        aliases = {len(args) - 2: 1, len(args) - 1: 2}
    cache_spec = pl.BlockSpec((None, None, LANES, keep), lambda b, p: (layer, b, p, 0))
    cache_shape = jax.ShapeDtypeStruct((DEPTH, n, GROUP_WIDTH, keep), f32)
    return pl.pallas_call(
        functools.partial(_attn_prompt_kernel, seq=seq, keep=keep, wide_dil=wide_dil, wide_pitch=wide_pitch,
                          aliased=prev_out is not None),
        grid=(n, N_PAIRS),
        in_specs=in_specs,
        out_specs=[pl.BlockSpec((seq, LANES), lambda b, p: (b, p)), cache_spec, cache_spec],
        out_shape=[jax.ShapeDtypeStruct((n * seq, GROUP_WIDTH), f32), cache_shape, cache_shape],
        input_output_aliases=aliases,
        scratch_shapes=[slab(), slab(), slab2(), slab2(), branch(), branch(), padded(), padded(), padded(),
                        pltpu.VMEM((ATT_BLOCK, ATT_BLOCK), f32), pltpu.VMEM((ATT_BLOCK, 2 * ATT_BLOCK), f32)],
        compiler_params=_params(2), name=f"attn_prompt_l{layer}")(*args)


def _attn_decode_prep_kernel(z_ref, cos_ref, sin_ref, qg_ref, kg_ref, q_ref, k_ref, v_ref):
    mean_mat = _same_head_matrix(GROUP_WIDTH, 1.0 / HEAD_DIM)
    z = z_ref[...]
    q, k = z[:, :GROUP_WIDTH], z[:, GROUP_WIDTH:2 * GROUP_WIDTH]
    q = q * lax.rsqrt(_head_sum_sq(q, mean_mat) + EPS) * qg_ref[...]
    k = k * lax.rsqrt(_head_sum_sq(k, mean_mat) + EPS) * kg_ref[...]
    q_ref[...] = _rope_wide(q, cos_ref[...], sin_ref[...]) * (HEAD_DIM ** -0.5)
    k_ref[...] = _rope_wide(k, cos_ref[...], sin_ref[...])
    v_ref[...] = z[:, 2 * GROUP_WIDTH:]


def _attn_decode_prep(z, cos_row, sin_row, qgain, kgain):
    m = z.shape[0]
    small = lambda a: pl.BlockSpec(a.shape, lambda i: (0, 0))
    out = jax.ShapeDtypeStruct((m, GROUP_WIDTH), f32)
    ospec = pl.BlockSpec((m, GROUP_WIDTH), lambda i: (0, 0))
    return pl.pallas_call(
        _attn_decode_prep_kernel, grid=(1,),
        in_specs=[pl.BlockSpec((m, 3 * GROUP_WIDTH), lambda i: (0, 0)), small(cos_row), small(sin_row),
                  small(qgain), small(kgain)],
        out_specs=[ospec, ospec, ospec], out_shape=[out, out, out],
        compiler_params=_params(1), name="attn_decode_prep")(z, cos_row, sin_row, qgain, kgain)


def _attn_decode_kernel(*refs, buf, aliased):
    if aliased:
        kc_ref, vc_ref, q_ref, kn_ref, vn_ref, _, _, ko_ref, vo_ref, o_ref = refs
    else:
        kc_ref, vc_ref, q_ref, kn_ref, vn_ref, ko_ref, vo_ref, o_ref = refs
    kc = kc_ref[...]
    vc = vc_ref[...]
    qc = q_ref[...][:, :, :1]
    knew = kn_ref[...][:, :, :1]
    vnew = vn_ref[...][:, :, :1]
    dist = buf - _iota((1, 1, buf), 2)
    mult = jnp.zeros((1, 1, buf), f32)
    for window, dil in DIL_PAIRS:
        mult = mult + jnp.where((dist <= window) & (dist % dil == 0), 1.0, 0.0)
    s = jnp.sum(kc * qc, axis=1, keepdims=True)
    s_new = jnp.sum(knew * qc, axis=1, keepdims=True)
    s = jnp.where(mult > 0.0, s, NEG)
    mx = jnp.maximum(jnp.max(s, axis=2, keepdims=True), s_new)
    p = mult * jnp.exp(s - mx)
    p_new = float(len(DIL_PAIRS)) * jnp.exp(s_new - mx)
    den = jnp.sum(p, axis=2, keepdims=True) + p_new
    o = (jnp.sum(vc * p, axis=2, keepdims=True) + vnew * p_new) / den
    o_ref[...] = jnp.broadcast_to(o, o_ref.shape)
    last = _iota(kc.shape, 2) == buf - 1
    ko_ref[...] = jnp.where(last, knew, pltpu.roll(kc, buf - 1, 2))
    vo_ref[...] = jnp.where(last, vnew, pltpu.roll(vc, buf - 1, 2))


def _attn_decode(layer, cache_kt, cache_vt, q_col, k_col, v_col, prev_out):
    depth, nb, nh, hd, buf = cache_kt.shape
    cache_spec = pl.BlockSpec((None, None, nh, hd, buf), lambda b: (layer, b, 0, 0, 0))
    col_spec = pl.BlockSpec((None, nh, hd, LANES), lambda b: (b, 0, 0, 0))
    in_specs = [cache_spec, cache_spec, col_spec, col_spec, col_spec]
    args = [cache_kt, cache_vt, q_col, k_col, v_col]
    aliases = {}
    if prev_out is not None:
        in_specs += [pl.BlockSpec(memory_space=pl.ANY), pl.BlockSpec(memory_space=pl.ANY)]
        args += list(prev_out)
        aliases = {5: 0, 6: 1}
    cache_shape = jax.ShapeDtypeStruct(cache_kt.shape, f32)
    return pl.pallas_call(
        functools.partial(_attn_decode_kernel, buf=buf, aliased=prev_out is not None),
        grid=(nb,), in_specs=in_specs,
        out_specs=[cache_spec, cache_spec, col_spec],
        out_shape=[cache_shape, cache_shape, jax.ShapeDtypeStruct((nb, nh, hd, LANES), f32)],
        input_output_aliases=aliases,
        compiler_params=_params(1), name=f"attn_decode_l{layer}")(*args)


SOLVE_BLOCK = 16


def _solve_unit_lower_pairs(systems, first_head):
    ng = CHUNK // SUBLANES
    gpb = SOLVE_BLOCK // SUBLANES
    col_in_head = _iota((SOLVE_BLOCK, LANES), 1) % HEAD_DIM
    split = lambda x: [x[SUBLANES * g:SUBLANES * (g + 1), :] for g in range(ng)]
    ags = [split(a) for a, _ in systems]
    xgs = [split(x) for _, x in systems]
    for blk in range(CHUNK // SOLVE_BLOCK):
        r0 = blk * SOLVE_BLOCK
        if blk > 0:
            for (a, _), xg in zip(systems, xgs):
                left = jnp.where(col_in_head < r0, a[r0:r0 + SOLVE_BLOCK, :], 0.0).astype(bf16)
                solved = _stack_heads(jnp.concatenate(xg, axis=0), first_head).astype(bf16)
                upd = jnp.dot(left, solved, preferred_element_type=f32)
                for g in range(gpb):
                    xg[blk * gpb + g] = xg[blk * gpb + g] - upd[SUBLANES * g:SUBLANES * (g + 1), :]
        for j in range(r0, r0 + SOLVE_BLOCK - 1):
            g0, r = divmod(j, SUBLANES)
            for ag, xg in zip(ags, xgs):
                xrow = jnp.broadcast_to(xg[g0][r:r + 1, :], (SUBLANES, LANES))
                for g in range(g0, (blk + 1) * gpb):
                    c0 = jnp.broadcast_to(ag[g][:, j:j + 1], (SUBLANES, LANES))
                    c1 = jnp.broadcast_to(ag[g][:, HEAD_DIM + j:HEAD_DIM + j + 1], (SUBLANES, LANES))
                    xg[g] = xg[g] - jnp.where(first_head, c0, c1) * xrow
    return [jnp.concatenate(xg, axis=0) for xg in xgs]


def _bf16_terms(x, terms):
    pieces, rest = [], x
    for t in range(terms):
        pieces.append(rest.astype(bf16))
        if t + 1 < terms:
            rest = rest - pieces[-1].astype(f32)
    return pieces


def _dot_split_lhs(x, mat, terms):
    parts = [jnp.dot(p, mat, preferred_element_type=f32) for p in _bf16_terms(x, terms)]
    return functools.reduce(lambda a, b: a + b, parts)


def _dot_split_rhs(mat, x, terms):
    parts = [jnp.dot(mat, p, preferred_element_type=f32) for p in _bf16_terms(x, terms)]
    return functools.reduce(lambda a, b: a + b, parts)


def _state_in(s0, group):
    layer = s0[1]
    return pl.BlockSpec((None, group, N_HEADS, HEAD_DIM, HEAD_DIM), lambda b, t: (layer, b, 0, 0, 0))


def _load_state_pairs(sbd, s0_ref):
    sbd[...] = jnp.zeros_like(sbd)
    for h in range(N_HEADS):
        p, hh = divmod(h, 2)
        sbd[p, hh * HEAD_DIM:(hh + 1) * HEAD_DIM, hh * HEAD_DIM:(hh + 1) * HEAD_DIM] = s0_ref[h]


def _store_state_pairs(sn_ref, sbd):
    for h in range(N_HEADS):
        p, hh = divmod(h, 2)
        sn_ref[h] = sbd[p, hh * HEAD_DIM:(hh + 1) * HEAD_DIM, hh * HEAD_DIM:(hh + 1) * HEAD_DIM]


def _delta_chunks(q, k, v, g, beta, sbd, first_head, same_head, ts, group):
    ri, cj = _iota((ts, ts), 0), _iota((ts, ts), 1)
    same = (ri // CHUNK) == (cj // CHUNK)
    cum_mat, tot_mat = (same & (cj <= ri)).astype(bf16), same.astype(bf16)
    g_terms = _bf16_terms(g, 2)
    rsum = lambda mat: jnp.concatenate(
        [functools.reduce(lambda a, b: a + b,
                          [jnp.dot(mat, t[s * ts:(s + 1) * ts, :], preferred_element_type=f32) for t in g_terms])
         for s in range(group)], axis=0)
    gcum = rsum(cum_mat)
    glast = rsum(tot_mat)
    eg = jnp.exp(gcum)
    qg, kb, vb = q * eg, k * beta, v * beta
    kbe = kb * eg
    kd = k * jnp.exp(glast - gcum)
    dl = jnp.exp(glast)

    ii = _iota((CHUNK, LANES), 0)
    jj = _iota((CHUNK, LANES), 1) % HEAD_DIM
    eye2 = (ii == jj).astype(f32)
    ones = jnp.ones((CHUNK, CHUNK), bf16)
    n_chunks = ts // CHUNK
    where = {(s, c, p): (slice(s * ts + c * CHUNK, s * ts + (c + 1) * CHUNK), slice(p * LANES, (p + 1) * LANES))
             for s in range(group) for c in range(n_chunks) for p in range(N_PAIRS)}

    amats, qks = {}, {}
    for key, (r, l) in where.items():
        gc = gcum[r, l]
        grow = _dot_split_rhs(ones, eye2 * gc, 2)
        dm = jnp.where(ii >= jj, jnp.exp(jnp.minimum(gc - grow, 0.0)), 0.0)
        kst = _stack_heads(k[r, l], first_head).astype(bf16)
        kk = _dot_nt(kb[r, l].astype(bf16), kst)
        qks[key] = (_dot_nt(q[r, l].astype(bf16), kst) * dm).astype(bf16)
        amats[key] = jnp.where(ii > jj, kk * dm, 0.0)

    outs = {}
    for c in range(n_chunks):
        keys = [(s, c, p) for s in range(group) for p in range(N_PAIRS)]
        s_prev = {key: sbd[key[0], key[2]] for key in keys}
        s_bf = {key: s_prev[key].astype(bf16) for key in keys}
        rhs = [vb[where[key]] - jnp.dot(kbe[where[key]].astype(bf16), s_bf[key], preferred_element_type=f32)
               for key in keys]
        vnews = _solve_unit_lower_pairs([(amats[key], x) for key, x in zip(keys, rhs)], first_head)
        for key, vnew in zip(keys, vnews):
            r, l = where[key]
            vst = _stack_heads(vnew, first_head).astype(bf16)
            outs[key] = (jnp.dot(qg[r, l].astype(bf16), s_bf[key], preferred_element_type=f32)
                         + jnp.dot(qks[key], vst, preferred_element_type=f32))
            upd = _dot_tn(kd[r, l].astype(bf16), vnew.astype(bf16))
            sbd[key[0], key[2]] = s_prev[key] * dl[r.start:r.start + 1, l] + jnp.where(same_head, upd, 0.0)
    return jnp.concatenate(
        [jnp.concatenate([outs[(s, c, p)] for p in range(N_PAIRS)], axis=1)
         for s in range(group) for c in range(n_chunks)], axis=0)


def _delta_kernel(zqkv_ref, zz_ref, zab_ref, conv0_ref, s0_ref, cw_ref, alog_ref, dtb_ref, on_ref,
                  o_ref, convn_ref, sn_ref, xp, sbd, *, ts, n_valid, n_tiles, group):
    t = pl.program_id(1)
    if n_valid == 1:
        _delta_single_tokens(zqkv_ref, zz_ref, zab_ref, conv0_ref, s0_ref, cw_ref, alog_ref, dtb_ref, on_ref,
                             o_ref, convn_ref, sn_ref, xp, sbd, ts=ts, group=group)
        return
    _delta_tiles(t, zqkv_ref, zz_ref, zab_ref, conv0_ref, s0_ref, cw_ref, alog_ref, dtb_ref, on_ref,
                 o_ref, convn_ref, sn_ref, xp, sbd, ts=ts, n_tiles=n_tiles, group=group)


def _delta_gates(zab, alog_ref, dtb_ref):
    src = _iota((LANES, GROUP_WIDTH), 0)
    head = _iota((LANES, GROUP_WIDTH), 1) // HEAD_DIM
    zab_terms = _bf16_terms(zab, 2)
    pick = lambda sel: functools.reduce(
        lambda a, b: a + b, [jnp.dot(t, sel.astype(bf16), preferred_element_type=f32) for t in zab_terms])
    a_bc, b_bc = pick(src == head), pick(src == head + N_HEADS)
    xa = a_bc + dtb_ref[...]
    g = -jnp.exp(alog_ref[...]) * (jnp.maximum(xa, 0.0) + jnp.log1p(jnp.exp(-jnp.abs(xa))))
    return g, jax.nn.sigmoid(b_bc)


def _delta_single_tokens(zqkv_ref, zz_ref, zab_ref, conv0_ref, s0_ref, cw_ref, alog_ref, dtb_ref, on_ref,
                         o_ref, convn_ref, sn_ref, xp, sbd, *, ts, group):
    lead, rows, w3 = SUBLANES, group * ts, 3 * GROUP_WIDTH
    xp[:, 0:lead, :] = conv0_ref[...]
    xp[:, lead:lead + ts, :] = zqkv_ref[...]
    cw = cw_ref[...]
    y = cw[0:1, :][None] * xp[:, lead - 3:lead - 3 + ts, :]
    for j in range(1, DELTA_CONV):
        y = y + cw[j:j + 1, :][None] * xp[:, lead - 3 + j:lead - 3 + j + ts, :]
    convn_ref[...] = xp[:, lead + 1 - (DELTA_CONV - 1):lead + 1, :]
    y = _silu(y).reshape(rows, w3)

    sum_mat = _same_head_matrix(GROUP_WIDTH, 1.0)
    yq, yk, v = y[:, :GROUP_WIDTH], y[:, GROUP_WIDTH:2 * GROUP_WIDTH], y[:, 2 * GROUP_WIDTH:]
    q = yq * lax.rsqrt(_head_sum_sq(yq, sum_mat) + EPS) * (HEAD_DIM ** -0.5)
    k = yk * lax.rsqrt(_head_sum_sq(yk, sum_mat) + EPS)
    g, beta = _delta_gates(zab_ref[...].reshape(rows, LANES), alog_ref, dtb_ref)
    live = _iota((rows, 1), 0) % ts == 0
    g, beta = jnp.where(live, g, 0.0), jnp.where(live, beta, 0.0)
    q, k, v = jnp.where(live, q, 0.0), jnp.where(live, k, 0.0), jnp.where(live, v, 0.0)
    eg = jnp.exp(g)
    qg, kbe, vb = q * eg, k * beta * eg, v * beta
    qk = _head_sum(q * k, sum_mat)

    same_head = (_iota((LANES, LANES), 0) // HEAD_DIM) == (_iota((LANES, LANES), 1) // HEAD_DIM)
    for s in range(group):
        _load_state_pairs(sbd.at[s], s0_ref.at[s])
    where = [(s, p, slice(s * ts, (s + 1) * ts), slice(p * LANES, (p + 1) * LANES))
             for s in range(group) for p in range(N_PAIRS)]
    s_prev = [sbd[s, p] for s, p, _, _ in where]
    s_bf = [x.astype(bf16) for x in s_prev]
    vnew = [vb[r, l] - jnp.dot(kbe[r, l].astype(bf16), sb, preferred_element_type=f32)
            for (_, _, r, l), sb in zip(where, s_bf)]
    outs = [jnp.dot(qg[r, l].astype(bf16), sb, preferred_element_type=f32) + qk[r, l] * vn
            for (_, _, r, l), sb, vn in zip(where, s_bf, vnew)]
    for (s, p, r, l), sp, vn in zip(where, s_prev, vnew):
        upd = _dot_tn(k[r, l].astype(bf16), vn.astype(bf16))
        sbd[s, p] = sp * eg[r.start:r.start + 1, l] + jnp.where(same_head, upd, 0.0)
    o = jnp.concatenate([jnp.concatenate(outs[s * N_PAIRS:(s + 1) * N_PAIRS], axis=1) for s in range(group)], axis=0)
    o = o * lax.rsqrt(_head_sum_sq(o, sum_mat) * (1.0 / HEAD_DIM) + EPS) * on_ref[...]
    o_ref[...] = (o * _silu(zz_ref[...].reshape(rows, GROUP_WIDTH))).reshape(group, ts, GROUP_WIDTH)
    for s in range(group):
        _store_state_pairs(sn_ref.at[s], sbd.at[s])


def _delta_tiles(t, zqkv_ref, zz_ref, zab_ref, conv0_ref, s0_ref, cw_ref, alog_ref, dtb_ref, on_ref,
                 o_ref, convn_ref, sn_ref, xp, sbd, *, ts, n_tiles, group):
    lead, rows, w3 = SUBLANES, group * ts, 3 * GROUP_WIDTH

    @pl.when(t == 0)
    def _():
        xp[:, 0:lead, :] = conv0_ref[...]
        for s in range(group):
            _load_state_pairs(sbd.at[s], s0_ref.at[s])

    xp[:, lead:lead + ts, :] = zqkv_ref[...]
    cw = cw_ref[...]
    y = cw[0:1, :][None] * xp[:, lead - 3:lead - 3 + ts, :]
    for j in range(1, DELTA_CONV):
        y = y + cw[j:j + 1, :][None] * xp[:, lead - 3 + j:lead - 3 + j + ts, :]
    convn_ref[...] = xp[:, lead + ts - (DELTA_CONV - 1):lead + ts, :]
    xp[:, 0:lead, :] = xp[:, ts:ts + lead, :]
    y = _silu(y).reshape(rows, w3)

    sum_mat = _same_head_matrix(GROUP_WIDTH, 1.0)
    yq, yk, v = y[:, :GROUP_WIDTH], y[:, GROUP_WIDTH:2 * GROUP_WIDTH], y[:, 2 * GROUP_WIDTH:]
    q = yq * lax.rsqrt(_head_sum_sq(yq, sum_mat) + EPS) * (HEAD_DIM ** -0.5)
    k = yk * lax.rsqrt(_head_sum_sq(yk, sum_mat) + EPS)
    g, beta = _delta_gates(zab_ref[...].reshape(rows, LANES), alog_ref, dtb_ref)

    first_head = _iota((1, LANES), 1) < HEAD_DIM
    same_head = (_iota((LANES, LANES), 0) // HEAD_DIM) == (_iota((LANES, LANES), 1) // HEAD_DIM)
    o = _delta_chunks(q, k, v, g, beta, sbd, first_head, same_head, ts, group)
    o = o * lax.rsqrt(_head_sum_sq(o, sum_mat) * (1.0 / HEAD_DIM) + EPS) * on_ref[...]
    o_ref[...] = (o * _silu(zz_ref[...].reshape(rows, GROUP_WIDTH))).reshape(group, ts, GROUP_WIDTH)

    @pl.when(t == n_tiles - 1)
    def _():
        for s in range(group):
            _store_state_pairs(sn_ref.at[s], sbd.at[s])


def _delta(z, n, seq, n_valid, conv0, s0, conv_w, alog_bc, dtb_bc, onorm_bc, ts, group=1):
    assert n_valid == seq or (n_valid == 1 and seq == ts)
    n_tiles = seq // ts
    assert n % group == 0
    w3 = 3 * GROUP_WIDTH
    z3 = z.reshape(n, seq, Z_WIDTH)
    row = lambda width, blk: pl.BlockSpec((group, ts, width), lambda b, t, blk=blk: (b, t, blk))
    small = lambda a: pl.BlockSpec(a.shape, lambda b, t: (0, 0))
    state = pl.BlockSpec((group, N_HEADS, HEAD_DIM, HEAD_DIM), lambda b, t: (b, 0, 0, 0))
    o, convn, sn = pl.pallas_call(
        functools.partial(_delta_kernel, ts=ts, n_valid=n_valid, n_tiles=n_tiles, group=group),
        grid=(n // group, n_tiles),
        in_specs=[row(w3, ZB_QKV_BLK), row(GROUP_WIDTH, ZB_Z_BLK), row(LANES, ZAB_BLK),
                  pl.BlockSpec((group, SUBLANES, w3), lambda b, t: (b, 0, 0)), _state_in(s0, group),
                  small(conv_w), small(alog_bc), small(dtb_bc), small(onorm_bc)],
        out_specs=[pl.BlockSpec((group, ts, GROUP_WIDTH), lambda b, t: (b, t, 0)),
                   pl.BlockSpec((group, DELTA_CONV - 1, w3), lambda b, t: (b, 0, 0)), state],
        out_shape=[jax.ShapeDtypeStruct((n, seq, GROUP_WIDTH), f32),
                   jax.ShapeDtypeStruct((n, DELTA_CONV - 1, w3), f32),
                   jax.ShapeDtypeStruct((n, N_HEADS, HEAD_DIM, HEAD_DIM), f32)],
        scratch_shapes=[pltpu.VMEM((group, ts + SUBLANES, w3), f32),
                        pltpu.VMEM((group, N_PAIRS, LANES, LANES), f32)],
        compiler_params=_params(2), name="delta")(z3, z3, z3, conv0, s0[0], conv_w, alog_bc, dtb_bc, onorm_bc)
    return o.reshape(n * seq, GROUP_WIDTH), convn, sn


def _retention_kernel(zq_ref, zk_ref, zv_ref, zg_ref, dx_ref, dc_ref, db_ref, cos_ref, sin_ref, lg_ref, on_ref,
                      cwd_ref, s0_ref, dconv0_ref, o_ref, od_ref, sn_ref, dconvn_ref, sbd, dmask, xpd,
                      *, tc, n_valid, n_tiles, group):
    b, t = pl.program_id(0), pl.program_id(1)
    lg = lg_ref[...]

    @pl.when((b == 0) & (t == 0))
    def _():
        col = _iota((tc, 2 * tc), 1)
        diff = (_iota((tc, 2 * tc), 0) - col % tc).astype(f32)
        for p in range(N_PAIRS):
            lgp = lg[:, p * LANES:(p + 1) * LANES]
            rate = jnp.where(col < tc, lgp[:, 0:1], lgp[:, HEAD_DIM:HEAD_DIM + 1])
            dmask[p] = jnp.where(diff >= 0.0, jnp.exp(jnp.maximum(diff, 0.0) * rate), 0.0)

    if n_valid == 1:
        _retention_single_tokens(zq_ref, zk_ref, zv_ref, zg_ref, dx_ref, dc_ref, db_ref, cos_ref, sin_ref, lg, on_ref,
                                 cwd_ref, s0_ref, dconv0_ref, o_ref, od_ref, sn_ref, dconvn_ref, sbd, xpd,
                                 tc=tc, group=group)
        return
    for s in range(group):
        _retention_sequence(t, zq_ref.at[s], zk_ref.at[s], zv_ref.at[s], zg_ref.at[s], dx_ref.at[s],
                            dc_ref.at[s], db_ref.at[s], cos_ref, sin_ref, lg, on_ref, cwd_ref, s0_ref.at[s],
                            dconv0_ref.at[s], o_ref.at[s], od_ref.at[s], sn_ref.at[s], dconvn_ref.at[s],
                            sbd.at[s], dmask, xpd.at[s], tc=tc, n_tiles=n_tiles)


def _retention_single_tokens(zq_ref, zk_ref, zv_ref, zg_ref, dx_ref, dc_ref, db_ref, cos_ref, sin_ref, lg, on_ref,
                             cwd_ref, s0_ref, dconv0_ref, o_ref, od_ref, sn_ref, dconvn_ref, sbd, xpd, *, tc, group):
    lead, rows = SUBLANES, group * tc
    cos = jnp.concatenate([cos_ref[...]] * group, axis=0)
    sin = jnp.concatenate([sin_ref[...]] * group, axis=0)
    live = _iota((rows, 1), 0) % tc == 0
    flat = lambda ref: ref[...].reshape(rows, GROUP_WIDTH)
    q = jnp.where(live, _rope_wide(flat(zq_ref), cos, sin), 0.0)
    k = jnp.where(live, _rope_wide(flat(zk_ref), cos, sin) * (HEAD_DIM ** -0.5), 0.0)
    v = jnp.where(live, flat(zv_ref), 0.0)
    gamma = jnp.exp(lg)
    qd = q * gamma
    qk = _head_sum(q * k, _same_head_matrix(GROUP_WIDTH, 1.0))

    same_head = (_iota((LANES, LANES), 0) // HEAD_DIM) == (_iota((LANES, LANES), 1) // HEAD_DIM)
    for s in range(group):
        _load_state_pairs(sbd.at[s], s0_ref.at[s])
    where = [(s, p, slice(s * tc, (s + 1) * tc), slice(p * LANES, (p + 1) * LANES))
             for s in range(group) for p in range(N_PAIRS)]
    s_prev = [sbd[s, p] for s, p, _, _ in where]
    outs = [jnp.dot(qd[r, l].astype(bf16), sp.astype(bf16), preferred_element_type=f32) + qk[r, l] * v[r, l]
            for (_, _, r, l), sp in zip(where, s_prev)]
    for (s, p, r, l), sp in zip(where, s_prev):
        upd = _dot_tn(k[r, l].astype(bf16), v[r, l].astype(bf16))
        sbd[s, p] = sp * gamma[:, l] + jnp.where(same_head, upd, 0.0)
    o = jnp.concatenate([jnp.concatenate(outs[s * N_PAIRS:(s + 1) * N_PAIRS], axis=1) for s in range(group)], axis=0)
    mean_mat = _same_head_matrix(GROUP_WIDTH, 1.0 / HEAD_DIM)
    o = o * lax.rsqrt(_head_sum_sq(o, mean_mat) + EPS) * on_ref[...]
    o_ref[...] = (o * _silu(flat(zg_ref))).reshape(group, tc, GROUP_WIDTH)
    for s in range(group):
        _store_state_pairs(sn_ref.at[s], sbd.at[s])

    xpd[:, 0:lead, :] = dconv0_ref[...]
    xpd[:, lead:lead + tc, :] = dc_ref[...] * dx_ref[...]
    cwd = cwd_ref[...]
    yd = cwd[0:1, :][None] * xpd[:, lead - 2:lead - 2 + tc, :]
    for j in range(1, SHORT_CONV):
        yd = yd + cwd[j:j + 1, :][None] * xpd[:, lead - 2 + j:lead - 2 + j + tc, :]
    od_ref[...] = db_ref[...] * yd
    dconvn_ref[...] = xpd[:, lead + 1 - (SHORT_CONV - 1):lead + 1, :]


def _retention_sequence(t, zq_ref, zk_ref, zv_ref, zg_ref, dx_ref, dc_ref, db_ref, cos_ref, sin_ref, lg, on_ref,
                        cwd_ref, s0_ref, dconv0_ref, o_ref, od_ref, sn_ref, dconvn_ref, sbd, dmask, xpd,
                        *, tc, n_tiles):
    lead = SUBLANES

    @pl.when(t == 0)
    def _():
        _load_state_pairs(sbd, s0_ref)
        xpd[0:lead, :] = dconv0_ref[...]

    cos, sin = cos_ref[...], sin_ref[...]
    q = _rope_wide(zq_ref[...], cos, sin)
    k = _rope_wide(zk_ref[...], cos, sin) * (HEAD_DIM ** -0.5)
    v = zv_ref[...]
    pos = _iota((tc, 1), 0).astype(f32)
    qd = q * jnp.exp((pos + 1.0) * lg)
    kdk = k * jnp.exp((float(tc - 1) - pos) * lg)
    tile_decay = jnp.exp(float(tc) * lg)

    first_head = _iota((1, LANES), 1) < HEAD_DIM
    same_head = (_iota((LANES, LANES), 0) // HEAD_DIM) == (_iota((LANES, LANES), 1) // HEAD_DIM)
    outs = []
    for p in range(N_PAIRS):
        l = slice(p * LANES, (p + 1) * LANES)
        kst = _stack_heads(k[:, l], first_head).astype(bf16)
        vst = _stack_heads(v[:, l], first_head).astype(bf16)
        sc = _dot_nt(q[:, l].astype(bf16), kst) * dmask[p]
        s_prev = sbd[p]
        o = (jnp.dot(sc.astype(bf16), vst, preferred_element_type=f32)
             + jnp.dot(qd[:, l].astype(bf16), s_prev.astype(bf16), preferred_element_type=f32))
        upd = _dot_tn(kdk[:, l].astype(bf16), v[:, l].astype(bf16))
        sbd[p] = s_prev * tile_decay[:, l] + jnp.where(same_head, upd, 0.0)
        outs.append(o)
    o = jnp.concatenate(outs, axis=1)
    mean_mat = _same_head_matrix(GROUP_WIDTH, 1.0 / HEAD_DIM)
    o = o * lax.rsqrt(_head_sum_sq(o, mean_mat) + EPS) * on_ref[...]
    o_ref[...] = o * _silu(zg_ref[...])

    xpd[lead:lead + tc, :] = dc_ref[...] * dx_ref[...]
    cwd = cwd_ref[...]
    yd = cwd[0:1, :] * xpd[lead - 2:lead - 2 + tc, :]
    for j in range(1, SHORT_CONV):
        yd = yd + cwd[j:j + 1, :] * xpd[lead - 2 + j:lead - 2 + j + tc, :]
    od_ref[...] = db_ref[...] * yd
    dconvn_ref[...] = xpd[lead + tc - (SHORT_CONV - 1):lead + tc, :]
    xpd[0:lead, :] = xpd[tc:tc + lead, :]

    @pl.when(t == n_tiles - 1)
    def _():
        _store_state_pairs(sn_ref, sbd)


def _retention(z, n, seq, n_valid, s0, dconv0, cos_t, sin_t, lg_bc, onorm_bc, conv_d, tc, group=1):
    assert n_valid == seq or (n_valid == 1 and seq == tc)
    n_tiles = seq // tc
    assert n % group == 0
    z = z.reshape(n, seq, Z_WIDTH)
    row = lambda blk: pl.BlockSpec((group, tc, GROUP_WIDTH), lambda b, t, blk=blk: (b, t, blk))
    small = lambda a: pl.BlockSpec(a.shape, lambda b, t: (0, 0))
    tab = pl.BlockSpec((tc, LANES), lambda b, t: (t, 0))
    state = pl.BlockSpec((group, N_HEADS, HEAD_DIM, HEAD_DIM), lambda b, t: (b, 0, 0, 0))
    out_row = pl.BlockSpec((group, tc, GROUP_WIDTH), lambda b, t: (b, t, 0))
    out_rows = jax.ShapeDtypeStruct((n, seq, GROUP_WIDTH), f32)
    o, od, sn, dconvn = pl.pallas_call(
        functools.partial(_retention_kernel, tc=tc, n_valid=n_valid, n_tiles=n_tiles, group=group),
        grid=(n // group, n_tiles),
        in_specs=[row(ZC_Q_BLK), row(ZC_K_BLK), row(ZC_V_BLK), row(ZC_G_BLK),
                  row(ZD_X_BLK), row(ZD_C_BLK), row(ZD_B_BLK), tab, tab,
                  small(lg_bc), small(onorm_bc), small(conv_d), _state_in(s0, group),
                  pl.BlockSpec((group, SUBLANES, GROUP_WIDTH), lambda b, t: (b, 0, 0))],
        out_specs=[out_row, out_row, state,
                   pl.BlockSpec((group, SHORT_CONV - 1, GROUP_WIDTH), lambda b, t: (b, 0, 0))],
        out_shape=[out_rows, out_rows, jax.ShapeDtypeStruct((n, N_HEADS, HEAD_DIM, HEAD_DIM), f32),
                   jax.ShapeDtypeStruct((n, SHORT_CONV - 1, GROUP_WIDTH), f32)],
        scratch_shapes=[pltpu.VMEM((group, N_PAIRS, LANES, LANES), f32), pltpu.VMEM((N_PAIRS, tc, 2 * tc), f32),
                        pltpu.VMEM((group, tc + SUBLANES, GROUP_WIDTH), f32)],
        compiler_params=_params(2), name="retention")(
            z, z, z, z, z, z, z, cos_t, sin_t, lg_bc, onorm_bc, conv_d, s0[0], dconv0)
    return o.reshape(n * seq, GROUP_WIDTH), od.reshape(n * seq, GROUP_WIDTH), sn, dconvn


def _rope_tables(pos, inv_freq):
    ang = pos.astype(f32)[:, None] * inv_freq[None, :]
    cos, sin = jnp.cos(ang), jnp.sin(ang)
    reps = LANES // HEAD_DIM
    return (jnp.tile(jnp.concatenate([cos, cos], axis=1), (1, reps)),
            jnp.tile(jnp.concatenate([-sin, sin], axis=1), (1, reps)))


def _per_head_lanes(v):
    return jnp.repeat(v.astype(f32), HEAD_DIM)[None, :]


def _tiled_lanes(v, width):
    return jnp.tile(v.astype(f32), width // HEAD_DIM)[None, :]


def _lead_pad(state):
    return jnp.pad(state, ((0, 0), (SUBLANES - state.shape[1], 0), (0, 0)))


def kernel(x_prompt, x_sample, cache_a_k, cache_a_v, state_b_conv, state_b_rec, state_c_rec, state_d_conv,
           norm_mix, w_in, q_norm_a, k_norm_a, conv_b, a_log_b, dt_bias_b, onorm_b, onorm_c, conv_d,
           w_out, norm_ffn, w_up, w_down):
    nb, seq, _ = x_prompt.shape
    nd, dseq, _ = x_sample.shape
    assert dseq == 1
    dpad = SUBLANES
    dgroup = max(g for g in (8, 4, 2, 1) if nd % g == 0)
    rope_freq = ROPE_THETA ** (-jnp.arange(0, HEAD_DIM, 2, dtype=f32) / HEAD_DIM)
    ret_freq = 1.0 / (10000.0 ** jnp.linspace(0.0, 1.0, HEAD_DIM // 2, dtype=f32))
    ret_lg = _per_head_lanes(jnp.log(1.0 - 2.0 ** (-5.0 - jnp.arange(N_HEADS, dtype=f32))))
    pos_p = jnp.arange(seq, dtype=jnp.int32)
    pos_d = PAST_LEN + jnp.arange(dpad, dtype=jnp.int32)
    cos_ap, sin_ap = _rope_tables(pos_p, rope_freq)
    cos_cp, sin_cp = _rope_tables(pos_p, ret_freq)
    cos_ad, sin_ad = _rope_tables(pos_d[:1], rope_freq)
    cos_cd, sin_cd = _rope_tables(pos_d, ret_freq)

    cache_kt = jnp.transpose(cache_a_k, (0, 1, 3, 4, 2))
    cache_vt = jnp.transpose(cache_a_v, (0, 1, 3, 4, 2))

    yp = x_prompt.reshape(nb * seq, D_MODEL)
    ys = x_sample.reshape(nd, D_MODEL)
    zeros_conv_b = jnp.zeros((nb, SUBLANES, 3 * GROUP_WIDTH), f32)
    zeros_conv_d = jnp.zeros((nb, SUBLANES, GROUP_WIDTH), f32)
    zeros_rec = (jnp.zeros((1, nb, N_HEADS, HEAD_DIM, HEAD_DIM), f32), 0)
    new_p, new_s = [], []
    dec_cache = prompt_cache = None
    w_pack = _pack_w_in(w_in)
    wo, wu, wd = w_out.astype(bf16), w_up.astype(bf16), w_down.astype(bf16)
    for l in range(DEPTH):
        g_mix, g_ffn = norm_mix[l][None, :], norm_ffn[l][None, :]
        qg, kg = _tiled_lanes(q_norm_a[l], LANES), _tiled_lanes(k_norm_a[l], LANES)
        qg3, kg3 = _tiled_lanes(q_norm_a[l], GROUP_WIDTH), _tiled_lanes(k_norm_a[l], GROUP_WIDTH)
        alog, dtb = _per_head_lanes(a_log_b[l]), _per_head_lanes(dt_bias_b[l])
        on_b, on_c = _tiled_lanes(onorm_b[l], GROUP_WIDTH), _tiled_lanes(onorm_c[l], GROUP_WIDTH)

        z = _inproj(yp, g_mix, w_pack, l, DENSE_ROW_TILE)
        oa, kt, vt = _attn_prompt(l, z, nb, seq, cos_ap, sin_ap, qg, kg, prompt_cache)
        prompt_cache = (kt, vt)
        ob, p_bconv, p_brec = _delta(z, nb, seq, seq, zeros_conv_b, zeros_rec, conv_b[l], alog, dtb, on_b, RECURRENT_TILE,
                                     group=2 if nb % 2 == 0 else 1)
        oc, od, p_crec, p_dconv = _retention(z, nb, seq, seq, zeros_rec, zeros_conv_d, cos_cp, sin_cp, ret_lg,
                                             on_c, conv_d[l], RECURRENT_TILE, group=2 if nb % 2 == 0 else 1)
        yp = _ffn(yp, oa, ob, oc, od, wo, g_ffn, wu, wd, l, DENSE_ROW_TILE)
        new_p.append((kt, vt, p_bconv, p_brec, p_crec, p_dconv))

        zs = _inproj(ys, g_mix, w_pack, l, nd)
        qa, ka, va = _attn_decode_prep(zs, cos_ad, sin_ad, qg3, kg3)
        col = lambda a: jnp.broadcast_to(a.reshape(nd, N_HEADS, HEAD_DIM, 1), (nd, N_HEADS, HEAD_DIM, LANES))
        s_kt, s_vt, oa_col = _attn_decode(l, cache_kt, cache_vt, col(qa), col(ka), col(va), dec_cache)
        dec_cache = (s_kt, s_vt)
        oa_s = oa_col[..., 0].reshape(nd, GROUP_WIDTH)
        zpad = jnp.zeros((nd, dpad, Z_WIDTH), f32).at[:, 0, :].set(zs).reshape(nd * dpad, Z_WIDTH)
        ob_s, s_bconv, s_brec = _delta(zpad, nd, dpad, 1, _lead_pad(state_b_conv[l]), (state_b_rec, l), conv_b[l],
                                       alog, dtb, on_b, dpad, group=dgroup)
        oc_s, od_s, s_crec, s_dconv = _retention(zpad, nd, dpad, 1, (state_c_rec, l), _lead_pad(state_d_conv[l]),
                                                 cos_cd, sin_cd, ret_lg, on_c, conv_d[l], dpad, group=dgroup)
        first = lambda a: a.reshape(nd, dpad, GROUP_WIDTH)[:, 0, :]
        ys = _ffn(ys, oa_s, first(ob_s), first(oc_s), first(od_s), wo, g_ffn, wu, wd, l, nd)
        new_s.append((s_bconv, s_brec, s_crec, s_dconv))

    keep = min(WIN_MAX, seq)
    cache_out = lambda i: jnp.transpose(
        prompt_cache[i].reshape(DEPTH, nb, N_HEADS, HEAD_DIM, keep), (0, 1, 4, 2, 3))
    stack_p = lambda i: jnp.stack([s[i] for s in new_p])
    stack_s = lambda i: jnp.stack([s[i] for s in new_s])
    return (yp.reshape(nb, seq, D_MODEL), ys.reshape(nd, 1, D_MODEL),
            cache_out(0), cache_out(1), stack_p(2), stack_p(3), stack_p(4), stack_p(5),
            jnp.transpose(dec_cache[0], (0, 1, 4, 2, 3)), jnp.transpose(dec_cache[1], (0, 1, 4, 2, 3)),
            stack_s(0), stack_s(1), stack_s(2), stack_s(3))
```

```python
import functools

import jax
import jax.numpy as jnp
from jax import lax
from jax.experimental import pallas as pl
from jax.experimental.pallas import tpu as pltpu

f32, bf16, i32 = jnp.float32, jnp.bfloat16, jnp.int32

D_MODEL = 1024
HEAD_DIM = 64
N_HEADS = 6
N_PAIRS = N_HEADS // 2
GROUP_WIDTH = N_HEADS * HEAD_DIM
D_FF = 4 * D_MODEL
DIL_PAIRS = ((128, 1), (512, 4), (2048, 16))
WIN_MAX = 2048
ATT_BLOCK = 128
ROPE_THETA = 10000.0
DELTA_CONV = 4
SHORT_CONV = 3
CHUNK = 64
EPS = 1e-6
NEG = -1e30
PAST_LEN = 16384
DEPTH = 2
LANES = 128
SUBLANES = 8
VMEM_LIMIT = 56 * 1024 * 1024
DENSE_ROW_TILE = 512
RECURRENT_TILE = 256

IN_WIDTH = 14 * GROUP_WIDTH + 2 * N_HEADS
AB_OFFSET_SRC = 7 * GROUP_WIDTH
Z_WIDTH = 14 * GROUP_WIDTH + LANES
ZB_QKV_BLK = 1
ZB_Z_BLK = 6
ZC_Q_BLK, ZC_K_BLK, ZC_V_BLK, ZC_G_BLK = 7, 8, 9, 10
ZD_X_BLK, ZD_C_BLK, ZD_B_BLK = 11, 12, 13
ZAB_BLK = 14 * GROUP_WIDTH // LANES


def _iota(shape, dim):
    return lax.broadcasted_iota(i32, shape, dim)


def _params(n_axes):
    return pltpu.CompilerParams(dimension_semantics=("arbitrary",) * n_axes, vmem_limit_bytes=VMEM_LIMIT)


def _same_head_matrix(width, value):
    r = _iota((width, width), 0) // HEAD_DIM
    c = _iota((width, width), 1) // HEAD_DIM
    return jnp.where(r == c, value, 0.0).astype(bf16)


def _head_sum(x, mat, terms=2):
    return _dot_split_lhs(x, mat, terms)


def _head_sum_sq(x, mat):
    return _dot_split_lhs(x * x, mat, 1)


def _rope_slab(x, cos, sin_signed):
    p = _iota((1, LANES), 1) % HEAD_DIM
    partner = jnp.where(p < HEAD_DIM // 2, pltpu.roll(x, LANES - HEAD_DIM // 2, 1), pltpu.roll(x, HEAD_DIM // 2, 1))
    return x * cos + partner * sin_signed


def _rope_wide(x, cos, sin_signed):
    return jnp.concatenate(
        [_rope_slab(x[:, LANES * i:LANES * (i + 1)], cos, sin_signed) for i in range(x.shape[1] // LANES)], axis=1)


def _stack_heads(x, first_head):
    return jnp.concatenate([jnp.where(first_head, x, 0.0), jnp.where(first_head, 0.0, x)], axis=0)


def _dot_nt(a, b):
    return lax.dot_general(a, b, (((1,), (1,)), ((), ())), preferred_element_type=f32)


def _dot_tn(a, b):
    return lax.dot_general(a, b, (((0,), (0,)), ((), ())), preferred_element_type=f32)


def _silu(x):
    return x * jax.nn.sigmoid(x)


def _inproj_kernel(x_ref, g_ref, w_ref, o_ref):
    x = x_ref[...]
    ms = jnp.mean(x * x, axis=-1, keepdims=True)
    xn = (x * lax.rsqrt(ms + EPS) * g_ref[...]).astype(bf16)
    step = 4 * LANES
    for c0 in range(0, Z_WIDTH, step):
        cw = min(step, Z_WIDTH - c0)
        o_ref[:, c0:c0 + cw] = jnp.dot(xn, w_ref[:, c0:c0 + cw], preferred_element_type=f32)


def _pack_w_in_kernel(lo_ref, hi_ref, ab_ref, o_ref, *, n_lo, n_hi):
    i = pl.program_id(0)

    def emit(src):
        for l in range(DEPTH):
            o_ref[l] = src(l).T.astype(bf16)

    @pl.when(i < n_lo)
    def _():
        emit(lambda l: lo_ref[:, l, :])

    @pl.when((i >= n_lo) & (i < n_lo + n_hi))
    def _():
        emit(lambda l: hi_ref[:, l, :])

    @pl.when(i == n_lo + n_hi)
    def _():
        live = _iota((LANES, D_MODEL), 0) < 2 * N_HEADS
        emit(lambda l: jnp.where(live, ab_ref[:, l, :], 0.0))


def _pack_w_in(w_in):
    wt = jnp.transpose(w_in, (2, 0, 1))
    n_ab = 2 * N_HEADS
    n_lo = AB_OFFSET_SRC // LANES
    n_hi = (IN_WIDTH - AB_OFFSET_SRC - n_ab) // LANES
    assert n_lo * LANES == AB_OFFSET_SRC and (n_lo + n_hi + 1) * LANES == Z_WIDTH
    rows = lambda start: pl.BlockSpec((pl.Element(LANES), pl.Element(DEPTH), pl.Element(D_MODEL)),
                                      lambda i: (start(i), 0, 0))
    return pl.pallas_call(
        functools.partial(_pack_w_in_kernel, n_lo=n_lo, n_hi=n_hi), grid=(Z_WIDTH // LANES,),
        in_specs=[rows(lambda i: LANES * jnp.minimum(i, n_lo - 1)),
                  rows(lambda i: AB_OFFSET_SRC + n_ab + LANES * jnp.clip(i - n_lo, 0, n_hi - 1)),
                  rows(lambda i: AB_OFFSET_SRC)],
        out_specs=pl.BlockSpec((DEPTH, D_MODEL, LANES), lambda i: (0, 0, i)),
        out_shape=jax.ShapeDtypeStruct((DEPTH, D_MODEL, Z_WIDTH), bf16),
        compiler_params=_params(1), name="pack_w_in")(wt, wt, wt)


def _layer_block(a, layer):
    return pl.BlockSpec((None,) + a.shape[1:], lambda *_: (layer, 0, 0), pipeline_mode=pl.Buffered(1))


def _inproj(x2d, gain, w_pack, layer, tm):
    m = x2d.shape[0]
    return pl.pallas_call(
        _inproj_kernel,
        grid=(m // tm,),
        in_specs=[pl.BlockSpec((tm, D_MODEL), lambda i: (i, 0)),
                  pl.BlockSpec((1, D_MODEL), lambda i: (0, 0)),
                  _layer_block(w_pack, layer)],
        out_specs=pl.BlockSpec((tm, Z_WIDTH), lambda i: (i, 0)),
        out_shape=jax.ShapeDtypeStruct((m, Z_WIDTH), f32),
        compiler_params=_params(1), name="inproj")(x2d, gain, w_pack)


def _ffn_kernel(x_ref, oa_ref, ob_ref, oc_ref, od_ref, wo_ref, g_ref, wu_ref, wd_ref, y_ref):
    mix = jnp.concatenate([r[...].astype(bf16) for r in (oa_ref, ob_ref, oc_ref, od_ref)], axis=1)
    h = x_ref[...] + jnp.dot(mix, wo_ref[...], preferred_element_type=f32)
    ms = jnp.mean(h * h, axis=-1, keepdims=True)
    hn = (h * lax.rsqrt(ms + EPS) * g_ref[...]).astype(bf16)
    y_ref[...] = h
    step = D_FF // 4
    for c in range(0, D_FF, step):
        u = jnp.dot(hn, wu_ref[:, c:c + step], preferred_element_type=f32)
        u = jnp.square(jnp.maximum(u, 0.0)).astype(bf16)
        y_ref[...] += jnp.dot(u, wd_ref[c:c + step, :], preferred_element_type=f32)


def _ffn(x2d, oa, ob, oc, od, w_out, gain, w_up, w_down, layer, tm):
    m = x2d.shape[0]
    row = lambda width: pl.BlockSpec((tm, width), lambda i: (i, 0))
    whole = lambda a: _layer_block(a, layer)
    return pl.pallas_call(
        _ffn_kernel,
        grid=(m // tm,),
        in_specs=[row(D_MODEL), row(GROUP_WIDTH), row(GROUP_WIDTH), row(GROUP_WIDTH), row(GROUP_WIDTH),
                  whole(w_out), pl.BlockSpec((1, D_MODEL), lambda i: (0, 0)), whole(w_up), whole(w_down)],
        out_specs=row(D_MODEL),
        out_shape=jax.ShapeDtypeStruct((m, D_MODEL), f32),
        compiler_params=_params(1), name="ffn")(x2d, oa, ob, oc, od, w_out, gain, w_up, w_down)


def _attn_prompt_kernel(q_ref, k_ref, v_ref, cos_ref, sin_ref, qg_ref, kg_ref, *rest,
                        seq, keep, wide_dil, wide_pitch, aliased):
    if aliased:
        rest = rest[2:]
    (o_ref, kt_ref, vt_ref, q_s, ks, acc_s, m_s, t_acc, t_m, qw, kw, vw, bias_first, bias_rest) = rest
    mean_mat = _same_head_matrix(LANES, 1.0 / HEAD_DIM)
    first_head = _iota((1, LANES), 1) < HEAD_DIM
    rc = 512
    blk = ATT_BLOCK

    def prep(i, carry):
        rows = pl.ds(pl.multiple_of(i * rc, rc), rc)
        cos, sin = cos_ref[rows, :], sin_ref[rows, :]
        q, k, v = q_ref[rows, :], k_ref[rows, :], v_ref[rows, :]
        q = q * lax.rsqrt(_head_sum_sq(q, mean_mat) + EPS) * qg_ref[...]
        k = k * lax.rsqrt(_head_sum_sq(k, mean_mat) + EPS) * kg_ref[...]
        q = _rope_slab(q, cos, sin) * (HEAD_DIM ** -0.5)
        q_s[rows, :] = q
        ks[rows, :] = _rope_slab(k, cos, sin)
        k = ks[rows, :]
        for g in range(rc // wide_dil):
            src = slice(g * wide_dil, (g + 1) * wide_dil)
            dst = pl.ds(pl.multiple_of(i * (rc // wide_dil * wide_pitch), SUBLANES) + g * wide_pitch, wide_dil)
            qw[dst, :], kw[dst, :], vw[dst, :] = q[src, :], k[src, :], v[src, :]
        return carry

    lax.fori_loop(0, seq // rc, prep, 0)

    for c in range(keep // rc):
        r0 = seq - keep + c * rc
        kt_ref[:, c * rc:(c + 1) * rc] = ks[r0:r0 + rc, :].T
        vt_ref[:, c * rc:(c + 1) * rc] = v_ref[r0:r0 + rc, :].T

    qi, kj = _iota((blk, blk), 0), _iota((blk, blk), 1)
    bias_first[...] = jnp.where(kj <= qi, 0.0, NEG)
    qi, kj = _iota((blk, 2 * blk), 0), _iota((blk, 2 * blk), 1)
    bias_rest[...] = jnp.where((kj >= qi) & (kj <= qi + blk), 0.0, NEG)

    def rows_of(st, n, dil):
        return pl.ds(st, n, stride=dil) if dil > 1 else pl.ds(st, n)

    def tile_group(tiles, dil, dst_acc, dst_m, pitch):
        staged = []
        for r, b, first in tiles:
            nk = blk if first else 2 * blk
            if dil == wide_dil:
                qrow = wide_pitch * (b * blk) + r
                krow = qrow if first else qrow - wide_pitch * blk
                rows, krows = pl.ds(qrow, blk, stride=wide_pitch), pl.ds(krow, nk, stride=wide_pitch)
                q, k, v = qw[rows, :], kw[krows, :].astype(bf16), vw[krows, :]
            else:
                start = r + b * (blk * dil)
                kstart = start if first else start - blk * dil
                rows, krows = rows_of(start, blk, dil), rows_of(kstart, nk, dil)
                q, k, v = q_s[rows, :], ks[krows, :].astype(bf16), v_ref[krows, :]
            qs = [jnp.where(first_head, q, 0.0).astype(bf16), jnp.where(first_head, 0.0, q).astype(bf16)]
            vs = [jnp.where(first_head, v, 1.0).astype(bf16), jnp.where(first_head, 1.0, v).astype(bf16)]
            bias = bias_first[...] if first else bias_rest[...]
            dst = pl.ds(pl.multiple_of(r * pitch + b * blk, SUBLANES), blk)
            staged.append((dst, vs, [_dot_nt(qs[hh], k) + bias for hh in range(2)]))
        probs = []
        for dst, vs, scores in staged:
            for hh in range(2):
                mt = jnp.max(scores[hh], axis=1, keepdims=True)
                probs.append((dst, hh, vs[hh], jnp.exp(scores[hh] - mt).astype(bf16), mt))
        for dst, hh, v, p, mt in probs:
            dst_acc[hh, dst, :] = jnp.dot(p, v, preferred_element_type=f32)
            dst_m[hh, dst, :] = jnp.broadcast_to(mt, (blk, LANES))

    merge_group = 4
    tiles_per_body = 16

    for bi, (window, dil) in enumerate(DIL_PAIRS):
        nb = seq // dil // blk
        dst_acc, dst_m = (acc_s, m_s) if bi == 0 else (t_acc, t_m)
        pitch = nb * blk + (SUBLANES if dil == wide_dil else 0)
        if nb >= tiles_per_body:
            assert nb % tiles_per_body == 0

            def per_residue(r, carry, dil=dil, nb=nb, dst_acc=dst_acc, dst_m=dst_m, pitch=pitch):
                tile_group([(r, b, b == 0) for b in range(tiles_per_body)], dil, dst_acc, dst_m, pitch)

                def per_group(g, c):
                    tile_group([(r, g * tiles_per_body + u, False) for u in range(tiles_per_body)],
                               dil, dst_acc, dst_m, pitch)
                    return c

                lax.fori_loop(1, nb // tiles_per_body, per_group, 0)
                return carry

            lax.fori_loop(0, dil, per_residue, 0)
        else:
            res_per_body = tiles_per_body // nb
            assert dil % res_per_body == 0

            def per_residues(g, carry, dil=dil, nb=nb, dst_acc=dst_acc, dst_m=dst_m, pitch=pitch, rpb=res_per_body):
                tile_group([(g * rpb + u, b, b == 0) for u in range(rpb) for b in range(nb)],
                           dil, dst_acc, dst_m, pitch)
                return carry

            lax.fori_loop(0, dil // res_per_body, per_residues, 0)

        if bi > 0 and dil != wide_dil:
            def merge(g, carry, dil=dil, nb=nb):
                pending = []
                for u in range(merge_group):
                    idx = g * merge_group + u
                    r, b = idx // nb, idx % nb
                    rows = rows_of(r + b * (blk * dil), blk, dil)
                    src = pl.ds(pl.multiple_of(idx * blk, blk), blk)
                    for hh in range(2):
                        mo, mt = m_s[hh, rows, :], t_m[hh, src, :]
                        mn = jnp.maximum(mo, mt)
                        an = acc_s[hh, rows, :] * jnp.exp(mo - mn) + t_acc[hh, src, :] * jnp.exp(mt - mn)
                        pending.append((hh, rows, an, mn))
                for hh, rows, an, mn in pending:
                    acc_s[hh, rows, :] = an
                    m_s[hh, rows, :] = mn
                return carry

            lax.fori_loop(0, seq // blk // merge_group, merge, 0)
        elif bi > 0:
            def merge_wide(g, carry, dil=dil, pitch=pitch):
                pending = []
                for u in range(merge_group):
                    t0 = pl.multiple_of((g * merge_group + u) * blk, blk)
                    rows = pl.ds(t0, blk)
                    for hh in range(2):
                        ta, tm = [], []
                        for v8 in range(blk // SUBLANES):
                            src = pl.ds(((v8 * SUBLANES) % dil) * pitch + t0 // dil + (v8 * SUBLANES) // dil,
                                        SUBLANES, stride=pitch)
                            ta.append(t_acc[hh, src, :])
                            tm.append(t_m[hh, src, :])
                        ta, mt = jnp.concatenate(ta, axis=0), jnp.concatenate(tm, axis=0)
                        mo = m_s[hh, rows, :]
                        mn = jnp.maximum(mo, mt)
                        pending.append(acc_s[hh, rows, :] * jnp.exp(mo - mn) + ta * jnp.exp(mt - mn))
                    a0, a1 = pending[-2:]
                    half = LANES // 2
                    o_ref[rows, :] = jnp.where(first_head, a0 / pltpu.roll(a0, half, 1), a1 / pltpu.roll(a1, half, 1))
                return carry

            lax.fori_loop(0, seq // blk // merge_group, merge_wide, 0)


def _attn_prompt(layer, z, n, seq, cos_t, sin_t, qgain, kgain, prev_out):
    keep = min(WIN_MAX, seq)
    for window, dil in DIL_PAIRS:
        assert window // dil == ATT_BLOCK and seq % (dil * ATT_BLOCK) == 0
    wide = [dil for _, dil in DIL_PAIRS if dil % SUBLANES == 0]
    assert len(wide) == 1 and ATT_BLOCK % wide[0] == 0 and DIL_PAIRS[-1][1] == wide[0]
    wide_dil, wide_pitch = wide[0], wide[0] + SUBLANES
    per_group = GROUP_WIDTH // LANES
    zspec = lambda base: pl.BlockSpec((seq, LANES), lambda b, p, base=base: (b, base * per_group + p),
                                      pipeline_mode=pl.Buffered(1))
    whole = lambda rows: pl.BlockSpec((rows, LANES), lambda b, p: (0, 0), pipeline_mode=pl.Buffered(1))
    slab = lambda: pltpu.VMEM((seq, LANES), f32)
    slab2 = lambda: pltpu.VMEM((2, seq, LANES), f32)
    branch = lambda: pltpu.VMEM((2, seq + wide_dil * SUBLANES, LANES), f32)
    padded = lambda: pltpu.VMEM((seq // wide_dil * wide_pitch, LANES), f32)
    in_specs = [zspec(0), zspec(1), zspec(2), whole(seq), whole(seq), whole(1), whole(1)]
    args = [z, z, z, cos_t, sin_t, qgain, kgain]
    aliases = {}
    if prev_out is not None:
        in_specs += [pl.BlockSpec(memory_space=pl.ANY), pl.BlockSpec(memory_space=pl.ANY)]
        args += list(prev_out)
        aliases = {len(args) - 2: 1, len(args) - 1: 2}
    cache_spec = pl.BlockSpec((None, None, LANES, keep), lambda b, p: (layer, b, p, 0))
    cache_shape = jax.ShapeDtypeStruct((DEPTH, n, GROUP_WIDTH, keep), f32)
    return pl.pallas_call(
        functools.partial(_attn_prompt_kernel, seq=seq, keep=keep, wide_dil=wide_dil, wide_pitch=wide_pitch,
                          aliased=prev_out is not None),
        grid=(n, N_PAIRS),
        in_specs=in_specs,
        out_specs=[pl.BlockSpec((seq, LANES), lambda b, p: (b, p)), cache_spec, cache_spec],
        out_shape=[jax.ShapeDtypeStruct((n * seq, GROUP_WIDTH), f32), cache_shape, cache_shape],
        input_output_aliases=aliases,
        scratch_shapes=[slab(), slab(), slab2(), slab2(), branch(), branch(), padded(), padded(), padded(),
                        pltpu.VMEM((ATT_BLOCK, ATT_BLOCK), f32), pltpu.VMEM((ATT_BLOCK, 2 * ATT_BLOCK), f32)],
        compiler_params=_params(2), name=f"attn_prompt_l{layer}")(*args)


def _attn_decode_prep_kernel(z_ref, cos_ref, sin_ref, qg_ref, kg_ref, q_ref, k_ref, v_ref):
    mean_mat = _same_head_matrix(GROUP_WIDTH, 1.0 / HEAD_DIM)
    z = z_ref[...]
    q, k = z[:, :GROUP_WIDTH], z[:, GROUP_WIDTH:2 * GROUP_WIDTH]
    q = q * lax.rsqrt(_head_sum_sq(q, mean_mat) + EPS) * qg_ref[...]
    k = k * lax.rsqrt(_head_sum_sq(k, mean_mat) + EPS) * kg_ref[...]
    q_ref[...] = _rope_wide(q, cos_ref[...], sin_ref[...]) * (HEAD_DIM ** -0.5)
    k_ref[...] = _rope_wide(k, cos_ref[...], sin_ref[...])
    v_ref[...] = z[:, 2 * GROUP_WIDTH:]


def _attn_decode_prep(z, cos_row, sin_row, qgain, kgain):
    m = z.shape[0]
    small = lambda a: pl.BlockSpec(a.shape, lambda i: (0, 0))
    out = jax.ShapeDtypeStruct((m, GROUP_WIDTH), f32)
    ospec = pl.BlockSpec((m, GROUP_WIDTH), lambda i: (0, 0))
    return pl.pallas_call(
        _attn_decode_prep_kernel, grid=(1,),
        in_specs=[pl.BlockSpec((m, 3 * GROUP_WIDTH), lambda i: (0, 0)), small(cos_row), small(sin_row),
                  small(qgain), small(kgain)],
        out_specs=[ospec, ospec, ospec], out_shape=[out, out, out],
        compiler_params=_params(1), name="attn_decode_prep")(z, cos_row, sin_row, qgain, kgain)


def _attn_decode_kernel(*refs, buf, aliased):
    if aliased:
        kc_ref, vc_ref, q_ref, kn_ref, vn_ref, _, _, ko_ref, vo_ref, o_ref = refs
    else:
        kc_ref, vc_ref, q_ref, kn_ref, vn_ref, ko_ref, vo_ref, o_ref = refs
    kc = kc_ref[...]
    vc = vc_ref[...]
    qc = q_ref[...][:, :, :1]
    knew = kn_ref[...][:, :, :1]
    vnew = vn_ref[...][:, :, :1]
    dist = buf - _iota((1, 1, buf), 2)
    mult = jnp.zeros((1, 1, buf), f32)
    for window, dil in DIL_PAIRS:
        mult = mult + jnp.where((dist <= window) & (dist % dil == 0), 1.0, 0.0)
    s = jnp.sum(kc * qc, axis=1, keepdims=True)
    s_new = jnp.sum(knew * qc, axis=1, keepdims=True)
    s = jnp.where(mult > 0.0, s, NEG)
    mx = jnp.maximum(jnp.max(s, axis=2, keepdims=True), s_new)
    p = mult * jnp.exp(s - mx)
    p_new = float(len(DIL_PAIRS)) * jnp.exp(s_new - mx)
    den = jnp.sum(p, axis=2, keepdims=True) + p_new
    o = (jnp.sum(vc * p, axis=2, keepdims=True) + vnew * p_new) / den
    o_ref[...] = jnp.broadcast_to(o, o_ref.shape)
    last = _iota(kc.shape, 2) == buf - 1
    ko_ref[...] = jnp.where(last, knew, pltpu.roll(kc, buf - 1, 2))
    vo_ref[...] = jnp.where(last, vnew, pltpu.roll(vc, buf - 1, 2))


def _attn_decode(layer, cache_kt, cache_vt, q_col, k_col, v_col, prev_out):
    depth, nb, nh, hd, buf = cache_kt.shape
    cache_spec = pl.BlockSpec((None, None, nh, hd, buf), lambda b: (layer, b, 0, 0, 0))
    col_spec = pl.BlockSpec((None, nh, hd, LANES), lambda b: (b, 0, 0, 0))
    in_specs = [cache_spec, cache_spec, col_spec, col_spec, col_spec]
    args = [cache_kt, cache_vt, q_col, k_col, v_col]
    aliases = {}
    if prev_out is not None:
        in_specs += [pl.BlockSpec(memory_space=pl.ANY), pl.BlockSpec(memory_space=pl.ANY)]
        args += list(prev_out)
        aliases = {5: 0, 6: 1}
    cache_shape = jax.ShapeDtypeStruct(cache_kt.shape, f32)
    return pl.pallas_call(
        functools.partial(_attn_decode_kernel, buf=buf, aliased=prev_out is not None),
        grid=(nb,), in_specs=in_specs,
        out_specs=[cache_spec, cache_spec, col_spec],
        out_shape=[cache_shape, cache_shape, jax.ShapeDtypeStruct((nb, nh, hd, LANES), f32)],
        input_output_aliases=aliases,
        compiler_params=_params(1), name=f"attn_decode_l{layer}")(*args)


SOLVE_BLOCK = 16


def _solve_unit_lower_pairs(systems, first_head):
    ng = CHUNK // SUBLANES
    gpb = SOLVE_BLOCK // SUBLANES
    col_in_head = _iota((SOLVE_BLOCK, LANES), 1) % HEAD_DIM
    split = lambda x: [x[SUBLANES * g:SUBLANES * (g + 1), :] for g in range(ng)]
    ags = [split(a) for a, _ in systems]
    xgs = [split(x) for _, x in systems]
    for blk in range(CHUNK // SOLVE_BLOCK):
        r0 = blk * SOLVE_BLOCK
        if blk > 0:
            for (a, _), xg in zip(systems, xgs):
                left = jnp.where(col_in_head < r0, a[r0:r0 + SOLVE_BLOCK, :], 0.0).astype(bf16)
                solved = _stack_heads(jnp.concatenate(xg, axis=0), first_head).astype(bf16)
                upd = jnp.dot(left, solved, preferred_element_type=f32)
                for g in range(gpb):
                    xg[blk * gpb + g] = xg[blk * gpb + g] - upd[SUBLANES * g:SUBLANES * (g + 1), :]
        for j in range(r0, r0 + SOLVE_BLOCK - 1):
            g0, r = divmod(j, SUBLANES)
            for ag, xg in zip(ags, xgs):
                xrow = jnp.broadcast_to(xg[g0][r:r + 1, :], (SUBLANES, LANES))
                for g in range(g0, (blk + 1) * gpb):
                    c0 = jnp.broadcast_to(ag[g][:, j:j + 1], (SUBLANES, LANES))
                    c1 = jnp.broadcast_to(ag[g][:, HEAD_DIM + j:HEAD_DIM + j + 1], (SUBLANES, LANES))
                    xg[g] = xg[g] - jnp.where(first_head, c0, c1) * xrow
    return [jnp.concatenate(xg, axis=0) for xg in xgs]


def _bf16_terms(x, terms):
    pieces, rest = [], x
    for t in range(terms):
        pieces.append(rest.astype(bf16))
        if t + 1 < terms:
            rest = rest - pieces[-1].astype(f32)
    return pieces


def _dot_split_lhs(x, mat, terms):
    parts = [jnp.dot(p, mat, preferred_element_type=f32) for p in _bf16_terms(x, terms)]
    return functools.reduce(lambda a, b: a + b, parts)


def _dot_split_rhs(mat, x, terms):
    parts = [jnp.dot(mat, p, preferred_element_type=f32) for p in _bf16_terms(x, terms)]
    return functools.reduce(lambda a, b: a + b, parts)


def _state_in(s0, group):
    layer = s0[1]
    return pl.BlockSpec((None, group, N_HEADS, HEAD_DIM, HEAD_DIM), lambda b, t: (layer, b, 0, 0, 0))


def _load_state_pairs(sbd, s0_ref):
    sbd[...] = jnp.zeros_like(sbd)
    for h in range(N_HEADS):
        p, hh = divmod(h, 2)
        sbd[p, hh * HEAD_DIM:(hh + 1) * HEAD_DIM, hh * HEAD_DIM:(hh + 1) * HEAD_DIM] = s0_ref[h]


def _store_state_pairs(sn_ref, sbd):
    for h in range(N_HEADS):
        p, hh = divmod(h, 2)
        sn_ref[h] = sbd[p, hh * HEAD_DIM:(hh + 1) * HEAD_DIM, hh * HEAD_DIM:(hh + 1) * HEAD_DIM]


def _delta_chunks(q, k, v, g, beta, sbd, first_head, same_head, ts, group):
    ri, cj = _iota((ts, ts), 0), _iota((ts, ts), 1)
    same = (ri // CHUNK) == (cj // CHUNK)
    cum_mat, tot_mat = (same & (cj <= ri)).astype(bf16), same.astype(bf16)
    g_terms = _bf16_terms(g, 2)
    rsum = lambda mat: jnp.concatenate(
        [functools.reduce(lambda a, b: a + b,
                          [jnp.dot(mat, t[s * ts:(s + 1) * ts, :], preferred_element_type=f32) for t in g_terms])
         for s in range(group)], axis=0)
    gcum = rsum(cum_mat)
    glast = rsum(tot_mat)
    eg = jnp.exp(gcum)
    qg, kb, vb = q * eg, k * beta, v * beta
    kbe = kb * eg
    kd = k * jnp.exp(glast - gcum)
    dl = jnp.exp(glast)

    ii = _iota((CHUNK, LANES), 0)
    jj = _iota((CHUNK, LANES), 1) % HEAD_DIM
    eye2 = (ii == jj).astype(f32)
    ones = jnp.ones((CHUNK, CHUNK), bf16)
    n_chunks = ts // CHUNK
    where = {(s, c, p): (slice(s * ts + c * CHUNK, s * ts + (c + 1) * CHUNK), slice(p * LANES, (p + 1) * LANES))
             for s in range(group) for c in range(n_chunks) for p in range(N_PAIRS)}

    amats, qks = {}, {}
    for key, (r, l) in where.items():
        gc = gcum[r, l]
        grow = _dot_split_rhs(ones, eye2 * gc, 2)
        dm = jnp.where(ii >= jj, jnp.exp(jnp.minimum(gc - grow, 0.0)), 0.0)
        kst = _stack_heads(k[r, l], first_head).astype(bf16)
        kk = _dot_nt(kb[r, l].astype(bf16), kst)
        qks[key] = (_dot_nt(q[r, l].astype(bf16), kst) * dm).astype(bf16)
        amats[key] = jnp.where(ii > jj, kk * dm, 0.0)

    outs = {}
    for c in range(n_chunks):
        keys = [(s, c, p) for s in range(group) for p in range(N_PAIRS)]
        s_prev = {key: sbd[key[0], key[2]] for key in keys}
        s_bf = {key: s_prev[key].astype(bf16) for key in keys}
        rhs = [vb[where[key]] - jnp.dot(kbe[where[key]].astype(bf16), s_bf[key], preferred_element_type=f32)
               for key in keys]
        vnews = _solve_unit_lower_pairs([(amats[key], x) for key, x in zip(keys, rhs)], first_head)
        for key, vnew in zip(keys, vnews):
            r, l = where[key]
            vst = _stack_heads(vnew, first_head).astype(bf16)
            outs[key] = (jnp.dot(qg[r, l].astype(bf16), s_bf[key], preferred_element_type=f32)
                         + jnp.dot(qks[key], vst, preferred_element_type=f32))
            upd = _dot_tn(kd[r, l].astype(bf16), vnew.astype(bf16))
            sbd[key[0], key[2]] = s_prev[key] * dl[r.start:r.start + 1, l] + jnp.where(same_head, upd, 0.0)
    return jnp.concatenate(
        [jnp.concatenate([outs[(s, c, p)] for p in range(N_PAIRS)], axis=1)
         for s in range(group) for c in range(n_chunks)], axis=0)


def _delta_kernel(zqkv_ref, zz_ref, zab_ref, conv0_ref, s0_ref, cw_ref, alog_ref, dtb_ref, on_ref,
                  o_ref, convn_ref, sn_ref, xp, sbd, *, ts, n_valid, n_tiles, group):
    t = pl.program_id(1)
    if n_valid == 1:
        _delta_single_tokens(zqkv_ref, zz_ref, zab_ref, conv0_ref, s0_ref, cw_ref, alog_ref, dtb_ref, on_ref,
                             o_ref, convn_ref, sn_ref, xp, sbd, ts=ts, group=group)
        return
    _delta_tiles(t, zqkv_ref, zz_ref, zab_ref, conv0_ref, s0_ref, cw_ref, alog_ref, dtb_ref, on_ref,
                 o_ref, convn_ref, sn_ref, xp, sbd, ts=ts, n_tiles=n_tiles, group=group)


def _delta_gates(zab, alog_ref, dtb_ref):
    src = _iota((LANES, GROUP_WIDTH), 0)
    head = _iota((LANES, GROUP_WIDTH), 1) // HEAD_DIM
    zab_terms = _bf16_terms(zab, 2)
    pick = lambda sel: functools.reduce(
        lambda a, b: a + b, [jnp.dot(t, sel.astype(bf16), preferred_element_type=f32) for t in zab_terms])
    a_bc, b_bc = pick(src == head), pick(src == head + N_HEADS)
    xa = a_bc + dtb_ref[...]
    g = -jnp.exp(alog_ref[...]) * (jnp.maximum(xa, 0.0) + jnp.log1p(jnp.exp(-jnp.abs(xa))))
    return g, jax.nn.sigmoid(b_bc)


def _delta_single_tokens(zqkv_ref, zz_ref, zab_ref, conv0_ref, s0_ref, cw_ref, alog_ref, dtb_ref, on_ref,
                         o_ref, convn_ref, sn_ref, xp, sbd, *, ts, group):
    lead, rows, w3 = SUBLANES, group * ts, 3 * GROUP_WIDTH
    xp[:, 0:lead, :] = conv0_ref[...]
    xp[:, lead:lead + ts, :] = zqkv_ref[...]
    cw = cw_ref[...]
    y = cw[0:1, :][None] * xp[:, lead - 3:lead - 3 + ts, :]
    for j in range(1, DELTA_CONV):
        y = y + cw[j:j + 1, :][None] * xp[:, lead - 3 + j:lead - 3 + j + ts, :]
    convn_ref[...] = xp[:, lead + 1 - (DELTA_CONV - 1):lead + 1, :]
    y = _silu(y).reshape(rows, w3)

    sum_mat = _same_head_matrix(GROUP_WIDTH, 1.0)
    yq, yk, v = y[:, :GROUP_WIDTH], y[:, GROUP_WIDTH:2 * GROUP_WIDTH], y[:, 2 * GROUP_WIDTH:]
    q = yq * lax.rsqrt(_head_sum_sq(yq, sum_mat) + EPS) * (HEAD_DIM ** -0.5)
    k = yk * lax.rsqrt(_head_sum_sq(yk, sum_mat) + EPS)
    g, beta = _delta_gates(zab_ref[...].reshape(rows, LANES), alog_ref, dtb_ref)
    live = _iota((rows, 1), 0) % ts == 0
    g, beta = jnp.where(live, g, 0.0), jnp.where(live, beta, 0.0)
    q, k, v = jnp.where(live, q, 0.0), jnp.where(live, k, 0.0), jnp.where(live, v, 0.0)
    eg = jnp.exp(g)
    qg, kbe, vb = q * eg, k * beta * eg, v * beta
    qk = _head_sum(q * k, sum_mat)

    same_head = (_iota((LANES, LANES), 0) // HEAD_DIM) == (_iota((LANES, LANES), 1) // HEAD_DIM)
    for s in range(group):
        _load_state_pairs(sbd.at[s], s0_ref.at[s])
    where = [(s, p, slice(s * ts, (s + 1) * ts), slice(p * LANES, (p + 1) * LANES))
             for s in range(group) for p in range(N_PAIRS)]
    s_prev = [sbd[s, p] for s, p, _, _ in where]
    s_bf = [x.astype(bf16) for x in s_prev]
    vnew = [vb[r, l] - jnp.dot(kbe[r, l].astype(bf16), sb, preferred_element_type=f32)
            for (_, _, r, l), sb in zip(where, s_bf)]
    outs = [jnp.dot(qg[r, l].astype(bf16), sb, preferred_element_type=f32) + qk[r, l] * vn
            for (_, _, r, l), sb, vn in zip(where, s_bf, vnew)]
    for (s, p, r, l), sp, vn in zip(where, s_prev, vnew):
        upd = _dot_tn(k[r, l].astype(bf16), vn.astype(bf16))
        sbd[s, p] = sp * eg[r.start:r.start + 1, l] + jnp.where(same_head, upd, 0.0)
    o = jnp.concatenate([jnp.concatenate(outs[s * N_PAIRS:(s + 1) * N_PAIRS], axis=1) for s in range(group)], axis=0)
    o = o * lax.rsqrt(_head_sum_sq(o, sum_mat) * (1.0 / HEAD_DIM) + EPS) * on_ref[...]
    o_ref[...] = (o * _silu(zz_ref[...].reshape(rows, GROUP_WIDTH))).reshape(group, ts, GROUP_WIDTH)
    for s in range(group):
        _store_state_pairs(sn_ref.at[s], sbd.at[s])


def _delta_tiles(t, zqkv_ref, zz_ref, zab_ref, conv0_ref, s0_ref, cw_ref, alog_ref, dtb_ref, on_ref,
                 o_ref, convn_ref, sn_ref, xp, sbd, *, ts, n_tiles, group):
    lead, rows, w3 = SUBLANES, group * ts, 3 * GROUP_WIDTH

    @pl.when(t == 0)
    def _():
        xp[:, 0:lead, :] = conv0_ref[...]
        for s in range(group):
            _load_state_pairs(sbd.at[s], s0_ref.at[s])

    xp[:, lead:lead + ts, :] = zqkv_ref[...]
    cw = cw_ref[...]
    y = cw[0:1, :][None] * xp[:, lead - 3:lead - 3 + ts, :]
    for j in range(1, DELTA_CONV):
        y = y + cw[j:j + 1, :][None] * xp[:, lead - 3 + j:lead - 3 + j + ts, :]
    convn_ref[...] = xp[:, lead + ts - (DELTA_CONV - 1):lead + ts, :]
    xp[:, 0:lead, :] = xp[:, ts:ts + lead, :]
    y = _silu(y).reshape(rows, w3)

    sum_mat = _same_head_matrix(GROUP_WIDTH, 1.0)
    yq, yk, v = y[:, :GROUP_WIDTH], y[:, GROUP_WIDTH:2 * GROUP_WIDTH], y[:, 2 * GROUP_WIDTH:]
    q = yq * lax.rsqrt(_head_sum_sq(yq, sum_mat) + EPS) * (HEAD_DIM ** -0.5)
    k = yk * lax.rsqrt(_head_sum_sq(yk, sum_mat) + EPS)
    g, beta = _delta_gates(zab_ref[...].reshape(rows, LANES), alog_ref, dtb_ref)

    first_head = _iota((1, LANES), 1) < HEAD_DIM
    same_head = (_iota((LANES, LANES), 0) // HEAD_DIM) == (_iota((LANES, LANES), 1) // HEAD_DIM)
    o = _delta_chunks(q, k, v, g, beta, sbd, first_head, same_head, ts, group)
    o = o * lax.rsqrt(_head_sum_sq(o, sum_mat) * (1.0 / HEAD_DIM) + EPS) * on_ref[...]
    o_ref[...] = (o * _silu(zz_ref[...].reshape(rows, GROUP_WIDTH))).reshape(group, ts, GROUP_WIDTH)

    @pl.when(t == n_tiles - 1)
    def _():
        for s in range(group):
            _store_state_pairs(sn_ref.at[s], sbd.at[s])


def _delta(z, n, seq, n_valid, conv0, s0, conv_w, alog_bc, dtb_bc, onorm_bc, ts, group=1):
    assert n_valid == seq or (n_valid == 1 and seq == ts)
    n_tiles = seq // ts
    assert n % group == 0
    w3 = 3 * GROUP_WIDTH
    z3 = z.reshape(n, seq, Z_WIDTH)
    row = lambda width, blk: pl.BlockSpec((group, ts, width), lambda b, t, blk=blk: (b, t, blk))
    small = lambda a: pl.BlockSpec(a.shape, lambda b, t: (0, 0))
    state = pl.BlockSpec((group, N_HEADS, HEAD_DIM, HEAD_DIM), lambda b, t: (b, 0, 0, 0))
    o, convn, sn = pl.pallas_call(
        functools.partial(_delta_kernel, ts=ts, n_valid=n_valid, n_tiles=n_tiles, group=group),
        grid=(n // group, n_tiles),
        in_specs=[row(w3, ZB_QKV_BLK), row(GROUP_WIDTH, ZB_Z_BLK), row(LANES, ZAB_BLK),
                  pl.BlockSpec((group, SUBLANES, w3), lambda b, t: (b, 0, 0)), _state_in(s0, group),
                  small(conv_w), small(alog_bc), small(dtb_bc), small(onorm_bc)],
        out_specs=[pl.BlockSpec((group, ts, GROUP_WIDTH), lambda b, t: (b, t, 0)),
                   pl.BlockSpec((group, DELTA_CONV - 1, w3), lambda b, t: (b, 0, 0)), state],
        out_shape=[jax.ShapeDtypeStruct((n, seq, GROUP_WIDTH), f32),
                   jax.ShapeDtypeStruct((n, DELTA_CONV - 1, w3), f32),
                   jax.ShapeDtypeStruct((n, N_HEADS, HEAD_DIM, HEAD_DIM), f32)],
        scratch_shapes=[pltpu.VMEM((group, ts + SUBLANES, w3), f32),
                        pltpu.VMEM((group, N_PAIRS, LANES, LANES), f32)],
        compiler_params=_params(2), name="delta")(z3, z3, z3, conv0, s0[0], conv_w, alog_bc, dtb_bc, onorm_bc)
    return o.reshape(n * seq, GROUP_WIDTH), convn, sn


def _retention_kernel(zq_ref, zk_ref, zv_ref, zg_ref, dx_ref, dc_ref, db_ref, cos_ref, sin_ref, lg_ref, on_ref,
                      cwd_ref, s0_ref, dconv0_ref, o_ref, od_ref, sn_ref, dconvn_ref, sbd, dmask, xpd,
                      *, tc, n_valid, n_tiles, group):
    b, t = pl.program_id(0), pl.program_id(1)
    lg = lg_ref[...]

    @pl.when((b == 0) & (t == 0))
    def _():
        col = _iota((tc, 2 * tc), 1)
        diff = (_iota((tc, 2 * tc), 0) - col % tc).astype(f32)
        for p in range(N_PAIRS):
            lgp = lg[:, p * LANES:(p + 1) * LANES]
            rate = jnp.where(col < tc, lgp[:, 0:1], lgp[:, HEAD_DIM:HEAD_DIM + 1])
            dmask[p] = jnp.where(diff >= 0.0, jnp.exp(jnp.maximum(diff, 0.0) * rate), 0.0)

    if n_valid == 1:
        _retention_single_tokens(zq_ref, zk_ref, zv_ref, zg_ref, dx_ref, dc_ref, db_ref, cos_ref, sin_ref, lg, on_ref,
                                 cwd_ref, s0_ref, dconv0_ref, o_ref, od_ref, sn_ref, dconvn_ref, sbd, xpd,
                                 tc=tc, group=group)
        return
    for s in range(group):
        _retention_sequence(t, zq_ref.at[s], zk_ref.at[s], zv_ref.at[s], zg_ref.at[s], dx_ref.at[s],
                            dc_ref.at[s], db_ref.at[s], cos_ref, sin_ref, lg, on_ref, cwd_ref, s0_ref.at[s],
                            dconv0_ref.at[s], o_ref.at[s], od_ref.at[s], sn_ref.at[s], dconvn_ref.at[s],
                            sbd.at[s], dmask, xpd.at[s], tc=tc, n_tiles=n_tiles)


def _retention_single_tokens(zq_ref, zk_ref, zv_ref, zg_ref, dx_ref, dc_ref, db_ref, cos_ref, sin_ref, lg, on_ref,
                             cwd_ref, s0_ref, dconv0_ref, o_ref, od_ref, sn_ref, dconvn_ref, sbd, xpd, *, tc, group):
    lead, rows = SUBLANES, group * tc
    cos = jnp.concatenate([cos_ref[...]] * group, axis=0)
    sin = jnp.concatenate([sin_ref[...]] * group, axis=0)
    live = _iota((rows, 1), 0) % tc == 0
    flat = lambda ref: ref[...].reshape(rows, GROUP_WIDTH)
    q = jnp.where(live, _rope_wide(flat(zq_ref), cos, sin), 0.0)
    k = jnp.where(live, _rope_wide(flat(zk_ref), cos, sin) * (HEAD_DIM ** -0.5), 0.0)
    v = jnp.where(live, flat(zv_ref), 0.0)
    gamma = jnp.exp(lg)
    qd = q * gamma
    qk = _head_sum(q * k, _same_head_matrix(GROUP_WIDTH, 1.0))

    same_head = (_iota((LANES, LANES), 0) // HEAD_DIM) == (_iota((LANES, LANES), 1) // HEAD_DIM)
    for s in range(group):
        _load_state_pairs(sbd.at[s], s0_ref.at[s])
    where = [(s, p, slice(s * tc, (s + 1) * tc), slice(p * LANES, (p + 1) * LANES))
             for s in range(group) for p in range(N_PAIRS)]
    s_prev = [sbd[s, p] for s, p, _, _ in where]
    outs = [jnp.dot(qd[r, l].astype(bf16), sp.astype(bf16), preferred_element_type=f32) + qk[r, l] * v[r, l]
            for (_, _, r, l), sp in zip(where, s_prev)]
    for (s, p, r, l), sp in zip(where, s_prev):
        upd = _dot_tn(k[r, l].astype(bf16), v[r, l].astype(bf16))
        sbd[s, p] = sp * gamma[:, l] + jnp.where(same_head, upd, 0.0)
    o = jnp.concatenate([jnp.concatenate(outs[s * N_PAIRS:(s + 1) * N_PAIRS], axis=1) for s in range(group)], axis=0)
    mean_mat = _same_head_matrix(GROUP_WIDTH, 1.0 / HEAD_DIM)
    o = o * lax.rsqrt(_head_sum_sq(o, mean_mat) + EPS) * on_ref[...]
    o_ref[...] = (o * _silu(flat(zg_ref))).reshape(group, tc, GROUP_WIDTH)
    for s in range(group):
        _store_state_pairs(sn_ref.at[s], sbd.at[s])

    xpd[:, 0:lead, :] = dconv0_ref[...]
    xpd[:, lead:lead + tc, :] = dc_ref[...] * dx_ref[...]
    cwd = cwd_ref[...]
    yd = cwd[0:1, :][None] * xpd[:, lead - 2:lead - 2 + tc, :]
    for j in range(1, SHORT_CONV):
        yd = yd + cwd[j:j + 1, :][None] * xpd[:, lead - 2 + j:lead - 2 + j + tc, :]
    od_ref[...] = db_ref[...] * yd
    dconvn_ref[...] = xpd[:, lead + 1 - (SHORT_CONV - 1):lead + 1, :]


def _retention_sequence(t, zq_ref, zk_ref, zv_ref, zg_ref, dx_ref, dc_ref, db_ref, cos_ref, sin_ref, lg, on_ref,
                        cwd_ref, s0_ref, dconv0_ref, o_ref, od_ref, sn_ref, dconvn_ref, sbd, dmask, xpd,
                        *, tc, n_tiles):
    lead = SUBLANES

    @pl.when(t == 0)
    def _():
        _load_state_pairs(sbd, s0_ref)
        xpd[0:lead, :] = dconv0_ref[...]

    cos, sin = cos_ref[...], sin_ref[...]
    q = _rope_wide(zq_ref[...], cos, sin)
    k = _rope_wide(zk_ref[...], cos, sin) * (HEAD_DIM ** -0.5)
    v = zv_ref[...]
    pos = _iota((tc, 1), 0).astype(f32)
    qd = q * jnp.exp((pos + 1.0) * lg)
    kdk = k * jnp.exp((float(tc - 1) - pos) * lg)
    tile_decay = jnp.exp(float(tc) * lg)

    first_head = _iota((1, LANES), 1) < HEAD_DIM
    same_head = (_iota((LANES, LANES), 0) // HEAD_DIM) == (_iota((LANES, LANES), 1) // HEAD_DIM)
    outs = []
    for p in range(N_PAIRS):
        l = slice(p * LANES, (p + 1) * LANES)
        kst = _stack_heads(k[:, l], first_head).astype(bf16)
        vst = _stack_heads(v[:, l], first_head).astype(bf16)
        sc = _dot_nt(q[:, l].astype(bf16), kst) * dmask[p]
        s_prev = sbd[p]
        o = (jnp.dot(sc.astype(bf16), vst, preferred_element_type=f32)
             + jnp.dot(qd[:, l].astype(bf16), s_prev.astype(bf16), preferred_element_type=f32))
        upd = _dot_tn(kdk[:, l].astype(bf16), v[:, l].astype(bf16))
        sbd[p] = s_prev * tile_decay[:, l] + jnp.where(same_head, upd, 0.0)
        outs.append(o)
    o = jnp.concatenate(outs, axis=1)
    mean_mat = _same_head_matrix(GROUP_WIDTH, 1.0 / HEAD_DIM)
    o = o * lax.rsqrt(_head_sum_sq(o, mean_mat) + EPS) * on_ref[...]
    o_ref[...] = o * _silu(zg_ref[...])

    xpd[lead:lead + tc, :] = dc_ref[...] * dx_ref[...]
    cwd = cwd_ref[...]
    yd = cwd[0:1, :] * xpd[lead - 2:lead - 2 + tc, :]
    for j in range(1, SHORT_CONV):
        yd = yd + cwd[j:j + 1, :] * xpd[lead - 2 + j:lead - 2 + j + tc, :]
    od_ref[...] = db_ref[...] * yd
    dconvn_ref[...] = xpd[lead + tc - (SHORT_CONV - 1):lead + tc, :]
    xpd[0:lead, :] = xpd[tc:tc + lead, :]

    @pl.when(t == n_tiles - 1)
    def _():
        _store_state_pairs(sn_ref, sbd)


def _retention(z, n, seq, n_valid, s0, dconv0, cos_t, sin_t, lg_bc, onorm_bc, conv_d, tc, group=1):
    assert n_valid == seq or (n_valid == 1 and seq == tc)
    n_tiles = seq // tc
    assert n % group == 0
    z = z.reshape(n, seq, Z_WIDTH)
    row = lambda blk: pl.BlockSpec((group, tc, GROUP_WIDTH), lambda b, t, blk=blk: (b, t, blk))
    small = lambda a: pl.BlockSpec(a.shape, lambda b, t: (0, 0))
    tab = pl.BlockSpec((tc, LANES), lambda b, t: (t, 0))
    state = pl.BlockSpec((group, N_HEADS, HEAD_DIM, HEAD_DIM), lambda b, t: (b, 0, 0, 0))
    out_row = pl.BlockSpec((group, tc, GROUP_WIDTH), lambda b, t: (b, t, 0))
    out_rows = jax.ShapeDtypeStruct((n, seq, GROUP_WIDTH), f32)
    o, od, sn, dconvn = pl.pallas_call(
        functools.partial(_retention_kernel, tc=tc, n_valid=n_valid, n_tiles=n_tiles, group=group),
        grid=(n // group, n_tiles),
        in_specs=[row(ZC_Q_BLK), row(ZC_K_BLK), row(ZC_V_BLK), row(ZC_G_BLK),
                  row(ZD_X_BLK), row(ZD_C_BLK), row(ZD_B_BLK), tab, tab,
                  small(lg_bc), small(onorm_bc), small(conv_d), _state_in(s0, group),
                  pl.BlockSpec((group, SUBLANES, GROUP_WIDTH), lambda b, t: (b, 0, 0))],
        out_specs=[out_row, out_row, state,
                   pl.BlockSpec((group, SHORT_CONV - 1, GROUP_WIDTH), lambda b, t: (b, 0, 0))],
        out_shape=[out_rows, out_rows, jax.ShapeDtypeStruct((n, N_HEADS, HEAD_DIM, HEAD_DIM), f32),
                   jax.ShapeDtypeStruct((n, SHORT_CONV - 1, GROUP_WIDTH), f32)],
        scratch_shapes=[pltpu.VMEM((group, N_PAIRS, LANES, LANES), f32), pltpu.VMEM((N_PAIRS, tc, 2 * tc), f32),
                        pltpu.VMEM((group, tc + SUBLANES, GROUP_WIDTH), f32)],
        compiler_params=_params(2), name="retention")(
            z, z, z, z, z, z, z, cos_t, sin_t, lg_bc, onorm_bc, conv_d, s0[0], dconv0)
    return o.reshape(n * seq, GROUP_WIDTH), od.reshape(n * seq, GROUP_WIDTH), sn, dconvn


def _rope_tables(pos, inv_freq):
    ang = pos.astype(f32)[:, None] * inv_freq[None, :]
    cos, sin = jnp.cos(ang), jnp.sin(ang)
    reps = LANES // HEAD_DIM
    return (jnp.tile(jnp.concatenate([cos, cos], axis=1), (1, reps)),
            jnp.tile(jnp.concatenate([-sin, sin], axis=1), (1, reps)))


def _per_head_lanes(v):
    return jnp.repeat(v.astype(f32), HEAD_DIM)[None, :]


def _tiled_lanes(v, width):
    return jnp.tile(v.astype(f32), width // HEAD_DIM)[None, :]


def _lead_pad(state):
    return jnp.pad(state, ((0, 0), (SUBLANES - state.shape[1], 0), (0, 0)))


def kernel(x_prompt, x_sample, cache_a_k, cache_a_v, state_b_conv, state_b_rec, state_c_rec, state_d_conv,
           norm_mix, w_in, q_norm_a, k_norm_a, conv_b, a_log_b, dt_bias_b, onorm_b, onorm_c, conv_d,
           w_out, norm_ffn, w_up, w_down):
    nb, seq, _ = x_prompt.shape
    nd, dseq, _ = x_sample.shape
    assert dseq == 1
    dpad = SUBLANES
    dgroup = max(g for g in (8, 4, 2, 1) if nd % g == 0)
    rope_freq = ROPE_THETA ** (-jnp.arange(0, HEAD_DIM, 2, dtype=f32) / HEAD_DIM)
    ret_freq = 1.0 / (10000.0 ** jnp.linspace(0.0, 1.0, HEAD_DIM // 2, dtype=f32))
    ret_lg = _per_head_lanes(jnp.log(1.0 - 2.0 ** (-5.0 - jnp.arange(N_HEADS, dtype=f32))))
    pos_p = jnp.arange(seq, dtype=jnp.int32)
    pos_d = PAST_LEN + jnp.arange(dpad, dtype=jnp.int32)
    cos_ap, sin_ap = _rope_tables(pos_p, rope_freq)
    cos_cp, sin_cp = _rope_tables(pos_p, ret_freq)
    cos_ad, sin_ad = _rope_tables(pos_d[:1], rope_freq)
    cos_cd, sin_cd = _rope_tables(pos_d, ret_freq)

    cache_kt = jnp.transpose(cache_a_k, (0, 1, 3, 4, 2))
    cache_vt = jnp.transpose(cache_a_v, (0, 1, 3, 4, 2))

    yp = x_prompt.reshape(nb * seq, D_MODEL)
    ys = x_sample.reshape(nd, D_MODEL)
    zeros_conv_b = jnp.zeros((nb, SUBLANES, 3 * GROUP_WIDTH), f32)
    zeros_conv_d = jnp.zeros((nb, SUBLANES, GROUP_WIDTH), f32)
    zeros_rec = (jnp.zeros((1, nb, N_HEADS, HEAD_DIM, HEAD_DIM), f32), 0)
    new_p, new_s = [], []
    dec_cache = prompt_cache = None
    w_pack = _pack_w_in(w_in)
    wo, wu, wd = w_out.astype(bf16), w_up.astype(bf16), w_down.astype(bf16)
    for l in range(DEPTH):
        g_mix, g_ffn = norm_mix[l][None, :], norm_ffn[l][None, :]
        qg, kg = _tiled_lanes(q_norm_a[l], LANES), _tiled_lanes(k_norm_a[l], LANES)
        qg3, kg3 = _tiled_lanes(q_norm_a[l], GROUP_WIDTH), _tiled_lanes(k_norm_a[l], GROUP_WIDTH)
        alog, dtb = _per_head_lanes(a_log_b[l]), _per_head_lanes(dt_bias_b[l])
        on_b, on_c = _tiled_lanes(onorm_b[l], GROUP_WIDTH), _tiled_lanes(onorm_c[l], GROUP_WIDTH)

        z = _inproj(yp, g_mix, w_pack, l, DENSE_ROW_TILE)
        oa, kt, vt = _attn_prompt(l, z, nb, seq, cos_ap, sin_ap, qg, kg, prompt_cache)
        prompt_cache = (kt, vt)
        ob, p_bconv, p_brec = _delta(z, nb, seq, seq, zeros_conv_b, zeros_rec, conv_b[l], alog, dtb, on_b, RECURRENT_TILE,
                                     group=2 if nb % 2 == 0 else 1)
        oc, od, p_crec, p_dconv = _retention(z, nb, seq, seq, zeros_rec, zeros_conv_d, cos_cp, sin_cp, ret_lg,
                                             on_c, conv_d[l], RECURRENT_TILE,
                                             group=max(g for g in (4, 2, 1) if nb % g == 0))
        yp = _ffn(yp, oa, ob, oc, od, wo, g_ffn, wu, wd, l, DENSE_ROW_TILE)
        new_p.append((kt, vt, p_bconv, p_brec, p_crec, p_dconv))

        zs = _inproj(ys, g_mix, w_pack, l, nd)
        qa, ka, va = _attn_decode_prep(zs, cos_ad, sin_ad, qg3, kg3)
        col = lambda a: jnp.broadcast_to(a.reshape(nd, N_HEADS, HEAD_DIM, 1), (nd, N_HEADS, HEAD_DIM, LANES))
        s_kt, s_vt, oa_col = _attn_decode(l, cache_kt, cache_vt, col(qa), col(ka), col(va), dec_cache)
        dec_cache = (s_kt, s_vt)
        oa_s = oa_col[..., 0].reshape(nd, GROUP_WIDTH)
        zpad = jnp.zeros((nd, dpad, Z_WIDTH), f32).at[:, 0, :].set(zs).reshape(nd * dpad, Z_WIDTH)
        ob_s, s_bconv, s_brec = _delta(zpad, nd, dpad, 1, _lead_pad(state_b_conv[l]), (state_b_rec, l), conv_b[l],
                                       alog, dtb, on_b, dpad, group=dgroup)
        oc_s, od_s, s_crec, s_dconv = _retention(zpad, nd, dpad, 1, (state_c_rec, l), _lead_pad(state_d_conv[l]),
                                                 cos_cd, sin_cd, ret_lg, on_c, conv_d[l], dpad, group=dgroup)
        first = lambda a: a.reshape(nd, dpad, GROUP_WIDTH)[:, 0, :]
        ys = _ffn(ys, oa_s, first(ob_s), first(oc_s), first(od_s), wo, g_ffn, wu, wd, l, nd)
        new_s.append((s_bconv, s_brec, s_crec, s_dconv))

    keep = min(WIN_MAX, seq)
    cache_out = lambda i: jnp.transpose(
        prompt_cache[i].reshape(DEPTH, nb, N_HEADS, HEAD_DIM, keep), (0, 1, 4, 2, 3))
    stack_p = lambda i: jnp.stack([s[i] for s in new_p])
    stack_s = lambda i: jnp.stack([s[i] for s in new_s])
    return (yp.reshape(nb, seq, D_MODEL), ys.reshape(nd, 1, D_MODEL),
            cache_out(0), cache_out(1), stack_p(2), stack_p(3), stack_p(4), stack_p(5),
            jnp.transpose(dec_cache[0], (0, 1, 4, 2, 3)), jnp.transpose(dec_cache[1], (0, 1, 4, 2, 3)),
            stack_s(0), stack_s(1), stack_s(2), stack_s(3))
```

```python
import functools

import jax
import jax.numpy as jnp
from jax import lax
from jax.experimental import pallas as pl
from jax.experimental.pallas import tpu as pltpu

f32, bf16, i32 = jnp.float32, jnp.bfloat16, jnp.int32

D_MODEL = 1024
HEAD_DIM = 64
N_HEADS = 6
N_PAIRS = N_HEADS // 2
GROUP_WIDTH = N_HEADS * HEAD_DIM
D_FF = 4 * D_MODEL
DIL_PAIRS = ((128, 1), (512, 4), (2048, 16))
WIN_MAX = 2048
ATT_BLOCK = 128
ROPE_THETA = 10000.0
DELTA_CONV = 4
SHORT_CONV = 3
CHUNK = 64
EPS = 1e-6
NEG = -1e30
PAST_LEN = 16384
DEPTH = 2
LANES = 128
SUBLANES = 8
VMEM_LIMIT = 56 * 1024 * 1024
DENSE_ROW_TILE = 512
RECURRENT_TILE = 256

IN_WIDTH = 14 * GROUP_WIDTH + 2 * N_HEADS
AB_OFFSET_SRC = 7 * GROUP_WIDTH
Z_WIDTH = 14 * GROUP_WIDTH + LANES
ZB_QKV_BLK = 1
ZB_Z_BLK = 6
ZC_Q_BLK, ZC_K_BLK, ZC_V_BLK, ZC_G_BLK = 7, 8, 9, 10
ZD_X_BLK, ZD_C_BLK, ZD_B_BLK = 11, 12, 13
ZAB_BLK = 14 * GROUP_WIDTH // LANES


def _iota(shape, dim):
    return lax.broadcasted_iota(i32, shape, dim)


def _params(n_axes):
    return pltpu.CompilerParams(dimension_semantics=("arbitrary",) * n_axes, vmem_limit_bytes=VMEM_LIMIT)


def _same_head_matrix(width, value):
    r = _iota((width, width), 0) // HEAD_DIM
    c = _iota((width, width), 1) // HEAD_DIM
    return jnp.where(r == c, value, 0.0).astype(bf16)


def _head_sum(x, mat, terms=2):
    return _dot_split_lhs(x, mat, terms)


def _head_sum_sq(x, mat):
    return _dot_split_lhs(x * x, mat, 1)


def _rope_slab(x, cos, sin_signed):
    p = _iota((1, LANES), 1) % HEAD_DIM
    partner = jnp.where(p < HEAD_DIM // 2, pltpu.roll(x, LANES - HEAD_DIM // 2, 1), pltpu.roll(x, HEAD_DIM // 2, 1))
    return x * cos + partner * sin_signed


def _rope_wide(x, cos, sin_signed):
    return jnp.concatenate(
        [_rope_slab(x[:, LANES * i:LANES * (i + 1)], cos, sin_signed) for i in range(x.shape[1] // LANES)], axis=1)


def _stack_heads(x, first_head):
    return jnp.concatenate([jnp.where(first_head, x, 0.0), jnp.where(first_head, 0.0, x)], axis=0)


def _dot_nt(a, b):
    return lax.dot_general(a, b, (((1,), (1,)), ((), ())), preferred_element_type=f32)


def _dot_tn(a, b):
    return lax.dot_general(a, b, (((0,), (0,)), ((), ())), preferred_element_type=f32)


def _silu(x):
    return x * jax.nn.sigmoid(x)


def _inproj_kernel(x_ref, g_ref, w_ref, o_ref):
    x = x_ref[...]
    ms = jnp.mean(x * x, axis=-1, keepdims=True)
    xn = (x * lax.rsqrt(ms + EPS) * g_ref[...]).astype(bf16)
    step = 4 * LANES
    for c0 in range(0, Z_WIDTH, step):
        cw = min(step, Z_WIDTH - c0)
        o_ref[:, c0:c0 + cw] = jnp.dot(xn, w_ref[:, c0:c0 + cw], preferred_element_type=f32)


def _pack_w_in_kernel(lo_ref, hi_ref, ab_ref, o_ref, *, n_lo, n_hi):
    i = pl.program_id(0)

    def emit(src):
        for l in range(DEPTH):
            o_ref[l] = src(l).T.astype(bf16)

    @pl.when(i < n_lo)
    def _():
        emit(lambda l: lo_ref[:, l, :])

    @pl.when((i >= n_lo) & (i < n_lo + n_hi))
    def _():
        emit(lambda l: hi_ref[:, l, :])

    @pl.when(i == n_lo + n_hi)
    def _():
        live = _iota((LANES, D_MODEL), 0) < 2 * N_HEADS
        emit(lambda l: jnp.where(live, ab_ref[:, l, :], 0.0))


def _pack_w_in(w_in):
    wt = jnp.transpose(w_in, (2, 0, 1))
    n_ab = 2 * N_HEADS
    n_lo = AB_OFFSET_SRC // LANES
    n_hi = (IN_WIDTH - AB_OFFSET_SRC - n_ab) // LANES
    assert n_lo * LANES == AB_OFFSET_SRC and (n_lo + n_hi + 1) * LANES == Z_WIDTH
    rows = lambda start: pl.BlockSpec((pl.Element(LANES), pl.Element(DEPTH), pl.Element(D_MODEL)),
                                      lambda i: (start(i), 0, 0))
    return pl.pallas_call(
        functools.partial(_pack_w_in_kernel, n_lo=n_lo, n_hi=n_hi), grid=(Z_WIDTH // LANES,),
        in_specs=[rows(lambda i: LANES * jnp.minimum(i, n_lo - 1)),
                  rows(lambda i: AB_OFFSET_SRC + n_ab + LANES * jnp.clip(i - n_lo, 0, n_hi - 1)),
                  rows(lambda i: AB_OFFSET_SRC)],
        out_specs=pl.BlockSpec((DEPTH, D_MODEL, LANES), lambda i: (0, 0, i)),
        out_shape=jax.ShapeDtypeStruct((DEPTH, D_MODEL, Z_WIDTH), bf16),
        compiler_params=_params(1), name="pack_w_in")(wt, wt, wt)


def _layer_block(a, layer):
    return pl.BlockSpec((None,) + a.shape[1:], lambda *_: (layer, 0, 0), pipeline_mode=pl.Buffered(1))


def _inproj(x2d, gain, w_pack, layer, tm):
    m = x2d.shape[0]
    return pl.pallas_call(
        _inproj_kernel,
        grid=(m // tm,),
        in_specs=[pl.BlockSpec((tm, D_MODEL), lambda i: (i, 0)),
                  pl.BlockSpec((1, D_MODEL), lambda i: (0, 0)),
                  _layer_block(w_pack, layer)],
        out_specs=pl.BlockSpec((tm, Z_WIDTH), lambda i: (i, 0)),
        out_shape=jax.ShapeDtypeStruct((m, Z_WIDTH), f32),
        compiler_params=_params(1), name="inproj")(x2d, gain, w_pack)


def _ffn_kernel(x_ref, oa_ref, ob_ref, oc_ref, od_ref, wo_ref, g_ref, wu_ref, wd_ref, y_ref):
    mix = jnp.concatenate([r[...].astype(bf16) for r in (oa_ref, ob_ref, oc_ref, od_ref)], axis=1)
    h = x_ref[...] + jnp.dot(mix, wo_ref[...], preferred_element_type=f32)
    ms = jnp.mean(h * h, axis=-1, keepdims=True)
    hn = (h * lax.rsqrt(ms + EPS) * g_ref[...]).astype(bf16)
    y_ref[...] = h
    step = D_FF // 4
    for c in range(0, D_FF, step):
        u = jnp.dot(hn, wu_ref[:, c:c + step], preferred_element_type=f32)
        u = jnp.square(jnp.maximum(u, 0.0)).astype(bf16)
        y_ref[...] += jnp.dot(u, wd_ref[c:c + step, :], preferred_element_type=f32)


def _ffn(x2d, oa, ob, oc, od, w_out, gain, w_up, w_down, layer, tm):
    m = x2d.shape[0]
    row = lambda width: pl.BlockSpec((tm, width), lambda i: (i, 0))
    whole = lambda a: _layer_block(a, layer)
    return pl.pallas_call(
        _ffn_kernel,
        grid=(m // tm,),
        in_specs=[row(D_MODEL), row(GROUP_WIDTH), row(GROUP_WIDTH), row(GROUP_WIDTH), row(GROUP_WIDTH),
                  whole(w_out), pl.BlockSpec((1, D_MODEL), lambda i: (0, 0)), whole(w_up), whole(w_down)],
        out_specs=row(D_MODEL),
        out_shape=jax.ShapeDtypeStruct((m, D_MODEL), f32),
        compiler_params=_params(1), name="ffn")(x2d, oa, ob, oc, od, w_out, gain, w_up, w_down)


def _attn_prompt_kernel(q_ref, k_ref, v_ref, cos_ref, sin_ref, qg_ref, kg_ref, *rest,
                        seq, keep, wide_dil, wide_pitch, aliased):
    if aliased:
        rest = rest[2:]
    (o_ref, kt_ref, vt_ref, q_s, ks, acc_s, m_s, t_acc, t_m, qw, kw, vw, bias_first, bias_rest) = rest
    mean_mat = _same_head_matrix(LANES, 1.0 / HEAD_DIM)
    first_head = _iota((1, LANES), 1) < HEAD_DIM
    rc = 512
    blk = ATT_BLOCK

    def prep(i, carry):
        rows = pl.ds(pl.multiple_of(i * rc, rc), rc)
        cos, sin = cos_ref[rows, :], sin_ref[rows, :]
        q, k, v = q_ref[rows, :], k_ref[rows, :], v_ref[rows, :]
        q = q * lax.rsqrt(_head_sum_sq(q, mean_mat) + EPS) * qg_ref[...]
        k = k * lax.rsqrt(_head_sum_sq(k, mean_mat) + EPS) * kg_ref[...]
        q = _rope_slab(q, cos, sin) * (HEAD_DIM ** -0.5)
        q_s[rows, :] = q
        ks[rows, :] = _rope_slab(k, cos, sin)
        k = ks[rows, :]
        for g in range(rc // wide_dil):
            src = slice(g * wide_dil, (g + 1) * wide_dil)
            dst = pl.ds(pl.multiple_of(i * (rc // wide_dil * wide_pitch), SUBLANES) + g * wide_pitch, wide_dil)
            qw[dst, :], kw[dst, :], vw[dst, :] = q[src, :], k[src, :], v[src, :]
        return carry

    lax.fori_loop(0, seq // rc, prep, 0)

    for c in range(keep // rc):
        r0 = seq - keep + c * rc
        kt_ref[:, c * rc:(c + 1) * rc] = ks[r0:r0 + rc, :].T
        vt_ref[:, c * rc:(c + 1) * rc] = v_ref[r0:r0 + rc, :].T

    qi, kj = _iota((blk, blk), 0), _iota((blk, blk), 1)
    bias_first[...] = jnp.where(kj <= qi, 0.0, NEG)
    qi, kj = _iota((blk, 2 * blk), 0), _iota((blk, 2 * blk), 1)
    bias_rest[...] = jnp.where((kj >= qi) & (kj <= qi + blk), 0.0, NEG)

    def rows_of(st, n, dil):
        return pl.ds(st, n, stride=dil) if dil > 1 else pl.ds(st, n)

    def tile_group(tiles, dil, dst_acc, dst_m, pitch):
        staged = []
        for r, b, first in tiles:
            nk = blk if first else 2 * blk
            if dil == wide_dil:
                qrow = wide_pitch * (b * blk) + r
                krow = qrow if first else qrow - wide_pitch * blk
                rows, krows = pl.ds(qrow, blk, stride=wide_pitch), pl.ds(krow, nk, stride=wide_pitch)
                q, k, v = qw[rows, :], kw[krows, :].astype(bf16), vw[krows, :]
            else:
                start = r + b * (blk * dil)
                kstart = start if first else start - blk * dil
                rows, krows = rows_of(start, blk, dil), rows_of(kstart, nk, dil)
                q, k, v = q_s[rows, :], ks[krows, :].astype(bf16), v_ref[krows, :]
            qs = [jnp.where(first_head, q, 0.0).astype(bf16), jnp.where(first_head, 0.0, q).astype(bf16)]
            vs = [jnp.where(first_head, v, 1.0).astype(bf16), jnp.where(first_head, 1.0, v).astype(bf16)]
            bias = bias_first[...] if first else bias_rest[...]
            dst = pl.ds(pl.multiple_of(r * pitch + b * blk, SUBLANES), blk)
            staged.append((dst, vs, [_dot_nt(qs[hh], k) + bias for hh in range(2)]))
        probs = []
        for dst, vs, scores in staged:
            for hh in range(2):
                mt = jnp.max(scores[hh], axis=1, keepdims=True)
                probs.append((dst, hh, vs[hh], jnp.exp(scores[hh] - mt).astype(bf16), mt))
        for dst, hh, v, p, mt in probs:
            dst_acc[hh, dst, :] = jnp.dot(p, v, preferred_element_type=f32)
            dst_m[hh, dst, :] = jnp.broadcast_to(mt, (blk, LANES))

    merge_group = 4
    tiles_per_body = 16

    for bi, (window, dil) in enumerate(DIL_PAIRS):
        nb = seq // dil // blk
        dst_acc, dst_m = (acc_s, m_s) if bi == 0 else (t_acc, t_m)
        pitch = nb * blk + (SUBLANES if dil == wide_dil else 0)
        if nb >= tiles_per_body:
            assert nb % tiles_per_body == 0

            def per_residue(r, carry, dil=dil, nb=nb, dst_acc=dst_acc, dst_m=dst_m, pitch=pitch):
                tile_group([(r, b, b == 0) for b in range(tiles_per_body)], dil, dst_acc, dst_m, pitch)

                def per_group(g, c):
                    tile_group([(r, g * tiles_per_body + u, False) for u in range(tiles_per_body)],
                               dil, dst_acc, dst_m, pitch)
                    return c

                lax.fori_loop(1, nb // tiles_per_body, per_group, 0)
                return carry

            lax.fori_loop(0, dil, per_residue, 0)
        else:
            res_per_body = tiles_per_body // nb
            assert dil % res_per_body == 0

            def per_residues(g, carry, dil=dil, nb=nb, dst_acc=dst_acc, dst_m=dst_m, pitch=pitch, rpb=res_per_body):
                tile_group([(g * rpb + u, b, b == 0) for u in range(rpb) for b in range(nb)],
                           dil, dst_acc, dst_m, pitch)
                return carry

            lax.fori_loop(0, dil // res_per_body, per_residues, 0)

        if bi > 0 and dil != wide_dil:
            def merge(g, carry, dil=dil, nb=nb):
                pending = []
                for u in range(merge_group):
                    idx = g * merge_group + u
                    r, b = idx // nb, idx % nb
                    rows = rows_of(r + b * (blk * dil), blk, dil)
                    src = pl.ds(pl.multiple_of(idx * blk, blk), blk)
                    for hh in range(2):
                        mo, mt = m_s[hh, rows, :], t_m[hh, src, :]
                        mn = jnp.maximum(mo, mt)
                        an = acc_s[hh, rows, :] * jnp.exp(mo - mn) + t_acc[hh, src, :] * jnp.exp(mt - mn)
                        pending.append((hh, rows, an, mn))
                for hh, rows, an, mn in pending:
                    acc_s[hh, rows, :] = an
                    m_s[hh, rows, :] = mn
                return carry

            lax.fori_loop(0, seq // blk // merge_group, merge, 0)
        elif bi > 0:
            def merge_wide(g, carry, dil=dil, pitch=pitch):
                pending = []
                for u in range(merge_group):
                    t0 = pl.multiple_of((g * merge_group + u) * blk, blk)
                    rows = pl.ds(t0, blk)
                    for hh in range(2):
                        ta, tm = [], []
                        for v8 in range(blk // SUBLANES):
                            src = pl.ds(((v8 * SUBLANES) % dil) * pitch + t0 // dil + (v8 * SUBLANES) // dil,
                                        SUBLANES, stride=pitch)
                            ta.append(t_acc[hh, src, :])
                            tm.append(t_m[hh, src, :])
                        ta, mt = jnp.concatenate(ta, axis=0), jnp.concatenate(tm, axis=0)
                        mo = m_s[hh, rows, :]
                        mn = jnp.maximum(mo, mt)
                        pending.append(acc_s[hh, rows, :] * jnp.exp(mo - mn) + ta * jnp.exp(mt - mn))
                    a0, a1 = pending[-2:]
                    half = LANES // 2
                    o_ref[rows, :] = jnp.where(first_head, a0 / pltpu.roll(a0, half, 1),
                                               a1 / pltpu.roll(a1, half, 1)).astype(o_ref.dtype)
                return carry

            lax.fori_loop(0, seq // blk // merge_group, merge_wide, 0)


def _attn_prompt(layer, z, n, seq, cos_t, sin_t, qgain, kgain, prev_out):
    keep = min(WIN_MAX, seq)
    for window, dil in DIL_PAIRS:
        assert window // dil == ATT_BLOCK and seq % (dil * ATT_BLOCK) == 0
    wide = [dil for _, dil in DIL_PAIRS if dil % SUBLANES == 0]
    assert len(wide) == 1 and ATT_BLOCK % wide[0] == 0 and DIL_PAIRS[-1][1] == wide[0]
    wide_dil, wide_pitch = wide[0], wide[0] + SUBLANES
    per_group = GROUP_WIDTH // LANES
    zspec = lambda base: pl.BlockSpec((seq, LANES), lambda b, p, base=base: (b, base * per_group + p),
                                      pipeline_mode=pl.Buffered(1))
    whole = lambda rows: pl.BlockSpec((rows, LANES), lambda b, p: (0, 0), pipeline_mode=pl.Buffered(1))
    slab = lambda: pltpu.VMEM((seq, LANES), f32)
    slab2 = lambda: pltpu.VMEM((2, seq, LANES), f32)
    branch = lambda: pltpu.VMEM((2, seq + wide_dil * SUBLANES, LANES), f32)
    padded = lambda: pltpu.VMEM((seq // wide_dil * wide_pitch, LANES), f32)
    in_specs = [zspec(0), zspec(1), zspec(2), whole(seq), whole(seq), whole(1), whole(1)]
    args = [z, z, z, cos_t, sin_t, qgain, kgain]
    aliases = {}
    if prev_out is not None:
        in_specs += [pl.BlockSpec(memory_space=pl.ANY), pl.BlockSpec(memory_space=pl.ANY)]
        args += list(prev_out)
        aliases = {len(args) - 2: 1, len(args) - 1: 2}
    cache_spec = pl.BlockSpec((None, None, LANES, keep), lambda b, p: (layer, b, p, 0))
    cache_shape = jax.ShapeDtypeStruct((DEPTH, n, GROUP_WIDTH, keep), f32)
    return pl.pallas_call(
        functools.partial(_attn_prompt_kernel, seq=seq, keep=keep, wide_dil=wide_dil, wide_pitch=wide_pitch,
                          aliased=prev_out is not None),
        grid=(n, N_PAIRS),
        in_specs=in_specs,
        out_specs=[pl.BlockSpec((seq, LANES), lambda b, p: (b, p)), cache_spec, cache_spec],
        out_shape=[jax.ShapeDtypeStruct((n * seq, GROUP_WIDTH), bf16), cache_shape, cache_shape],
        input_output_aliases=aliases,
        scratch_shapes=[slab(), slab(), slab2(), slab2(), branch(), branch(), padded(), padded(), padded(),
                        pltpu.VMEM((ATT_BLOCK, ATT_BLOCK), f32), pltpu.VMEM((ATT_BLOCK, 2 * ATT_BLOCK), f32)],
        compiler_params=_params(2), name=f"attn_prompt_l{layer}")(*args)


def _attn_decode_prep_kernel(z_ref, cos_ref, sin_ref, qg_ref, kg_ref, q_ref, k_ref, v_ref):
    mean_mat = _same_head_matrix(GROUP_WIDTH, 1.0 / HEAD_DIM)
    z = z_ref[...]
    q, k = z[:, :GROUP_WIDTH], z[:, GROUP_WIDTH:2 * GROUP_WIDTH]
    q = q * lax.rsqrt(_head_sum_sq(q, mean_mat) + EPS) * qg_ref[...]
    k = k * lax.rsqrt(_head_sum_sq(k, mean_mat) + EPS) * kg_ref[...]
    q_ref[...] = _rope_wide(q, cos_ref[...], sin_ref[...]) * (HEAD_DIM ** -0.5)
    k_ref[...] = _rope_wide(k, cos_ref[...], sin_ref[...])
    v_ref[...] = z[:, 2 * GROUP_WIDTH:]


def _attn_decode_prep(z, cos_row, sin_row, qgain, kgain):
    m = z.shape[0]
    small = lambda a: pl.BlockSpec(a.shape, lambda i: (0, 0))
    out = jax.ShapeDtypeStruct((m, GROUP_WIDTH), f32)
    ospec = pl.BlockSpec((m, GROUP_WIDTH), lambda i: (0, 0))
    return pl.pallas_call(
        _attn_decode_prep_kernel, grid=(1,),
        in_specs=[pl.BlockSpec((m, 3 * GROUP_WIDTH), lambda i: (0, 0)), small(cos_row), small(sin_row),
                  small(qgain), small(kgain)],
        out_specs=[ospec, ospec, ospec], out_shape=[out, out, out],
        compiler_params=_params(1), name="attn_decode_prep")(z, cos_row, sin_row, qgain, kgain)


def _attn_decode_kernel(*refs, buf, aliased):
    if aliased:
        kc_ref, vc_ref, q_ref, kn_ref, vn_ref, _, _, ko_ref, vo_ref, o_ref = refs
    else:
        kc_ref, vc_ref, q_ref, kn_ref, vn_ref, ko_ref, vo_ref, o_ref = refs
    kc = kc_ref[...]
    vc = vc_ref[...]
    qc = q_ref[...][:, :, :1]
    knew = kn_ref[...][:, :, :1]
    vnew = vn_ref[...][:, :, :1]
    dist = buf - _iota((1, 1, buf), 2)
    mult = jnp.zeros((1, 1, buf), f32)
    for window, dil in DIL_PAIRS:
        mult = mult + jnp.where((dist <= window) & (dist % dil == 0), 1.0, 0.0)
    s = jnp.sum(kc * qc, axis=1, keepdims=True)
    s_new = jnp.sum(knew * qc, axis=1, keepdims=True)
    s = jnp.where(mult > 0.0, s, NEG)
    mx = jnp.maximum(jnp.max(s, axis=2, keepdims=True), s_new)
    p = mult * jnp.exp(s - mx)
    p_new = float(len(DIL_PAIRS)) * jnp.exp(s_new - mx)
    den = jnp.sum(p, axis=2, keepdims=True) + p_new
    o = (jnp.sum(vc * p, axis=2, keepdims=True) + vnew * p_new) / den
    o_ref[...] = jnp.broadcast_to(o, o_ref.shape)
    last = _iota(kc.shape, 2) == buf - 1
    ko_ref[...] = jnp.where(last, knew, pltpu.roll(kc, buf - 1, 2))
    vo_ref[...] = jnp.where(last, vnew, pltpu.roll(vc, buf - 1, 2))


def _attn_decode(layer, cache_kt, cache_vt, q_col, k_col, v_col, prev_out):
    depth, nb, nh, hd, buf = cache_kt.shape
    cache_spec = pl.BlockSpec((None, None, nh, hd, buf), lambda b: (layer, b, 0, 0, 0))
    col_spec = pl.BlockSpec((None, nh, hd, LANES), lambda b: (b, 0, 0, 0))
    in_specs = [cache_spec, cache_spec, col_spec, col_spec, col_spec]
    args = [cache_kt, cache_vt, q_col, k_col, v_col]
    aliases = {}
    if prev_out is not None:
        in_specs += [pl.BlockSpec(memory_space=pl.ANY), pl.BlockSpec(memory_space=pl.ANY)]
        args += list(prev_out)
        aliases = {5: 0, 6: 1}
    cache_shape = jax.ShapeDtypeStruct(cache_kt.shape, f32)
    return pl.pallas_call(
        functools.partial(_attn_decode_kernel, buf=buf, aliased=prev_out is not None),
        grid=(nb,), in_specs=in_specs,
        out_specs=[cache_spec, cache_spec, col_spec],
        out_shape=[cache_shape, cache_shape, jax.ShapeDtypeStruct((nb, nh, hd, LANES), f32)],
        input_output_aliases=aliases,
        compiler_params=_params(1), name=f"attn_decode_l{layer}")(*args)


SOLVE_BLOCK = 16


def _solve_unit_lower_pairs(systems, first_head):
    ng = CHUNK // SUBLANES
    gpb = SOLVE_BLOCK // SUBLANES
    col_in_head = _iota((SOLVE_BLOCK, LANES), 1) % HEAD_DIM
    split = lambda x: [x[SUBLANES * g:SUBLANES * (g + 1), :] for g in range(ng)]
    ags = [split(a) for a, _ in systems]
    xgs = [split(x) for _, x in systems]
    for blk in range(CHUNK // SOLVE_BLOCK):
        r0 = blk * SOLVE_BLOCK
        if blk > 0:
            for (a, _), xg in zip(systems, xgs):
                left = jnp.where(col_in_head < r0, a[r0:r0 + SOLVE_BLOCK, :], 0.0).astype(bf16)
                solved = _stack_heads(jnp.concatenate(xg, axis=0), first_head).astype(bf16)
                upd = jnp.dot(left, solved, preferred_element_type=f32)
                for g in range(gpb):
                    xg[blk * gpb + g] = xg[blk * gpb + g] - upd[SUBLANES * g:SUBLANES * (g + 1), :]
        for j in range(r0, r0 + SOLVE_BLOCK - 1):
            g0, r = divmod(j, SUBLANES)
            for ag, xg in zip(ags, xgs):
                xrow = jnp.broadcast_to(xg[g0][r:r + 1, :], (SUBLANES, LANES))
                for g in range(g0, (blk + 1) * gpb):
                    c0 = jnp.broadcast_to(ag[g][:, j:j + 1], (SUBLANES, LANES))
                    c1 = jnp.broadcast_to(ag[g][:, HEAD_DIM + j:HEAD_DIM + j + 1], (SUBLANES, LANES))
                    xg[g] = xg[g] - jnp.where(first_head, c0, c1) * xrow
    return [jnp.concatenate(xg, axis=0) for xg in xgs]


def _bf16_terms(x, terms):
    pieces, rest = [], x
    for t in range(terms):
        pieces.append(rest.astype(bf16))
        if t + 1 < terms:
            rest = rest - pieces[-1].astype(f32)
    return pieces


def _dot_split_lhs(x, mat, terms):
    parts = [jnp.dot(p, mat, preferred_element_type=f32) for p in _bf16_terms(x, terms)]
    return functools.reduce(lambda a, b: a + b, parts)


def _dot_split_rhs(mat, x, terms):
    parts = [jnp.dot(mat, p, preferred_element_type=f32) for p in _bf16_terms(x, terms)]
    return functools.reduce(lambda a, b: a + b, parts)


def _state_in(s0, group):
    layer = s0[1]
    return pl.BlockSpec((None, group, N_HEADS, HEAD_DIM, HEAD_DIM), lambda b, t: (layer, b, 0, 0, 0))


def _load_state_pairs(sbd, s0_ref):
    sbd[...] = jnp.zeros_like(sbd)
    for h in range(N_HEADS):
        p, hh = divmod(h, 2)
        sbd[p, hh * HEAD_DIM:(hh + 1) * HEAD_DIM, hh * HEAD_DIM:(hh + 1) * HEAD_DIM] = s0_ref[h]


def _store_state_pairs(sn_ref, sbd):
    for h in range(N_HEADS):
        p, hh = divmod(h, 2)
        sn_ref[h] = sbd[p, hh * HEAD_DIM:(hh + 1) * HEAD_DIM, hh * HEAD_DIM:(hh + 1) * HEAD_DIM]


def _delta_chunks(q, k, v, g, beta, sbd, first_head, same_head, ts, group):
    ri, cj = _iota((ts, ts), 0), _iota((ts, ts), 1)
    same = (ri // CHUNK) == (cj // CHUNK)
    cum_mat, tot_mat = (same & (cj <= ri)).astype(bf16), same.astype(bf16)
    g_terms = _bf16_terms(g, 2)
    rsum = lambda mat: jnp.concatenate(
        [functools.reduce(lambda a, b: a + b,
                          [jnp.dot(mat, t[s * ts:(s + 1) * ts, :], preferred_element_type=f32) for t in g_terms])
         for s in range(group)], axis=0)
    gcum = rsum(cum_mat)
    glast = rsum(tot_mat)
    eg = jnp.exp(gcum)
    qg, kb, vb = q * eg, k * beta, v * beta
    kbe = kb * eg
    kd = k * jnp.exp(glast - gcum)
    dl = jnp.exp(glast)

    ii = _iota((CHUNK, LANES), 0)
    jj = _iota((CHUNK, LANES), 1) % HEAD_DIM
    eye2 = (ii == jj).astype(f32)
    ones = jnp.ones((CHUNK, CHUNK), bf16)
    n_chunks = ts // CHUNK
    where = {(s, c, p): (slice(s * ts + c * CHUNK, s * ts + (c + 1) * CHUNK), slice(p * LANES, (p + 1) * LANES))
             for s in range(group) for c in range(n_chunks) for p in range(N_PAIRS)}

    amats, qks = {}, {}
    for key, (r, l) in where.items():
        gc = gcum[r, l]
        grow = _dot_split_rhs(ones, eye2 * gc, 2)
        dm = jnp.where(ii >= jj, jnp.exp(jnp.minimum(gc - grow, 0.0)), 0.0)
        kst = _stack_heads(k[r, l], first_head).astype(bf16)
        kk = _dot_nt(kb[r, l].astype(bf16), kst)
        qks[key] = (_dot_nt(q[r, l].astype(bf16), kst) * dm).astype(bf16)
        amats[key] = jnp.where(ii > jj, kk * dm, 0.0)

    outs = {}
    for c in range(n_chunks):
        keys = [(s, c, p) for s in range(group) for p in range(N_PAIRS)]
        s_prev = {key: sbd[key[0], key[2]] for key in keys}
        s_bf = {key: s_prev[key].astype(bf16) for key in keys}
        rhs = [vb[where[key]] - jnp.dot(kbe[where[key]].astype(bf16), s_bf[key], preferred_element_type=f32)
               for key in keys]
        vnews = _solve_unit_lower_pairs([(amats[key], x) for key, x in zip(keys, rhs)], first_head)
        for key, vnew in zip(keys, vnews):
            r, l = where[key]
            vst = _stack_heads(vnew, first_head).astype(bf16)
            outs[key] = (jnp.dot(qg[r, l].astype(bf16), s_bf[key], preferred_element_type=f32)
                         + jnp.dot(qks[key], vst, preferred_element_type=f32))
            upd = _dot_tn(kd[r, l].astype(bf16), vnew.astype(bf16))
            sbd[key[0], key[2]] = s_prev[key] * dl[r.start:r.start + 1, l] + jnp.where(same_head, upd, 0.0)
    return jnp.concatenate(
        [jnp.concatenate([outs[(s, c, p)] for p in range(N_PAIRS)], axis=1)
         for s in range(group) for c in range(n_chunks)], axis=0)


def _delta_kernel(zqkv_ref, zz_ref, zab_ref, conv0_ref, s0_ref, cw_ref, alog_ref, dtb_ref, on_ref,
                  o_ref, convn_ref, sn_ref, xp, sbd, *, ts, n_valid, n_tiles, group):
    t = pl.program_id(1)
    if n_valid == 1:
        _delta_single_tokens(zqkv_ref, zz_ref, zab_ref, conv0_ref, s0_ref, cw_ref, alog_ref, dtb_ref, on_ref,
                             o_ref, convn_ref, sn_ref, xp, sbd, ts=ts, group=group)
        return
    _delta_tiles(t, zqkv_ref, zz_ref, zab_ref, conv0_ref, s0_ref, cw_ref, alog_ref, dtb_ref, on_ref,
                 o_ref, convn_ref, sn_ref, xp, sbd, ts=ts, n_tiles=n_tiles, group=group)


def _delta_gates(zab, alog_ref, dtb_ref):
    src = _iota((LANES, GROUP_WIDTH), 0)
    head = _iota((LANES, GROUP_WIDTH), 1) // HEAD_DIM
    zab_terms = _bf16_terms(zab, 2)
    pick = lambda sel: functools.reduce(
        lambda a, b: a + b, [jnp.dot(t, sel.astype(bf16), preferred_element_type=f32) for t in zab_terms])
    a_bc, b_bc = pick(src == head), pick(src == head + N_HEADS)
    xa = a_bc + dtb_ref[...]
    g = -jnp.exp(alog_ref[...]) * (jnp.maximum(xa, 0.0) + jnp.log1p(jnp.exp(-jnp.abs(xa))))
    return g, jax.nn.sigmoid(b_bc)


def _delta_single_tokens(zqkv_ref, zz_ref, zab_ref, conv0_ref, s0_ref, cw_ref, alog_ref, dtb_ref, on_ref,
                         o_ref, convn_ref, sn_ref, xp, sbd, *, ts, group):
    lead, rows, w3 = SUBLANES, group * ts, 3 * GROUP_WIDTH
    xp[:, 0:lead, :] = conv0_ref[...]
    xp[:, lead:lead + ts, :] = zqkv_ref[...]
    cw = cw_ref[...]
    y = cw[0:1, :][None] * xp[:, lead - 3:lead - 3 + ts, :]
    for j in range(1, DELTA_CONV):
        y = y + cw[j:j + 1, :][None] * xp[:, lead - 3 + j:lead - 3 + j + ts, :]
    convn_ref[...] = xp[:, lead + 1 - (DELTA_CONV - 1):lead + 1, :]
    y = _silu(y).reshape(rows, w3)

    sum_mat = _same_head_matrix(GROUP_WIDTH, 1.0)
    yq, yk, v = y[:, :GROUP_WIDTH], y[:, GROUP_WIDTH:2 * GROUP_WIDTH], y[:, 2 * GROUP_WIDTH:]
    q = yq * lax.rsqrt(_head_sum_sq(yq, sum_mat) + EPS) * (HEAD_DIM ** -0.5)
    k = yk * lax.rsqrt(_head_sum_sq(yk, sum_mat) + EPS)
    g, beta = _delta_gates(zab_ref[...].reshape(rows, LANES), alog_ref, dtb_ref)
    live = _iota((rows, 1), 0) % ts == 0
    g, beta = jnp.where(live, g, 0.0), jnp.where(live, beta, 0.0)
    q, k, v = jnp.where(live, q, 0.0), jnp.where(live, k, 0.0), jnp.where(live, v, 0.0)
    eg = jnp.exp(g)
    qg, kbe, vb = q * eg, k * beta * eg, v * beta
    qk = _head_sum(q * k, sum_mat)

    same_head = (_iota((LANES, LANES), 0) // HEAD_DIM) == (_iota((LANES, LANES), 1) // HEAD_DIM)
    for s in range(group):
        _load_state_pairs(sbd.at[s], s0_ref.at[s])
    where = [(s, p, slice(s * ts, (s + 1) * ts), slice(p * LANES, (p + 1) * LANES))
             for s in range(group) for p in range(N_PAIRS)]
    s_prev = [sbd[s, p] for s, p, _, _ in where]
    s_bf = [x.astype(bf16) for x in s_prev]
    vnew = [vb[r, l] - jnp.dot(kbe[r, l].astype(bf16), sb, preferred_element_type=f32)
            for (_, _, r, l), sb in zip(where, s_bf)]
    outs = [jnp.dot(qg[r, l].astype(bf16), sb, preferred_element_type=f32) + qk[r, l] * vn
            for (_, _, r, l), sb, vn in zip(where, s_bf, vnew)]
    for (s, p, r, l), sp, vn in zip(where, s_prev, vnew):
        upd = _dot_tn(k[r, l].astype(bf16), vn.astype(bf16))
        sbd[s, p] = sp * eg[r.start:r.start + 1, l] + jnp.where(same_head, upd, 0.0)
    o = jnp.concatenate([jnp.concatenate(outs[s * N_PAIRS:(s + 1) * N_PAIRS], axis=1) for s in range(group)], axis=0)
    o = o * lax.rsqrt(_head_sum_sq(o, sum_mat) * (1.0 / HEAD_DIM) + EPS) * on_ref[...]
    o_ref[...] = (o * _silu(zz_ref[...].reshape(rows, GROUP_WIDTH))).reshape(group, ts, GROUP_WIDTH)
    for s in range(group):
        _store_state_pairs(sn_ref.at[s], sbd.at[s])


def _delta_tiles(t, zqkv_ref, zz_ref, zab_ref, conv0_ref, s0_ref, cw_ref, alog_ref, dtb_ref, on_ref,
                 o_ref, convn_ref, sn_ref, xp, sbd, *, ts, n_tiles, group):
    lead, rows, w3 = SUBLANES, group * ts, 3 * GROUP_WIDTH

    @pl.when(t == 0)
    def _():
        xp[:, 0:lead, :] = conv0_ref[...]
        for s in range(group):
            _load_state_pairs(sbd.at[s], s0_ref.at[s])

    xp[:, lead:lead + ts, :] = zqkv_ref[...]
    cw = cw_ref[...]
    y = cw[0:1, :][None] * xp[:, lead - 3:lead - 3 + ts, :]
    for j in range(1, DELTA_CONV):
        y = y + cw[j:j + 1, :][None] * xp[:, lead - 3 + j:lead - 3 + j + ts, :]
    convn_ref[...] = xp[:, lead + ts - (DELTA_CONV - 1):lead + ts, :]
    xp[:, 0:lead, :] = xp[:, ts:ts + lead, :]
    y = _silu(y).reshape(rows, w3)

    sum_mat = _same_head_matrix(GROUP_WIDTH, 1.0)
    yq, yk, v = y[:, :GROUP_WIDTH], y[:, GROUP_WIDTH:2 * GROUP_WIDTH], y[:, 2 * GROUP_WIDTH:]
    q = yq * lax.rsqrt(_head_sum_sq(yq, sum_mat) + EPS) * (HEAD_DIM ** -0.5)
    k = yk * lax.rsqrt(_head_sum_sq(yk, sum_mat) + EPS)
    g, beta = _delta_gates(zab_ref[...].reshape(rows, LANES), alog_ref, dtb_ref)

    first_head = _iota((1, LANES), 1) < HEAD_DIM
    same_head = (_iota((LANES, LANES), 0) // HEAD_DIM) == (_iota((LANES, LANES), 1) // HEAD_DIM)
    o = _delta_chunks(q, k, v, g, beta, sbd, first_head, same_head, ts, group)
    o = o * lax.rsqrt(_head_sum_sq(o, sum_mat) * (1.0 / HEAD_DIM) + EPS) * on_ref[...]
    o = o * _silu(zz_ref[...].reshape(rows, GROUP_WIDTH))
    o_ref[...] = o.astype(o_ref.dtype).reshape(group, ts, GROUP_WIDTH)

    @pl.when(t == n_tiles - 1)
    def _():
        for s in range(group):
            _store_state_pairs(sn_ref.at[s], sbd.at[s])


def _delta(z, n, seq, n_valid, conv0, s0, conv_w, alog_bc, dtb_bc, onorm_bc, ts, group=1):
    assert n_valid == seq or (n_valid == 1 and seq == ts)
    n_tiles = seq // ts
    assert n % group == 0
    w3 = 3 * GROUP_WIDTH
    z3 = z.reshape(n, seq, Z_WIDTH)
    row = lambda width, blk: pl.BlockSpec((group, ts, width), lambda b, t, blk=blk: (b, t, blk))
    small = lambda a: pl.BlockSpec(a.shape, lambda b, t: (0, 0))
    state = pl.BlockSpec((group, N_HEADS, HEAD_DIM, HEAD_DIM), lambda b, t: (b, 0, 0, 0))
    o, convn, sn = pl.pallas_call(
        functools.partial(_delta_kernel, ts=ts, n_valid=n_valid, n_tiles=n_tiles, group=group),
        grid=(n // group, n_tiles),
        in_specs=[row(w3, ZB_QKV_BLK), row(GROUP_WIDTH, ZB_Z_BLK), row(LANES, ZAB_BLK),
                  pl.BlockSpec((group, SUBLANES, w3), lambda b, t: (b, 0, 0)), _state_in(s0, group),
                  small(conv_w), small(alog_bc), small(dtb_bc), small(onorm_bc)],
        out_specs=[pl.BlockSpec((group, ts, GROUP_WIDTH), lambda b, t: (b, t, 0)),
                   pl.BlockSpec((group, DELTA_CONV - 1, w3), lambda b, t: (b, 0, 0)), state],
        out_shape=[jax.ShapeDtypeStruct((n, seq, GROUP_WIDTH), f32 if n_valid == 1 else bf16),
                   jax.ShapeDtypeStruct((n, DELTA_CONV - 1, w3), f32),
                   jax.ShapeDtypeStruct((n, N_HEADS, HEAD_DIM, HEAD_DIM), f32)],
        scratch_shapes=[pltpu.VMEM((group, ts + SUBLANES, w3), f32),
                        pltpu.VMEM((group, N_PAIRS, LANES, LANES), f32)],
        compiler_params=_params(2), name="delta")(z3, z3, z3, conv0, s0[0], conv_w, alog_bc, dtb_bc, onorm_bc)
    return o.reshape(n * seq, GROUP_WIDTH), convn, sn


def _retention_kernel(zq_ref, zk_ref, zv_ref, zg_ref, dx_ref, dc_ref, db_ref, cos_ref, sin_ref, lg_ref, on_ref,
                      cwd_ref, s0_ref, dconv0_ref, o_ref, od_ref, sn_ref, dconvn_ref, sbd, dmask, xpd,
                      *, tc, n_valid, n_tiles, group):
    b, t = pl.program_id(0), pl.program_id(1)
    lg = lg_ref[...]

    @pl.when((b == 0) & (t == 0))
    def _():
        col = _iota((tc, 2 * tc), 1)
        diff = (_iota((tc, 2 * tc), 0) - col % tc).astype(f32)
        for p in range(N_PAIRS):
            lgp = lg[:, p * LANES:(p + 1) * LANES]
            rate = jnp.where(col < tc, lgp[:, 0:1], lgp[:, HEAD_DIM:HEAD_DIM + 1])
            dmask[p] = jnp.where(diff >= 0.0, jnp.exp(jnp.maximum(diff, 0.0) * rate), 0.0)

    if n_valid == 1:
        _retention_single_tokens(zq_ref, zk_ref, zv_ref, zg_ref, dx_ref, dc_ref, db_ref, cos_ref, sin_ref, lg, on_ref,
                                 cwd_ref, s0_ref, dconv0_ref, o_ref, od_ref, sn_ref, dconvn_ref, sbd, xpd,
                                 tc=tc, group=group)
        return
    for s in range(group):
        _retention_sequence(t, zq_ref.at[s], zk_ref.at[s], zv_ref.at[s], zg_ref.at[s], dx_ref.at[s],
                            dc_ref.at[s], db_ref.at[s], cos_ref, sin_ref, lg, on_ref, cwd_ref, s0_ref.at[s],
                            dconv0_ref.at[s], o_ref.at[s], od_ref.at[s], sn_ref.at[s], dconvn_ref.at[s],
                            sbd.at[s], dmask, xpd.at[s], tc=tc, n_tiles=n_tiles)


def _retention_single_tokens(zq_ref, zk_ref, zv_ref, zg_ref, dx_ref, dc_ref, db_ref, cos_ref, sin_ref, lg, on_ref,
                             cwd_ref, s0_ref, dconv0_ref, o_ref, od_ref, sn_ref, dconvn_ref, sbd, xpd, *, tc, group):
    lead, rows = SUBLANES, group * tc
    cos = jnp.concatenate([cos_ref[...]] * group, axis=0)
    sin = jnp.concatenate([sin_ref[...]] * group, axis=0)
    live = _iota((rows, 1), 0) % tc == 0
    flat = lambda ref: ref[...].reshape(rows, GROUP_WIDTH)
    q = jnp.where(live, _rope_wide(flat(zq_ref), cos, sin), 0.0)
    k = jnp.where(live, _rope_wide(flat(zk_ref), cos, sin) * (HEAD_DIM ** -0.5), 0.0)
    v = jnp.where(live, flat(zv_ref), 0.0)
    gamma = jnp.exp(lg)
    qd = q * gamma
    qk = _head_sum(q * k, _same_head_matrix(GROUP_WIDTH, 1.0))

    same_head = (_iota((LANES, LANES), 0) // HEAD_DIM) == (_iota((LANES, LANES), 1) // HEAD_DIM)
    for s in range(group):
        _load_state_pairs(sbd.at[s], s0_ref.at[s])
    where = [(s, p, slice(s * tc, (s + 1) * tc), slice(p * LANES, (p + 1) * LANES))
             for s in range(group) for p in range(N_PAIRS)]
    s_prev = [sbd[s, p] for s, p, _, _ in where]
    outs = [jnp.dot(qd[r, l].astype(bf16), sp.astype(bf16), preferred_element_type=f32) + qk[r, l] * v[r, l]
            for (_, _, r, l), sp in zip(where, s_prev)]
    for (s, p, r, l), sp in zip(where, s_prev):
        upd = _dot_tn(k[r, l].astype(bf16), v[r, l].astype(bf16))
        sbd[s, p] = sp * gamma[:, l] + jnp.where(same_head, upd, 0.0)
    o = jnp.concatenate([jnp.concatenate(outs[s * N_PAIRS:(s + 1) * N_PAIRS], axis=1) for s in range(group)], axis=0)
    mean_mat = _same_head_matrix(GROUP_WIDTH, 1.0 / HEAD_DIM)
    o = o * lax.rsqrt(_head_sum_sq(o, mean_mat) + EPS) * on_ref[...]
    o_ref[...] = (o * _silu(flat(zg_ref))).reshape(group, tc, GROUP_WIDTH)
    for s in range(group):
        _store_state_pairs(sn_ref.at[s], sbd.at[s])

    xpd[:, 0:lead, :] = dconv0_ref[...]
    xpd[:, lead:lead + tc, :] = dc_ref[...] * dx_ref[...]
    cwd = cwd_ref[...]
    yd = cwd[0:1, :][None] * xpd[:, lead - 2:lead - 2 + tc, :]
    for j in range(1, SHORT_CONV):
        yd = yd + cwd[j:j + 1, :][None] * xpd[:, lead - 2 + j:lead - 2 + j + tc, :]
    od_ref[...] = db_ref[...] * yd
    dconvn_ref[...] = xpd[:, lead + 1 - (SHORT_CONV - 1):lead + 1, :]


def _retention_sequence(t, zq_ref, zk_ref, zv_ref, zg_ref, dx_ref, dc_ref, db_ref, cos_ref, sin_ref, lg, on_ref,
                        cwd_ref, s0_ref, dconv0_ref, o_ref, od_ref, sn_ref, dconvn_ref, sbd, dmask, xpd,
                        *, tc, n_tiles):
    lead = SUBLANES

    @pl.when(t == 0)
    def _():
        _load_state_pairs(sbd, s0_ref)
        xpd[0:lead, :] = dconv0_ref[...]

    cos, sin = cos_ref[...], sin_ref[...]
    q = _rope_wide(zq_ref[...], cos, sin)
    k = _rope_wide(zk_ref[...], cos, sin) * (HEAD_DIM ** -0.5)
    v = zv_ref[...]
    pos = _iota((tc, 1), 0).astype(f32)
    qd = q * jnp.exp((pos + 1.0) * lg)
    kdk = k * jnp.exp((float(tc - 1) - pos) * lg)
    tile_decay = jnp.exp(float(tc) * lg)

    first_head = _iota((1, LANES), 1) < HEAD_DIM
    same_head = (_iota((LANES, LANES), 0) // HEAD_DIM) == (_iota((LANES, LANES), 1) // HEAD_DIM)
    outs = []
    for p in range(N_PAIRS):
        l = slice(p * LANES, (p + 1) * LANES)
        kst = _stack_heads(k[:, l], first_head).astype(bf16)
        vst = _stack_heads(v[:, l], first_head).astype(bf16)
        sc = _dot_nt(q[:, l].astype(bf16), kst) * dmask[p]
        s_prev = sbd[p]
        o = (jnp.dot(sc.astype(bf16), vst, preferred_element_type=f32)
             + jnp.dot(qd[:, l].astype(bf16), s_prev.astype(bf16), preferred_element_type=f32))
        upd = _dot_tn(kdk[:, l].astype(bf16), v[:, l].astype(bf16))
        sbd[p] = s_prev * tile_decay[:, l] + jnp.where(same_head, upd, 0.0)
        outs.append(o)
    o = jnp.concatenate(outs, axis=1)
    mean_mat = _same_head_matrix(GROUP_WIDTH, 1.0 / HEAD_DIM)
    o = o * lax.rsqrt(_head_sum_sq(o, mean_mat) + EPS) * on_ref[...]
    o_ref[...] = (o * _silu(zg_ref[...])).astype(o_ref.dtype)

    xpd[lead:lead + tc, :] = dc_ref[...] * dx_ref[...]
    cwd = cwd_ref[...]
    yd = cwd[0:1, :] * xpd[lead - 2:lead - 2 + tc, :]
    for j in range(1, SHORT_CONV):
        yd = yd + cwd[j:j + 1, :] * xpd[lead - 2 + j:lead - 2 + j + tc, :]
    od_ref[...] = (db_ref[...] * yd).astype(od_ref.dtype)
    dconvn_ref[...] = xpd[lead + tc - (SHORT_CONV - 1):lead + tc, :]
    xpd[0:lead, :] = xpd[tc:tc + lead, :]

    @pl.when(t == n_tiles - 1)
    def _():
        _store_state_pairs(sn_ref, sbd)


def _retention(z, n, seq, n_valid, s0, dconv0, cos_t, sin_t, lg_bc, onorm_bc, conv_d, tc, group=1):
    assert n_valid == seq or (n_valid == 1 and seq == tc)
    n_tiles = seq // tc
    assert n % group == 0
    z = z.reshape(n, seq, Z_WIDTH)
    row = lambda blk: pl.BlockSpec((group, tc, GROUP_WIDTH), lambda b, t, blk=blk: (b, t, blk))
    small = lambda a: pl.BlockSpec(a.shape, lambda b, t: (0, 0))
    tab = pl.BlockSpec((tc, LANES), lambda b, t: (t, 0))
    state = pl.BlockSpec((group, N_HEADS, HEAD_DIM, HEAD_DIM), lambda b, t: (b, 0, 0, 0))
    out_row = pl.BlockSpec((group, tc, GROUP_WIDTH), lambda b, t: (b, t, 0))
    out_rows = jax.ShapeDtypeStruct((n, seq, GROUP_WIDTH), f32 if n_valid == 1 else bf16)
    o, od, sn, dconvn = pl.pallas_call(
        functools.partial(_retention_kernel, tc=tc, n_valid=n_valid, n_tiles=n_tiles, group=group),
        grid=(n // group, n_tiles),
        in_specs=[row(ZC_Q_BLK), row(ZC_K_BLK), row(ZC_V_BLK), row(ZC_G_BLK),
                  row(ZD_X_BLK), row(ZD_C_BLK), row(ZD_B_BLK), tab, tab,
                  small(lg_bc), small(onorm_bc), small(conv_d), _state_in(s0, group),
                  pl.BlockSpec((group, SUBLANES, GROUP_WIDTH), lambda b, t: (b, 0, 0))],
        out_specs=[out_row, out_row, state,
                   pl.BlockSpec((group, SHORT_CONV - 1, GROUP_WIDTH), lambda b, t: (b, 0, 0))],
        out_shape=[out_rows, out_rows, jax.ShapeDtypeStruct((n, N_HEADS, HEAD_DIM, HEAD_DIM), f32),
                   jax.ShapeDtypeStruct((n, SHORT_CONV - 1, GROUP_WIDTH), f32)],
        scratch_shapes=[pltpu.VMEM((group, N_PAIRS, LANES, LANES), f32), pltpu.VMEM((N_PAIRS, tc, 2 * tc), f32),
                        pltpu.VMEM((group, tc + SUBLANES, GROUP_WIDTH), f32)],
        compiler_params=_params(2), name="retention")(
            z, z, z, z, z, z, z, cos_t, sin_t, lg_bc, onorm_bc, conv_d, s0[0], dconv0)
    return o.reshape(n * seq, GROUP_WIDTH), od.reshape(n * seq, GROUP_WIDTH), sn, dconvn


def _rope_tables(pos, inv_freq):
    ang = pos.astype(f32)[:, None] * inv_freq[None, :]
    cos, sin = jnp.cos(ang), jnp.sin(ang)
    reps = LANES // HEAD_DIM
    return (jnp.tile(jnp.concatenate([cos, cos], axis=1), (1, reps)),
            jnp.tile(jnp.concatenate([-sin, sin], axis=1), (1, reps)))


def _per_head_lanes(v):
    return jnp.repeat(v.astype(f32), HEAD_DIM)[None, :]


def _tiled_lanes(v, width):
    return jnp.tile(v.astype(f32), width // HEAD_DIM)[None, :]


def _lead_pad(state):
    return jnp.pad(state, ((0, 0), (SUBLANES - state.shape[1], 0), (0, 0)))


def kernel(x_prompt, x_sample, cache_a_k, cache_a_v, state_b_conv, state_b_rec, state_c_rec, state_d_conv,
           norm_mix, w_in, q_norm_a, k_norm_a, conv_b, a_log_b, dt_bias_b, onorm_b, onorm_c, conv_d,
           w_out, norm_ffn, w_up, w_down):
    nb, seq, _ = x_prompt.shape
    nd, dseq, _ = x_sample.shape
    assert dseq == 1
    dpad = SUBLANES
    dgroup = max(g for g in (8, 4, 2, 1) if nd % g == 0)
    rope_freq = ROPE_THETA ** (-jnp.arange(0, HEAD_DIM, 2, dtype=f32) / HEAD_DIM)
    ret_freq = 1.0 / (10000.0 ** jnp.linspace(0.0, 1.0, HEAD_DIM // 2, dtype=f32))
    ret_lg = _per_head_lanes(jnp.log(1.0 - 2.0 ** (-5.0 - jnp.arange(N_HEADS, dtype=f32))))
    pos_p = jnp.arange(seq, dtype=jnp.int32)
    pos_d = PAST_LEN + jnp.arange(dpad, dtype=jnp.int32)
    cos_ap, sin_ap = _rope_tables(pos_p, rope_freq)
    cos_cp, sin_cp = _rope_tables(pos_p, ret_freq)
    cos_ad, sin_ad = _rope_tables(pos_d[:1], rope_freq)
    cos_cd, sin_cd = _rope_tables(pos_d, ret_freq)

    cache_kt = jnp.transpose(cache_a_k, (0, 1, 3, 4, 2))
    cache_vt = jnp.transpose(cache_a_v, (0, 1, 3, 4, 2))

    yp = x_prompt.reshape(nb * seq, D_MODEL)
    ys = x_sample.reshape(nd, D_MODEL)
    zeros_conv_b = jnp.zeros((nb, SUBLANES, 3 * GROUP_WIDTH), f32)
    zeros_conv_d = jnp.zeros((nb, SUBLANES, GROUP_WIDTH), f32)
    zeros_rec = (jnp.zeros((1, nb, N_HEADS, HEAD_DIM, HEAD_DIM), f32), 0)
    new_p, new_s = [], []
    dec_cache = prompt_cache = None
    w_pack = _pack_w_in(w_in)
    wo, wu, wd = w_out.astype(bf16), w_up.astype(bf16), w_down.astype(bf16)
    for l in range(DEPTH):
        g_mix, g_ffn = norm_mix[l][None, :], norm_ffn[l][None, :]
        qg, kg = _tiled_lanes(q_norm_a[l], LANES), _tiled_lanes(k_norm_a[l], LANES)
        qg3, kg3 = _tiled_lanes(q_norm_a[l], GROUP_WIDTH), _tiled_lanes(k_norm_a[l], GROUP_WIDTH)
        alog, dtb = _per_head_lanes(a_log_b[l]), _per_head_lanes(dt_bias_b[l])
        on_b, on_c = _tiled_lanes(onorm_b[l], GROUP_WIDTH), _tiled_lanes(onorm_c[l], GROUP_WIDTH)

        z = _inproj(yp, g_mix, w_pack, l, DENSE_ROW_TILE)
        oa, kt, vt = _attn_prompt(l, z, nb, seq, cos_ap, sin_ap, qg, kg, prompt_cache)
        prompt_cache = (kt, vt)
        ob, p_bconv, p_brec = _delta(z, nb, seq, seq, zeros_conv_b, zeros_rec, conv_b[l], alog, dtb, on_b, RECURRENT_TILE,
                                     group=2 if nb % 2 == 0 else 1)
        oc, od, p_crec, p_dconv = _retention(z, nb, seq, seq, zeros_rec, zeros_conv_d, cos_cp, sin_cp, ret_lg,
                                             on_c, conv_d[l], RECURRENT_TILE,
                                             group=max(g for g in (4, 2, 1) if nb % g == 0))
        yp = _ffn(yp, oa, ob, oc, od, wo, g_ffn, wu, wd, l, DENSE_ROW_TILE)
        new_p.append((kt, vt, p_bconv, p_brec, p_crec, p_dconv))

        zs = _inproj(ys, g_mix, w_pack, l, nd)
        qa, ka, va = _attn_decode_prep(zs, cos_ad, sin_ad, qg3, kg3)
        col = lambda a: jnp.broadcast_to(a.reshape(nd, N_HEADS, HEAD_DIM, 1), (nd, N_HEADS, HEAD_DIM, LANES))
        s_kt, s_vt, oa_col = _attn_decode(l, cache_kt, cache_vt, col(qa), col(ka), col(va), dec_cache)
        dec_cache = (s_kt, s_vt)
        oa_s = oa_col[..., 0].reshape(nd, GROUP_WIDTH)
        zpad = jnp.zeros((nd, dpad, Z_WIDTH), f32).at[:, 0, :].set(zs).reshape(nd * dpad, Z_WIDTH)
        ob_s, s_bconv, s_brec = _delta(zpad, nd, dpad, 1, _lead_pad(state_b_conv[l]), (state_b_rec, l), conv_b[l],
                                       alog, dtb, on_b, dpad, group=dgroup)
        oc_s, od_s, s_crec, s_dconv = _retention(zpad, nd, dpad, 1, (state_c_rec, l), _lead_pad(state_d_conv[l]),
                                                 cos_cd, sin_cd, ret_lg, on_c, conv_d[l], dpad, group=dgroup)
        first = lambda a: a.reshape(nd, dpad, GROUP_WIDTH)[:, 0, :]
        ys = _ffn(ys, oa_s, first(ob_s), first(oc_s), first(od_s), wo, g_ffn, wu, wd, l, nd)
        new_s.append((s_bconv, s_brec, s_crec, s_dconv))

    keep = min(WIN_MAX, seq)
    cache_out = lambda i: jnp.transpose(
        prompt_cache[i].reshape(DEPTH, nb, N_HEADS, HEAD_DIM, keep), (0, 1, 4, 2, 3))
    stack_p = lambda i: jnp.stack([s[i] for s in new_p])
    stack_s = lambda i: jnp.stack([s[i] for s in new_s])
    return (yp.reshape(nb, seq, D_MODEL), ys.reshape(nd, 1, D_MODEL),
            cache_out(0), cache_out(1), stack_p(2), stack_p(3), stack_p(4), stack_p(5),
            jnp.transpose(dec_cache[0], (0, 1, 4, 2, 3)), jnp.transpose(dec_cache[1], (0, 1, 4, 2, 3)),
            stack_s(0), stack_s(1), stack_s(2), stack_s(3))
```
